```python
import jax, jax.numpy as jnp
from jax import lax
import numpy as np

D_MODEL = 4096
BATCH = 2
SEQ = 8192
DEPTH = 2

HEAD_DIM = 128
MIX_WIDTH = D_MODEL
N_MIX_HEADS = MIX_WIDTH // HEAD_DIM
N_MIXERS = 4
GROUP_HEADS = N_MIX_HEADS // N_MIXERS
GROUP_WIDTH = GROUP_HEADS * HEAD_DIM
D_FF = 256 * ((8 * D_MODEL // 3 + 255) // 256)
FFN_RESIDUAL = 0.5
RMS_EPS = 1e-6

MOBA_BLOCK = 256
MOBA_TOPK = 3
MOBA_Q_CHUNK = 32

NSA_KV_HEADS = max(1, GROUP_HEADS // 4)
NSA_Q_PER_KV = GROUP_HEADS // NSA_KV_HEADS
NSA_KV_WIDTH = NSA_KV_HEADS * HEAD_DIM
NSA_CMP_STRIDE = 16
NSA_CMP_LEN = 2 * NSA_CMP_STRIDE
NSA_SEL_BLOCK = 64
NSA_SEL_TOPK = 16
NSA_WINDOW = 512
NSA_N_BRANCHES = 3
NSA_Q_CHUNK = 64

DENSE_Q_CHUNK = 128
FORGET_BIAS_MEAN = 3.0

PROJ_SIZES = (
    GROUP_WIDTH, GROUP_WIDTH, GROUP_WIDTH,
    GROUP_WIDTH,
    NSA_KV_WIDTH, NSA_KV_WIDTH,
    NSA_KV_WIDTH, NSA_KV_WIDTH,
    NSA_KV_WIDTH, NSA_KV_WIDTH,
    NSA_N_BRANCHES * GROUP_HEADS,
    GROUP_WIDTH, GROUP_WIDTH, GROUP_WIDTH,
    GROUP_HEADS,
    GROUP_WIDTH, GROUP_WIDTH, GROUP_WIDTH,
)
IN_PROJ_WIDTH = sum(PROJ_SIZES)

kernel_name = 'hybrid_moba_nsa_fox_stickbreaking_macaron'


def _rmsnorm(x, gain):
    xf = x.astype(jnp.float32)
    y = xf * lax.rsqrt(jnp.mean(xf * xf, axis=-1, keepdims=True) + RMS_EPS)
    return (y * gain.astype(jnp.float32)).astype(x.dtype)


def _swiglu(h, w_gate, w_up, w_down):
    return (jax.nn.silu(h @ w_gate) * (h @ w_up)) @ w_down


def _alibi_slopes():
    n = 2 * GROUP_HEADS
    slopes = 2.0 ** (-8.0 * np.arange(1, n + 1) / n)
    return jnp.asarray(slopes[0::2], jnp.float32), jnp.asarray(slopes[1::2], jnp.float32)


def _split_points():
    return [int(v) for v in np.cumsum(PROJ_SIZES)[:-1]]


def _split_heads(t, n_heads):
    b, s, _ = t.shape
    return t.reshape(b, s, n_heads, HEAD_DIM).transpose(0, 2, 1, 3)


def _merge_heads(o):
    b, h, s, d = o.shape
    return o.transpose(0, 2, 1, 3).reshape(b, s, h * d)


def _masked_softmax(s, mask):
    s = jnp.where(mask, s, -jnp.inf)
    m = jnp.max(s, axis=-1, keepdims=True)
    m = jnp.where(jnp.isfinite(m), m, 0.0)
    p = jnp.exp(s - m)
    return p / jnp.maximum(jnp.sum(p, axis=-1, keepdims=True), jnp.finfo(jnp.float32).tiny)


def _chunk_map(fn, seq_len, chunk):
    return lax.map(fn, jnp.arange(seq_len // chunk, dtype=jnp.int32) * chunk)


def _moba_attention(q, k, v, slopes):
    B, H, T, hd = q.shape
    L, C = MOBA_BLOCK, MOBA_Q_CHUNK
    n_blk = -(-T // L)
    n_sel = min(MOBA_TOPK, n_blk)
    padw = ((0, 0), (0, 0), (0, n_blk * L - T), (0, 0))
    kp, vp = jnp.pad(k, padw), jnp.pad(v, padw)
    kb = kp.reshape(B, H, n_blk, L, hd)
    vb = vp.reshape(B, H, n_blk, L, hd)
    k_mean = jnp.mean(kb.astype(jnp.float32), axis=3)
    scale = hd ** -0.5
    blk = jnp.arange(n_blk)
    bi = jnp.arange(B)[:, None, None, None]
    hi = jnp.arange(H)[None, :, None, None]
    sl = slopes[None, :, None, None]

    def chunk(start):
        qc = lax.dynamic_slice_in_dim(q, start, C, axis=2)
        t = start + jnp.arange(C)
        own = start // L
        gate = jnp.einsum('bhcd,bhnd->bhcn', qc, k_mean, preferred_element_type=jnp.float32)
        gate = jnp.where(blk < own, gate, -jnp.inf)
        _, idx = lax.top_k(gate, n_sel)
        valid = idx < own
        ks = kb[bi, hi, idx]
        vs = vb[bi, hi, idx]
        pos_sel = idx[..., None] * L + jnp.arange(L)
        s_sel = jnp.einsum('bhcd,bhcnld->bhcnl', qc, ks, preferred_element_type=jnp.float32) * scale
        s_sel = s_sel - sl[..., None] * (t[:, None, None] - pos_sel)
        s_sel = jnp.where(valid[..., None], s_sel, -jnp.inf).reshape(B, H, C, n_sel * L)
        ko = lax.dynamic_slice_in_dim(kp, own * L, L, axis=2)
        vo = lax.dynamic_slice_in_dim(vp, own * L, L, axis=2)
        pos_own = own * L + jnp.arange(L)
        s_own = jnp.einsum('bhcd,bhld->bhcl', qc, ko, preferred_element_type=jnp.float32) * scale
        s_own = s_own - sl * (t[:, None] - pos_own[None, :])
        s_own = jnp.where(pos_own[None, :] <= t[:, None], s_own, -jnp.inf)
        p = jax.nn.softmax(jnp.concatenate([s_sel, s_own], axis=-1), axis=-1)
        o = (jnp.einsum('bhcn,bhcnd->bhcd', p[..., :n_sel * L], vs.reshape(B, H, C, n_sel * L, hd))
             + jnp.einsum('bhcl,bhld->bhcd', p[..., n_sel * L:], vo))
        return o.astype(v.dtype)

    out = _chunk_map(chunk, T, C)
    return jnp.moveaxis(out, 0, 2).reshape(B, H, T, hd)


def _nsa_compress(x, pos_emb, w1, w2):
    B, G, T, hd = x.shape
    c = x.reshape(B, G, T // NSA_CMP_STRIDE, NSA_CMP_STRIDE, hd)
    blocks = jnp.concatenate([c[:, :, :-1], c[:, :, 1:]], axis=3) + pos_emb
    hmid = jax.nn.silu(blocks.reshape(B, G, -1, NSA_CMP_LEN * hd) @ w1)
    return hmid @ w2


def _nsa_attention(q, k_cmp, v_cmp, k_slc, v_slc, k_win, v_win, gates,
                   pos_k, w1_k, w2_k, pos_v, w1_v, w2_v, slopes):
    B, G, Hg, T, hd = q.shape
    S, LEN, SEL, C, W = NSA_CMP_STRIDE, NSA_CMP_LEN, NSA_SEL_BLOCK, NSA_Q_CHUNK, NSA_WINDOW
    scale = hd ** -0.5
    kc = _nsa_compress(k_cmp, pos_k, w1_k, w2_k)
    vc = _nsa_compress(v_cmp, pos_v, w1_v, w2_v)
    n_cmp = kc.shape[2]
    cmp_end = jnp.arange(n_cmp) * S + LEN - 1
    n_blk = T // SEL
    n_sel = min(NSA_SEL_TOPK, n_blk)
    ratio = SEL // S
    ksb = k_slc.reshape(B, G, n_blk, SEL, hd)
    vsb = v_slc.reshape(B, G, n_blk, SEL, hd)
    padw = ((0, 0), (0, 0), (W, 0), (0, 0))
    kw, vw = jnp.pad(k_win, padw), jnp.pad(v_win, padw)
    sl = slopes.reshape(G, Hg)[None, :, :, None, None]
    bi = jnp.arange(B)[:, None, None, None]
    gi = jnp.arange(G)[None, :, None, None]
    blk = jnp.arange(n_blk)

    def chunk(start):
        qc = lax.dynamic_slice_in_dim(q, start, C, axis=3)
        gc = lax.dynamic_slice_in_dim(gates, start, C, axis=3)
        t = start + jnp.arange(C)
        s_c = jnp.einsum('bghcd,bgnd->bghcn', qc, kc, preferred_element_type=jnp.float32) * scale
        s_c = s_c - sl * (t[:, None] - cmp_end[None, :])
        p_c = _masked_softmax(s_c, cmp_end[None, :] <= t[:, None])
        o_c = jnp.einsum('bghcn,bgnd->bghcd', p_c, vc)
        imp = jnp.pad(jnp.sum(p_c, axis=2), ((0, 0), (0, 0), (0, 0), (1, 1)))
        imp = imp[..., :ratio * n_blk].reshape(B, G, C, n_blk, ratio).sum(-1) + imp[..., ratio::ratio]
        jt = t // SEL
        forced = (blk == 0) | (blk == jt[:, None]) | (blk == jt[:, None] - 1)
        score = jnp.where(blk > jt[:, None], -jnp.inf, jnp.where(forced, jnp.inf, imp))
        _, idx = lax.top_k(score, n_sel)
        valid = idx <= jt[:, None]
        ks = ksb[bi, gi, idx]
        vs = vsb[bi, gi, idx]
        pos_s = idx[..., None] * SEL + jnp.arange(SEL)
        s_s = jnp.einsum('bghcd,bgcnld->bghcnl', qc, ks, preferred_element_type=jnp.float32) * scale
        s_s = s_s - sl[..., None] * (t[:, None, None] - pos_s)[:, :, None]
        m_s = (valid[..., None] & (pos_s <= t[:, None, None]))[:, :, None]
        p_s = _masked_softmax(s_s.reshape(B, G, Hg, C, n_sel * SEL),
                              m_s.reshape(B, G, 1, C, n_sel * SEL)).reshape(B, G, Hg, C, n_sel, SEL)
        o_s = jnp.einsum('bghcnl,bgcnld->bghcd', p_s, vs)
        kwc = lax.dynamic_slice_in_dim(kw, start, C + W, axis=2)
        vwc = lax.dynamic_slice_in_dim(vw, start, C + W, axis=2)
        pos_w = start - W + jnp.arange(C + W)
        d_w = t[:, None] - pos_w[None, :]
        m_w = (pos_w[None, :] >= 0) & (d_w >= 0) & (d_w < W)
        s_w = jnp.einsum('bghcd,bgwd->bghcw', qc, kwc, preferred_element_type=jnp.float32) * scale - sl * d_w
        p_w = _masked_softmax(s_w, m_w)
        o_w = jnp.einsum('bghcw,bgwd->bghcd', p_w, vwc)
        o = gc[..., 0:1] * o_c + gc[..., 1:2] * o_s + gc[..., 2:3] * o_w
        return o.astype(q.dtype)

    out = _chunk_map(chunk, T, C)
    return jnp.moveaxis(out, 0, 3).reshape(B, G * Hg, T, hd)


def _forgetting_attention(q, k, v, log_f):
    B, H, T, hd = q.shape
    C = DENSE_Q_CHUNK
    scale = hd ** -0.5
    cum_f = jnp.cumsum(log_f, axis=-1)
    s_pos = jnp.arange(T)

    def chunk(start):
        qc = lax.dynamic_slice_in_dim(q, start, C, axis=2)
        fc = lax.dynamic_slice_in_dim(cum_f, start, C, axis=2)
        t = start + jnp.arange(C)
        s = jnp.einsum('bhcd,bhsd->bhcs', qc, k, preferred_element_type=jnp.float32) * scale
        s = s + fc[..., :, None] - cum_f[:, :, None, :]
        s = jnp.where(s_pos[None, :] <= t[:, None], s, -jnp.inf)
        p = jax.nn.softmax(s, axis=-1)
        return jnp.einsum('bhcs,bhsd->bhcd', p, v).astype(v.dtype)

    out = _chunk_map(chunk, T, C)
    return jnp.moveaxis(out, 0, 2).reshape(B, H, T, hd)


def _stick_breaking_attention(q, k, v):
    B, H, T, hd = q.shape
    C = DENSE_Q_CHUNK
    scale = hd ** -0.5
    s_pos = jnp.arange(T)

    def chunk(start):
        qc = lax.dynamic_slice_in_dim(q, start, C, axis=2)
        t = start + jnp.arange(C)
        z = jnp.einsum('bhcd,bhsd->bhcs', qc, k, preferred_element_type=jnp.float32) * scale
        causal = s_pos[None, :] < t[:, None]
        log_keep = jnp.where(causal, jax.nn.log_sigmoid(-z), 0.0)
        log_later = lax.cumsum(log_keep, axis=3, reverse=True) - log_keep
        a = jnp.where(causal, jnp.exp(jax.nn.log_sigmoid(z) + log_later), 0.0)
        return jnp.einsum('bhcs,bhsd->bhcd', a, v).astype(v.dtype)

    out = _chunk_map(chunk, T, C)
    return jnp.moveaxis(out, 0, 2).reshape(B, H, T, hd)


def _hybrid_mixer(h, w_in, forget_bias, cmp_pos_k, cmp_w1_k, cmp_w2_k,
                  cmp_pos_v, cmp_w1_v, cmp_w2_v, group_gain, w_out):
    B, T, _ = h.shape
    (mq, mk, mv, nq, nkc, nvc, nks, nvs, nkw, nvw, ng,
     fq, fk, fv, ff, sq, sk, sv) = jnp.split(h @ w_in, _split_points(), axis=-1)
    slopes_moba, slopes_nsa = _alibi_slopes()
    o_a = _moba_attention(_split_heads(mq, GROUP_HEADS), _split_heads(mk, GROUP_HEADS),
                          _split_heads(mv, GROUP_HEADS), slopes_moba)
    G, Hg = NSA_KV_HEADS, NSA_Q_PER_KV
    q_b = nq.reshape(B, T, G, Hg, HEAD_DIM).transpose(0, 2, 3, 1, 4)
    gates = jax.nn.sigmoid(ng.astype(jnp.float32)).reshape(B, T, G, Hg, NSA_N_BRANCHES).transpose(0, 2, 3, 1, 4)
    o_b = _nsa_attention(q_b, _split_heads(nkc, G), _split_heads(nvc, G), _split_heads(nks, G),
                         _split_heads(nvs, G), _split_heads(nkw, G), _split_heads(nvw, G), gates,
                         cmp_pos_k, cmp_w1_k, cmp_w2_k, cmp_pos_v, cmp_w1_v, cmp_w2_v, slopes_nsa)
    log_f = jax.nn.log_sigmoid((ff + forget_bias).astype(jnp.float32)).transpose(0, 2, 1)
    o_c = _forgetting_attention(_split_heads(fq, GROUP_HEADS), _split_heads(fk, GROUP_HEADS),
                                _split_heads(fv, GROUP_HEADS), log_f)
    o_d = _stick_breaking_attention(_split_heads(sq, GROUP_HEADS), _split_heads(sk, GROUP_HEADS),
                                    _split_heads(sv, GROUP_HEADS))
    o = jnp.stack([_merge_heads(o_a), _merge_heads(o_b), _merge_heads(o_c), _merge_heads(o_d)], axis=2)
    o = _rmsnorm(o, group_gain.reshape(N_MIXERS, GROUP_WIDTH))
    return o.reshape(B, T, MIX_WIDTH) @ w_out


def setup_inputs(seed: int = 0) -> dict:
    key = jax.random.key(seed)
    ks = jax.random.split(key, 21)
    f32 = jnp.float32
    L = DEPTH
    cmp_in = NSA_CMP_LEN * HEAD_DIM

    def w(k, shape, fan_in):
        return jax.random.normal(k, shape, f32) * (fan_in ** -0.5)

    def gain(k, shape):
        return 1.0 + 0.02 * jax.random.normal(k, shape, f32)

    return {
        'x': jax.random.normal(ks[0], (BATCH, SEQ, D_MODEL), f32),
        'ffn1_norm': gain(ks[1], (L, D_MODEL)),
        'ffn1_w_gate': w(ks[2], (L, D_MODEL, D_FF), D_MODEL),
        'ffn1_w_up': w(ks[3], (L, D_MODEL, D_FF), D_MODEL),
        'ffn1_w_down': w(ks[4], (L, D_FF, D_MODEL), D_FF),
        'mix_norm': gain(ks[5], (L, D_MODEL)),
        'w_in': w(ks[6], (L, D_MODEL, IN_PROJ_WIDTH), D_MODEL),
        'fox_forget_bias': FORGET_BIAS_MEAN + 0.5 * jax.random.normal(ks[7], (L, GROUP_HEADS), f32),
        'nsa_cmp_pos_k': 0.1 * jax.random.normal(ks[8], (L, NSA_CMP_LEN, HEAD_DIM), f32),
        'nsa_cmp_w1_k': w(ks[9], (L, cmp_in, HEAD_DIM), cmp_in),
        'nsa_cmp_w2_k': w(ks[10], (L, HEAD_DIM, HEAD_DIM), HEAD_DIM),
        'nsa_cmp_pos_v': 0.1 * jax.random.normal(ks[11], (L, NSA_CMP_LEN, HEAD_DIM), f32),
        'nsa_cmp_w1_v': w(ks[12], (L, cmp_in, HEAD_DIM), cmp_in),
        'nsa_cmp_w2_v': w(ks[13], (L, HEAD_DIM, HEAD_DIM), HEAD_DIM),
        'group_norm': gain(ks[14], (L, MIX_WIDTH)),
        'w_out': w(ks[15], (L, MIX_WIDTH, D_MODEL), MIX_WIDTH),
        'ffn2_norm': gain(ks[16], (L, D_MODEL)),
        'ffn2_w_gate': w(ks[17], (L, D_MODEL, D_FF), D_MODEL),
        'ffn2_w_up': w(ks[18], (L, D_MODEL, D_FF), D_MODEL),
        'ffn2_w_down': w(ks[19], (L, D_FF, D_MODEL), D_FF),
        'final_norm': gain(ks[20], (D_MODEL,)),
    }


def reference(x, ffn1_norm, ffn1_w_gate, ffn1_w_up, ffn1_w_down, mix_norm, w_in,
              fox_forget_bias, nsa_cmp_pos_k, nsa_cmp_w1_k, nsa_cmp_w2_k,
              nsa_cmp_pos_v, nsa_cmp_w1_v, nsa_cmp_w2_v, group_norm, w_out,
              ffn2_norm, ffn2_w_gate, ffn2_w_up, ffn2_w_down, final_norm):
    for i in range(DEPTH):
        h = _rmsnorm(x, ffn1_norm[i])
        x = x + FFN_RESIDUAL * _swiglu(h, ffn1_w_gate[i], ffn1_w_up[i], ffn1_w_down[i])
        h = _rmsnorm(x, mix_norm[i])
        x = x + _hybrid_mixer(h, w_in[i], fox_forget_bias[i], nsa_cmp_pos_k[i], nsa_cmp_w1_k[i],
                              nsa_cmp_w2_k[i], nsa_cmp_pos_v[i], nsa_cmp_w1_v[i], nsa_cmp_w2_v[i],
                              group_norm[i], w_out[i])
        h = _rmsnorm(x, ffn2_norm[i])
        x = x + FFN_RESIDUAL * _swiglu(h, ffn2_w_gate[i], ffn2_w_up[i], ffn2_w_down[i])
    return _rmsnorm(x, final_norm)
```

```python
import functools
import math

import numpy as np
import jax
import jax.numpy as jnp
from jax import lax
from jax.experimental import pallas as pl
from jax.experimental.pallas import tpu as pltpu

F32 = jnp.float32
BF16 = jnp.bfloat16

HEAD_DIM = 128
GROUP_HEADS = 8
GROUP_WIDTH = GROUP_HEADS * HEAD_DIM
N_MIXERS = 4
FFN_RESIDUAL = 0.5
RMS_EPS = 1e-6

MOBA_BLOCK = 256
MOBA_TOPK = 3

NSA_KV_HEADS = 2
NSA_Q_PER_KV = 4
NSA_CMP_STRIDE = 16
NSA_CMP_LEN = 32
NSA_SEL_BLOCK = 64
NSA_SEL_TOPK = 16
NSA_WINDOW = 512
NSA_N_BRANCHES = 3

ATTN_SCALE = HEAD_DIM ** -0.5
NEG_BIG = -1e30
F32_TINY = float(np.finfo(np.float32).tiny)

LANES = 128
VMEM_LIMIT = 56 * 1024 * 1024

_MAIN_PIECES = (
    ("mq", 8), ("mk", 8), ("mv", 8), ("nq", 8),
    ("nkc", 2), ("nvc", 2), ("nks", 2), ("nvs", 2), ("nkw", 2), ("nvw", 2),
    ("fq", 8), ("fk", 8), ("fv", 8), ("sq", 8), ("sk", 8), ("sv", 8),
)
COL = {}
_o = 0
for _n, _w in _MAIN_PIECES:
    COL[_n] = _o
    _o += _w
MAIN_BLOCKS = _o

_REF_SIZES = (
    ("mq", 1024), ("mk", 1024), ("mv", 1024), ("nq", 1024),
    ("nkc", 256), ("nvc", 256), ("nks", 256), ("nvs", 256), ("nkw", 256), ("nvw", 256),
    ("ng", 24),
    ("fq", 1024), ("fk", 1024), ("fv", 1024), ("ff", 8),
    ("sq", 1024), ("sk", 1024), ("sv", 1024),
)
REF_OFF = {}
_o = 0
for _n, _w in _REF_SIZES:
    REF_OFF[_n] = (_o, _w)
    _o += _w


def _cparams(sem):
    return pltpu.CompilerParams(dimension_semantics=sem, vmem_limit_bytes=VMEM_LIMIT)


def _rms_kernel(x_ref, g_ref, o_ref):
    x = x_ref[...]
    ms = jnp.mean(x * x, axis=-1, keepdims=True)
    o_ref[...] = (x * lax.rsqrt(ms + RMS_EPS) * g_ref[...]).astype(o_ref.dtype)


def _rmsnorm(x2d, gain, out_dtype, tm=256):
    m, d = x2d.shape
    return pl.pallas_call(
        _rms_kernel,
        grid=(m // tm,),
        in_specs=[pl.BlockSpec((tm, d), lambda i: (i, 0)),
                  pl.BlockSpec((1, d), lambda i: (0, 0))],
        out_specs=pl.BlockSpec((tm, d), lambda i: (i, 0)),
        out_shape=jax.ShapeDtypeStruct((m, d), out_dtype),
        compiler_params=_cparams(("parallel",)),
        name="rmsnorm",
    )(x2d, gain.reshape(1, d).astype(F32))


def _groupnorm_kernel(a_ref, b_ref, c_ref, d_ref, g_ref, o_ref):
    for gi, ref in enumerate((a_ref, b_ref, c_ref, d_ref)):
        x = ref[...]
        ms = jnp.mean(x * x, axis=-1, keepdims=True)
        lo, hi = gi * GROUP_WIDTH, (gi + 1) * GROUP_WIDTH
        o_ref[:, lo:hi] = (x * lax.rsqrt(ms + RMS_EPS) * g_ref[:, lo:hi]).astype(o_ref.dtype)


def _groupnorm(parts, gain, tm=256):
    m = parts[0].shape[0]
    d = GROUP_WIDTH * N_MIXERS
    part_spec = pl.BlockSpec((tm, GROUP_WIDTH), lambda i: (i, 0))
    return pl.pallas_call(
        _groupnorm_kernel,
        grid=(m // tm,),
        in_specs=[part_spec] * 4 + [pl.BlockSpec((1, d), lambda i: (0, 0))],
        out_specs=pl.BlockSpec((tm, d), lambda i: (i, 0)),
        out_shape=jax.ShapeDtypeStruct((m, d), BF16),
        compiler_params=_cparams(("parallel",)),
        name="groupnorm",
    )(*parts, gain.reshape(1, d).astype(F32))


def _mm_kernel(a_ref, w_ref, o_ref):
    o_ref[...] = jnp.dot(a_ref[...], w_ref[...], preferred_element_type=F32).astype(o_ref.dtype)


def _matmul(a, w, out_dtype, tm, tn):
    m, k = a.shape
    n = w.shape[1]
    return pl.pallas_call(
        _mm_kernel,
        grid=(m // tm, n // tn),
        in_specs=[pl.BlockSpec((tm, k), lambda i, j: (i, 0)),
                  pl.BlockSpec((k, tn), lambda i, j: (0, j))],
        out_specs=pl.BlockSpec((tm, tn), lambda i, j: (i, j)),
        out_shape=jax.ShapeDtypeStruct((m, n), out_dtype),
        compiler_params=_cparams(("parallel", "arbitrary")),
        name="matmul",
    )(a, w)


def _mm_res_kernel(a_ref, w_ref, x_ref, o_ref):
    o_ref[...] = x_ref[...] + jnp.dot(a_ref[...], w_ref[...], preferred_element_type=F32)


def _matmul_residual(a, w, x, tm, tn):
    m, k = a.shape
    n = w.shape[1]
    return pl.pallas_call(
        _mm_res_kernel,
        grid=(m // tm, n // tn),
        in_specs=[pl.BlockSpec((tm, k), lambda i, j: (i, 0)),
                  pl.BlockSpec((k, tn), lambda i, j: (0, j)),
                  pl.BlockSpec((tm, tn), lambda i, j: (i, j))],
        out_specs=pl.BlockSpec((tm, tn), lambda i, j: (i, j)),
        out_shape=jax.ShapeDtypeStruct((m, n), F32),
        compiler_params=_cparams(("parallel", "arbitrary")),
        name="matmul_residual",
    )(a, w, x)


def _gateup_kernel(h_ref, wg_ref, wu_ref, o_ref):
    h = h_ref[...]
    g = jnp.dot(h, wg_ref[...], preferred_element_type=F32)
    u = jnp.dot(h, wu_ref[...], preferred_element_type=F32)
    o_ref[...] = (g / (1.0 + jnp.exp(-g)) * u).astype(o_ref.dtype)


def _gateup(h, wg, wu, tm, tf):
    m, d = h.shape
    f = wg.shape[1]
    return pl.pallas_call(
        _gateup_kernel,
        grid=(m // tm, f // tf),
        in_specs=[pl.BlockSpec((tm, d), lambda i, j: (i, 0)),
                  pl.BlockSpec((d, tf), lambda i, j: (0, j)),
                  pl.BlockSpec((d, tf), lambda i, j: (0, j))],
        out_specs=pl.BlockSpec((tm, tf), lambda i, j: (i, j)),
        out_shape=jax.ShapeDtypeStruct((m, f), BF16),
        compiler_params=_cparams(("parallel", "arbitrary")),
        name="ffn_gateup",
    )(h, wg, wu)


def _down_kernel(a_ref, w_ref, x_ref, o_ref):
    part = FFN_RESIDUAL * jnp.dot(a_ref[...], w_ref[...], preferred_element_type=F32)

    @pl.when(pl.program_id(2) == 0)
    def _():
        o_ref[...] = x_ref[...] + part

    @pl.when(pl.program_id(2) != 0)
    def _():
        o_ref[...] += part


def _down(act, wd, x, tm, tn, tk):
    m, f = act.shape
    n = wd.shape[1]
    return pl.pallas_call(
        _down_kernel,
        grid=(m // tm, n // tn, f // tk),
        in_specs=[pl.BlockSpec((tm, tk), lambda i, j, k: (i, k)),
                  pl.BlockSpec((tk, tn), lambda i, j, k: (k, j)),
                  pl.BlockSpec((tm, tn), lambda i, j, k: (i, j))],
        out_specs=pl.BlockSpec((tm, tn), lambda i, j, k: (i, j)),
        out_shape=jax.ShapeDtypeStruct((m, n), F32),
        compiler_params=_cparams(("parallel", "parallel", "arbitrary")),
        name="ffn_down",
    )(act, wd, x)


def _pick_tile(n, prefs):
    for t in prefs:
        if n % t == 0:
            return t
    return n


def _ffn(x2d, gain, wg, wu, wd):
    m = x2d.shape[0]
    h = _rmsnorm(x2d, gain, BF16)
    tm = _pick_tile(m, (1024, 512, 256))
    act = _gateup(h, wg, wu, tm, _pick_tile(wg.shape[1], (512, 256, 128)))
    return _down(act, wd, x2d, tm, _pick_tile(wd.shape[1], (1024, 512, 256, 128)),
                 _pick_tile(wd.shape[0], (2816, 1024, 512, 256, 128)))


def _qk(q, k):
    return lax.dot_general(q, k, (((1,), (1,)), ((), ())), preferred_element_type=F32) * ATTN_SCALE


def _online_update(s, v, m_ref, l_ref, acc_ref):
    m_prev = m_ref[...]
    m_new = jnp.maximum(m_prev, jnp.max(s, axis=-1, keepdims=True))
    alpha = jnp.exp(m_prev - m_new)
    p = jnp.exp(s - m_new)
    l_ref[...] = alpha * l_ref[...] + jnp.sum(p, axis=-1, keepdims=True)
    acc_ref[...] = alpha * acc_ref[...] + jnp.dot(p.astype(BF16), v, preferred_element_type=F32)
    m_ref[...] = m_new


def _topk_rounds(score, blkf, sel, rounds):
    for _ in range(rounds):
        mx = jnp.max(score, axis=-1, keepdims=True)
        idx = jnp.min(jnp.where(score == mx, blkf, 1e9), axis=-1, keepdims=True)
        pick = (blkf == idx) & (mx > -jnp.inf)
        sel = jnp.where(pick, 1.0, sel)
        score = jnp.where(pick, -jnp.inf, score)
    return sel


def _split3(x):
    hi = x.astype(BF16)
    r1 = x - hi.astype(F32)
    mid = r1.astype(BF16)
    lo = (r1 - mid.astype(F32)).astype(BF16)
    return hi, mid, lo


def _dot_split(x, w_bf16, terms):
    parts = _split3(x)[:terms]
    out = jnp.dot(parts[0], w_bf16, preferred_element_type=F32)
    for p in parts[1:]:
        out = out + jnp.dot(p, w_bf16, preferred_element_type=F32)
    return out


def _logf_cumsum_kernel(x_ref, tri_ref, low_ref, o_ref):
    x = x_ref[0]
    logf = jnp.minimum(x, 0.0) - jnp.log1p(jnp.exp(-jnp.abs(x)))
    within = _dot_split(logf, tri_ref[...], 3)
    hi, mid, lo = _split3(logf)
    low = low_ref[...]
    before = (jnp.dot(low, hi, preferred_element_type=F32) + jnp.dot(low, mid, preferred_element_type=F32)
              + jnp.dot(low, lo, preferred_element_type=F32))
    o_ref[0] = -(within + jnp.sum(before, axis=-1, keepdims=True))


def _neg_cum_logf(logits_rows):
    r, t = logits_rows.shape
    nc = t // LANES
    li = np.arange(LANES)
    tri = jnp.asarray((li[:, None] <= li[None, :]).astype(np.float32), BF16)
    ci = np.arange(nc)
    low = jnp.asarray((ci[None, :] < ci[:, None]).astype(np.float32), BF16)
    out = pl.pallas_call(
        _logf_cumsum_kernel,
        grid=(r,),
        in_specs=[pl.BlockSpec((1, nc, LANES), lambda i: (i, 0, 0)),
                  pl.BlockSpec((LANES, LANES), lambda i: (0, 0)),
                  pl.BlockSpec((nc, nc), lambda i: (0, 0))],
        out_specs=pl.BlockSpec((1, nc, LANES), lambda i: (i, 0, 0)),
        out_shape=jax.ShapeDtypeStruct((r, nc, LANES), F32),
        compiler_params=_cparams(("parallel",)),
        name="fox_logf_cumsum",
    )(logits_rows.reshape(r, nc, LANES), tri, low)
    return out.reshape(r, t)


def _fox_kernel(q_ref, k_ref, v_ref, nf_ref, o_ref, m_ref, l_ref, acc_ref, *, tq):
    i = pl.program_id(2)
    q = q_ref[0]
    m_ref[...] = jnp.full(m_ref.shape, NEG_BIG, F32)
    l_ref[...] = jnp.zeros(l_ref.shape, F32)
    acc_ref[...] = jnp.zeros(acc_ref.shape, F32)

    def tile(j, causal):
        off = pl.multiple_of(j * tq, tq)
        s = _qk(q, k_ref[0, pl.ds(off, tq), :]) + nf_ref[0, 0, pl.ds(j, 1), :]
        if causal:
            row = lax.broadcasted_iota(jnp.int32, (tq, tq), 0)
            col = lax.broadcasted_iota(jnp.int32, (tq, tq), 1)
            s = jnp.where(col <= row, s, NEG_BIG)
        _online_update(s, v_ref[0, pl.ds(off, tq), :], m_ref, l_ref, acc_ref)

    def body(j, c):
        tile(j, False)
        return c

    lax.fori_loop(0, i, body, 0)
    tile(i, True)
    o_ref[0] = acc_ref[...] / l_ref[...]


def _fox_attention(p3, neg_cum_f, tq):
    b, t, _ = p3.shape
    nq = t // tq
    nf = neg_cum_f.reshape(b, GROUP_HEADS, nq, tq)
    return pl.pallas_call(
        functools.partial(_fox_kernel, tq=tq),
        grid=(b, GROUP_HEADS, nq),
        in_specs=[pl.BlockSpec((1, tq, HEAD_DIM), lambda bi, h, i: (bi, i, COL["fq"] + h)),
                  pl.BlockSpec((1, t, HEAD_DIM), lambda bi, h, i: (bi, 0, COL["fk"] + h)),
                  pl.BlockSpec((1, t, HEAD_DIM), lambda bi, h, i: (bi, 0, COL["fv"] + h)),
                  pl.BlockSpec((1, 1, nq, tq), lambda bi, h, i: (bi, h, 0, 0))],
        out_specs=pl.BlockSpec((1, tq, HEAD_DIM), lambda bi, h, i: (bi, i, h)),
        out_shape=jax.ShapeDtypeStruct((b, t, GROUP_WIDTH), F32),
        scratch_shapes=[pltpu.VMEM((tq, 1), F32), pltpu.VMEM((tq, 1), F32),
                        pltpu.VMEM((tq, HEAD_DIM), F32)],
        compiler_params=_cparams(("parallel", "parallel", "arbitrary")),
        name="fox_attention",
    )(p3, p3, p3, nf)


def _stick_kernel(q_ref, k_ref, v_ref, u_ref, o_ref, r_ref, acc_ref, *, tq):
    i = pl.program_id(2)
    q = q_ref[0]
    r_ref[...] = jnp.zeros(r_ref.shape, F32)
    acc_ref[...] = jnp.zeros(acc_ref.shape, F32)
    u = u_ref[...]

    def tile(j, diag):
        off = pl.multiple_of(j * tq, tq)
        z = _qk(q, k_ref[0, pl.ds(off, tq), :])
        log_beta = jnp.minimum(z, 0.0) - jnp.log(1.0 + jnp.exp(-jnp.abs(z)))
        log_keep = log_beta - z
        if diag:
            row = lax.broadcasted_iota(jnp.int32, (tq, tq), 0)
            col = lax.broadcasted_iota(jnp.int32, (tq, tq), 1)
            causal = col < row
            log_keep = jnp.where(causal, log_keep, 0.0)
        later = _dot_split(log_keep, u, 2) + r_ref[...]
        a = jnp.exp(log_beta + later)
        if diag:
            a = jnp.where(causal, a, 0.0)
        acc_ref[...] += jnp.dot(a.astype(BF16), v_ref[0, pl.ds(off, tq), :], preferred_element_type=F32)
        r_ref[...] += jnp.sum(log_keep, axis=-1, keepdims=True)

    tile(i, True)

    def body(n, c):
        tile(i - 1 - n, False)
        return c

    lax.fori_loop(0, i, body, 0)
    o_ref[0] = acc_ref[...]


def _stick_attention(p3, tq):
    b, t, _ = p3.shape
    nq = t // tq
    ki = np.arange(tq)
    u = jnp.asarray((ki[:, None] > ki[None, :]).astype(np.float32), BF16)
    return pl.pallas_call(
        functools.partial(_stick_kernel, tq=tq),
        grid=(b, GROUP_HEADS, nq),
        in_specs=[pl.BlockSpec((1, tq, HEAD_DIM), lambda bi, h, i: (bi, i, COL["sq"] + h)),
                  pl.BlockSpec((1, t, HEAD_DIM), lambda bi, h, i: (bi, 0, COL["sk"] + h)),
                  pl.BlockSpec((1, t, HEAD_DIM), lambda bi, h, i: (bi, 0, COL["sv"] + h)),
                  pl.BlockSpec((tq, tq), lambda bi, h, i: (0, 0))],
        out_specs=pl.BlockSpec((1, tq, HEAD_DIM), lambda bi, h, i: (bi, i, h)),
        out_shape=jax.ShapeDtypeStruct((b, t, GROUP_WIDTH), F32),
        scratch_shapes=[pltpu.VMEM((tq, 1), F32), pltpu.VMEM((tq, HEAD_DIM), F32)],
        compiler_params=_cparams(("parallel", "parallel", "arbitrary")),
        name="stick_attention",
    )(p3, p3, p3, u)


def _moba_kernel(slope_ref, q_ref, k_ref, v_ref, o_ref, kmean_ref, m_ref, l_ref, acc_ref, *, n_blk):
    h = pl.program_id(1)
    i = pl.program_id(2)
    tq = MOBA_BLOCK
    slope = slope_ref[h]

    @pl.when(i == 0)
    def _():
        kmean_ref[...] = jnp.zeros(kmean_ref.shape, F32)
        kb = k_ref[0].astype(F32).reshape(n_blk, tq, HEAD_DIM)
        kmean_ref[0:n_blk, :] = jnp.mean(kb, axis=1)

    q = q_ref[0]
    gate = lax.dot_general(q, kmean_ref[...].astype(BF16), (((1,), (1,)), ((), ())),
                           preferred_element_type=F32)
    blkf = lax.broadcasted_iota(jnp.int32, (tq, LANES), 1).astype(F32)
    score = jnp.where(blkf < i.astype(F32), gate, -jnp.inf)
    sel = _topk_rounds(score, blkf, jnp.zeros((tq, LANES), F32), MOBA_TOPK)

    m_ref[...] = jnp.full(m_ref.shape, NEG_BIG, F32)
    l_ref[...] = jnp.zeros(l_ref.shape, F32)
    acc_ref[...] = jnp.zeros(acc_ref.shape, F32)
    colf = lax.broadcasted_iota(jnp.int32, (1, tq), 1).astype(F32)

    def tile(j, own):
        off = pl.multiple_of(j * tq, tq)
        bias = slope * (colf + (j - i).astype(F32) * tq)
        s = _qk(q, k_ref[0, pl.ds(off, tq), :]) + bias
        if own:
            row = lax.broadcasted_iota(jnp.int32, (tq, tq), 0)
            col = lax.broadcasted_iota(jnp.int32, (tq, tq), 1)
            s = jnp.where(col <= row, s, NEG_BIG)
        else:
            chosen = jnp.sum(jnp.where(blkf == j.astype(F32), sel, 0.0), axis=-1, keepdims=True)
            s = jnp.where(chosen > 0.5, s, NEG_BIG)
        _online_update(s, v_ref[0, pl.ds(off, tq), :], m_ref, l_ref, acc_ref)

    tile(i, True)

    def body(j, c):
        tile(j, False)
        return c

    lax.fori_loop(0, i, body, 0)
    o_ref[0] = acc_ref[...] / l_ref[...]


def _moba_attention(p3, slopes):
    b, t, _ = p3.shape
    tq = MOBA_BLOCK
    n_blk = t // tq
    assert t % tq == 0 and n_blk <= LANES
    return pl.pallas_call(
        functools.partial(_moba_kernel, n_blk=n_blk),
        grid=(b, GROUP_HEADS, n_blk),
        in_specs=[pl.BlockSpec(memory_space=pltpu.SMEM),
                  pl.BlockSpec((1, tq, HEAD_DIM), lambda bi, h, i: (bi, i, COL["mq"] + h)),
                  pl.BlockSpec((1, t, HEAD_DIM), lambda bi, h, i: (bi, 0, COL["mk"] + h)),
                  pl.BlockSpec((1, t, HEAD_DIM), lambda bi, h, i: (bi, 0, COL["mv"] + h))],
        out_specs=pl.BlockSpec((1, tq, HEAD_DIM), lambda bi, h, i: (bi, i, h)),
        out_shape=jax.ShapeDtypeStruct((b, t, GROUP_WIDTH), F32),
        scratch_shapes=[pltpu.VMEM((LANES, HEAD_DIM), F32),
                        pltpu.VMEM((tq, 1), F32), pltpu.VMEM((tq, 1), F32),
                        pltpu.VMEM((tq, HEAD_DIM), F32)],
        compiler_params=_cparams(("parallel", "parallel", "arbitrary")),
        name="moba_attention",
    )(slopes, p3, p3, p3)


def _nsa_compress_kernel(x_ref, pos_ref, w1_ref, w2_ref, o_ref):
    x = (x_ref[0].astype(F32) + pos_ref[...]).astype(BF16)
    hmid = jnp.dot(x, w1_ref[...], preferred_element_type=F32)
    hmid = hmid / (1.0 + jnp.exp(-hmid))
    o_ref[0] = jnp.dot(hmid.astype(BF16), w2_ref[...], preferred_element_type=F32).astype(o_ref.dtype)


def _nsa_compress(blocks, pos, w1, w2):
    r, n, width = blocks.shape
    return pl.pallas_call(
        _nsa_compress_kernel,
        grid=(r,),
        in_specs=[pl.BlockSpec((1, n, width), lambda i: (i, 0, 0)),
                  pl.BlockSpec((1, width), lambda i: (0, 0)),
                  pl.BlockSpec((width, HEAD_DIM), lambda i: (0, 0)),
                  pl.BlockSpec((HEAD_DIM, HEAD_DIM), lambda i: (0, 0))],
        out_specs=pl.BlockSpec((1, n, HEAD_DIM), lambda i: (i, 0, 0)),
        out_shape=jax.ShapeDtypeStruct((r, n, HEAD_DIM), BF16),
        compiler_params=_cparams(("parallel",)),
        name="nsa_compress",
    )(blocks, pos.reshape(1, width).astype(F32), w1.astype(BF16), w2.astype(BF16))


def _nsa_select_kernel(slope_ref, q_ref, kc_ref, vc_ref, a_ref, oc_ref, sel_ref, *, tq, n_cmp, n_blk):
    g = pl.program_id(1)
    i = pl.program_id(2)
    t0 = i * tq
    kc = kc_ref[0, 0]
    vc = vc_ref[0, 0]
    ncp = kc.shape[0]
    tok = lax.broadcasted_iota(jnp.int32, (1, ncp), 1)
    cmp_end = tok * NSA_CMP_STRIDE + (NSA_CMP_LEN - 1)
    row = lax.broadcasted_iota(jnp.int32, (tq, 1), 0)
    admissible = (cmp_end <= row + t0) & (tok < n_cmp)
    rel_end = (cmp_end - t0).astype(F32)

    imp = jnp.zeros((tq, ncp), F32)
    for hh in range(NSA_Q_PER_KV):
        slope = slope_ref[g * NSA_Q_PER_KV + hh]
        q = q_ref[0, :, hh * HEAD_DIM:(hh + 1) * HEAD_DIM]
        s = jnp.where(admissible, _qk(q, kc) + slope * rel_end, NEG_BIG)
        m = jnp.max(s, axis=-1, keepdims=True)
        p = jnp.where(admissible, jnp.exp(s - m), 0.0)
        p = p / jnp.maximum(jnp.sum(p, axis=-1, keepdims=True), F32_TINY)
        oc_ref[0, :, hh * HEAD_DIM:(hh + 1) * HEAD_DIM] = jnp.dot(
            p.astype(BF16), vc, preferred_element_type=F32)
        imp = imp + p

    imp_blk = _dot_split(imp, a_ref[...], 3)
    blk = lax.broadcasted_iota(jnp.int32, (tq, LANES), 1)
    blkf = blk.astype(F32)
    jt = jnp.right_shift(row + t0, int(math.log2(NSA_SEL_BLOCK)))
    live = (blk <= jt) & (blk < n_blk)
    forced = ((blk == 0) | (blk == jt) | (blk == jt - 1)) & live
    sel = jnp.where(forced, 1.0, 0.0)
    score = jnp.where(live & jnp.logical_not(forced), imp_blk, -jnp.inf)
    sel = _topk_rounds(score, blkf, sel, NSA_SEL_TOPK - 3)
    sel_ref[0, 0] = sel.astype(sel_ref.dtype)


def _nsa_select(p3, kc, vc, slopes, tq):
    b, t, _ = p3.shape
    n_blk = t // NSA_SEL_BLOCK
    assert n_blk <= LANES and NSA_SEL_TOPK >= 3
    n_cmp = t // NSA_CMP_STRIDE - 1
    ncp = kc.shape[2]
    ratio = NSA_SEL_BLOCK // NSA_CMP_STRIDE
    ti = np.arange(ncp)[:, None]
    bj = np.arange(LANES)[None, :]
    amat = ((ti >= ratio * bj - 1) & (ti <= ratio * bj + ratio - 1) & (ti < n_cmp) & (bj < n_blk))
    amat = jnp.asarray(amat.astype(np.float32), BF16)
    width = NSA_Q_PER_KV * HEAD_DIM
    qblk = COL["nq"] * HEAD_DIM // width
    return pl.pallas_call(
        functools.partial(_nsa_select_kernel, tq=tq, n_cmp=n_cmp, n_blk=n_blk),
        grid=(b, NSA_KV_HEADS, t // tq),
        in_specs=[pl.BlockSpec(memory_space=pltpu.SMEM),
                  pl.BlockSpec((1, tq, width), lambda bi, g, i: (bi, i, qblk + g)),
                  pl.BlockSpec((1, 1, ncp, HEAD_DIM), lambda bi, g, i: (bi, g, 0, 0)),
                  pl.BlockSpec((1, 1, ncp, HEAD_DIM), lambda bi, g, i: (bi, g, 0, 0)),
                  pl.BlockSpec((ncp, LANES), lambda bi, g, i: (0, 0))],
        out_specs=[pl.BlockSpec((1, tq, width), lambda bi, g, i: (bi, i, g)),
                   pl.BlockSpec((1, 1, tq, LANES), lambda bi, g, i: (bi, g, i, 0))],
        out_shape=[jax.ShapeDtypeStruct((b, t, GROUP_WIDTH), F32),
                   jax.ShapeDtypeStruct((b, NSA_KV_HEADS, t, LANES), BF16)],
        compiler_params=_cparams(("parallel", "parallel", "parallel")),
        name="nsa_select",
    )(slopes, p3, kc, vc, amat)


def _nsa_attn_kernel(slope_ref, q_ref, ks_ref, vs_ref, kw_ref, vw_ref, sel_ref, e_ref, oc_ref, gate_ref,
                     o_ref, m_ref, l_ref, acc_ref, *, tq, tk):
    g = pl.program_id(1)
    i = pl.program_id(2)
    t0 = i * tq
    j_last = t0 // tk
    nh = NSA_Q_PER_KV
    m_ref[...] = jnp.full(m_ref.shape, NEG_BIG, F32)
    l_ref[...] = jnp.zeros(l_ref.shape, F32)
    acc_ref[...] = jnp.zeros(acc_ref.shape, F32)
    sel = sel_ref[0, 0]
    row = lax.broadcasted_iota(jnp.int32, (tq, 1), 0)
    col = lax.broadcasted_iota(jnp.int32, (1, tk), 1)

    def tile(j, diag, window):
        off = pl.multiple_of(j * tk, tk)
        rel = col + (j * tk - t0)
        relf = rel.astype(F32)
        chosen = jnp.dot(sel, e_ref[j], preferred_element_type=F32) > 0.5
        if diag:
            chosen = chosen & (rel <= row)
        k_s = ks_ref[0, pl.ds(off, tk), :]
        v_s = vs_ref[0, pl.ds(off, tk), :]
        if window:
            in_win = (row - rel) < NSA_WINDOW
            if diag:
                in_win = in_win & (rel <= row)
            k_w = kw_ref[0, pl.ds(off, tk), :]
            v_w = vw_ref[0, pl.ds(off, tk), :]
        for hh in range(nh):
            slope = slope_ref[g * nh + hh]
            q = q_ref[0, :, hh * HEAD_DIM:(hh + 1) * HEAD_DIM]
            bias = slope * relf
            s = jnp.where(chosen, _qk(q, k_s) + bias, NEG_BIG)
            _online_update(s, v_s, m_ref.at[hh], l_ref.at[hh], acc_ref.at[hh])
            if window:
                s = jnp.where(in_win, _qk(q, k_w) + bias, NEG_BIG)
                _online_update(s, v_w, m_ref.at[nh + hh], l_ref.at[nh + hh], acc_ref.at[nh + hh])

    def body(j, c):
        tile(j, False, False)
        return c

    lax.fori_loop(0, jnp.maximum(j_last - 1, 0), body, 0)

    @pl.when(j_last >= 1)
    def _():
        tile(j_last - 1, False, True)

    tile(j_last, True, True)

    gates = gate_ref[0, 0]
    gates = 1.0 / (1.0 + jnp.exp(-gates))
    for hh in range(nh):
        c0 = hh * NSA_N_BRANCHES
        o_c = oc_ref[0, :, hh * HEAD_DIM:(hh + 1) * HEAD_DIM]
        o_s = acc_ref[hh] / l_ref[hh]
        o_w = acc_ref[nh + hh] / l_ref[nh + hh]
        o_ref[0, :, hh * HEAD_DIM:(hh + 1) * HEAD_DIM] = (
            gates[:, c0:c0 + 1] * o_c + gates[:, c0 + 1:c0 + 2] * o_s + gates[:, c0 + 2:c0 + 3] * o_w)


def _nsa_attention(p3, sel, o_cmp, gate_logits, slopes, tq, tk):
    b, t, _ = p3.shape
    assert tk % tq == 0 and tk >= NSA_WINDOW and tk % NSA_SEL_BLOCK == 0
    nk = t // tk
    pos = np.arange(t).reshape(nk, 1, tk) // NSA_SEL_BLOCK
    emat = jnp.asarray((np.arange(LANES).reshape(1, LANES, 1) == pos).astype(np.float32), BF16)
    width = NSA_Q_PER_KV * HEAD_DIM
    qblk = COL["nq"] * HEAD_DIM // width
    kv = lambda name: pl.BlockSpec((1, t, HEAD_DIM), lambda bi, g, i: (bi, 0, COL[name] + g))
    nst = 2 * NSA_Q_PER_KV
    return pl.pallas_call(
        functools.partial(_nsa_attn_kernel, tq=tq, tk=tk),
        grid=(b, NSA_KV_HEADS, t // tq),
        in_specs=[pl.BlockSpec(memory_space=pltpu.SMEM),
                  pl.BlockSpec((1, tq, width), lambda bi, g, i: (bi, i, qblk + g)),
                  kv("nks"), kv("nvs"), kv("nkw"), kv("nvw"),
                  pl.BlockSpec((1, 1, tq, LANES), lambda bi, g, i: (bi, g, i, 0)),
                  pl.BlockSpec((nk, LANES, tk), lambda bi, g, i: (0, 0, 0)),
                  pl.BlockSpec((1, tq, width), lambda bi, g, i: (bi, i, g)),
                  pl.BlockSpec((1, 1, tq, LANES), lambda bi, g, i: (bi, g, i, 0))],
        out_specs=pl.BlockSpec((1, tq, width), lambda bi, g, i: (bi, i, g)),
        out_shape=jax.ShapeDtypeStruct((b, t, GROUP_WIDTH), F32),
        scratch_shapes=[pltpu.VMEM((nst, tq, 1), F32), pltpu.VMEM((nst, tq, 1), F32),
                        pltpu.VMEM((nst, tq, HEAD_DIM), F32)],
        compiler_params=_cparams(("parallel", "parallel", "arbitrary")),
        name="nsa_attention",
    )(slopes, p3, p3, p3, p3, p3, sel, emat, o_cmp, gate_logits)


def _nsa_blocks(p3, name):
    b, t, _ = p3.shape
    c0 = COL[name] * HEAD_DIM
    x = p3[:, :, c0:c0 + NSA_KV_HEADS * HEAD_DIM].reshape(b, t // NSA_CMP_STRIDE, NSA_CMP_STRIDE,
                                                         NSA_KV_HEADS, HEAD_DIM)
    x = x.transpose(0, 3, 1, 2, 4).reshape(b * NSA_KV_HEADS, t // NSA_CMP_STRIDE, NSA_CMP_STRIDE * HEAD_DIM)
    blocks = jnp.concatenate([x[:, :-1], x[:, 1:]], axis=-1)
    return jnp.pad(blocks, ((0, 0), (0, 1), (0, 0)))


def _alibi_slopes():
    n = 2 * GROUP_HEADS
    slopes = 2.0 ** (-8.0 * np.arange(1, n + 1) / n)
    return jnp.asarray(slopes[0::2], F32), jnp.asarray(slopes[1::2], F32)


def _regroup_w_in(w_in):
    main = jnp.concatenate([w_in[:, REF_OFF[n][0]:REF_OFF[n][0] + REF_OFF[n][1]] for n, _ in _MAIN_PIECES],
                           axis=1).astype(BF16)
    ng0, ngw = REF_OFF["ng"]
    ff0, ffw = REF_OFF["ff"]
    small = jnp.concatenate([w_in[:, ng0:ng0 + ngw], w_in[:, ff0:ff0 + ffw]], axis=1)
    small = jnp.pad(small, ((0, 0), (0, LANES - ngw - ffw))).astype(BF16)
    return main, small


def _mixer(x2d, b, t, norm_gain, w_in, forget_bias, pos_k, w1_k, w2_k, pos_v, w1_v, w2_v, group_gain, w_out):
    m = x2d.shape[0]
    h = _rmsnorm(x2d, norm_gain, BF16)
    w_main, w_small = _regroup_w_in(w_in)
    tm = _pick_tile(m, (1024, 512, 256))
    p3 = _matmul(h, w_main, BF16, tm, 512).reshape(b, t, MAIN_BLOCKS * HEAD_DIM)
    small = _matmul(h, w_small, F32, tm, LANES).reshape(b, t, LANES)
    slopes_moba, slopes_nsa = _alibi_slopes()

    o_a = _moba_attention(p3, slopes_moba)

    n_gate = NSA_N_BRANCHES * GROUP_HEADS
    kc = _nsa_compress(_nsa_blocks(p3, "nkc"), pos_k, w1_k, w2_k)
    vc = _nsa_compress(_nsa_blocks(p3, "nvc"), pos_v, w1_v, w2_v)
    ncp = kc.shape[1]
    kc = kc.reshape(b, NSA_KV_HEADS, ncp, HEAD_DIM)
    vc = vc.reshape(b, NSA_KV_HEADS, ncp, HEAD_DIM)
    tq_n = _pick_tile(t, (256, 128))
    o_cmp, sel = _nsa_select(p3, kc, vc, slopes_nsa, tq_n)
    per_group = n_gate // NSA_KV_HEADS
    gate_logits = small[:, :, :n_gate].reshape(b, t, NSA_KV_HEADS, per_group).transpose(0, 2, 1, 3)
    gate_logits = jnp.pad(gate_logits, ((0, 0), (0, 0), (0, 0), (0, LANES - per_group)))
    o_b = _nsa_attention(p3, sel, o_cmp, gate_logits, slopes_nsa, tq_n, NSA_WINDOW)

    ff = small[:, :, n_gate:n_gate + GROUP_HEADS] + forget_bias
    neg_cum_f = _neg_cum_logf(ff.transpose(0, 2, 1).reshape(b * GROUP_HEADS, t))
    tq_d = _pick_tile(t, (512, 256, 128))
    o_c = _fox_attention(p3, neg_cum_f, tq_d)

    o_d = _stick_attention(p3, _pick_tile(t, (256, 128)))

    on = _groupnorm([o.reshape(m, GROUP_WIDTH) for o in (o_a, o_b, o_c, o_d)], group_gain)
    return _matmul_residual(on, w_out.astype(BF16), x2d, tm, 512)


def _pad_ffn(wg, wu, wd):
    f = wg.shape[1]
    fp = -(-f // 512) * 512
    return (jnp.pad(wg, ((0, 0), (0, fp - f))).astype(BF16), jnp.pad(wu, ((0, 0), (0, fp - f))).astype(BF16),
            jnp.pad(wd, ((0, fp - f), (0, 0))).astype(BF16))


def kernel(x, ffn1_norm, ffn1_w_gate, ffn1_w_up, ffn1_w_down, mix_norm, w_in, fox_forget_bias, nsa_cmp_pos_k, nsa_cmp_w1_k, nsa_cmp_w2_k, nsa_cmp_pos_v, nsa_cmp_w1_v, nsa_cmp_w2_v, group_norm, w_out, ffn2_norm, ffn2_w_gate, ffn2_w_up, ffn2_w_down, final_norm):
    b, t, d = x.shape
    x2d = x.reshape(b * t, d)
    for i in range(ffn1_norm.shape[0]):
        x2d = _ffn(x2d, ffn1_norm[i], *_pad_ffn(ffn1_w_gate[i], ffn1_w_up[i], ffn1_w_down[i]))
        x2d = _mixer(x2d, b, t, mix_norm[i], w_in[i], fox_forget_bias[i], nsa_cmp_pos_k[i], nsa_cmp_w1_k[i],
                     nsa_cmp_w2_k[i], nsa_cmp_pos_v[i], nsa_cmp_w1_v[i], nsa_cmp_w2_v[i],
                     group_norm[i], w_out[i])
        x2d = _ffn(x2d, ffn2_norm[i], *_pad_ffn(ffn2_w_gate[i], ffn2_w_up[i], ffn2_w_down[i]))
    return _rmsnorm(x2d, final_norm, F32).reshape(b, t, d)
```

```python
import functools
import math

import numpy as np
import jax
import jax.numpy as jnp
from jax import lax
from jax.experimental import pallas as pl
from jax.experimental.pallas import tpu as pltpu

F32 = jnp.float32
BF16 = jnp.bfloat16

HEAD_DIM = 128
GROUP_HEADS = 8
GROUP_WIDTH = GROUP_HEADS * HEAD_DIM
N_MIXERS = 4
FFN_RESIDUAL = 0.5
RMS_EPS = 1e-6

MOBA_BLOCK = 256
MOBA_TOPK = 3

NSA_KV_HEADS = 2
NSA_Q_PER_KV = 4
NSA_CMP_STRIDE = 16
NSA_CMP_LEN = 32
NSA_SEL_BLOCK = 64
NSA_SEL_TOPK = 16
NSA_WINDOW = 512
NSA_N_BRANCHES = 3

ATTN_SCALE = HEAD_DIM ** -0.5
LOG2E = math.log2(math.e)
SCALE_LOG2E = ATTN_SCALE * LOG2E
NEG_BIG = -1e30
KV_UNIT = 256
KV_WIDE = 1024
Q_ROWS = 256
Q_CHAINS = 2
F32_TINY = float(np.finfo(np.float32).tiny)

LANES = 128
VMEM_LIMIT = 56 * 1024 * 1024

_MAIN_PIECES = (
    ("mq", 8), ("mk", 8), ("mv", 8), ("nq", 8),
    ("nkc", 2), ("nvc", 2), ("nks", 2), ("nvs", 2), ("nkw", 2), ("nvw", 2),
    ("fq", 8), ("fk", 8), ("fv", 8), ("sq", 8), ("sk", 8), ("sv", 8),
)
COL = {}
_o = 0
for _n, _w in _MAIN_PIECES:
    COL[_n] = _o
    _o += _w
MAIN_BLOCKS = _o

_REF_SIZES = (
    ("mq", 1024), ("mk", 1024), ("mv", 1024), ("nq", 1024),
    ("nkc", 256), ("nvc", 256), ("nks", 256), ("nvs", 256), ("nkw", 256), ("nvw", 256),
    ("ng", 24),
    ("fq", 1024), ("fk", 1024), ("fv", 1024), ("ff", 8),
    ("sq", 1024), ("sk", 1024), ("sv", 1024),
)
REF_OFF = {}
_o = 0
for _n, _w in _REF_SIZES:
    REF_OFF[_n] = (_o, _w)
    _o += _w


def _cparams(sem):
    return pltpu.CompilerParams(dimension_semantics=sem, vmem_limit_bytes=VMEM_LIMIT)


def _rms_kernel(x_ref, g_ref, o_ref):
    x = x_ref[...]
    ms = jnp.mean(x * x, axis=-1, keepdims=True)
    o_ref[...] = (x * lax.rsqrt(ms + RMS_EPS) * g_ref[...]).astype(o_ref.dtype)


def _rmsnorm(x2d, gain, out_dtype, tm=256):
    m, d = x2d.shape
    return pl.pallas_call(
        _rms_kernel,
        grid=(m // tm,),
        in_specs=[pl.BlockSpec((tm, d), lambda i: (i, 0)),
                  pl.BlockSpec((1, d), lambda i: (0, 0))],
        out_specs=pl.BlockSpec((tm, d), lambda i: (i, 0)),
        out_shape=jax.ShapeDtypeStruct((m, d), out_dtype),
        compiler_params=_cparams(("parallel",)),
        name="rmsnorm",
    )(x2d, gain.reshape(1, d).astype(F32))


def _groupnorm_kernel(a_ref, b_ref, c_ref, d_ref, g_ref, o_ref):
    for gi, ref in enumerate((a_ref, b_ref, c_ref, d_ref)):
        x = ref[...]
        ms = jnp.mean(x * x, axis=-1, keepdims=True)
        lo, hi = gi * GROUP_WIDTH, (gi + 1) * GROUP_WIDTH
        o_ref[:, lo:hi] = (x * lax.rsqrt(ms + RMS_EPS) * g_ref[:, lo:hi]).astype(o_ref.dtype)


def _groupnorm(parts, gain, tm=256):
    m = parts[0].shape[0]
    d = GROUP_WIDTH * N_MIXERS
    part_spec = pl.BlockSpec((tm, GROUP_WIDTH), lambda i: (i, 0))
    return pl.pallas_call(
        _groupnorm_kernel,
        grid=(m // tm,),
        in_specs=[part_spec] * 4 + [pl.BlockSpec((1, d), lambda i: (0, 0))],
        out_specs=pl.BlockSpec((tm, d), lambda i: (i, 0)),
        out_shape=jax.ShapeDtypeStruct((m, d), BF16),
        compiler_params=_cparams(("parallel",)),
        name="groupnorm",
    )(*parts, gain.reshape(1, d).astype(F32))


def _mm_kernel(a_ref, w_ref, o_ref):
    o_ref[...] = jnp.dot(a_ref[...], w_ref[...], preferred_element_type=F32).astype(o_ref.dtype)


def _matmul(a, w, out_dtype, tm, tn):
    m, k = a.shape
    n = w.shape[1]
    return pl.pallas_call(
        _mm_kernel,
        grid=(m // tm, n // tn),
        in_specs=[pl.BlockSpec((tm, k), lambda i, j: (i, 0)),
                  pl.BlockSpec((k, tn), lambda i, j: (0, j))],
        out_specs=pl.BlockSpec((tm, tn), lambda i, j: (i, j)),
        out_shape=jax.ShapeDtypeStruct((m, n), out_dtype),
        compiler_params=_cparams(("parallel", "arbitrary")),
        name="matmul",
    )(a, w)


def _mm_res_kernel(a_ref, w_ref, x_ref, o_ref):
    o_ref[...] = x_ref[...] + jnp.dot(a_ref[...], w_ref[...], preferred_element_type=F32)


def _matmul_residual(a, w, x, tm, tn):
    m, k = a.shape
    n = w.shape[1]
    return pl.pallas_call(
        _mm_res_kernel,
        grid=(m // tm, n // tn),
        in_specs=[pl.BlockSpec((tm, k), lambda i, j: (i, 0)),
                  pl.BlockSpec((k, tn), lambda i, j: (0, j)),
                  pl.BlockSpec((tm, tn), lambda i, j: (i, j))],
        out_specs=pl.BlockSpec((tm, tn), lambda i, j: (i, j)),
        out_shape=jax.ShapeDtypeStruct((m, n), F32),
        compiler_params=_cparams(("parallel", "arbitrary")),
        name="matmul_residual",
    )(a, w, x)


def _gateup_kernel(h_ref, wg_ref, wu_ref, o_ref):
    h = h_ref[...]
    g = jnp.dot(h, wg_ref[...], preferred_element_type=F32)
    u = jnp.dot(h, wu_ref[...], preferred_element_type=F32)
    o_ref[...] = (g / (1.0 + jnp.exp(-g)) * u).astype(o_ref.dtype)


def _gateup(h, wg, wu, tm, tf):
    m, d = h.shape
    f = wg.shape[1]
    return pl.pallas_call(
        _gateup_kernel,
        grid=(m // tm, f // tf),
        in_specs=[pl.BlockSpec((tm, d), lambda i, j: (i, 0)),
                  pl.BlockSpec((d, tf), lambda i, j: (0, j)),
                  pl.BlockSpec((d, tf), lambda i, j: (0, j))],
        out_specs=pl.BlockSpec((tm, tf), lambda i, j: (i, j)),
        out_shape=jax.ShapeDtypeStruct((m, f), BF16),
        compiler_params=_cparams(("parallel", "arbitrary")),
        name="ffn_gateup",
    )(h, wg, wu)


def _down_kernel(a_ref, w_ref, x_ref, o_ref):
    part = FFN_RESIDUAL * jnp.dot(a_ref[...], w_ref[...], preferred_element_type=F32)

    @pl.when(pl.program_id(2) == 0)
    def _():
        o_ref[...] = x_ref[...] + part

    @pl.when(pl.program_id(2) != 0)
    def _():
        o_ref[...] += part


def _down(act, wd, x, tm, tn, tk):
    m, f = act.shape
    n = wd.shape[1]
    return pl.pallas_call(
        _down_kernel,
        grid=(m // tm, n // tn, f // tk),
        in_specs=[pl.BlockSpec((tm, tk), lambda i, j, k: (i, k)),
                  pl.BlockSpec((tk, tn), lambda i, j, k: (k, j)),
                  pl.BlockSpec((tm, tn), lambda i, j, k: (i, j))],
        out_specs=pl.BlockSpec((tm, tn), lambda i, j, k: (i, j)),
        out_shape=jax.ShapeDtypeStruct((m, n), F32),
        compiler_params=_cparams(("parallel", "parallel", "arbitrary")),
        name="ffn_down",
    )(act, wd, x)


def _pick_tile(n, prefs):
    for t in prefs:
        if n % t == 0:
            return t
    return n


def _ffn(x2d, gain, wg, wu, wd):
    m = x2d.shape[0]
    h = _rmsnorm(x2d, gain, BF16)
    tm = _pick_tile(m, (1024, 512, 256))
    act = _gateup(h, wg, wu, tm, _pick_tile(wg.shape[1], (512, 256, 128)))
    return _down(act, wd, x2d, tm, _pick_tile(wd.shape[1], (1024, 512, 256, 128)),
                 _pick_tile(wd.shape[0], (2816, 1024, 512, 256, 128)))


def _qk_raw(q, k):
    return lax.dot_general(q, k, (((1,), (1,)), ((), ())), preferred_element_type=F32)


def _qk(q, k):
    return _qk_raw(q, k) * ATTN_SCALE


def _online_update_multi(raws, vs, states):
    m_prev = [m_ref[...] for m_ref, _, _ in states]
    m_new = [jnp.maximum(mp, jnp.max(raw, axis=-1, keepdims=True)) for mp, raw in zip(m_prev, raws)]
    ps = [jnp.exp2((raw - mn) * SCALE_LOG2E) for raw, mn in zip(raws, m_new)]
    alphas = [jnp.exp2((mp - mn) * SCALE_LOG2E) for mp, mn in zip(m_prev, m_new)]
    pvs = [jnp.dot(p.astype(BF16), v, preferred_element_type=F32) for p, v in zip(ps, vs)]
    for (m_ref, l_ref, acc_ref), mn, a, p, pv in zip(states, m_new, alphas, ps, pvs):
        l_ref[...] = a * l_ref[...] + jnp.sum(p, axis=-1, keepdims=True)
        acc_ref[...] = a * acc_ref[...] + pv
        m_ref[...] = mn


def _flash_init(m_ref, l_ref, acc_ref):
    m_ref[...] = jnp.full(m_ref.shape, NEG_BIG, F32)
    l_ref[...] = jnp.zeros(l_ref.shape, F32)
    acc_ref[...] = jnp.zeros(acc_ref.shape, F32)


def _sweep_tiles(t0, wide, tile, reverse=False):
    n_full = t0 // wide
    tail_off = pl.multiple_of(n_full * wide, wide)

    def full(jj, c):
        tile(pl.multiple_of((n_full - 1 - jj if reverse else jj) * wide, wide), False)
        return c

    if reverse:
        tile(tail_off, True)
        lax.fori_loop(0, n_full, full, 0)
    else:
        lax.fori_loop(0, n_full, full, 0)
        tile(tail_off, True)


def _fill_key_aug(kaug_ref, k_ref, aux_fn):
    t = kaug_ref.shape[0]

    def body(c, carry):
        off = pl.multiple_of(c * KV_UNIT, KV_UNIT)
        pos = lax.broadcasted_iota(jnp.int32, (KV_UNIT, LANES), 0) + off
        lane = lax.broadcasted_iota(jnp.int32, (KV_UNIT, LANES), 1)
        kaug_ref[pl.ds(off, KV_UNIT), 0:HEAD_DIM] = k_ref[0, pl.ds(off, KV_UNIT), :]
        kaug_ref[pl.ds(off, KV_UNIT), HEAD_DIM:] = aux_fn(pos, lane).astype(BF16)
        return carry

    lax.fori_loop(0, t // KV_UNIT, body, 0)


def _topk_rounds(scores, blkf, sels, rounds):
    for _ in range(rounds):
        mx = [jnp.max(s, axis=-1, keepdims=True) for s in scores]
        idx = [jnp.min(jnp.where(s == m, blkf, 1e9), axis=-1, keepdims=True) for s, m in zip(scores, mx)]
        picks = [(blkf == i) & (m > -jnp.inf) for i, m in zip(idx, mx)]
        sels = [jnp.where(p, 1.0, sel) for p, sel in zip(picks, sels)]
        scores = [jnp.where(p, -jnp.inf, s) for p, s in zip(picks, scores)]
    return sels


def _split3(x):
    hi = x.astype(BF16)
    r1 = x - hi.astype(F32)
    mid = r1.astype(BF16)
    lo = (r1 - mid.astype(F32)).astype(BF16)
    return hi, mid, lo


def _dot_split(x, w_bf16, terms):
    parts = _split3(x)[:terms]
    out = jnp.dot(parts[0], w_bf16, preferred_element_type=F32)
    for p in parts[1:]:
        out = out + jnp.dot(p, w_bf16, preferred_element_type=F32)
    return out


def _logf_cumsum_kernel(x_ref, tri_ref, low_ref, o_ref):
    x = x_ref[0]
    logf = jnp.minimum(x, 0.0) - jnp.log1p(jnp.exp(-jnp.abs(x)))
    within = _dot_split(logf, tri_ref[...], 3)
    hi, mid, lo = _split3(logf)
    low = low_ref[...]
    before = (jnp.dot(low, hi, preferred_element_type=F32) + jnp.dot(low, mid, preferred_element_type=F32)
              + jnp.dot(low, lo, preferred_element_type=F32))
    o_ref[0] = -(within + jnp.sum(before, axis=-1, keepdims=True))


def _neg_cum_logf(logits_rows):
    r, t = logits_rows.shape
    nc = t // LANES
    li = np.arange(LANES)
    tri = jnp.asarray((li[:, None] <= li[None, :]).astype(np.float32), BF16)
    ci = np.arange(nc)
    low = jnp.asarray((ci[None, :] < ci[:, None]).astype(np.float32), BF16)
    out = pl.pallas_call(
        _logf_cumsum_kernel,
        grid=(r,),
        in_specs=[pl.BlockSpec((1, nc, LANES), lambda i: (i, 0, 0)),
                  pl.BlockSpec((LANES, LANES), lambda i: (0, 0)),
                  pl.BlockSpec((nc, nc), lambda i: (0, 0))],
        out_specs=pl.BlockSpec((1, nc, LANES), lambda i: (i, 0, 0)),
        out_shape=jax.ShapeDtypeStruct((r, nc, LANES), F32),
        compiler_params=_cparams(("parallel",)),
        name="fox_logf_cumsum",
    )(logits_rows.reshape(r, nc, LANES), tri, low)
    return out.reshape(r, t)


def _chain_rows(c):
    return slice(c * Q_ROWS, (c + 1) * Q_ROWS)


def _causal_masks(off, wide, t0, strict=False):
    col = lax.broadcasted_iota(jnp.int32, (1, wide), 1) + off
    rows = [lax.broadcasted_iota(jnp.int32, (Q_ROWS, 1), 0) + (t0 + c * Q_ROWS) for c in range(Q_CHAINS)]
    return [(col < r) if strict else (col <= r) for r in rows]


def _fox_kernel(q_ref, k_ref, v_ref, nf_ref, o_ref, m_ref, l_ref, acc_ref, *, wide):
    t0 = pl.program_id(2) * (Q_CHAINS * Q_ROWS)
    qs = [q_ref[0, _chain_rows(c), :] for c in range(Q_CHAINS)]
    states = [(m_ref.at[c], l_ref.at[c], acc_ref.at[c]) for c in range(Q_CHAINS)]
    _flash_init(m_ref, l_ref, acc_ref)

    def tile(off, tail):
        bias = nf_ref[0, 0, pl.ds(off // wide, 1), :] * (1.0 / ATTN_SCALE)
        k = k_ref[0, pl.ds(off, wide), :]
        raws = [_qk_raw(q, k) + bias for q in qs]
        if tail:
            raws = [jnp.where(ok, raw, NEG_BIG) for ok, raw in zip(_causal_masks(off, wide, t0), raws)]
        _online_update_multi(raws, [v_ref[0, pl.ds(off, wide), :]] * Q_CHAINS, states)

    _sweep_tiles(t0, wide, tile)
    for c in range(Q_CHAINS):
        o_ref[0, _chain_rows(c), :] = acc_ref[c] / l_ref[c]


def _attn_scratch():
    return [pltpu.VMEM((Q_CHAINS, Q_ROWS, 1), F32), pltpu.VMEM((Q_CHAINS, Q_ROWS, 1), F32),
            pltpu.VMEM((Q_CHAINS, Q_ROWS, HEAD_DIM), F32)]


def _fox_attention(p3, neg_cum_f):
    b, t, _ = p3.shape
    tq = Q_CHAINS * Q_ROWS
    wide = min(KV_WIDE, t)
    nf = neg_cum_f.reshape(b, GROUP_HEADS, t // wide, wide)
    return pl.pallas_call(
        functools.partial(_fox_kernel, wide=wide),
        grid=(b, GROUP_HEADS, t // tq),
        in_specs=[pl.BlockSpec((1, tq, HEAD_DIM), lambda bi, h, i: (bi, i, COL["fq"] + h)),
                  pl.BlockSpec((1, t, HEAD_DIM), lambda bi, h, i: (bi, 0, COL["fk"] + h)),
                  pl.BlockSpec((1, t, HEAD_DIM), lambda bi, h, i: (bi, 0, COL["fv"] + h)),
                  pl.BlockSpec((1, 1, t // wide, wide), lambda bi, h, i: (bi, h, 0, 0))],
        out_specs=pl.BlockSpec((1, tq, HEAD_DIM), lambda bi, h, i: (bi, i, h)),
        out_shape=jax.ShapeDtypeStruct((b, t, GROUP_WIDTH), F32),
        scratch_shapes=_attn_scratch(),
        compiler_params=_cparams(("parallel", "parallel", "arbitrary")),
        name="fox_attention",
    )(p3, p3, p3, nf)


def _stick_kernel(q_ref, k_ref, v_ref, u_ref, o_ref, r_ref, acc_ref, *, wide):
    t0 = pl.program_id(2) * (Q_CHAINS * Q_ROWS)
    chains = range(Q_CHAINS)
    qs = [q_ref[0, _chain_rows(c), :] for c in chains]
    r_ref[...] = jnp.zeros(r_ref.shape, F32)
    acc_ref[...] = jnp.zeros(acc_ref.shape, F32)
    u = u_ref[...]

    def tile(off, tail):
        k = k_ref[0, pl.ds(off, wide), :]
        z_all = [_qk_raw(q, k) * SCALE_LOG2E for q in qs]
        later = [r_ref[c] for c in chains]
        weights = [[] for _ in chains]
        for sb in reversed(range(wide // Q_ROWS)):
            cols = slice(sb * Q_ROWS, (sb + 1) * Q_ROWS)
            zs = [z[:, cols] for z in z_all]
            log_beta = [jnp.minimum(z, 0.0) - jnp.log(1.0 + jnp.exp2(-jnp.abs(z))) * LOG2E for z in zs]
            log_keep = [lb - z for lb, z in zip(log_beta, zs)]
            if tail:
                before = _causal_masks(off + sb * Q_ROWS, Q_ROWS, t0, strict=True)
                log_keep = [jnp.where(ok, lk, 0.0) for ok, lk in zip(before, log_keep)]
            inside = [_dot_split(lk, u, 2) for lk in log_keep]
            a = [jnp.exp2(lb + cs + lt) for lb, cs, lt in zip(log_beta, inside, later)]
            if tail:
                a = [jnp.where(ok, x, 0.0) for ok, x in zip(before, a)]
            for c in chains:
                weights[c].append(a[c].astype(BF16))
            later = [lt + jnp.sum(lk, axis=-1, keepdims=True) for lt, lk in zip(later, log_keep)]
        v = v_ref[0, pl.ds(off, wide), :]
        for c in chains:
            acc_ref[c] += jnp.dot(jnp.concatenate(weights[c][::-1], axis=1), v, preferred_element_type=F32)
            r_ref[c] = later[c]

    _sweep_tiles(t0, wide, tile, reverse=True)
    for c in chains:
        o_ref[0, _chain_rows(c), :] = acc_ref[c]


def _stick_attention(p3):
    b, t, _ = p3.shape
    tq = Q_CHAINS * Q_ROWS
    wide = min(KV_WIDE, t)
    ki = np.arange(Q_ROWS)
    u = jnp.asarray((ki[:, None] > ki[None, :]).astype(np.float32), BF16)
    return pl.pallas_call(
        functools.partial(_stick_kernel, wide=wide),
        grid=(b, GROUP_HEADS, t // tq),
        in_specs=[pl.BlockSpec((1, tq, HEAD_DIM), lambda bi, h, i: (bi, i, COL["sq"] + h)),
                  pl.BlockSpec((1, t, HEAD_DIM), lambda bi, h, i: (bi, 0, COL["sk"] + h)),
                  pl.BlockSpec((1, t, HEAD_DIM), lambda bi, h, i: (bi, 0, COL["sv"] + h)),
                  pl.BlockSpec((Q_ROWS, Q_ROWS), lambda bi, h, i: (0, 0))],
        out_specs=pl.BlockSpec((1, tq, HEAD_DIM), lambda bi, h, i: (bi, i, h)),
        out_shape=jax.ShapeDtypeStruct((b, t, GROUP_WIDTH), F32),
        scratch_shapes=[pltpu.VMEM((Q_CHAINS, Q_ROWS, 1), F32), pltpu.VMEM((Q_CHAINS, Q_ROWS, HEAD_DIM), F32)],
        compiler_params=_cparams(("parallel", "parallel", "arbitrary")),
        name="stick_attention",
    )(p3, p3, p3, u)


def _moba_kernel(slope_ref, q_ref, k_ref, v_ref, o_ref, kaug_ref, kmean_ref, m_ref, l_ref, acc_ref, *,
                 n_blk, wide):
    h = pl.program_id(1)
    i = pl.program_id(2)
    tq = MOBA_BLOCK
    slope = slope_ref[h]

    @pl.when(i == 0)
    def _():
        kmean_ref[...] = jnp.zeros(kmean_ref.shape, F32)
        kb = k_ref[0].astype(F32).reshape(n_blk, tq, HEAD_DIM)
        kmean_ref[0:n_blk, :] = jnp.mean(kb, axis=1)

        def key_aux(pos, lane):
            onehot = jnp.where(lane == jnp.right_shift(pos, int(math.log2(tq))), 1.0, 0.0)
            hi, mid, lo = _split3(slope * pos.astype(F32) * (1.0 / ATTN_SCALE))
            return jnp.where(lane == n_blk, hi.astype(F32),
                             jnp.where(lane == n_blk + 1, mid.astype(F32),
                                       jnp.where(lane == n_blk + 2, lo.astype(F32), onehot)))

        _fill_key_aug(kaug_ref, k_ref, key_aux)

    chains = range(Q_CHAINS)
    t0 = i * (Q_CHAINS * Q_ROWS)
    qs = [q_ref[0, _chain_rows(c), :] for c in chains]
    own = [i * Q_CHAINS + c for c in chains]
    kmean = kmean_ref[...].astype(BF16)
    blk = lax.broadcasted_iota(jnp.int32, (Q_ROWS, LANES), 1)
    blkf = blk.astype(F32)
    scores = [jnp.where(blk < own[c], _qk_raw(qs[c], kmean), -jnp.inf) for c in chains]
    sels = _topk_rounds(scores, blkf, [jnp.where(blk == own[c], 1.0, 0.0) for c in chains], MOBA_TOPK)
    q_aux = [jnp.where(blk < n_blk, jnp.where(sel > 0.5, 0.0, NEG_BIG), jnp.where(blk < n_blk + 3, 1.0, 0.0))
             for sel in sels]
    q_aug = [jnp.concatenate([q, aux.astype(BF16)], axis=1) for q, aux in zip(qs, q_aux)]
    states = [(m_ref.at[c], l_ref.at[c], acc_ref.at[c]) for c in chains]
    _flash_init(m_ref, l_ref, acc_ref)

    def tile(off, tail):
        k = kaug_ref[pl.ds(off, wide), :]
        raws = [_qk_raw(q, k) for q in q_aug]
        if tail:
            raws = [jnp.where(ok, raw, NEG_BIG) for ok, raw in zip(_causal_masks(off, wide, t0), raws)]
        _online_update_multi(raws, [v_ref[0, pl.ds(off, wide), :]] * Q_CHAINS, states)

    _sweep_tiles(t0, wide, tile)
    for c in chains:
        o_ref[0, _chain_rows(c), :] = acc_ref[c] / l_ref[c]


def _moba_attention(p3, slopes):
    b, t, _ = p3.shape
    tq = Q_CHAINS * Q_ROWS
    n_blk = t // MOBA_BLOCK
    wide = min(KV_WIDE, t)
    assert Q_ROWS == MOBA_BLOCK and t % tq == 0 and n_blk + 3 <= LANES
    return pl.pallas_call(
        functools.partial(_moba_kernel, n_blk=n_blk, wide=wide),
        grid=(b, GROUP_HEADS, t // tq),
        in_specs=[pl.BlockSpec(memory_space=pltpu.SMEM),
                  pl.BlockSpec((1, tq, HEAD_DIM), lambda bi, h, i: (bi, i, COL["mq"] + h)),
                  pl.BlockSpec((1, t, HEAD_DIM), lambda bi, h, i: (bi, 0, COL["mk"] + h)),
                  pl.BlockSpec((1, t, HEAD_DIM), lambda bi, h, i: (bi, 0, COL["mv"] + h))],
        out_specs=pl.BlockSpec((1, tq, HEAD_DIM), lambda bi, h, i: (bi, i, h)),
        out_shape=jax.ShapeDtypeStruct((b, t, GROUP_WIDTH), F32),
        scratch_shapes=[pltpu.VMEM((t, HEAD_DIM + LANES), BF16),
                        pltpu.VMEM((LANES, HEAD_DIM), F32)] + _attn_scratch(),
        compiler_params=_cparams(("parallel", "parallel", "arbitrary")),
        name="moba_attention",
    )(slopes, p3, p3, p3)


def _nsa_compress_kernel(x_ref, pos_ref, w1_ref, w2_ref, o_ref):
    x = (x_ref[0].astype(F32) + pos_ref[...]).astype(BF16)
    hmid = jnp.dot(x, w1_ref[...], preferred_element_type=F32)
    hmid = hmid / (1.0 + jnp.exp(-hmid))
    o_ref[0] = jnp.dot(hmid.astype(BF16), w2_ref[...], preferred_element_type=F32).astype(o_ref.dtype)


def _nsa_compress(blocks, pos, w1, w2):
    r, n, width = blocks.shape
    return pl.pallas_call(
        _nsa_compress_kernel,
        grid=(r,),
        in_specs=[pl.BlockSpec((1, n, width), lambda i: (i, 0, 0)),
                  pl.BlockSpec((1, width), lambda i: (0, 0)),
                  pl.BlockSpec((width, HEAD_DIM), lambda i: (0, 0)),
                  pl.BlockSpec((HEAD_DIM, HEAD_DIM), lambda i: (0, 0))],
        out_specs=pl.BlockSpec((1, n, HEAD_DIM), lambda i: (i, 0, 0)),
        out_shape=jax.ShapeDtypeStruct((r, n, HEAD_DIM), BF16),
        compiler_params=_cparams(("parallel",)),
        name="nsa_compress",
    )(blocks, pos.reshape(1, width).astype(F32), w1.astype(BF16), w2.astype(BF16))


def _nsa_select_kernel(slope_ref, q_ref, kc_ref, vc_ref, a_ref, oc_ref, sel_ref, *, tq, n_cmp, n_blk):
    g = pl.program_id(1)
    i = pl.program_id(2)
    t0 = i * tq
    kc = kc_ref[0, 0]
    vc = vc_ref[0, 0]
    ncp = kc.shape[0]
    tok = lax.broadcasted_iota(jnp.int32, (1, ncp), 1)
    cmp_end = tok * NSA_CMP_STRIDE + (NSA_CMP_LEN - 1)
    row = lax.broadcasted_iota(jnp.int32, (tq, 1), 0)
    admissible = (cmp_end <= row + t0) & (tok < n_cmp)
    rel_end = (cmp_end - t0).astype(F32)

    imp = jnp.zeros((tq, ncp), F32)
    for hh in range(NSA_Q_PER_KV):
        slope = slope_ref[g * NSA_Q_PER_KV + hh]
        q = q_ref[0, :, hh * HEAD_DIM:(hh + 1) * HEAD_DIM]
        s = jnp.where(admissible, _qk(q, kc) + slope * rel_end, NEG_BIG)
        m = jnp.max(s, axis=-1, keepdims=True)
        p = jnp.where(admissible, jnp.exp(s - m), 0.0)
        p = p / jnp.maximum(jnp.sum(p, axis=-1, keepdims=True), F32_TINY)
        oc_ref[0, :, hh * HEAD_DIM:(hh + 1) * HEAD_DIM] = jnp.dot(
            p.astype(BF16), vc, preferred_element_type=F32)
        imp = imp + p

    imp_blk = _dot_split(imp, a_ref[...], 3)
    blk = lax.broadcasted_iota(jnp.int32, (tq, LANES), 1)
    blkf = blk.astype(F32)
    jt = jnp.right_shift(row + t0, int(math.log2(NSA_SEL_BLOCK)))
    live = (blk <= jt) & (blk < n_blk)
    forced = ((blk == 0) | (blk == jt) | (blk == jt - 1)) & live
    sel = jnp.where(forced, 1.0, 0.0)
    score = jnp.where(live & jnp.logical_not(forced), imp_blk, -jnp.inf)
    sel = _topk_rounds([score], blkf, [sel], NSA_SEL_TOPK - 3)[0]
    sel_ref[0, 0] = sel.astype(sel_ref.dtype)


def _nsa_select(p3, kc, vc, slopes, tq):
    b, t, _ = p3.shape
    n_blk = t // NSA_SEL_BLOCK
    assert n_blk <= LANES and NSA_SEL_TOPK >= 3
    n_cmp = t // NSA_CMP_STRIDE - 1
    ncp = kc.shape[2]
    ratio = NSA_SEL_BLOCK // NSA_CMP_STRIDE
    ti = np.arange(ncp)[:, None]
    bj = np.arange(LANES)[None, :]
    amat = ((ti >= ratio * bj - 1) & (ti <= ratio * bj + ratio - 1) & (ti < n_cmp) & (bj < n_blk))
    amat = jnp.asarray(amat.astype(np.float32), BF16)
    width = NSA_Q_PER_KV * HEAD_DIM
    qblk = COL["nq"] * HEAD_DIM // width
    return pl.pallas_call(
        functools.partial(_nsa_select_kernel, tq=tq, n_cmp=n_cmp, n_blk=n_blk),
        grid=(b, NSA_KV_HEADS, t // tq),
        in_specs=[pl.BlockSpec(memory_space=pltpu.SMEM),
                  pl.BlockSpec((1, tq, width), lambda bi, g, i: (bi, i, qblk + g)),
                  pl.BlockSpec((1, 1, ncp, HEAD_DIM), lambda bi, g, i: (bi, g, 0, 0)),
                  pl.BlockSpec((1, 1, ncp, HEAD_DIM), lambda bi, g, i: (bi, g, 0, 0)),
                  pl.BlockSpec((ncp, LANES), lambda bi, g, i: (0, 0))],
        out_specs=[pl.BlockSpec((1, tq, width), lambda bi, g, i: (bi, i, g)),
                   pl.BlockSpec((1, 1, tq, LANES), lambda bi, g, i: (bi, g, i, 0))],
        out_shape=[jax.ShapeDtypeStruct((b, t, GROUP_WIDTH), F32),
                   jax.ShapeDtypeStruct((b, NSA_KV_HEADS, t, LANES), BF16)],
        compiler_params=_cparams(("parallel", "parallel", "parallel")),
        name="nsa_select",
    )(slopes, p3, kc, vc, amat)


def _nsa_attn_kernel(slope_ref, q_ref, ks_ref, vs_ref, kw_ref, vw_ref, sel_ref, oc_ref, gate_ref,
                     o_ref, kaug_ref, m_ref, l_ref, acc_ref, *, tq, wide):
    g = pl.program_id(1)
    i = pl.program_id(2)
    t0 = pl.multiple_of(i * tq, tq)
    nh = NSA_Q_PER_KV

    @pl.when(i == 0)
    def _():
        sel_shift = int(math.log2(NSA_SEL_BLOCK))
        _fill_key_aug(kaug_ref, ks_ref,
                      lambda pos, lane: jnp.where(lane == jnp.right_shift(pos, sel_shift), 1.0, 0.0))

    _flash_init(m_ref, l_ref, acc_ref)
    sel_bias = jnp.where(sel_ref[0, 0].astype(F32) > 0.5, 0.0, NEG_BIG).astype(BF16)
    qs = [q_ref[0, :, hh * HEAD_DIM:(hh + 1) * HEAD_DIM] for hh in range(nh)]
    q_aug = [jnp.concatenate([q, sel_bias], axis=1) for q in qs]
    slope_raw = [slope_ref[g * nh + hh] * (1.0 / ATTN_SCALE) for hh in range(nh)]
    row = lax.broadcasted_iota(jnp.int32, (tq, 1), 0)

    def rel_pos(off, width):
        return lax.broadcasted_iota(jnp.int32, (1, width), 1) + (off - t0)

    def sel_tile(off, tail):
        rel = rel_pos(off, wide)
        relf = rel.astype(F32)
        k = kaug_ref[pl.ds(off, wide), :]
        v = vs_ref[0, pl.ds(off, wide), :]
        raws = [_qk_raw(q_aug[hh], k) + slope_raw[hh] * relf for hh in range(nh)]
        if tail:
            raws = [jnp.where(rel <= row, raw, NEG_BIG) for raw in raws]
        _online_update_multi(raws, [v] * nh, [(m_ref.at[hh], l_ref.at[hh], acc_ref.at[hh]) for hh in range(nh)])

    _sweep_tiles(t0, wide, sel_tile)

    def window(off, width, far):
        rel = rel_pos(off, width)
        relf = rel.astype(F32)
        k = kw_ref[0, pl.ds(off, width), :]
        v = vw_ref[0, pl.ds(off, width), :]
        ok = rel <= row
        if far:
            ok = ok & (row - rel < NSA_WINDOW)
        raws = [jnp.where(ok, _qk_raw(qs[hh], k) + slope_raw[hh] * relf, NEG_BIG) for hh in range(nh)]
        _online_update_multi(raws, [v] * nh,
                             [(m_ref.at[nh + hh], l_ref.at[nh + hh], acc_ref.at[nh + hh]) for hh in range(nh)])

    n_back = NSA_WINDOW // tq
    for back in range(n_back + 1):
        @pl.when((i == back) if back < n_back else (i >= back))
        def _(back=back):
            window(pl.multiple_of(t0 - back * tq, tq), (back + 1) * tq, far=(back == n_back))

    gates = gate_ref[0, 0]
    gates = 1.0 / (1.0 + jnp.exp(-gates))
    for hh in range(nh):
        c0 = hh * NSA_N_BRANCHES
        o_c = oc_ref[0, :, hh * HEAD_DIM:(hh + 1) * HEAD_DIM]
        o_s = acc_ref[hh] / l_ref[hh]
        o_w = acc_ref[nh + hh] / l_ref[nh + hh]
        o_ref[0, :, hh * HEAD_DIM:(hh + 1) * HEAD_DIM] = (
            gates[:, c0:c0 + 1] * o_c + gates[:, c0 + 1:c0 + 2] * o_s + gates[:, c0 + 2:c0 + 3] * o_w)


def _nsa_attention(p3, sel, o_cmp, gate_logits, slopes, tq):
    b, t, _ = p3.shape
    assert tq == KV_UNIT and NSA_WINDOW == 2 * tq and t // NSA_SEL_BLOCK <= LANES
    width = NSA_Q_PER_KV * HEAD_DIM
    qblk = COL["nq"] * HEAD_DIM // width
    kv = lambda name: pl.BlockSpec((1, t, HEAD_DIM), lambda bi, g, i: (bi, 0, COL[name] + g))
    nst = 2 * NSA_Q_PER_KV
    return pl.pallas_call(
        functools.partial(_nsa_attn_kernel, tq=tq, wide=min(KV_WIDE, t)),
        grid=(b, NSA_KV_HEADS, t // tq),
        in_specs=[pl.BlockSpec(memory_space=pltpu.SMEM),
                  pl.BlockSpec((1, tq, width), lambda bi, g, i: (bi, i, qblk + g)),
                  kv("nks"), kv("nvs"), kv("nkw"), kv("nvw"),
                  pl.BlockSpec((1, 1, tq, LANES), lambda bi, g, i: (bi, g, i, 0)),
                  pl.BlockSpec((1, tq, width), lambda bi, g, i: (bi, i, g)),
                  pl.BlockSpec((1, 1, tq, LANES), lambda bi, g, i: (bi, g, i, 0))],
        out_specs=pl.BlockSpec((1, tq, width), lambda bi, g, i: (bi, i, g)),
        out_shape=jax.ShapeDtypeStruct((b, t, GROUP_WIDTH), F32),
        scratch_shapes=[pltpu.VMEM((t, HEAD_DIM + LANES), BF16),
                        pltpu.VMEM((nst, tq, 1), F32), pltpu.VMEM((nst, tq, 1), F32),
                        pltpu.VMEM((nst, tq, HEAD_DIM), F32)],
        compiler_params=_cparams(("parallel", "parallel", "arbitrary")),
        name="nsa_attention",
    )(slopes, p3, p3, p3, p3, p3, sel, o_cmp, gate_logits)


def _nsa_blocks(p3, name):
    b, t, _ = p3.shape
    c0 = COL[name] * HEAD_DIM
    x = p3[:, :, c0:c0 + NSA_KV_HEADS * HEAD_DIM].reshape(b, t // NSA_CMP_STRIDE, NSA_CMP_STRIDE,
                                                         NSA_KV_HEADS, HEAD_DIM)
    x = x.transpose(0, 3, 1, 2, 4).reshape(b * NSA_KV_HEADS, t // NSA_CMP_STRIDE, NSA_CMP_STRIDE * HEAD_DIM)
    blocks = jnp.concatenate([x[:, :-1], x[:, 1:]], axis=-1)
    return jnp.pad(blocks, ((0, 0), (0, 1), (0, 0)))


def _alibi_slopes():
    n = 2 * GROUP_HEADS
    slopes = 2.0 ** (-8.0 * np.arange(1, n + 1) / n)
    return jnp.asarray(slopes[0::2], F32), jnp.asarray(slopes[1::2], F32)


def _regroup_w_in(w_in):
    main = jnp.concatenate([w_in[:, REF_OFF[n][0]:REF_OFF[n][0] + REF_OFF[n][1]] for n, _ in _MAIN_PIECES],
                           axis=1).astype(BF16)
    ng0, ngw = REF_OFF["ng"]
    ff0, ffw = REF_OFF["ff"]
    small = jnp.concatenate([w_in[:, ng0:ng0 + ngw], w_in[:, ff0:ff0 + ffw]], axis=1)
    small = jnp.pad(small, ((0, 0), (0, LANES - ngw - ffw))).astype(BF16)
    return main, small


def _mixer(x2d, b, t, norm_gain, w_in, forget_bias, pos_k, w1_k, w2_k, pos_v, w1_v, w2_v, group_gain, w_out):
    m = x2d.shape[0]
    h = _rmsnorm(x2d, norm_gain, BF16)
    w_main, w_small = _regroup_w_in(w_in)
    tm = _pick_tile(m, (1024, 512, 256))
    p3 = _matmul(h, w_main, BF16, tm, 512).reshape(b, t, MAIN_BLOCKS * HEAD_DIM)
    small = _matmul(h, w_small, F32, tm, LANES).reshape(b, t, LANES)
    slopes_moba, slopes_nsa = _alibi_slopes()

    o_a = _moba_attention(p3, slopes_moba)

    n_gate = NSA_N_BRANCHES * GROUP_HEADS
    kc = _nsa_compress(_nsa_blocks(p3, "nkc"), pos_k, w1_k, w2_k)
    vc = _nsa_compress(_nsa_blocks(p3, "nvc"), pos_v, w1_v, w2_v)
    ncp = kc.shape[1]
    kc = kc.reshape(b, NSA_KV_HEADS, ncp, HEAD_DIM)
    vc = vc.reshape(b, NSA_KV_HEADS, ncp, HEAD_DIM)
    o_cmp, sel = _nsa_select(p3, kc, vc, slopes_nsa, KV_UNIT)
    per_group = n_gate // NSA_KV_HEADS
    gate_logits = small[:, :, :n_gate].reshape(b, t, NSA_KV_HEADS, per_group).transpose(0, 2, 1, 3)
    gate_logits = jnp.pad(gate_logits, ((0, 0), (0, 0), (0, 0), (0, LANES - per_group)))
    o_b = _nsa_attention(p3, sel, o_cmp, gate_logits, slopes_nsa, KV_UNIT)

    ff = small[:, :, n_gate:n_gate + GROUP_HEADS] + forget_bias
    neg_cum_f = _neg_cum_logf(ff.transpose(0, 2, 1).reshape(b * GROUP_HEADS, t))
    o_c = _fox_attention(p3, neg_cum_f)

    o_d = _stick_attention(p3)

    on = _groupnorm([o.reshape(m, GROUP_WIDTH) for o in (o_a, o_b, o_c, o_d)], group_gain)
    return _matmul_residual(on, w_out.astype(BF16), x2d, tm, 512)


def _pad_ffn(wg, wu, wd):
    f = wg.shape[1]
    fp = -(-f // 512) * 512
    return (jnp.pad(wg, ((0, 0), (0, fp - f))).astype(BF16), jnp.pad(wu, ((0, 0), (0, fp - f))).astype(BF16),
            jnp.pad(wd, ((0, fp - f), (0, 0))).astype(BF16))


def kernel(x, ffn1_norm, ffn1_w_gate, ffn1_w_up, ffn1_w_down, mix_norm, w_in, fox_forget_bias, nsa_cmp_pos_k, nsa_cmp_w1_k, nsa_cmp_w2_k, nsa_cmp_pos_v, nsa_cmp_w1_v, nsa_cmp_w2_v, group_norm, w_out, ffn2_norm, ffn2_w_gate, ffn2_w_up, ffn2_w_down, final_norm):
    b, t, d = x.shape
    x2d = x.reshape(b * t, d)
    for i in range(ffn1_norm.shape[0]):
        x2d = _ffn(x2d, ffn1_norm[i], *_pad_ffn(ffn1_w_gate[i], ffn1_w_up[i], ffn1_w_down[i]))
        x2d = _mixer(x2d, b, t, mix_norm[i], w_in[i], fox_forget_bias[i], nsa_cmp_pos_k[i], nsa_cmp_w1_k[i],
                     nsa_cmp_w2_k[i], nsa_cmp_pos_v[i], nsa_cmp_w1_v[i], nsa_cmp_w2_v[i],
                     group_norm[i], w_out[i])
        x2d = _ffn(x2d, ffn2_norm[i], *_pad_ffn(ffn2_w_gate[i], ffn2_w_up[i], ffn2_w_down[i]))
    return _rmsnorm(x2d, final_norm, F32).reshape(b, t, d)
```

```python
import functools
import math

import numpy as np
import jax
import jax.numpy as jnp
from jax import lax
from jax.experimental import pallas as pl
from jax.experimental.pallas import tpu as pltpu

F32 = jnp.float32
BF16 = jnp.bfloat16

HEAD_DIM = 128
GROUP_HEADS = 8
GROUP_WIDTH = GROUP_HEADS * HEAD_DIM
N_MIXERS = 4
FFN_RESIDUAL = 0.5
RMS_EPS = 1e-6

MOBA_BLOCK = 256
MOBA_TOPK = 3

NSA_KV_HEADS = 2
NSA_Q_PER_KV = 4
NSA_CMP_STRIDE = 16
NSA_CMP_LEN = 32
NSA_SEL_BLOCK = 64
NSA_SEL_TOPK = 16
NSA_WINDOW = 512
NSA_N_BRANCHES = 3

ATTN_SCALE = HEAD_DIM ** -0.5
LOG2E = math.log2(math.e)
SCALE_LOG2E = ATTN_SCALE * LOG2E
NEG_BIG = -1e30
STICK_DEAD_LOG2 = 160.0
KV_UNIT = 256
KV_WIDE = 1024
Q_ROWS = 256
Q_CHAINS = 4
F32_TINY = float(np.finfo(np.float32).tiny)

LANES = 128
VMEM_LIMIT = 56 * 1024 * 1024
FFN_TILE = 512

_MAIN_PIECES = (
    ("mq", 8), ("mk", 8), ("mv", 8), ("nq", 8),
    ("nkc", 2), ("nvc", 2), ("nks", 2), ("nvs", 2), ("nkw", 2), ("nvw", 2),
    ("fq", 8), ("fk", 8), ("fv", 8), ("sq", 8), ("sk", 8), ("sv", 8),
)
COL = {}
_o = 0
for _n, _w in _MAIN_PIECES:
    COL[_n] = _o
    _o += _w
MAIN_BLOCKS = _o

_REF_SIZES = (
    ("mq", 1024), ("mk", 1024), ("mv", 1024), ("nq", 1024),
    ("nkc", 256), ("nvc", 256), ("nks", 256), ("nvs", 256), ("nkw", 256), ("nvw", 256),
    ("ng", 24),
    ("fq", 1024), ("fk", 1024), ("fv", 1024), ("ff", 8),
    ("sq", 1024), ("sk", 1024), ("sv", 1024),
)
REF_OFF = {}
_o = 0
for _n, _w in _REF_SIZES:
    REF_OFF[_n] = (_o, _w)
    _o += _w


def _cparams(sem):
    return pltpu.CompilerParams(dimension_semantics=sem, vmem_limit_bytes=VMEM_LIMIT)


def _rms_kernel(x_ref, g_ref, o_ref):
    x = x_ref[...]
    ms = jnp.mean(x * x, axis=-1, keepdims=True)
    o_ref[...] = (x * lax.rsqrt(ms + RMS_EPS) * g_ref[...]).astype(o_ref.dtype)


def _rmsnorm(x2d, gain, out_dtype, tm=256):
    m, d = x2d.shape
    return pl.pallas_call(
        _rms_kernel,
        grid=(m // tm,),
        in_specs=[pl.BlockSpec((tm, d), lambda i: (i, 0)),
                  pl.BlockSpec((1, d), lambda i: (0, 0))],
        out_specs=pl.BlockSpec((tm, d), lambda i: (i, 0)),
        out_shape=jax.ShapeDtypeStruct((m, d), out_dtype),
        compiler_params=_cparams(("parallel",)),
        name="rmsnorm",
    )(x2d, gain.reshape(1, d).astype(F32))


def _groupnorm_kernel(a_ref, b_ref, c_ref, d_ref, g_ref, o_ref):
    for gi, ref in enumerate((a_ref, b_ref, c_ref, d_ref)):
        x = ref[...]
        ms = jnp.mean(x * x, axis=-1, keepdims=True)
        lo, hi = gi * GROUP_WIDTH, (gi + 1) * GROUP_WIDTH
        o_ref[:, lo:hi] = (x * lax.rsqrt(ms + RMS_EPS) * g_ref[:, lo:hi]).astype(o_ref.dtype)


def _groupnorm(parts, gain, tm=256):
    m = parts[0].shape[0]
    d = GROUP_WIDTH * N_MIXERS
    part_spec = pl.BlockSpec((tm, GROUP_WIDTH), lambda i: (i, 0))
    return pl.pallas_call(
        _groupnorm_kernel,
        grid=(m // tm,),
        in_specs=[part_spec] * 4 + [pl.BlockSpec((1, d), lambda i: (0, 0))],
        out_specs=pl.BlockSpec((tm, d), lambda i: (i, 0)),
        out_shape=jax.ShapeDtypeStruct((m, d), BF16),
        compiler_params=_cparams(("parallel",)),
        name="groupnorm",
    )(*parts, gain.reshape(1, d).astype(F32))


def _mm_kernel(a_ref, w_ref, o_ref):
    o_ref[...] = jnp.dot(a_ref[...], w_ref[...], preferred_element_type=F32).astype(o_ref.dtype)


def _matmul(a, w, out_dtype, tm, tn):
    m, k = a.shape
    n = w.shape[1]
    return pl.pallas_call(
        _mm_kernel,
        grid=(m // tm, n // tn),
        in_specs=[pl.BlockSpec((tm, k), lambda i, j: (i, 0)),
                  pl.BlockSpec((k, tn), lambda i, j: (0, j))],
        out_specs=pl.BlockSpec((tm, tn), lambda i, j: (i, j)),
        out_shape=jax.ShapeDtypeStruct((m, n), out_dtype),
        compiler_params=_cparams(("parallel", "arbitrary")),
        name="matmul",
    )(a, w)


def _mm_res_kernel(a_ref, w_ref, x_ref, o_ref):
    o_ref[...] = x_ref[...] + jnp.dot(a_ref[...], w_ref[...], preferred_element_type=F32)


def _matmul_residual(a, w, x, tm, tn):
    m, k = a.shape
    n = w.shape[1]
    return pl.pallas_call(
        _mm_res_kernel,
        grid=(m // tm, n // tn),
        in_specs=[pl.BlockSpec((tm, k), lambda i, j: (i, 0)),
                  pl.BlockSpec((k, tn), lambda i, j: (0, j)),
                  pl.BlockSpec((tm, tn), lambda i, j: (i, j))],
        out_specs=pl.BlockSpec((tm, tn), lambda i, j: (i, j)),
        out_shape=jax.ShapeDtypeStruct((m, n), F32),
        compiler_params=_cparams(("parallel", "arbitrary")),
        name="matmul_residual",
    )(a, w, x)


def _gateup_kernel(h_ref, wg_ref, wu_ref, o_ref):
    h = h_ref[...]
    g = jnp.dot(h, wg_ref[...], preferred_element_type=F32)
    u = jnp.dot(h, wu_ref[...], preferred_element_type=F32)
    o_ref[...] = (g / (1.0 + jnp.exp(-g)) * u).astype(o_ref.dtype)


def _gateup(h, wg, wu, tm, tf):
    m, d = h.shape
    f = wg.shape[1]
    return pl.pallas_call(
        _gateup_kernel,
        grid=(m // tm, f // tf),
        in_specs=[pl.BlockSpec((tm, d), lambda i, j: (i, 0)),
                  pl.BlockSpec((d, tf), lambda i, j: (0, j)),
                  pl.BlockSpec((d, tf), lambda i, j: (0, j))],
        out_specs=pl.BlockSpec((tm, tf), lambda i, j: (i, j)),
        out_shape=jax.ShapeDtypeStruct((m, f), BF16),
        compiler_params=_cparams(("parallel", "arbitrary")),
        name="ffn_gateup",
    )(h, wg, wu)


def _down_kernel(a_ref, w_ref, x_ref, o_ref):
    part = FFN_RESIDUAL * jnp.dot(a_ref[...], w_ref[...], preferred_element_type=F32)

    @pl.when(pl.program_id(2) == 0)
    def _():
        o_ref[...] = x_ref[...] + part

    @pl.when(pl.program_id(2) != 0)
    def _():
        o_ref[...] += part


def _down(act, wd, x, tm, tn, tk):
    m, f = act.shape
    n = wd.shape[1]
    return pl.pallas_call(
        _down_kernel,
        grid=(m // tm, n // tn, f // tk),
        in_specs=[pl.BlockSpec((tm, tk), lambda i, j, k: (i, k)),
                  pl.BlockSpec((tk, tn), lambda i, j, k: (k, j)),
                  pl.BlockSpec((tm, tn), lambda i, j, k: (i, j))],
        out_specs=pl.BlockSpec((tm, tn), lambda i, j, k: (i, j)),
        out_shape=jax.ShapeDtypeStruct((m, n), F32),
        compiler_params=_cparams(("parallel", "parallel", "arbitrary")),
        name="ffn_down",
    )(act, wd, x)


def _pick_tile(n, prefs):
    for t in prefs:
        if n % t == 0:
            return t
    return n


def _ffn(x2d, gain, wg, wu, wd):
    m = x2d.shape[0]
    h = _rmsnorm(x2d, gain, BF16)
    tm = _pick_tile(m, (1024, 512, 256))
    act = _gateup(h, wg, wu, tm, _pick_tile(wg.shape[1], (512, 256, 128)))
    return _down(act, wd, x2d, tm, _pick_tile(wd.shape[1], (1024, 512, 256, 128)),
                 _pick_tile(wd.shape[0], (2816, 1024, 512, 256, 128)))


def _qk_raw(q, k):
    return lax.dot_general(q, k, (((1,), (1,)), ((), ())), preferred_element_type=F32)


def _qk(q, k):
    return _qk_raw(q, k) * ATTN_SCALE


def _online_update_multi(raws, vs, states):
    m_prev = [m_ref[...] for m_ref, _, _ in states]
    m_new = [jnp.maximum(mp, jnp.max(raw, axis=-1, keepdims=True)) for mp, raw in zip(m_prev, raws)]
    ps = [jnp.exp2((raw - mn) * SCALE_LOG2E) for raw, mn in zip(raws, m_new)]
    alphas = [jnp.exp2((mp - mn) * SCALE_LOG2E) for mp, mn in zip(m_prev, m_new)]
    pvs = [jnp.dot(p.astype(BF16), v, preferred_element_type=F32) for p, v in zip(ps, vs)]
    for (m_ref, l_ref, acc_ref), mn, a, p, pv in zip(states, m_new, alphas, ps, pvs):
        l_ref[...] = a * l_ref[...] + jnp.sum(p, axis=-1, keepdims=True)
        acc_ref[...] = a * acc_ref[...] + pv
        m_ref[...] = mn


def _flash_init(m_ref, l_ref, acc_ref):
    m_ref[...] = jnp.full(m_ref.shape, NEG_BIG, F32)
    l_ref[...] = jnp.zeros(l_ref.shape, F32)
    acc_ref[...] = jnp.zeros(acc_ref.shape, F32)


def _sweep_tiles(t0, wide, tile, reverse=False, keep_going=None):
    n_full = t0 // wide
    tail_off = pl.multiple_of(n_full * wide, wide)

    def full(jj, c):
        tile(pl.multiple_of((n_full - 1 - jj if reverse else jj) * wide, wide), False)
        return c

    if not reverse:
        lax.fori_loop(0, n_full, full, 0)
        tile(tail_off, True)
    elif keep_going is None:
        tile(tail_off, True)
        lax.fori_loop(0, n_full, full, 0)
    else:
        tile(tail_off, True)
        lax.while_loop(lambda s: (s[0] < n_full) & s[1],
                       lambda s: (full(s[0], s[0] + 1), keep_going()),
                       (jnp.int32(0), keep_going()))


def _fill_key_aug(kaug_ref, k_ref, aux_fn):
    t = kaug_ref.shape[0]

    def body(c, carry):
        off = pl.multiple_of(c * KV_UNIT, KV_UNIT)
        pos = lax.broadcasted_iota(jnp.int32, (KV_UNIT, LANES), 0) + off
        lane = lax.broadcasted_iota(jnp.int32, (KV_UNIT, LANES), 1)
        kaug_ref[pl.ds(off, KV_UNIT), 0:HEAD_DIM] = k_ref[0, pl.ds(off, KV_UNIT), :]
        kaug_ref[pl.ds(off, KV_UNIT), HEAD_DIM:] = aux_fn(pos, lane).astype(BF16)
        return carry

    lax.fori_loop(0, t // KV_UNIT, body, 0)


def _topk_rounds(scores, blkf, sels, rounds):
    for _ in range(rounds):
        mx = [jnp.max(s, axis=-1, keepdims=True) for s in scores]
        idx = [jnp.min(jnp.where(s == m, blkf, 1e9), axis=-1, keepdims=True) for s, m in zip(scores, mx)]
        picks = [(blkf == i) & (m > -jnp.inf) for i, m in zip(idx, mx)]
        sels = [jnp.where(p, 1.0, sel) for p, sel in zip(picks, sels)]
        scores = [jnp.where(p, -jnp.inf, s) for p, s in zip(picks, scores)]
    return sels


def _split3(x):
    hi = x.astype(BF16)
    r1 = x - hi.astype(F32)
    mid = r1.astype(BF16)
    lo = (r1 - mid.astype(F32)).astype(BF16)
    return hi, mid, lo


def _dot_split(x, w_bf16, terms):
    parts = _split3(x)[:terms]
    out = jnp.dot(parts[0], w_bf16, preferred_element_type=F32)
    for p in parts[1:]:
        out = out + jnp.dot(p, w_bf16, preferred_element_type=F32)
    return out


def _logf_cumsum_kernel(x_ref, tri_ref, low_ref, o_ref):
    x = x_ref[0]
    logf = jnp.minimum(x, 0.0) - jnp.log1p(jnp.exp(-jnp.abs(x)))
    within = _dot_split(logf, tri_ref[...], 3)
    hi, mid, lo = _split3(logf)
    low = low_ref[...]
    before = (jnp.dot(low, hi, preferred_element_type=F32) + jnp.dot(low, mid, preferred_element_type=F32)
              + jnp.dot(low, lo, preferred_element_type=F32))
    o_ref[0] = -(within + jnp.sum(before, axis=-1, keepdims=True))


def _neg_cum_logf(logits_rows):
    r, t = logits_rows.shape
    nc = t // LANES
    li = np.arange(LANES)
    tri = jnp.asarray((li[:, None] <= li[None, :]).astype(np.float32), BF16)
    ci = np.arange(nc)
    low = jnp.asarray((ci[None, :] < ci[:, None]).astype(np.float32), BF16)
    out = pl.pallas_call(
        _logf_cumsum_kernel,
        grid=(r,),
        in_specs=[pl.BlockSpec((1, nc, LANES), lambda i: (i, 0, 0)),
                  pl.BlockSpec((LANES, LANES), lambda i: (0, 0)),
                  pl.BlockSpec((nc, nc), lambda i: (0, 0))],
        out_specs=pl.BlockSpec((1, nc, LANES), lambda i: (i, 0, 0)),
        out_shape=jax.ShapeDtypeStruct((r, nc, LANES), F32),
        compiler_params=_cparams(("parallel",)),
        name="fox_logf_cumsum",
    )(logits_rows.reshape(r, nc, LANES), tri, low)
    return out.reshape(r, t)


def _chain_rows(c):
    return slice(c * Q_ROWS, (c + 1) * Q_ROWS)


def _causal_masks(off, wide, t0, strict=False):
    col = lax.broadcasted_iota(jnp.int32, (1, wide), 1) + off
    rows = [lax.broadcasted_iota(jnp.int32, (Q_ROWS, 1), 0) + (t0 + c * Q_ROWS) for c in range(Q_CHAINS)]
    return [(col < r) if strict else (col <= r) for r in rows]


def _fox_kernel(q_ref, k_ref, v_ref, nf_ref, o_ref, m_ref, l_ref, acc_ref, *, wide):
    t0 = pl.program_id(2) * (Q_CHAINS * Q_ROWS)
    qs = [q_ref[0, _chain_rows(c), :] for c in range(Q_CHAINS)]
    states = [(m_ref.at[c], l_ref.at[c], acc_ref.at[c]) for c in range(Q_CHAINS)]
    _flash_init(m_ref, l_ref, acc_ref)

    def tile(off, tail):
        bias = nf_ref[0, 0, pl.ds(off // wide, 1), :] * (1.0 / ATTN_SCALE)
        k = k_ref[0, pl.ds(off, wide), :]
        raws = [_qk_raw(q, k) + bias for q in qs]
        if tail:
            raws = [jnp.where(ok, raw, NEG_BIG) for ok, raw in zip(_causal_masks(off, wide, t0), raws)]
        _online_update_multi(raws, [v_ref[0, pl.ds(off, wide), :]] * Q_CHAINS, states)

    _sweep_tiles(t0, wide, tile)
    for c in range(Q_CHAINS):
        o_ref[0, _chain_rows(c), :] = acc_ref[c] / l_ref[c]


def _attn_scratch():
    return [pltpu.VMEM((Q_CHAINS, Q_ROWS, 1), F32), pltpu.VMEM((Q_CHAINS, Q_ROWS, 1), F32),
            pltpu.VMEM((Q_CHAINS, Q_ROWS, HEAD_DIM), F32)]


def _fox_attention(p3, neg_cum_f):
    b, t, _ = p3.shape
    tq = Q_CHAINS * Q_ROWS
    wide = min(KV_WIDE, t)
    nf = neg_cum_f.reshape(b, GROUP_HEADS, t // wide, wide)
    return pl.pallas_call(
        functools.partial(_fox_kernel, wide=wide),
        grid=(b, GROUP_HEADS, t // tq),
        in_specs=[pl.BlockSpec((1, tq, HEAD_DIM), lambda bi, h, i: (bi, i, COL["fq"] + h)),
                  pl.BlockSpec((1, t, HEAD_DIM), lambda bi, h, i: (bi, 0, COL["fk"] + h)),
                  pl.BlockSpec((1, t, HEAD_DIM), lambda bi, h, i: (bi, 0, COL["fv"] + h)),
                  pl.BlockSpec((1, 1, t // wide, wide), lambda bi, h, i: (bi, h, 0, 0))],
        out_specs=pl.BlockSpec((1, tq, HEAD_DIM), lambda bi, h, i: (bi, i, h)),
        out_shape=jax.ShapeDtypeStruct((b, t, GROUP_WIDTH), F32),
        scratch_shapes=_attn_scratch(),
        compiler_params=_cparams(("parallel", "parallel", "arbitrary")),
        name="fox_attention",
    )(p3, p3, p3, nf)


def _neg_abs(x):
    bits = lax.bitcast_convert_type(x, jnp.uint32) | jnp.uint32(0x80000000)
    return lax.bitcast_convert_type(bits, F32)


def _stick_kernel(q_ref, k_ref, v_ref, u_ref, o_ref, r_ref, acc_ref, *, wide):
    t0 = pl.program_id(2) * (Q_CHAINS * Q_ROWS)
    chains = range(Q_CHAINS)
    qs = [q_ref[0, _chain_rows(c), :] for c in chains]
    r_ref[...] = jnp.zeros(r_ref.shape, F32)
    acc_ref[...] = jnp.zeros(acc_ref.shape, F32)
    u = u_ref[...]

    def tile(off, tail):
        k = k_ref[0, pl.ds(off, wide), :]
        z_all = [_qk_raw(q, k) * SCALE_LOG2E for q in qs]
        later = [r_ref[c] for c in chains]
        weights = [[] for _ in chains]
        for sb in reversed(range(wide // Q_ROWS)):
            cols = slice(sb * Q_ROWS, (sb + 1) * Q_ROWS)
            zs = [z[:, cols] for z in z_all]
            log_beta = [jnp.minimum(z, 0.0) - jnp.log(1.0 + jnp.exp2(_neg_abs(z))) * LOG2E for z in zs]
            log_keep = [lb - z for lb, z in zip(log_beta, zs)]
            if tail:
                before = _causal_masks(off + sb * Q_ROWS, Q_ROWS, t0, strict=True)
                log_keep = [jnp.where(ok, lk, 0.0) for ok, lk in zip(before, log_keep)]
            inside = [jnp.dot(jnp.concatenate(_split3(lk)[:2], axis=1), u, preferred_element_type=F32)
                      for lk in log_keep]
            a = [jnp.exp2(lb + cs + lt) for lb, cs, lt in zip(log_beta, inside, later)]
            if tail:
                a = [jnp.where(ok, x, 0.0) for ok, x in zip(before, a)]
            for c in chains:
                weights[c].append(a[c].astype(BF16))
            later = [lt + cs[:, 0:1] + lk[:, 0:1] for lt, cs, lk in zip(later, inside, log_keep)]
        v = v_ref[0, pl.ds(off, wide), :]
        for c in chains:
            acc_ref[c] += jnp.dot(jnp.concatenate(weights[c][::-1], axis=1), v, preferred_element_type=F32)
            r_ref[c] = later[c]

    def keep_going():
        return jnp.max(r_ref[...]) > -STICK_DEAD_LOG2

    _sweep_tiles(t0, wide, tile, reverse=True, keep_going=keep_going)
    for c in chains:
        o_ref[0, _chain_rows(c), :] = acc_ref[c]


def _stick_attention(p3):
    b, t, _ = p3.shape
    tq = Q_CHAINS * Q_ROWS
    wide = min(KV_WIDE, t)
    ki = np.arange(Q_ROWS)
    u = (ki[:, None] > ki[None, :]).astype(np.float32)
    u = jnp.asarray(np.concatenate([u, u], axis=0), BF16)
    return pl.pallas_call(
        functools.partial(_stick_kernel, wide=wide),
        grid=(b, GROUP_HEADS, t // tq),
        in_specs=[pl.BlockSpec((1, tq, HEAD_DIM), lambda bi, h, i: (bi, i, COL["sq"] + h)),
                  pl.BlockSpec((1, t, HEAD_DIM), lambda bi, h, i: (bi, 0, COL["sk"] + h)),
                  pl.BlockSpec((1, t, HEAD_DIM), lambda bi, h, i: (bi, 0, COL["sv"] + h)),
                  pl.BlockSpec((2 * Q_ROWS, Q_ROWS), lambda bi, h, i: (0, 0))],
        out_specs=pl.BlockSpec((1, tq, HEAD_DIM), lambda bi, h, i: (bi, i, h)),
        out_shape=jax.ShapeDtypeStruct((b, t, GROUP_WIDTH), F32),
        scratch_shapes=[pltpu.VMEM((Q_CHAINS, Q_ROWS, 1), F32), pltpu.VMEM((Q_CHAINS, Q_ROWS, HEAD_DIM), F32)],
        compiler_params=_cparams(("parallel", "parallel", "arbitrary")),
        name="stick_attention",
    )(p3, p3, p3, u)


def _moba_kernel(slope_ref, q_ref, k_ref, v_ref, o_ref, kaug_ref, kmean_ref, m_ref, l_ref, acc_ref, *,
                 n_blk, wide):
    h = pl.program_id(1)
    i = pl.program_id(2)
    tq = MOBA_BLOCK
    slope = slope_ref[h]

    @pl.when(i == 0)
    def _():
        kmean_ref[...] = jnp.zeros(kmean_ref.shape, F32)
        kb = k_ref[0].astype(F32).reshape(n_blk, tq, HEAD_DIM)
        kmean_ref[0:n_blk, :] = jnp.mean(kb, axis=1)

        def key_aux(pos, lane):
            onehot = jnp.where(lane == jnp.right_shift(pos, int(math.log2(tq))), 1.0, 0.0)
            hi, mid, lo = _split3(slope * pos.astype(F32) * (1.0 / ATTN_SCALE))
            return jnp.where(lane == n_blk, hi.astype(F32),
                             jnp.where(lane == n_blk + 1, mid.astype(F32),
                                       jnp.where(lane == n_blk + 2, lo.astype(F32), onehot)))

        _fill_key_aug(kaug_ref, k_ref, key_aux)

    chains = range(Q_CHAINS)
    t0 = i * (Q_CHAINS * Q_ROWS)
    qs = [q_ref[0, _chain_rows(c), :] for c in chains]
    own = [i * Q_CHAINS + c for c in chains]
    kmean = kmean_ref[...].astype(BF16)
    blk = lax.broadcasted_iota(jnp.int32, (Q_ROWS, LANES), 1)
    blkf = blk.astype(F32)
    scores = [jnp.where(blk < own[c], _qk_raw(qs[c], kmean), -jnp.inf) for c in chains]
    sels = _topk_rounds(scores, blkf, [jnp.where(blk == own[c], 1.0, 0.0) for c in chains], MOBA_TOPK)
    q_aux = [jnp.where(blk < n_blk, jnp.where(sel > 0.5, 0.0, NEG_BIG), jnp.where(blk < n_blk + 3, 1.0, 0.0))
             for sel in sels]
    q_aug = [jnp.concatenate([q, aux.astype(BF16)], axis=1) for q, aux in zip(qs, q_aux)]
    states = [(m_ref.at[c], l_ref.at[c], acc_ref.at[c]) for c in chains]
    _flash_init(m_ref, l_ref, acc_ref)

    def tile(off, tail):
        k = kaug_ref[pl.ds(off, wide), :]
        raws = [_qk_raw(q, k) for q in q_aug]
        if tail:
            raws = [jnp.where(ok, raw, NEG_BIG) for ok, raw in zip(_causal_masks(off, wide, t0), raws)]
        _online_update_multi(raws, [v_ref[0, pl.ds(off, wide), :]] * Q_CHAINS, states)

    _sweep_tiles(t0, wide, tile)
    for c in chains:
        o_ref[0, _chain_rows(c), :] = acc_ref[c] / l_ref[c]


def _moba_attention(p3, slopes):
    b, t, _ = p3.shape
    tq = Q_CHAINS * Q_ROWS
    n_blk = t // MOBA_BLOCK
    wide = min(KV_WIDE, t)
    assert Q_ROWS == MOBA_BLOCK and t % tq == 0 and n_blk + 3 <= LANES
    return pl.pallas_call(
        functools.partial(_moba_kernel, n_blk=n_blk, wide=wide),
        grid=(b, GROUP_HEADS, t // tq),
        in_specs=[pl.BlockSpec(memory_space=pltpu.SMEM),
                  pl.BlockSpec((1, tq, HEAD_DIM), lambda bi, h, i: (bi, i, COL["mq"] + h)),
                  pl.BlockSpec((1, t, HEAD_DIM), lambda bi, h, i: (bi, 0, COL["mk"] + h)),
                  pl.BlockSpec((1, t, HEAD_DIM), lambda bi, h, i: (bi, 0, COL["mv"] + h))],
        out_specs=pl.BlockSpec((1, tq, HEAD_DIM), lambda bi, h, i: (bi, i, h)),
        out_shape=jax.ShapeDtypeStruct((b, t, GROUP_WIDTH), F32),
        scratch_shapes=[pltpu.VMEM((t, HEAD_DIM + LANES), BF16),
                        pltpu.VMEM((LANES, HEAD_DIM), F32)] + _attn_scratch(),
        compiler_params=_cparams(("parallel", "parallel", "arbitrary")),
        name="moba_attention",
    )(slopes, p3, p3, p3)


def _nsa_compress_kernel(x_ref, pos_ref, w1_ref, w2_ref, o_ref):
    x = (x_ref[0].astype(F32) + pos_ref[...]).astype(BF16)
    hmid = jnp.dot(x, w1_ref[...], preferred_element_type=F32)
    hmid = hmid / (1.0 + jnp.exp(-hmid))
    o_ref[0] = jnp.dot(hmid.astype(BF16), w2_ref[...], preferred_element_type=F32).astype(o_ref.dtype)


def _nsa_compress(blocks, pos, w1, w2):
    r, n, width = blocks.shape
    return pl.pallas_call(
        _nsa_compress_kernel,
        grid=(r,),
        in_specs=[pl.BlockSpec((1, n, width), lambda i: (i, 0, 0)),
                  pl.BlockSpec((1, width), lambda i: (0, 0)),
                  pl.BlockSpec((width, HEAD_DIM), lambda i: (0, 0)),
                  pl.BlockSpec((HEAD_DIM, HEAD_DIM), lambda i: (0, 0))],
        out_specs=pl.BlockSpec((1, n, HEAD_DIM), lambda i: (i, 0, 0)),
        out_shape=jax.ShapeDtypeStruct((r, n, HEAD_DIM), BF16),
        compiler_params=_cparams(("parallel",)),
        name="nsa_compress",
    )(blocks, pos.reshape(1, width).astype(F32), w1.astype(BF16), w2.astype(BF16))


def _nsa_select_kernel(slope_ref, q_ref, kc_ref, vc_ref, a_ref, oc_ref, sel_ref, *, tq, n_cmp, n_blk):
    g = pl.program_id(1)
    i = pl.program_id(2)
    t0 = i * tq
    kc = kc_ref[0, 0]
    vc = vc_ref[0, 0]
    ncp = kc.shape[0]
    tok = lax.broadcasted_iota(jnp.int32, (1, ncp), 1)
    cmp_end = tok * NSA_CMP_STRIDE + (NSA_CMP_LEN - 1)
    row = lax.broadcasted_iota(jnp.int32, (tq, 1), 0)
    admissible = (cmp_end <= row + t0) & (tok < n_cmp)
    rel_end = (cmp_end - t0).astype(F32)

    imp = jnp.zeros((tq, ncp), F32)
    for hh in range(NSA_Q_PER_KV):
        slope = slope_ref[g * NSA_Q_PER_KV + hh]
        q = q_ref[0, :, hh * HEAD_DIM:(hh + 1) * HEAD_DIM]
        s = jnp.where(admissible, _qk(q, kc) + slope * rel_end, NEG_BIG)
        m = jnp.max(s, axis=-1, keepdims=True)
        p = jnp.where(admissible, jnp.exp(s - m), 0.0)
        p = p / jnp.maximum(jnp.sum(p, axis=-1, keepdims=True), F32_TINY)
        oc_ref[0, :, hh * HEAD_DIM:(hh + 1) * HEAD_DIM] = jnp.dot(
            p.astype(BF16), vc, preferred_element_type=F32)
        imp = imp + p

    imp_blk = _dot_split(imp, a_ref[...], 3)
    blk = lax.broadcasted_iota(jnp.int32, (tq, LANES), 1)
    blkf = blk.astype(F32)
    jt = jnp.right_shift(row + t0, int(math.log2(NSA_SEL_BLOCK)))
    live = (blk <= jt) & (blk < n_blk)
    forced = ((blk == 0) | (blk == jt) | (blk == jt - 1)) & live
    sel = jnp.where(forced, 1.0, 0.0)
    score = jnp.where(live & jnp.logical_not(forced), imp_blk, -jnp.inf)
    sel = _topk_rounds([score], blkf, [sel], NSA_SEL_TOPK - 3)[0]
    sel_ref[0, 0] = sel.astype(sel_ref.dtype)


def _nsa_select(p3, kc, vc, slopes, tq):
    b, t, _ = p3.shape
    n_blk = t // NSA_SEL_BLOCK
    assert n_blk <= LANES and NSA_SEL_TOPK >= 3
    n_cmp = t // NSA_CMP_STRIDE - 1
    ncp = kc.shape[2]
    ratio = NSA_SEL_BLOCK // NSA_CMP_STRIDE
    ti = np.arange(ncp)[:, None]
    bj = np.arange(LANES)[None, :]
    amat = ((ti >= ratio * bj - 1) & (ti <= ratio * bj + ratio - 1) & (ti < n_cmp) & (bj < n_blk))
    amat = jnp.asarray(amat.astype(np.float32), BF16)
    width = NSA_Q_PER_KV * HEAD_DIM
    qblk = COL["nq"] * HEAD_DIM // width
    return pl.pallas_call(
        functools.partial(_nsa_select_kernel, tq=tq, n_cmp=n_cmp, n_blk=n_blk),
        grid=(b, NSA_KV_HEADS, t // tq),
        in_specs=[pl.BlockSpec(memory_space=pltpu.SMEM),
                  pl.BlockSpec((1, tq, width), lambda bi, g, i: (bi, i, qblk + g)),
                  pl.BlockSpec((1, 1, ncp, HEAD_DIM), lambda bi, g, i: (bi, g, 0, 0)),
                  pl.BlockSpec((1, 1, ncp, HEAD_DIM), lambda bi, g, i: (bi, g, 0, 0)),
                  pl.BlockSpec((ncp, LANES), lambda bi, g, i: (0, 0))],
        out_specs=[pl.BlockSpec((1, tq, width), lambda bi, g, i: (bi, i, g)),
                   pl.BlockSpec((1, 1, tq, LANES), lambda bi, g, i: (bi, g, i, 0))],
        out_shape=[jax.ShapeDtypeStruct((b, t, GROUP_WIDTH), F32),
                   jax.ShapeDtypeStruct((b, NSA_KV_HEADS, t, LANES), BF16)],
        compiler_params=_cparams(("parallel", "parallel", "parallel")),
        name="nsa_select",
    )(slopes, p3, kc, vc, amat)


def _nsa_attn_kernel(slope_ref, q_ref, ks_ref, vs_ref, kw_ref, vw_ref, sel_ref, oc_ref, gate_ref,
                     o_ref, kaug_ref, m_ref, l_ref, acc_ref, *, tq, wide):
    g = pl.program_id(1)
    i = pl.program_id(2)
    t0 = pl.multiple_of(i * tq, tq)
    nh = NSA_Q_PER_KV

    @pl.when(i == 0)
    def _():
        sel_shift = int(math.log2(NSA_SEL_BLOCK))
        _fill_key_aug(kaug_ref, ks_ref,
                      lambda pos, lane: jnp.where(lane == jnp.right_shift(pos, sel_shift), 1.0, 0.0))

    _flash_init(m_ref, l_ref, acc_ref)
    sel_bias = jnp.where(sel_ref[0, 0].astype(F32) > 0.5, 0.0, NEG_BIG).astype(BF16)
    qs = [q_ref[0, :, hh * HEAD_DIM:(hh + 1) * HEAD_DIM] for hh in range(nh)]
    q_aug = [jnp.concatenate([q, sel_bias], axis=1) for q in qs]
    slope_raw = [slope_ref[g * nh + hh] * (1.0 / ATTN_SCALE) for hh in range(nh)]
    row = lax.broadcasted_iota(jnp.int32, (tq, 1), 0)

    def rel_pos(off, width):
        return lax.broadcasted_iota(jnp.int32, (1, width), 1) + (off - t0)

    def sel_tile(off, tail):
        rel = rel_pos(off, wide)
        relf = rel.astype(F32)
        k = kaug_ref[pl.ds(off, wide), :]
        v = vs_ref[0, pl.ds(off, wide), :]
        raws = [_qk_raw(q_aug[hh], k) + slope_raw[hh] * relf for hh in range(nh)]
        if tail:
            raws = [jnp.where(rel <= row, raw, NEG_BIG) for raw in raws]
        _online_update_multi(raws, [v] * nh, [(m_ref.at[hh], l_ref.at[hh], acc_ref.at[hh]) for hh in range(nh)])

    _sweep_tiles(t0, wide, sel_tile)

    def window(off, width, far):
        rel = rel_pos(off, width)
        relf = rel.astype(F32)
        k = kw_ref[0, pl.ds(off, width), :]
        v = vw_ref[0, pl.ds(off, width), :]
        ok = rel <= row
        if far:
            ok = ok & (row - rel < NSA_WINDOW)
        raws = [jnp.where(ok, _qk_raw(qs[hh], k) + slope_raw[hh] * relf, NEG_BIG) for hh in range(nh)]
        _online_update_multi(raws, [v] * nh,
                             [(m_ref.at[nh + hh], l_ref.at[nh + hh], acc_ref.at[nh + hh]) for hh in range(nh)])

    n_back = NSA_WINDOW // tq
    for back in range(n_back + 1):
        @pl.when((i == back) if back < n_back else (i >= back))
        def _(back=back):
            window(pl.multiple_of(t0 - back * tq, tq), (back + 1) * tq, far=(back == n_back))

    gates = gate_ref[0, 0]
    gates = 1.0 / (1.0 + jnp.exp(-gates))
    for hh in range(nh):
        c0 = hh * NSA_N_BRANCHES
        o_c = oc_ref[0, :, hh * HEAD_DIM:(hh + 1) * HEAD_DIM]
        o_s = acc_ref[hh] / l_ref[hh]
        o_w = acc_ref[nh + hh] / l_ref[nh + hh]
        o_ref[0, :, hh * HEAD_DIM:(hh + 1) * HEAD_DIM] = (
            gates[:, c0:c0 + 1] * o_c + gates[:, c0 + 1:c0 + 2] * o_s + gates[:, c0 + 2:c0 + 3] * o_w)


def _nsa_attention(p3, sel, o_cmp, gate_logits, slopes, tq):
    b, t, _ = p3.shape
    assert tq == KV_UNIT and NSA_WINDOW == 2 * tq and t // NSA_SEL_BLOCK <= LANES
    width = NSA_Q_PER_KV * HEAD_DIM
    qblk = COL["nq"] * HEAD_DIM // width
    kv = lambda name: pl.BlockSpec((1, t, HEAD_DIM), lambda bi, g, i: (bi, 0, COL[name] + g))
    nst = 2 * NSA_Q_PER_KV
    return pl.pallas_call(
        functools.partial(_nsa_attn_kernel, tq=tq, wide=min(KV_WIDE, t)),
        grid=(b, NSA_KV_HEADS, t // tq),
        in_specs=[pl.BlockSpec(memory_space=pltpu.SMEM),
                  pl.BlockSpec((1, tq, width), lambda bi, g, i: (bi, i, qblk + g)),
                  kv("nks"), kv("nvs"), kv("nkw"), kv("nvw"),
                  pl.BlockSpec((1, 1, tq, LANES), lambda bi, g, i: (bi, g, i, 0)),
                  pl.BlockSpec((1, tq, width), lambda bi, g, i: (bi, i, g)),
                  pl.BlockSpec((1, 1, tq, LANES), lambda bi, g, i: (bi, g, i, 0))],
        out_specs=pl.BlockSpec((1, tq, width), lambda bi, g, i: (bi, i, g)),
        out_shape=jax.ShapeDtypeStruct((b, t, GROUP_WIDTH), F32),
        scratch_shapes=[pltpu.VMEM((t, HEAD_DIM + LANES), BF16),
                        pltpu.VMEM((nst, tq, 1), F32), pltpu.VMEM((nst, tq, 1), F32),
                        pltpu.VMEM((nst, tq, HEAD_DIM), F32)],
        compiler_params=_cparams(("parallel", "parallel", "arbitrary")),
        name="nsa_attention",
    )(slopes, p3, p3, p3, p3, p3, sel, o_cmp, gate_logits)


def _nsa_blocks(p3, name):
    b, t, _ = p3.shape
    c0 = COL[name] * HEAD_DIM
    x = p3[:, :, c0:c0 + NSA_KV_HEADS * HEAD_DIM].reshape(b, t // NSA_CMP_STRIDE, NSA_CMP_STRIDE,
                                                         NSA_KV_HEADS, HEAD_DIM)
    x = x.transpose(0, 3, 1, 2, 4).reshape(b * NSA_KV_HEADS, t // NSA_CMP_STRIDE, NSA_CMP_STRIDE * HEAD_DIM)
    blocks = jnp.concatenate([x[:, :-1], x[:, 1:]], axis=-1)
    return jnp.pad(blocks, ((0, 0), (0, 1), (0, 0)))


def _alibi_slopes():
    n = 2 * GROUP_HEADS
    slopes = 2.0 ** (-8.0 * np.arange(1, n + 1) / n)
    return jnp.asarray(slopes[0::2], F32), jnp.asarray(slopes[1::2], F32)


def _regroup_w_in(w_in):
    main = jnp.concatenate([w_in[:, REF_OFF[n][0]:REF_OFF[n][0] + REF_OFF[n][1]] for n, _ in _MAIN_PIECES],
                           axis=1).astype(BF16)
    ng0, ngw = REF_OFF["ng"]
    ff0, ffw = REF_OFF["ff"]
    small = jnp.concatenate([w_in[:, ng0:ng0 + ngw], w_in[:, ff0:ff0 + ffw]], axis=1)
    small = jnp.pad(small, ((0, 0), (0, LANES - ngw - ffw))).astype(BF16)
    return main, small


def _mixer(x2d, b, t, norm_gain, w_in, forget_bias, pos_k, w1_k, w2_k, pos_v, w1_v, w2_v, group_gain, w_out):
    m = x2d.shape[0]
    h = _rmsnorm(x2d, norm_gain, BF16)
    w_main, w_small = _regroup_w_in(w_in)
    tm = _pick_tile(m, (1024, 512, 256))
    p3 = _matmul(h, w_main, BF16, tm, 512).reshape(b, t, MAIN_BLOCKS * HEAD_DIM)
    small = _matmul(h, w_small, F32, tm, LANES).reshape(b, t, LANES)
    slopes_moba, slopes_nsa = _alibi_slopes()

    o_a = _moba_attention(p3, slopes_moba)

    n_gate = NSA_N_BRANCHES * GROUP_HEADS
    kc = _nsa_compress(_nsa_blocks(p3, "nkc"), pos_k, w1_k, w2_k)
    vc = _nsa_compress(_nsa_blocks(p3, "nvc"), pos_v, w1_v, w2_v)
    ncp = kc.shape[1]
    kc = kc.reshape(b, NSA_KV_HEADS, ncp, HEAD_DIM)
    vc = vc.reshape(b, NSA_KV_HEADS, ncp, HEAD_DIM)
    o_cmp, sel = _nsa_select(p3, kc, vc, slopes_nsa, _pick_tile(t, (512, 256)))
    per_group = n_gate // NSA_KV_HEADS
    gate_logits = small[:, :, :n_gate].reshape(b, t, NSA_KV_HEADS, per_group).transpose(0, 2, 1, 3)
    gate_logits = jnp.pad(gate_logits, ((0, 0), (0, 0), (0, 0), (0, LANES - per_group)))
    o_b = _nsa_attention(p3, sel, o_cmp, gate_logits, slopes_nsa, KV_UNIT)

    ff = small[:, :, n_gate:n_gate + GROUP_HEADS] + forget_bias
    neg_cum_f = _neg_cum_logf(ff.transpose(0, 2, 1).reshape(b * GROUP_HEADS, t))
    o_c = _fox_attention(p3, neg_cum_f)

    o_d = _stick_attention(p3)

    on = _groupnorm([o.reshape(m, GROUP_WIDTH) for o in (o_a, o_b, o_c, o_d)], group_gain)
    return _matmul_residual(on, w_out, x2d, tm, 512)


def _cast_pad_kernel(w_ref, o_ref, *, n_row_tiles, cols):
    @pl.when(pl.program_id(0) < n_row_tiles)
    def _():
        o_ref[:, 0:cols] = w_ref[0].astype(BF16)
        if o_ref.shape[1] > cols:
            o_ref[:, cols:] = jnp.zeros((o_ref.shape[0], o_ref.shape[1] - cols), BF16)

    @pl.when(pl.program_id(0) >= n_row_tiles)
    def _():
        o_ref[...] = jnp.zeros(o_ref.shape, BF16)


def _weight_bf16(stacked, layer, rows_out, cols_out, tr=256):
    _, rows, cols = stacked.shape
    assert rows % tr == 0 and rows_out % tr == 0 and cols % LANES == 0
    n_row_tiles = rows // tr
    return pl.pallas_call(
        functools.partial(_cast_pad_kernel, n_row_tiles=n_row_tiles, cols=cols),
        grid=(rows_out // tr,),
        in_specs=[pl.BlockSpec((1, tr, cols), lambda j: (layer, jnp.minimum(j, n_row_tiles - 1), 0))],
        out_specs=pl.BlockSpec((tr, cols_out), lambda j: (j, 0)),
        out_shape=jax.ShapeDtypeStruct((rows_out, cols_out), BF16),
        compiler_params=_cparams(("parallel",)),
        name="weight_bf16",
    )(stacked)


def _pad_ffn(wg, wu, wd, layer):
    d, f = wg.shape[1:]
    fp = -(-f // FFN_TILE) * FFN_TILE
    return (_weight_bf16(wg, layer, d, fp), _weight_bf16(wu, layer, d, fp), _weight_bf16(wd, layer, fp, d))


def kernel(x, ffn1_norm, ffn1_w_gate, ffn1_w_up, ffn1_w_down, mix_norm, w_in, fox_forget_bias, nsa_cmp_pos_k, nsa_cmp_w1_k, nsa_cmp_w2_k, nsa_cmp_pos_v, nsa_cmp_w1_v, nsa_cmp_w2_v, group_norm, w_out, ffn2_norm, ffn2_w_gate, ffn2_w_up, ffn2_w_down, final_norm):
    b, t, d = x.shape
    x2d = x.reshape(b * t, d)
    for i in range(ffn1_norm.shape[0]):
        x2d = _ffn(x2d, ffn1_norm[i], *_pad_ffn(ffn1_w_gate, ffn1_w_up, ffn1_w_down, i))
        x2d = _mixer(x2d, b, t, mix_norm[i], w_in[i], fox_forget_bias[i], nsa_cmp_pos_k[i], nsa_cmp_w1_k[i],
                     nsa_cmp_w2_k[i], nsa_cmp_pos_v[i], nsa_cmp_w1_v[i], nsa_cmp_w2_v[i],
                     group_norm[i], _weight_bf16(w_out, i, d, d))
        x2d = _ffn(x2d, ffn2_norm[i], *_pad_ffn(ffn2_w_gate, ffn2_w_up, ffn2_w_down, i))
    return _rmsnorm(x2d, final_norm, F32).reshape(b, t, d)
```

```python
import functools
import math

import numpy as np
import jax
import jax.numpy as jnp
from jax import lax
from jax.experimental import pallas as pl
from jax.experimental.pallas import tpu as pltpu

F32 = jnp.float32
BF16 = jnp.bfloat16

HEAD_DIM = 128
GROUP_HEADS = 8
GROUP_WIDTH = GROUP_HEADS * HEAD_DIM
N_MIXERS = 4
FFN_RESIDUAL = 0.5
RMS_EPS = 1e-6

MOBA_BLOCK = 256
MOBA_TOPK = 3

NSA_KV_HEADS = 2
NSA_Q_PER_KV = 4
NSA_CMP_STRIDE = 16
NSA_CMP_LEN = 32
NSA_SEL_BLOCK = 64
NSA_SEL_TOPK = 16
NSA_WINDOW = 512
NSA_N_BRANCHES = 3

ATTN_SCALE = HEAD_DIM ** -0.5
LOG2E = math.log2(math.e)
SCALE_LOG2E = ATTN_SCALE * LOG2E
NEG_BIG = -1e30
STICK_DEAD_LOG2 = 160.0
FLASH_DEAD_LOG2 = 170.0
BOUND_SLACK = 1.001
KV_UNIT = 256
KV_WIDE = 1024
Q_ROWS = 256
Q_CHAINS = 4
F32_TINY = float(np.finfo(np.float32).tiny)

LANES = 128
VMEM_LIMIT = 56 * 1024 * 1024
FFN_TILE = 512

_MAIN_PIECES = (
    ("mq", 8), ("mk", 8), ("mv", 8), ("nq", 8),
    ("nkc", 2), ("nvc", 2), ("nks", 2), ("nvs", 2), ("nkw", 2), ("nvw", 2),
    ("fq", 8), ("fk", 8), ("fv", 8), ("sq", 8), ("sk", 8), ("sv", 8),
)
COL = {}
_o = 0
for _n, _w in _MAIN_PIECES:
    COL[_n] = _o
    _o += _w
MAIN_BLOCKS = _o

_REF_SIZES = (
    ("mq", 1024), ("mk", 1024), ("mv", 1024), ("nq", 1024),
    ("nkc", 256), ("nvc", 256), ("nks", 256), ("nvs", 256), ("nkw", 256), ("nvw", 256),
    ("ng", 24),
    ("fq", 1024), ("fk", 1024), ("fv", 1024), ("ff", 8),
    ("sq", 1024), ("sk", 1024), ("sv", 1024),
)
REF_OFF = {}
_o = 0
for _n, _w in _REF_SIZES:
    REF_OFF[_n] = (_o, _w)
    _o += _w


def _cparams(sem):
    return pltpu.CompilerParams(dimension_semantics=sem, vmem_limit_bytes=VMEM_LIMIT)


def _rms_kernel(x_ref, g_ref, o_ref):
    x = x_ref[...]
    ms = jnp.mean(x * x, axis=-1, keepdims=True)
    o_ref[...] = (x * lax.rsqrt(ms + RMS_EPS) * g_ref[...]).astype(o_ref.dtype)


def _rmsnorm(x2d, gain, out_dtype, tm=256):
    m, d = x2d.shape
    return pl.pallas_call(
        _rms_kernel,
        grid=(m // tm,),
        in_specs=[pl.BlockSpec((tm, d), lambda i: (i, 0)),
                  pl.BlockSpec((1, d), lambda i: (0, 0))],
        out_specs=pl.BlockSpec((tm, d), lambda i: (i, 0)),
        out_shape=jax.ShapeDtypeStruct((m, d), out_dtype),
        compiler_params=_cparams(("parallel",)),
        name="rmsnorm",
    )(x2d, gain.reshape(1, d).astype(F32))


def _groupnorm_kernel(a_ref, b_ref, c_ref, d_ref, g_ref, o_ref):
    for gi, ref in enumerate((a_ref, b_ref, c_ref, d_ref)):
        x = ref[...]
        ms = jnp.mean(x * x, axis=-1, keepdims=True)
        lo, hi = gi * GROUP_WIDTH, (gi + 1) * GROUP_WIDTH
        o_ref[:, lo:hi] = (x * lax.rsqrt(ms + RMS_EPS) * g_ref[:, lo:hi]).astype(o_ref.dtype)


def _groupnorm(parts, gain, tm=256):
    m = parts[0].shape[0]
    d = GROUP_WIDTH * N_MIXERS
    part_spec = pl.BlockSpec((tm, GROUP_WIDTH), lambda i: (i, 0))
    return pl.pallas_call(
        _groupnorm_kernel,
        grid=(m // tm,),
        in_specs=[part_spec] * 4 + [pl.BlockSpec((1, d), lambda i: (0, 0))],
        out_specs=pl.BlockSpec((tm, d), lambda i: (i, 0)),
        out_shape=jax.ShapeDtypeStruct((m, d), BF16),
        compiler_params=_cparams(("parallel",)),
        name="groupnorm",
    )(*parts, gain.reshape(1, d).astype(F32))


def _mm_kernel(a_ref, w_ref, o_ref):
    o_ref[...] = jnp.dot(a_ref[...], w_ref[...], preferred_element_type=F32).astype(o_ref.dtype)


def _matmul(a, w, out_dtype, tm, tn):
    m, k = a.shape
    n = w.shape[1]
    return pl.pallas_call(
        _mm_kernel,
        grid=(m // tm, n // tn),
        in_specs=[pl.BlockSpec((tm, k), lambda i, j: (i, 0)),
                  pl.BlockSpec((k, tn), lambda i, j: (0, j))],
        out_specs=pl.BlockSpec((tm, tn), lambda i, j: (i, j)),
        out_shape=jax.ShapeDtypeStruct((m, n), out_dtype),
        compiler_params=_cparams(("parallel", "arbitrary")),
        name="matmul",
    )(a, w)


def _mm_res_kernel(a_ref, w_ref, x_ref, o_ref):
    o_ref[...] = x_ref[...] + jnp.dot(a_ref[...], w_ref[...], preferred_element_type=F32)


def _matmul_residual(a, w, x, tm, tn):
    m, k = a.shape
    n = w.shape[1]
    return pl.pallas_call(
        _mm_res_kernel,
        grid=(m // tm, n // tn),
        in_specs=[pl.BlockSpec((tm, k), lambda i, j: (i, 0)),
                  pl.BlockSpec((k, tn), lambda i, j: (0, j)),
                  pl.BlockSpec((tm, tn), lambda i, j: (i, j))],
        out_specs=pl.BlockSpec((tm, tn), lambda i, j: (i, j)),
        out_shape=jax.ShapeDtypeStruct((m, n), F32),
        compiler_params=_cparams(("parallel", "arbitrary")),
        name="matmul_residual",
    )(a, w, x)


def _gateup_kernel(h_ref, wg_ref, wu_ref, o_ref):
    h = h_ref[...]
    g = jnp.dot(h, wg_ref[...], preferred_element_type=F32)
    u = jnp.dot(h, wu_ref[...], preferred_element_type=F32)
    o_ref[...] = (g / (1.0 + jnp.exp(-g)) * u).astype(o_ref.dtype)


def _gateup(h, wg, wu, tm, tf):
    m, d = h.shape
    f = wg.shape[1]
    return pl.pallas_call(
        _gateup_kernel,
        grid=(m // tm, f // tf),
        in_specs=[pl.BlockSpec((tm, d), lambda i, j: (i, 0)),
                  pl.BlockSpec((d, tf), lambda i, j: (0, j)),
                  pl.BlockSpec((d, tf), lambda i, j: (0, j))],
        out_specs=pl.BlockSpec((tm, tf), lambda i, j: (i, j)),
        out_shape=jax.ShapeDtypeStruct((m, f), BF16),
        compiler_params=_cparams(("parallel", "arbitrary")),
        name="ffn_gateup",
    )(h, wg, wu)


def _down_kernel(a_ref, w_ref, x_ref, o_ref):
    part = FFN_RESIDUAL * jnp.dot(a_ref[...], w_ref[...], preferred_element_type=F32)

    @pl.when(pl.program_id(2) == 0)
    def _():
        o_ref[...] = x_ref[...] + part

    @pl.when(pl.program_id(2) != 0)
    def _():
        o_ref[...] += part


def _down(act, wd, x, tm, tn, tk):
    m, f = act.shape
    n = wd.shape[1]
    return pl.pallas_call(
        _down_kernel,
        grid=(m // tm, n // tn, f // tk),
        in_specs=[pl.BlockSpec((tm, tk), lambda i, j, k: (i, k)),
                  pl.BlockSpec((tk, tn), lambda i, j, k: (k, j)),
                  pl.BlockSpec((tm, tn), lambda i, j, k: (i, j))],
        out_specs=pl.BlockSpec((tm, tn), lambda i, j, k: (i, j)),
        out_shape=jax.ShapeDtypeStruct((m, n), F32),
        compiler_params=_cparams(("parallel", "parallel", "arbitrary")),
        name="ffn_down",
    )(act, wd, x)


def _pick_tile(n, prefs):
    for t in prefs:
        if n % t == 0:
            return t
    return n


def _ffn(x2d, gain, wg, wu, wd):
    m = x2d.shape[0]
    h = _rmsnorm(x2d, gain, BF16)
    tm = _pick_tile(m, (1024, 512, 256))
    act = _gateup(h, wg, wu, tm, _pick_tile(wg.shape[1], (512, 256, 128)))
    return _down(act, wd, x2d, tm, _pick_tile(wd.shape[1], (1024, 512, 256, 128)),
                 _pick_tile(wd.shape[0], (2816, 1024, 512, 256, 128)))


def _qk_raw(q, k):
    return lax.dot_general(q, k, (((1,), (1,)), ((), ())), preferred_element_type=F32)


def _qk(q, k):
    return _qk_raw(q, k) * ATTN_SCALE


def _online_update_multi(raws, vs, states):
    m_prev = [m_ref[...] for m_ref, _, _ in states]
    m_new = [jnp.maximum(mp, jnp.max(raw, axis=-1, keepdims=True)) for mp, raw in zip(m_prev, raws)]
    ps = [jnp.exp2((raw - mn) * SCALE_LOG2E) for raw, mn in zip(raws, m_new)]
    alphas = [jnp.exp2((mp - mn) * SCALE_LOG2E) for mp, mn in zip(m_prev, m_new)]
    pvs = [jnp.dot(p.astype(BF16), v, preferred_element_type=F32) for p, v in zip(ps, vs)]
    for (m_ref, l_ref, acc_ref), mn, a, p, pv in zip(states, m_new, alphas, ps, pvs):
        l_ref[...] = a * l_ref[...] + jnp.sum(p, axis=-1, keepdims=True)
        acc_ref[...] = a * acc_ref[...] + pv
        m_ref[...] = mn


def _flash_init(m_ref, l_ref, acc_ref):
    m_ref[...] = jnp.full(m_ref.shape, NEG_BIG, F32)
    l_ref[...] = jnp.zeros(l_ref.shape, F32)
    acc_ref[...] = jnp.zeros(acc_ref.shape, F32)


def _sweep_tiles(t0, wide, tile, reverse=False, keep_going=None):
    n_full = t0 // wide
    tail_off = pl.multiple_of(n_full * wide, wide)

    def full_off(jj):
        return pl.multiple_of((n_full - 1 - jj if reverse else jj) * wide, wide)

    def full(jj, c):
        tile(full_off(jj), False)
        return c

    if not reverse:
        lax.fori_loop(0, n_full, full, 0)
        tile(tail_off, True)
    elif keep_going is None:
        tile(tail_off, True)
        lax.fori_loop(0, n_full, full, 0)
    else:
        tile(tail_off, True)
        lax.while_loop(lambda s: (s[0] < n_full) & s[1],
                       lambda s: (full(s[0], s[0] + 1), keep_going(full_off(s[0]))),
                       (jnp.int32(0), keep_going(tail_off)))


def _row_norms(q):
    q = q.astype(F32)
    return jnp.sqrt(jnp.sum(q * q, axis=-1, keepdims=True))


def _max_key_norm(k_ref, out_ref):
    def body(c, mx):
        k = k_ref[0, pl.ds(pl.multiple_of(c * KV_UNIT, KV_UNIT), KV_UNIT), :].astype(F32)
        return jnp.maximum(mx, jnp.max(jnp.sum(k * k, axis=-1, keepdims=True), axis=0, keepdims=True))

    out_ref[...] = jnp.sqrt(lax.fori_loop(0, k_ref.shape[1] // KV_UNIT, body, jnp.zeros((1, 1), F32)))


def _softmax_alive(q_norms, k_norm, bias_caps, states):
    gaps = [qn * (k_norm * BOUND_SLACK) + cap - m_ref[...]
            for qn, cap, (m_ref, _, _) in zip(q_norms, bias_caps, states)]
    worst = jnp.max(functools.reduce(jnp.maximum, gaps))
    return (worst + 1.0) * SCALE_LOG2E > -FLASH_DEAD_LOG2


def _fill_key_aug(kaug_ref, k_ref, aux_fn):
    t = kaug_ref.shape[0]

    def body(c, carry):
        off = pl.multiple_of(c * KV_UNIT, KV_UNIT)
        pos = lax.broadcasted_iota(jnp.int32, (KV_UNIT, LANES), 0) + off
        lane = lax.broadcasted_iota(jnp.int32, (KV_UNIT, LANES), 1)
        kaug_ref[pl.ds(off, KV_UNIT), 0:HEAD_DIM] = k_ref[0, pl.ds(off, KV_UNIT), :]
        kaug_ref[pl.ds(off, KV_UNIT), HEAD_DIM:] = aux_fn(pos, lane).astype(BF16)
        return carry

    lax.fori_loop(0, t // KV_UNIT, body, 0)


def _topk_rounds(scores, blkf, sels, rounds):
    for _ in range(rounds):
        mx = [jnp.max(s, axis=-1, keepdims=True) for s in scores]
        idx = [jnp.min(jnp.where(s == m, blkf, 1e9), axis=-1, keepdims=True) for s, m in zip(scores, mx)]
        picks = [(blkf == i) & (m > -jnp.inf) for i, m in zip(idx, mx)]
        sels = [jnp.where(p, 1.0, sel) for p, sel in zip(picks, sels)]
        scores = [jnp.where(p, -jnp.inf, s) for p, s in zip(picks, scores)]
    return sels


def _split3(x):
    hi = x.astype(BF16)
    r1 = x - hi.astype(F32)
    mid = r1.astype(BF16)
    lo = (r1 - mid.astype(F32)).astype(BF16)
    return hi, mid, lo


def _dot_split(x, w_bf16, terms):
    parts = _split3(x)[:terms]
    out = jnp.dot(parts[0], w_bf16, preferred_element_type=F32)
    for p in parts[1:]:
        out = out + jnp.dot(p, w_bf16, preferred_element_type=F32)
    return out


def _logf_cumsum_kernel(x_ref, tri_ref, low_ref, o_ref):
    x = x_ref[0]
    logf = jnp.minimum(x, 0.0) - jnp.log1p(jnp.exp(-jnp.abs(x)))
    within = _dot_split(logf, tri_ref[...], 3)
    hi, mid, lo = _split3(logf)
    low = low_ref[...]
    before = (jnp.dot(low, hi, preferred_element_type=F32) + jnp.dot(low, mid, preferred_element_type=F32)
              + jnp.dot(low, lo, preferred_element_type=F32))
    o_ref[0] = -(within + jnp.sum(before, axis=-1, keepdims=True))


def _neg_cum_logf(logits_rows):
    r, t = logits_rows.shape
    nc = t // LANES
    li = np.arange(LANES)
    tri = jnp.asarray((li[:, None] <= li[None, :]).astype(np.float32), BF16)
    ci = np.arange(nc)
    low = jnp.asarray((ci[None, :] < ci[:, None]).astype(np.float32), BF16)
    out = pl.pallas_call(
        _logf_cumsum_kernel,
        grid=(r,),
        in_specs=[pl.BlockSpec((1, nc, LANES), lambda i: (i, 0, 0)),
                  pl.BlockSpec((LANES, LANES), lambda i: (0, 0)),
                  pl.BlockSpec((nc, nc), lambda i: (0, 0))],
        out_specs=pl.BlockSpec((1, nc, LANES), lambda i: (i, 0, 0)),
        out_shape=jax.ShapeDtypeStruct((r, nc, LANES), F32),
        compiler_params=_cparams(("parallel",)),
        name="fox_logf_cumsum",
    )(logits_rows.reshape(r, nc, LANES), tri, low)
    return out.reshape(r, t)


def _chain_rows(c):
    return slice(c * Q_ROWS, (c + 1) * Q_ROWS)


def _causal_masks(off, wide, t0, strict=False):
    col = lax.broadcasted_iota(jnp.int32, (1, wide), 1) + off
    rows = [lax.broadcasted_iota(jnp.int32, (Q_ROWS, 1), 0) + (t0 + c * Q_ROWS) for c in range(Q_CHAINS)]
    return [(col < r) if strict else (col <= r) for r in rows]


def _fox_kernel(q_ref, k_ref, v_ref, nf_ref, o_ref, knorm_ref, m_ref, l_ref, acc_ref, *, wide):
    t0 = pl.program_id(2) * (Q_CHAINS * Q_ROWS)

    @pl.when(pl.program_id(2) == 0)
    def _():
        _max_key_norm(k_ref, knorm_ref)

    qs = [q_ref[0, _chain_rows(c), :] for c in range(Q_CHAINS)]
    q_norms = [_row_norms(q) for q in qs]
    states = [(m_ref.at[c], l_ref.at[c], acc_ref.at[c]) for c in range(Q_CHAINS)]
    _flash_init(m_ref, l_ref, acc_ref)

    def keep_going(first_done):
        prev = jnp.maximum(first_done // wide - 1, 0)
        cap = nf_ref[0, 0, pl.ds(prev, 1), :][:, wide - 1:wide] * (1.0 / ATTN_SCALE)
        return _softmax_alive(q_norms, knorm_ref[...], [cap] * Q_CHAINS, states)

    def tile(off, tail):
        bias = nf_ref[0, 0, pl.ds(off // wide, 1), :] * (1.0 / ATTN_SCALE)
        k = k_ref[0, pl.ds(off, wide), :]
        raws = [_qk_raw(q, k) + bias for q in qs]
        if tail:
            raws = [jnp.where(ok, raw, NEG_BIG) for ok, raw in zip(_causal_masks(off, wide, t0), raws)]
        _online_update_multi(raws, [v_ref[0, pl.ds(off, wide), :]] * Q_CHAINS, states)

    _sweep_tiles(t0, wide, tile, reverse=True, keep_going=keep_going)
    for c in range(Q_CHAINS):
        o_ref[0, _chain_rows(c), :] = acc_ref[c] / l_ref[c]


def _attn_scratch():
    return [pltpu.VMEM((Q_CHAINS, Q_ROWS, 1), F32), pltpu.VMEM((Q_CHAINS, Q_ROWS, 1), F32),
            pltpu.VMEM((Q_CHAINS, Q_ROWS, HEAD_DIM), F32)]


def _fox_attention(p3, neg_cum_f):
    b, t, _ = p3.shape
    tq = Q_CHAINS * Q_ROWS
    wide = min(KV_WIDE, t)
    nf = neg_cum_f.reshape(b, GROUP_HEADS, t // wide, wide)
    return pl.pallas_call(
        functools.partial(_fox_kernel, wide=wide),
        grid=(b, GROUP_HEADS, t // tq),
        in_specs=[pl.BlockSpec((1, tq, HEAD_DIM), lambda bi, h, i: (bi, i, COL["fq"] + h)),
                  pl.BlockSpec((1, t, HEAD_DIM), lambda bi, h, i: (bi, 0, COL["fk"] + h)),
                  pl.BlockSpec((1, t, HEAD_DIM), lambda bi, h, i: (bi, 0, COL["fv"] + h)),
                  pl.BlockSpec((1, 1, t // wide, wide), lambda bi, h, i: (bi, h, 0, 0))],
        out_specs=pl.BlockSpec((1, tq, HEAD_DIM), lambda bi, h, i: (bi, i, h)),
        out_shape=jax.ShapeDtypeStruct((b, t, GROUP_WIDTH), F32),
        scratch_shapes=[pltpu.VMEM((1, 1), F32)] + _attn_scratch(),
        compiler_params=_cparams(("parallel", "parallel", "arbitrary")),
        name="fox_attention",
    )(p3, p3, p3, nf)


def _neg_abs(x):
    bits = lax.bitcast_convert_type(x, jnp.uint32) | jnp.uint32(0x80000000)
    return lax.bitcast_convert_type(bits, F32)


def _stick_kernel(q_ref, k_ref, v_ref, u_ref, o_ref, r_ref, acc_ref, *, wide):
    t0 = pl.program_id(2) * (Q_CHAINS * Q_ROWS)
    chains = range(Q_CHAINS)
    qs = [q_ref[0, _chain_rows(c), :] for c in chains]
    r_ref[...] = jnp.zeros(r_ref.shape, F32)
    acc_ref[...] = jnp.zeros(acc_ref.shape, F32)
    u = u_ref[...]

    def tile(off, tail):
        k = k_ref[0, pl.ds(off, wide), :]
        z_all = [_qk_raw(q, k) * SCALE_LOG2E for q in qs]
        later = [r_ref[c] for c in chains]
        weights = [[] for _ in chains]
        for sb in reversed(range(wide // Q_ROWS)):
            cols = slice(sb * Q_ROWS, (sb + 1) * Q_ROWS)
            zs = [z[:, cols] for z in z_all]
            log_beta = [jnp.minimum(z, 0.0) - jnp.log(1.0 + jnp.exp2(_neg_abs(z))) * LOG2E for z in zs]
            log_keep = [lb - z for lb, z in zip(log_beta, zs)]
            if tail:
                before = _causal_masks(off + sb * Q_ROWS, Q_ROWS, t0, strict=True)
                log_keep = [jnp.where(ok, lk, 0.0) for ok, lk in zip(before, log_keep)]
            inside = [jnp.dot(jnp.concatenate(_split3(lk)[:2], axis=1), u, preferred_element_type=F32)
                      for lk in log_keep]
            a = [jnp.exp2(lb + cs + lt) for lb, cs, lt in zip(log_beta, inside, later)]
            if tail:
                a = [jnp.where(ok, x, 0.0) for ok, x in zip(before, a)]
            for c in chains:
                weights[c].append(a[c].astype(BF16))
            later = [lt + cs[:, 0:1] + lk[:, 0:1] for lt, cs, lk in zip(later, inside, log_keep)]
        v = v_ref[0, pl.ds(off, wide), :]
        for c in chains:
            acc_ref[c] += jnp.dot(jnp.concatenate(weights[c][::-1], axis=1), v, preferred_element_type=F32)
            r_ref[c] = later[c]

    def keep_going(first_done):
        del first_done
        return jnp.max(r_ref[...]) > -STICK_DEAD_LOG2

    _sweep_tiles(t0, wide, tile, reverse=True, keep_going=keep_going)
    for c in chains:
        o_ref[0, _chain_rows(c), :] = acc_ref[c]


def _stick_attention(p3):
    b, t, _ = p3.shape
    tq = Q_CHAINS * Q_ROWS
    wide = min(KV_WIDE, t)
    ki = np.arange(Q_ROWS)
    u = (ki[:, None] > ki[None, :]).astype(np.float32)
    u = jnp.asarray(np.concatenate([u, u], axis=0), BF16)
    return pl.pallas_call(
        functools.partial(_stick_kernel, wide=wide),
        grid=(b, GROUP_HEADS, t // tq),
        in_specs=[pl.BlockSpec((1, tq, HEAD_DIM), lambda bi, h, i: (bi, i, COL["sq"] + h)),
                  pl.BlockSpec((1, t, HEAD_DIM), lambda bi, h, i: (bi, 0, COL["sk"] + h)),
                  pl.BlockSpec((1, t, HEAD_DIM), lambda bi, h, i: (bi, 0, COL["sv"] + h)),
                  pl.BlockSpec((2 * Q_ROWS, Q_ROWS), lambda bi, h, i: (0, 0))],
        out_specs=pl.BlockSpec((1, tq, HEAD_DIM), lambda bi, h, i: (bi, i, h)),
        out_shape=jax.ShapeDtypeStruct((b, t, GROUP_WIDTH), F32),
        scratch_shapes=[pltpu.VMEM((Q_CHAINS, Q_ROWS, 1), F32), pltpu.VMEM((Q_CHAINS, Q_ROWS, HEAD_DIM), F32)],
        compiler_params=_cparams(("parallel", "parallel", "arbitrary")),
        name="stick_attention",
    )(p3, p3, p3, u)


def _moba_kernel(slope_ref, q_ref, k_ref, v_ref, o_ref, kaug_ref, kmean_ref, knorm_ref, m_ref, l_ref, acc_ref, *,
                 n_blk, wide):
    h = pl.program_id(1)
    i = pl.program_id(2)
    tq = MOBA_BLOCK
    slope = slope_ref[h]

    @pl.when(i == 0)
    def _():
        kmean_ref[...] = jnp.zeros(kmean_ref.shape, F32)
        kb = k_ref[0].astype(F32).reshape(n_blk, tq, HEAD_DIM)
        kmean_ref[0:n_blk, :] = jnp.mean(kb, axis=1)

        def key_aux(pos, lane):
            onehot = jnp.where(lane == jnp.right_shift(pos, int(math.log2(tq))), 1.0, 0.0)
            hi, mid, lo = _split3(slope * pos.astype(F32) * (1.0 / ATTN_SCALE))
            return jnp.where(lane == n_blk, hi.astype(F32),
                             jnp.where(lane == n_blk + 1, mid.astype(F32),
                                       jnp.where(lane == n_blk + 2, lo.astype(F32), onehot)))

        _fill_key_aug(kaug_ref, k_ref, key_aux)
        _max_key_norm(k_ref, knorm_ref)

    chains = range(Q_CHAINS)
    t0 = i * (Q_CHAINS * Q_ROWS)
    qs = [q_ref[0, _chain_rows(c), :] for c in chains]
    q_norms = [_row_norms(q) for q in qs]
    own = [i * Q_CHAINS + c for c in chains]
    kmean = kmean_ref[...].astype(BF16)
    blk = lax.broadcasted_iota(jnp.int32, (Q_ROWS, LANES), 1)
    blkf = blk.astype(F32)
    scores = [jnp.where(blk < own[c], _qk_raw(qs[c], kmean), -jnp.inf) for c in chains]
    sels = _topk_rounds(scores, blkf, [jnp.where(blk == own[c], 1.0, 0.0) for c in chains], MOBA_TOPK)
    q_aux = [jnp.where(blk < n_blk, jnp.where(sel > 0.5, 0.0, NEG_BIG), jnp.where(blk < n_blk + 3, 1.0, 0.0))
             for sel in sels]
    q_aug = [jnp.concatenate([q, aux.astype(BF16)], axis=1) for q, aux in zip(qs, q_aux)]
    states = [(m_ref.at[c], l_ref.at[c], acc_ref.at[c]) for c in chains]
    _flash_init(m_ref, l_ref, acc_ref)

    def tile(off, tail):
        k = kaug_ref[pl.ds(off, wide), :]
        raws = [_qk_raw(q, k) for q in q_aug]
        if tail:
            raws = [jnp.where(ok, raw, NEG_BIG) for ok, raw in zip(_causal_masks(off, wide, t0), raws)]
        _online_update_multi(raws, [v_ref[0, pl.ds(off, wide), :]] * Q_CHAINS, states)

    def keep_going(first_done):
        cap = slope * first_done.astype(F32) * (1.0 / ATTN_SCALE)
        return _softmax_alive(q_norms, knorm_ref[...], [cap] * Q_CHAINS, states)

    _sweep_tiles(t0, wide, tile, reverse=True, keep_going=keep_going)
    for c in chains:
        o_ref[0, _chain_rows(c), :] = acc_ref[c] / l_ref[c]


def _moba_attention(p3, slopes):
    b, t, _ = p3.shape
    tq = Q_CHAINS * Q_ROWS
    n_blk = t // MOBA_BLOCK
    wide = min(KV_WIDE, t)
    assert Q_ROWS == MOBA_BLOCK and t % tq == 0 and n_blk + 3 <= LANES
    return pl.pallas_call(
        functools.partial(_moba_kernel, n_blk=n_blk, wide=wide),
        grid=(b, GROUP_HEADS, t // tq),
        in_specs=[pl.BlockSpec(memory_space=pltpu.SMEM),
                  pl.BlockSpec((1, tq, HEAD_DIM), lambda bi, h, i: (bi, i, COL["mq"] + h)),
                  pl.BlockSpec((1, t, HEAD_DIM), lambda bi, h, i: (bi, 0, COL["mk"] + h)),
                  pl.BlockSpec((1, t, HEAD_DIM), lambda bi, h, i: (bi, 0, COL["mv"] + h))],
        out_specs=pl.BlockSpec((1, tq, HEAD_DIM), lambda bi, h, i: (bi, i, h)),
        out_shape=jax.ShapeDtypeStruct((b, t, GROUP_WIDTH), F32),
        scratch_shapes=[pltpu.VMEM((t, HEAD_DIM + LANES), BF16),
                        pltpu.VMEM((LANES, HEAD_DIM), F32), pltpu.VMEM((1, 1), F32)] + _attn_scratch(),
        compiler_params=_cparams(("parallel", "parallel", "arbitrary")),
        name="moba_attention",
    )(slopes, p3, p3, p3)


def _nsa_compress_kernel(x_ref, pos_ref, w1_ref, w2_ref, o_ref):
    x = (x_ref[0].astype(F32) + pos_ref[...]).astype(BF16)
    hmid = jnp.dot(x, w1_ref[...], preferred_element_type=F32)
    hmid = hmid / (1.0 + jnp.exp(-hmid))
    o_ref[0] = jnp.dot(hmid.astype(BF16), w2_ref[...], preferred_element_type=F32).astype(o_ref.dtype)


def _nsa_compress(blocks, pos, w1, w2):
    r, n, width = blocks.shape
    return pl.pallas_call(
        _nsa_compress_kernel,
        grid=(r,),
        in_specs=[pl.BlockSpec((1, n, width), lambda i: (i, 0, 0)),
                  pl.BlockSpec((1, width), lambda i: (0, 0)),
                  pl.BlockSpec((width, HEAD_DIM), lambda i: (0, 0)),
                  pl.BlockSpec((HEAD_DIM, HEAD_DIM), lambda i: (0, 0))],
        out_specs=pl.BlockSpec((1, n, HEAD_DIM), lambda i: (i, 0, 0)),
        out_shape=jax.ShapeDtypeStruct((r, n, HEAD_DIM), BF16),
        compiler_params=_cparams(("parallel",)),
        name="nsa_compress",
    )(blocks, pos.reshape(1, width).astype(F32), w1.astype(BF16), w2.astype(BF16))


def _nsa_select_kernel(slope_ref, q_ref, kc_ref, vc_ref, a_ref, oc_ref, sel_ref, *, tq, n_cmp, n_blk):
    g = pl.program_id(1)
    i = pl.program_id(2)
    t0 = i * tq
    kc = kc_ref[0, 0]
    vc = vc_ref[0, 0]
    ncp = kc.shape[0]
    tok = lax.broadcasted_iota(jnp.int32, (1, ncp), 1)
    cmp_end = tok * NSA_CMP_STRIDE + (NSA_CMP_LEN - 1)
    row = lax.broadcasted_iota(jnp.int32, (tq, 1), 0)
    admissible = (cmp_end <= row + t0) & (tok < n_cmp)
    rel_end = (cmp_end - t0).astype(F32)

    imp = jnp.zeros((tq, ncp), F32)
    for hh in range(NSA_Q_PER_KV):
        slope = slope_ref[g * NSA_Q_PER_KV + hh]
        q = q_ref[0, :, hh * HEAD_DIM:(hh + 1) * HEAD_DIM]
        s = jnp.where(admissible, _qk(q, kc) + slope * rel_end, NEG_BIG)
        m = jnp.max(s, axis=-1, keepdims=True)
        p = jnp.where(admissible, jnp.exp(s - m), 0.0)
        p = p / jnp.maximum(jnp.sum(p, axis=-1, keepdims=True), F32_TINY)
        oc_ref[0, :, hh * HEAD_DIM:(hh + 1) * HEAD_DIM] = jnp.dot(
            p.astype(BF16), vc, preferred_element_type=F32)
        imp = imp + p

    imp_blk = _dot_split(imp, a_ref[...], 3)
    blk = lax.broadcasted_iota(jnp.int32, (tq, LANES), 1)
    blkf = blk.astype(F32)
    jt = jnp.right_shift(row + t0, int(math.log2(NSA_SEL_BLOCK)))
    live = (blk <= jt) & (blk < n_blk)
    forced = ((blk == 0) | (blk == jt) | (blk == jt - 1)) & live
    sel = jnp.where(forced, 1.0, 0.0)
    score = jnp.where(live & jnp.logical_not(forced), imp_blk, -jnp.inf)
    sel = _topk_rounds([score], blkf, [sel], NSA_SEL_TOPK - 3)[0]
    sel_ref[0, 0] = sel.astype(sel_ref.dtype)


def _nsa_select(p3, kc, vc, slopes, tq):
    b, t, _ = p3.shape
    n_blk = t // NSA_SEL_BLOCK
    assert n_blk <= LANES and NSA_SEL_TOPK >= 3
    n_cmp = t // NSA_CMP_STRIDE - 1
    ncp = kc.shape[2]
    ratio = NSA_SEL_BLOCK // NSA_CMP_STRIDE
    ti = np.arange(ncp)[:, None]
    bj = np.arange(LANES)[None, :]
    amat = ((ti >= ratio * bj - 1) & (ti <= ratio * bj + ratio - 1) & (ti < n_cmp) & (bj < n_blk))
    amat = jnp.asarray(amat.astype(np.float32), BF16)
    width = NSA_Q_PER_KV * HEAD_DIM
    qblk = COL["nq"] * HEAD_DIM // width
    return pl.pallas_call(
        functools.partial(_nsa_select_kernel, tq=tq, n_cmp=n_cmp, n_blk=n_blk),
        grid=(b, NSA_KV_HEADS, t // tq),
        in_specs=[pl.BlockSpec(memory_space=pltpu.SMEM),
                  pl.BlockSpec((1, tq, width), lambda bi, g, i: (bi, i, qblk + g)),
                  pl.BlockSpec((1, 1, ncp, HEAD_DIM), lambda bi, g, i: (bi, g, 0, 0)),
                  pl.BlockSpec((1, 1, ncp, HEAD_DIM), lambda bi, g, i: (bi, g, 0, 0)),
                  pl.BlockSpec((ncp, LANES), lambda bi, g, i: (0, 0))],
        out_specs=[pl.BlockSpec((1, tq, width), lambda bi, g, i: (bi, i, g)),
                   pl.BlockSpec((1, 1, tq, LANES), lambda bi, g, i: (bi, g, i, 0))],
        out_shape=[jax.ShapeDtypeStruct((b, t, GROUP_WIDTH), F32),
                   jax.ShapeDtypeStruct((b, NSA_KV_HEADS, t, LANES), BF16)],
        compiler_params=_cparams(("parallel", "parallel", "parallel")),
        name="nsa_select",
    )(slopes, p3, kc, vc, amat)


def _nsa_attn_kernel(slope_ref, q_ref, ks_ref, vs_ref, kw_ref, vw_ref, sel_ref, oc_ref, gate_ref,
                     o_ref, kaug_ref, knorm_ref, m_ref, l_ref, acc_ref, *, tq, wide):
    g = pl.program_id(1)
    i = pl.program_id(2)
    t0 = pl.multiple_of(i * tq, tq)
    nh = NSA_Q_PER_KV

    @pl.when(i == 0)
    def _():
        sel_shift = int(math.log2(NSA_SEL_BLOCK))
        _fill_key_aug(kaug_ref, ks_ref,
                      lambda pos, lane: jnp.where(lane == jnp.right_shift(pos, sel_shift), 1.0, 0.0))
        _max_key_norm(ks_ref, knorm_ref)

    _flash_init(m_ref, l_ref, acc_ref)
    sel_bias = jnp.where(sel_ref[0, 0].astype(F32) > 0.5, 0.0, NEG_BIG).astype(BF16)
    qs = [q_ref[0, :, hh * HEAD_DIM:(hh + 1) * HEAD_DIM] for hh in range(nh)]
    q_aug = [jnp.concatenate([q, sel_bias], axis=1) for q in qs]
    slope_raw = [slope_ref[g * nh + hh] * (1.0 / ATTN_SCALE) for hh in range(nh)]
    row = lax.broadcasted_iota(jnp.int32, (tq, 1), 0)

    def rel_pos(off, width):
        return lax.broadcasted_iota(jnp.int32, (1, width), 1) + (off - t0)

    def sel_tile(off, tail):
        rel = rel_pos(off, wide)
        relf = rel.astype(F32)
        k = kaug_ref[pl.ds(off, wide), :]
        v = vs_ref[0, pl.ds(off, wide), :]
        raws = [_qk_raw(q_aug[hh], k) + slope_raw[hh] * relf for hh in range(nh)]
        if tail:
            raws = [jnp.where(rel <= row, raw, NEG_BIG) for raw in raws]
        _online_update_multi(raws, [v] * nh, sel_states)

    sel_states = [(m_ref.at[hh], l_ref.at[hh], acc_ref.at[hh]) for hh in range(nh)]
    q_norms = [_row_norms(q) for q in qs]

    def keep_going(first_done):
        gap = (first_done - t0).astype(F32)
        return _softmax_alive(q_norms, knorm_ref[...], [s * gap for s in slope_raw], sel_states)

    _sweep_tiles(t0, wide, sel_tile, reverse=True, keep_going=keep_going)

    def window(off, width, far):
        rel = rel_pos(off, width)
        relf = rel.astype(F32)
        k = kw_ref[0, pl.ds(off, width), :]
        v = vw_ref[0, pl.ds(off, width), :]
        ok = rel <= row
        if far:
            ok = ok & (row - rel < NSA_WINDOW)
        raws = [jnp.where(ok, _qk_raw(qs[hh], k) + slope_raw[hh] * relf, NEG_BIG) for hh in range(nh)]
        _online_update_multi(raws, [v] * nh,
                             [(m_ref.at[nh + hh], l_ref.at[nh + hh], acc_ref.at[nh + hh]) for hh in range(nh)])

    n_back = NSA_WINDOW // tq
    for back in range(n_back + 1):
        @pl.when((i == back) if back < n_back else (i >= back))
        def _(back=back):
            window(pl.multiple_of(t0 - back * tq, tq), (back + 1) * tq, far=(back == n_back))

    gates = gate_ref[0, 0]
    gates = 1.0 / (1.0 + jnp.exp(-gates))
    for hh in range(nh):
        c0 = hh * NSA_N_BRANCHES
        o_c = oc_ref[0, :, hh * HEAD_DIM:(hh + 1) * HEAD_DIM]
        o_s = acc_ref[hh] / l_ref[hh]
        o_w = acc_ref[nh + hh] / l_ref[nh + hh]
        o_ref[0, :, hh * HEAD_DIM:(hh + 1) * HEAD_DIM] = (
            gates[:, c0:c0 + 1] * o_c + gates[:, c0 + 1:c0 + 2] * o_s + gates[:, c0 + 2:c0 + 3] * o_w)


def _nsa_attention(p3, sel, o_cmp, gate_logits, slopes, tq):
    b, t, _ = p3.shape
    assert tq == KV_UNIT and NSA_WINDOW == 2 * tq and t // NSA_SEL_BLOCK <= LANES
    width = NSA_Q_PER_KV * HEAD_DIM
    qblk = COL["nq"] * HEAD_DIM // width
    kv = lambda name: pl.BlockSpec((1, t, HEAD_DIM), lambda bi, g, i: (bi, 0, COL[name] + g))
    nst = 2 * NSA_Q_PER_KV
    return pl.pallas_call(
        functools.partial(_nsa_attn_kernel, tq=tq, wide=min(KV_WIDE, t)),
        grid=(b, NSA_KV_HEADS, t // tq),
        in_specs=[pl.BlockSpec(memory_space=pltpu.SMEM),
                  pl.BlockSpec((1, tq, width), lambda bi, g, i: (bi, i, qblk + g)),
                  kv("nks"), kv("nvs"), kv("nkw"), kv("nvw"),
                  pl.BlockSpec((1, 1, tq, LANES), lambda bi, g, i: (bi, g, i, 0)),
                  pl.BlockSpec((1, tq, width), lambda bi, g, i: (bi, i, g)),
                  pl.BlockSpec((1, 1, tq, LANES), lambda bi, g, i: (bi, g, i, 0))],
        out_specs=pl.BlockSpec((1, tq, width), lambda bi, g, i: (bi, i, g)),
        out_shape=jax.ShapeDtypeStruct((b, t, GROUP_WIDTH), F32),
        scratch_shapes=[pltpu.VMEM((t, HEAD_DIM + LANES), BF16), pltpu.VMEM((1, 1), F32),
                        pltpu.VMEM((nst, tq, 1), F32), pltpu.VMEM((nst, tq, 1), F32),
                        pltpu.VMEM((nst, tq, HEAD_DIM), F32)],
        compiler_params=_cparams(("parallel", "parallel", "arbitrary")),
        name="nsa_attention",
    )(slopes, p3, p3, p3, p3, p3, sel, o_cmp, gate_logits)


def _nsa_blocks(p3, name):
    b, t, _ = p3.shape
    c0 = COL[name] * HEAD_DIM
    x = p3[:, :, c0:c0 + NSA_KV_HEADS * HEAD_DIM].reshape(b, t // NSA_CMP_STRIDE, NSA_CMP_STRIDE,
                                                         NSA_KV_HEADS, HEAD_DIM)
    x = x.transpose(0, 3, 1, 2, 4).reshape(b * NSA_KV_HEADS, t // NSA_CMP_STRIDE, NSA_CMP_STRIDE * HEAD_DIM)
    blocks = jnp.concatenate([x[:, :-1], x[:, 1:]], axis=-1)
    return jnp.pad(blocks, ((0, 0), (0, 1), (0, 0)))


def _alibi_slopes():
    n = 2 * GROUP_HEADS
    slopes = 2.0 ** (-8.0 * np.arange(1, n + 1) / n)
    return jnp.asarray(slopes[0::2], F32), jnp.asarray(slopes[1::2], F32)


def _regroup_w_in(w_in):
    main = jnp.concatenate([w_in[:, REF_OFF[n][0]:REF_OFF[n][0] + REF_OFF[n][1]] for n, _ in _MAIN_PIECES],
                           axis=1).astype(BF16)
    ng0, ngw = REF_OFF["ng"]
    ff0, ffw = REF_OFF["ff"]
    small = jnp.concatenate([w_in[:, ng0:ng0 + ngw], w_in[:, ff0:ff0 + ffw]], axis=1)
    small = jnp.pad(small, ((0, 0), (0, LANES - ngw - ffw))).astype(BF16)
    return main, small


def _mixer(x2d, b, t, norm_gain, w_in, forget_bias, pos_k, w1_k, w2_k, pos_v, w1_v, w2_v, group_gain, w_out):
    m = x2d.shape[0]
    h = _rmsnorm(x2d, norm_gain, BF16)
    w_main, w_small = _regroup_w_in(w_in)
    tm = _pick_tile(m, (1024, 512, 256))
    p3 = _matmul(h, w_main, BF16, tm, 512).reshape(b, t, MAIN_BLOCKS * HEAD_DIM)
    small = _matmul(h, w_small, F32, tm, LANES).reshape(b, t, LANES)
    slopes_moba, slopes_nsa = _alibi_slopes()

    o_a = _moba_attention(p3, slopes_moba)

    n_gate = NSA_N_BRANCHES * GROUP_HEADS
    kc = _nsa_compress(_nsa_blocks(p3, "nkc"), pos_k, w1_k, w2_k)
    vc = _nsa_compress(_nsa_blocks(p3, "nvc"), pos_v, w1_v, w2_v)
    ncp = kc.shape[1]
    kc = kc.reshape(b, NSA_KV_HEADS, ncp, HEAD_DIM)
    vc = vc.reshape(b, NSA_KV_HEADS, ncp, HEAD_DIM)
    o_cmp, sel = _nsa_select(p3, kc, vc, slopes_nsa, _pick_tile(t, (512, 256)))
    per_group = n_gate // NSA_KV_HEADS
    gate_logits = small[:, :, :n_gate].reshape(b, t, NSA_KV_HEADS, per_group).transpose(0, 2, 1, 3)
    gate_logits = jnp.pad(gate_logits, ((0, 0), (0, 0), (0, 0), (0, LANES - per_group)))
    o_b = _nsa_attention(p3, sel, o_cmp, gate_logits, slopes_nsa, KV_UNIT)

    ff = small[:, :, n_gate:n_gate + GROUP_HEADS] + forget_bias
    neg_cum_f = _neg_cum_logf(ff.transpose(0, 2, 1).reshape(b * GROUP_HEADS, t))
    o_c = _fox_attention(p3, neg_cum_f)

    o_d = _stick_attention(p3)

    on = _groupnorm([o.reshape(m, GROUP_WIDTH) for o in (o_a, o_b, o_c, o_d)], group_gain)
    return _matmul_residual(on, w_out, x2d, tm, 512)


def _cast_pad_kernel(w_ref, o_ref, *, n_row_tiles, cols):
    @pl.when(pl.program_id(0) < n_row_tiles)
    def _():
        o_ref[:, 0:cols] = w_ref[0].astype(BF16)
        if o_ref.shape[1] > cols:
            o_ref[:, cols:] = jnp.zeros((o_ref.shape[0], o_ref.shape[1] - cols), BF16)

    @pl.when(pl.program_id(0) >= n_row_tiles)
    def _():
        o_ref[...] = jnp.zeros(o_ref.shape, BF16)


def _weight_bf16(stacked, layer, rows_out, cols_out, tr=256):
    _, rows, cols = stacked.shape
    assert rows % tr == 0 and rows_out % tr == 0 and cols % LANES == 0
    n_row_tiles = rows // tr
    return pl.pallas_call(
        functools.partial(_cast_pad_kernel, n_row_tiles=n_row_tiles, cols=cols),
        grid=(rows_out // tr,),
        in_specs=[pl.BlockSpec((1, tr, cols), lambda j: (layer, jnp.minimum(j, n_row_tiles - 1), 0))],
        out_specs=pl.BlockSpec((tr, cols_out), lambda j: (j, 0)),
        out_shape=jax.ShapeDtypeStruct((rows_out, cols_out), BF16),
        compiler_params=_cparams(("parallel",)),
        name="weight_bf16",
    )(stacked)


def _pad_ffn(wg, wu, wd, layer):
    d, f = wg.shape[1:]
    fp = -(-f // FFN_TILE) * FFN_TILE
    return (_weight_bf16(wg, layer, d, fp), _weight_bf16(wu, layer, d, fp), _weight_bf16(wd, layer, fp, d))


def kernel(x, ffn1_norm, ffn1_w_gate, ffn1_w_up, ffn1_w_down, mix_norm, w_in, fox_forget_bias, nsa_cmp_pos_k, nsa_cmp_w1_k, nsa_cmp_w2_k, nsa_cmp_pos_v, nsa_cmp_w1_v, nsa_cmp_w2_v, group_norm, w_out, ffn2_norm, ffn2_w_gate, ffn2_w_up, ffn2_w_down, final_norm):
    b, t, d = x.shape
    x2d = x.reshape(b * t, d)
    for i in range(ffn1_norm.shape[0]):
        x2d = _ffn(x2d, ffn1_norm[i], *_pad_ffn(ffn1_w_gate, ffn1_w_up, ffn1_w_down, i))
        x2d = _mixer(x2d, b, t, mix_norm[i], w_in[i], fox_forget_bias[i], nsa_cmp_pos_k[i], nsa_cmp_w1_k[i],
                     nsa_cmp_w2_k[i], nsa_cmp_pos_v[i], nsa_cmp_w1_v[i], nsa_cmp_w2_v[i],
                     group_norm[i], _weight_bf16(w_out, i, d, d))
        x2d = _ffn(x2d, ffn2_norm[i], *_pad_ffn(ffn2_w_gate, ffn2_w_up, ffn2_w_down, i))
    return _rmsnorm(x2d, final_norm, F32).reshape(b, t, d)
```

```python
import functools
import math

import numpy as np
import jax
import jax.numpy as jnp
from jax import lax
from jax.experimental import pallas as pl
from jax.experimental.pallas import tpu as pltpu

F32 = jnp.float32
BF16 = jnp.bfloat16

HEAD_DIM = 128
GROUP_HEADS = 8
GROUP_WIDTH = GROUP_HEADS * HEAD_DIM
N_MIXERS = 4
FFN_RESIDUAL = 0.5
RMS_EPS = 1e-6

MOBA_BLOCK = 256
MOBA_TOPK = 3

NSA_KV_HEADS = 2
NSA_Q_PER_KV = 4
NSA_CMP_STRIDE = 16
NSA_CMP_LEN = 32
NSA_SEL_BLOCK = 64
NSA_SEL_TOPK = 16
NSA_WINDOW = 512
NSA_N_BRANCHES = 3

ATTN_SCALE = HEAD_DIM ** -0.5
LOG2E = math.log2(math.e)
SCALE_LOG2E = ATTN_SCALE * LOG2E
NEG_BIG = -1e30
STICK_DEAD_LOG2 = 160.0
FLASH_DEAD_LOG2 = 170.0
BOUND_SLACK = 1.001
KV_UNIT = 256
KV_WIDE = 1024
Q_ROWS = 256
Q_CHAINS = 4
F32_TINY = float(np.finfo(np.float32).tiny)

LANES = 128
VMEM_LIMIT = 56 * 1024 * 1024
GATEUP_TILE = 256
FFN_PAD = 512

_MAIN_PIECES = (
    ("mq", 8), ("mk", 8), ("mv", 8), ("nq", 8),
    ("nkc", 2), ("nvc", 2), ("nks", 2), ("nvs", 2), ("nkw", 2), ("nvw", 2),
    ("fq", 8), ("fk", 8), ("fv", 8), ("sq", 8), ("sk", 8), ("sv", 8),
)
COL = {}
_o = 0
for _n, _w in _MAIN_PIECES:
    COL[_n] = _o
    _o += _w
MAIN_BLOCKS = _o

_REF_SIZES = (
    ("mq", 1024), ("mk", 1024), ("mv", 1024), ("nq", 1024),
    ("nkc", 256), ("nvc", 256), ("nks", 256), ("nvs", 256), ("nkw", 256), ("nvw", 256),
    ("ng", 24),
    ("fq", 1024), ("fk", 1024), ("fv", 1024), ("ff", 8),
    ("sq", 1024), ("sk", 1024), ("sv", 1024),
)
REF_OFF = {}
_o = 0
for _n, _w in _REF_SIZES:
    REF_OFF[_n] = (_o, _w)
    _o += _w


def _cparams(sem):
    return pltpu.CompilerParams(dimension_semantics=sem, vmem_limit_bytes=VMEM_LIMIT)


def _rms_kernel(x_ref, g_ref, o_ref):
    x = x_ref[...]
    ms = jnp.mean(x * x, axis=-1, keepdims=True)
    o_ref[...] = (x * lax.rsqrt(ms + RMS_EPS) * g_ref[...]).astype(o_ref.dtype)


def _rmsnorm(x2d, gain, out_dtype, tm=256):
    m, d = x2d.shape
    return pl.pallas_call(
        _rms_kernel,
        grid=(m // tm,),
        in_specs=[pl.BlockSpec((tm, d), lambda i: (i, 0)),
                  pl.BlockSpec((1, d), lambda i: (0, 0))],
        out_specs=pl.BlockSpec((tm, d), lambda i: (i, 0)),
        out_shape=jax.ShapeDtypeStruct((m, d), out_dtype),
        compiler_params=_cparams(("parallel",)),
        name="rmsnorm",
    )(x2d, gain.reshape(1, d).astype(F32))


def _groupnorm_kernel(a_ref, b_ref, c_ref, d_ref, g_ref, o_ref):
    for gi, ref in enumerate((a_ref, b_ref, c_ref, d_ref)):
        x = ref[...]
        ms = jnp.mean(x * x, axis=-1, keepdims=True)
        lo, hi = gi * GROUP_WIDTH, (gi + 1) * GROUP_WIDTH
        o_ref[:, lo:hi] = (x * lax.rsqrt(ms + RMS_EPS) * g_ref[:, lo:hi]).astype(o_ref.dtype)


def _groupnorm(parts, gain, tm=256):
    m = parts[0].shape[0]
    d = GROUP_WIDTH * N_MIXERS
    part_spec = pl.BlockSpec((tm, GROUP_WIDTH), lambda i: (i, 0))
    return pl.pallas_call(
        _groupnorm_kernel,
        grid=(m // tm,),
        in_specs=[part_spec] * 4 + [pl.BlockSpec((1, d), lambda i: (0, 0))],
        out_specs=pl.BlockSpec((tm, d), lambda i: (i, 0)),
        out_shape=jax.ShapeDtypeStruct((m, d), BF16),
        compiler_params=_cparams(("parallel",)),
        name="groupnorm",
    )(*parts, gain.reshape(1, d).astype(F32))


def _mm_kernel(a_ref, w_ref, o_ref):
    o_ref[...] = jnp.dot(a_ref[...], w_ref[...], preferred_element_type=F32).astype(o_ref.dtype)


def _matmul(a, w, out_dtype, tm, tn):
    m, k = a.shape
    n = w.shape[1]
    return pl.pallas_call(
        _mm_kernel,
        grid=(m // tm, n // tn),
        in_specs=[pl.BlockSpec((tm, k), lambda i, j: (i, 0)),
                  pl.BlockSpec((k, tn), lambda i, j: (0, j))],
        out_specs=pl.BlockSpec((tm, tn), lambda i, j: (i, j)),
        out_shape=jax.ShapeDtypeStruct((m, n), out_dtype),
        compiler_params=_cparams(("parallel", "arbitrary")),
        name="matmul",
    )(a, w)


def _mm_res_kernel(a_ref, w_ref, x_ref, o_ref):
    o_ref[...] = x_ref[...] + jnp.dot(a_ref[...], w_ref[...], preferred_element_type=F32)


def _matmul_residual(a, w, x, tm, tn):
    m, k = a.shape
    n = w.shape[1]
    return pl.pallas_call(
        _mm_res_kernel,
        grid=(m // tm, n // tn),
        in_specs=[pl.BlockSpec((tm, k), lambda i, j: (i, 0)),
                  pl.BlockSpec((k, tn), lambda i, j: (0, j)),
                  pl.BlockSpec((tm, tn), lambda i, j: (i, j))],
        out_specs=pl.BlockSpec((tm, tn), lambda i, j: (i, j)),
        out_shape=jax.ShapeDtypeStruct((m, n), F32),
        compiler_params=_cparams(("parallel", "arbitrary")),
        name="matmul_residual",
    )(a, w, x)


def _gateup_kernel(h_ref, wg_ref, wu_ref, o_ref, *, n_valid):
    @pl.when(pl.program_id(1) < n_valid)
    def _():
        h = h_ref[...]
        g = jnp.dot(h, wg_ref[0].astype(BF16), preferred_element_type=F32)
        u = jnp.dot(h, wu_ref[0].astype(BF16), preferred_element_type=F32)
        o_ref[...] = (g / (1.0 + jnp.exp(-g)) * u).astype(o_ref.dtype)

    @pl.when(pl.program_id(1) >= n_valid)
    def _():
        o_ref[...] = jnp.zeros(o_ref.shape, o_ref.dtype)


def _gateup(h, wg, wu, layer, f_out, tm, tf):
    m, d = h.shape
    f = wg.shape[2]
    assert f % tf == 0 and f_out % tf == 0
    n_valid = f // tf
    w_spec = pl.BlockSpec((1, d, tf), lambda i, j: (layer, 0, jnp.minimum(j, n_valid - 1)))
    return pl.pallas_call(
        functools.partial(_gateup_kernel, n_valid=n_valid),
        grid=(m // tm, f_out // tf),
        in_specs=[pl.BlockSpec((tm, d), lambda i, j: (i, 0)), w_spec, w_spec],
        out_specs=pl.BlockSpec((tm, tf), lambda i, j: (i, j)),
        out_shape=jax.ShapeDtypeStruct((m, f_out), BF16),
        compiler_params=_cparams(("parallel", "arbitrary")),
        name="ffn_gateup",
    )(h, wg, wu)


def _down_kernel(a_ref, w_ref, x_ref, o_ref):
    part = FFN_RESIDUAL * jnp.dot(a_ref[...], w_ref[...], preferred_element_type=F32)

    @pl.when(pl.program_id(2) == 0)
    def _():
        o_ref[...] = x_ref[...] + part

    @pl.when(pl.program_id(2) != 0)
    def _():
        o_ref[...] += part


def _down(act, wd, x, tm, tn, tk):
    m, f = act.shape
    n = wd.shape[1]
    return pl.pallas_call(
        _down_kernel,
        grid=(m // tm, n // tn, f // tk),
        in_specs=[pl.BlockSpec((tm, tk), lambda i, j, k: (i, k)),
                  pl.BlockSpec((tk, tn), lambda i, j, k: (k, j)),
                  pl.BlockSpec((tm, tn), lambda i, j, k: (i, j))],
        out_specs=pl.BlockSpec((tm, tn), lambda i, j, k: (i, j)),
        out_shape=jax.ShapeDtypeStruct((m, n), F32),
        compiler_params=_cparams(("parallel", "parallel", "arbitrary")),
        name="ffn_down",
    )(act, wd, x)


def _pick_tile(n, prefs):
    for t in prefs:
        if n % t == 0:
            return t
    return n


def _ffn(x2d, gain, wg, wu, wd, layer):
    m, d = x2d.shape
    f = wg.shape[2]
    fp = -(-f // FFN_PAD) * FFN_PAD
    h = _rmsnorm(x2d, gain, BF16)
    tm = _pick_tile(m, (1024, 512, 256))
    act = _gateup(h, wg, wu, layer, fp, tm, _pick_tile(f, (GATEUP_TILE, 128)))
    return _down(act, _weight_bf16(wd, layer, fp, d), x2d, tm, _pick_tile(d, (1024, 512, 256, 128)),
                 _pick_tile(fp, (2816, 1024, 512, 256, 128)))


def _qk_raw(q, k):
    return lax.dot_general(q, k, (((1,), (1,)), ((), ())), preferred_element_type=F32)


def _qk(q, k):
    return _qk_raw(q, k) * ATTN_SCALE


def _online_update_multi(raws, vs, states):
    m_prev = [m_ref[...] for m_ref, _, _ in states]
    m_new = [jnp.maximum(mp, jnp.max(raw, axis=-1, keepdims=True)) for mp, raw in zip(m_prev, raws)]
    ps = [jnp.exp2((raw - mn) * SCALE_LOG2E) for raw, mn in zip(raws, m_new)]
    alphas = [jnp.exp2((mp - mn) * SCALE_LOG2E) for mp, mn in zip(m_prev, m_new)]
    pvs = [jnp.dot(p.astype(BF16), v, preferred_element_type=F32) for p, v in zip(ps, vs)]
    for (m_ref, l_ref, acc_ref), mn, a, p, pv in zip(states, m_new, alphas, ps, pvs):
        l_ref[...] = a * l_ref[...] + jnp.sum(p, axis=-1, keepdims=True)
        acc_ref[...] = a * acc_ref[...] + pv
        m_ref[...] = mn


def _flash_init(m_ref, l_ref, acc_ref):
    m_ref[...] = jnp.full(m_ref.shape, NEG_BIG, F32)
    l_ref[...] = jnp.zeros(l_ref.shape, F32)
    acc_ref[...] = jnp.zeros(acc_ref.shape, F32)


def _sweep_tiles(t0, wide, tail_tile, full_tile, keep_going):
    n_full = t0 // wide
    tail_off = pl.multiple_of(n_full * wide, wide)

    def full_off(jj):
        return pl.multiple_of((n_full - 1 - jj) * wide, wide)

    def step(state):
        jj, _ = state
        full_tile(full_off(jj))
        return jj + 1, keep_going(full_off(jj))

    tail_tile(tail_off)
    lax.while_loop(lambda s: (s[0] < n_full) & s[1], step, (jnp.int32(0), keep_going(tail_off)))


def _row_norms(q):
    q = q.astype(F32)
    return jnp.sqrt(jnp.sum(q * q, axis=-1, keepdims=True))


def _max_key_norm(k_ref, out_ref):
    def body(c, mx):
        k = k_ref[0, pl.ds(pl.multiple_of(c * KV_UNIT, KV_UNIT), KV_UNIT), :].astype(F32)
        return jnp.maximum(mx, jnp.max(jnp.sum(k * k, axis=-1, keepdims=True), axis=0, keepdims=True))

    out_ref[...] = jnp.sqrt(lax.fori_loop(0, k_ref.shape[1] // KV_UNIT, body, jnp.zeros((1, 1), F32)))


def _softmax_alive(q_norms, k_norm, bias_caps, states):
    gaps = [qn * (k_norm * BOUND_SLACK) + cap - m_ref[...]
            for qn, cap, (m_ref, _, _) in zip(q_norms, bias_caps, states)]
    worst = jnp.max(functools.reduce(jnp.maximum, gaps))
    return (worst + 1.0) * SCALE_LOG2E > -FLASH_DEAD_LOG2


def _fill_key_aug(kaug_ref, k_ref, aux_fn):
    t = kaug_ref.shape[0]

    def body(c, carry):
        off = pl.multiple_of(c * KV_UNIT, KV_UNIT)
        pos = lax.broadcasted_iota(jnp.int32, (KV_UNIT, LANES), 0) + off
        lane = lax.broadcasted_iota(jnp.int32, (KV_UNIT, LANES), 1)
        kaug_ref[pl.ds(off, KV_UNIT), 0:HEAD_DIM] = k_ref[0, pl.ds(off, KV_UNIT), :]
        kaug_ref[pl.ds(off, KV_UNIT), HEAD_DIM:] = aux_fn(pos, lane).astype(BF16)
        return carry

    lax.fori_loop(0, t // KV_UNIT, body, 0)


def _topk_rounds(scores, blkf, sels, rounds):
    for _ in range(rounds):
        mx = [jnp.max(s, axis=-1, keepdims=True) for s in scores]
        idx = [jnp.min(jnp.where(s == m, blkf, 1e9), axis=-1, keepdims=True) for s, m in zip(scores, mx)]
        picks = [(blkf == i) & (m > -jnp.inf) for i, m in zip(idx, mx)]
        sels = [jnp.where(p, 1.0, sel) for p, sel in zip(picks, sels)]
        scores = [jnp.where(p, -jnp.inf, s) for p, s in zip(picks, scores)]
    return sels


def _split3(x):
    hi = x.astype(BF16)
    r1 = x - hi.astype(F32)
    mid = r1.astype(BF16)
    lo = (r1 - mid.astype(F32)).astype(BF16)
    return hi, mid, lo


def _dot_split(x, w_bf16, terms):
    parts = _split3(x)[:terms]
    out = jnp.dot(parts[0], w_bf16, preferred_element_type=F32)
    for p in parts[1:]:
        out = out + jnp.dot(p, w_bf16, preferred_element_type=F32)
    return out


def _logf_cumsum_kernel(x_ref, tri_ref, low_ref, o_ref):
    x = x_ref[0]
    logf = jnp.minimum(x, 0.0) - jnp.log1p(jnp.exp(-jnp.abs(x)))
    within = _dot_split(logf, tri_ref[...], 3)
    hi, mid, lo = _split3(logf)
    low = low_ref[...]
    before = (jnp.dot(low, hi, preferred_element_type=F32) + jnp.dot(low, mid, preferred_element_type=F32)
              + jnp.dot(low, lo, preferred_element_type=F32))
    o_ref[0] = -(within + jnp.sum(before, axis=-1, keepdims=True))


def _neg_cum_logf(logits_rows):
    r, t = logits_rows.shape
    nc = t // LANES
    li = np.arange(LANES)
    tri = jnp.asarray((li[:, None] <= li[None, :]).astype(np.float32), BF16)
    ci = np.arange(nc)
    low = jnp.asarray((ci[None, :] < ci[:, None]).astype(np.float32), BF16)
    out = pl.pallas_call(
        _logf_cumsum_kernel,
        grid=(r,),
        in_specs=[pl.BlockSpec((1, nc, LANES), lambda i: (i, 0, 0)),
                  pl.BlockSpec((LANES, LANES), lambda i: (0, 0)),
                  pl.BlockSpec((nc, nc), lambda i: (0, 0))],
        out_specs=pl.BlockSpec((1, nc, LANES), lambda i: (i, 0, 0)),
        out_shape=jax.ShapeDtypeStruct((r, nc, LANES), F32),
        compiler_params=_cparams(("parallel",)),
        name="fox_logf_cumsum",
    )(logits_rows.reshape(r, nc, LANES), tri, low)
    return out.reshape(r, t)


def _chain_rows(c):
    return slice(c * Q_ROWS, (c + 1) * Q_ROWS)


def _own_causal(c, width, strict=False):
    col = lax.broadcasted_iota(jnp.int32, (1, width), 1)
    row = lax.broadcasted_iota(jnp.int32, (Q_ROWS, 1), 0) + c * Q_ROWS
    return (col < row) if strict else (col <= row)


def _fox_kernel(q_ref, k_ref, v_ref, nf_ref, o_ref, knorm_ref, m_ref, l_ref, acc_ref, *, wide):
    t0 = pl.program_id(2) * (Q_CHAINS * Q_ROWS)

    @pl.when(pl.program_id(2) == 0)
    def _():
        _max_key_norm(k_ref, knorm_ref)

    qs = [q_ref[0, _chain_rows(c), :] for c in range(Q_CHAINS)]
    q_norms = [_row_norms(q) for q in qs]
    states = [(m_ref.at[c], l_ref.at[c], acc_ref.at[c]) for c in range(Q_CHAINS)]
    _flash_init(m_ref, l_ref, acc_ref)

    def keep_going(first_done):
        prev = jnp.maximum(first_done // wide - 1, 0)
        cap = nf_ref[0, 0, pl.ds(prev, 1), :][:, wide - 1:wide] * (1.0 / ATTN_SCALE)
        return _softmax_alive(q_norms, knorm_ref[...], [cap] * Q_CHAINS, states)

    def full_tile(off):
        bias = nf_ref[0, 0, pl.ds(off // wide, 1), :] * (1.0 / ATTN_SCALE)
        k = k_ref[0, pl.ds(off, wide), :]
        raws = [_qk_raw(q, k) + bias for q in qs]
        _online_update_multi(raws, [v_ref[0, pl.ds(off, wide), :]] * Q_CHAINS, states)

    def tail_tile(off):
        bias = nf_ref[0, 0, pl.ds(off // wide, 1), :] * (1.0 / ATTN_SCALE)
        k = k_ref[0, pl.ds(off, wide), :]
        v = v_ref[0, pl.ds(off, wide), :]
        ends = [(c + 1) * Q_ROWS for c in range(Q_CHAINS)]
        raws = [jnp.where(_own_causal(c, e), _qk_raw(q, k[:e]) + bias[:, :e], NEG_BIG)
                for c, (q, e) in enumerate(zip(qs, ends))]
        _online_update_multi(raws, [v[:e] for e in ends], states)

    _sweep_tiles(t0, wide, tail_tile, full_tile, keep_going)
    for c in range(Q_CHAINS):
        o_ref[0, _chain_rows(c), :] = acc_ref[c] / l_ref[c]


def _attn_scratch():
    return [pltpu.VMEM((Q_CHAINS, Q_ROWS, 1), F32), pltpu.VMEM((Q_CHAINS, Q_ROWS, 1), F32),
            pltpu.VMEM((Q_CHAINS, Q_ROWS, HEAD_DIM), F32)]


def _fox_attention(p3, neg_cum_f):
    b, t, _ = p3.shape
    tq = Q_CHAINS * Q_ROWS
    wide = tq
    nf = neg_cum_f.reshape(b, GROUP_HEADS, t // wide, wide)
    return pl.pallas_call(
        functools.partial(_fox_kernel, wide=wide),
        grid=(b, GROUP_HEADS, t // tq),
        in_specs=[pl.BlockSpec((1, tq, HEAD_DIM), lambda bi, h, i: (bi, i, COL["fq"] + h)),
                  pl.BlockSpec((1, t, HEAD_DIM), lambda bi, h, i: (bi, 0, COL["fk"] + h)),
                  pl.BlockSpec((1, t, HEAD_DIM), lambda bi, h, i: (bi, 0, COL["fv"] + h)),
                  pl.BlockSpec((1, 1, t // wide, wide), lambda bi, h, i: (bi, h, 0, 0))],
        out_specs=pl.BlockSpec((1, tq, HEAD_DIM), lambda bi, h, i: (bi, i, h)),
        out_shape=jax.ShapeDtypeStruct((b, t, GROUP_WIDTH), F32),
        scratch_shapes=[pltpu.VMEM((1, 1), F32)] + _attn_scratch(),
        compiler_params=_cparams(("parallel", "parallel", "arbitrary")),
        name="fox_attention",
    )(p3, p3, p3, nf)


def _neg_abs(x):
    bits = lax.bitcast_convert_type(x, jnp.uint32) | jnp.uint32(0x80000000)
    return lax.bitcast_convert_type(bits, F32)


def _stick_kernel(q_ref, k_ref, v_ref, u_ref, o_ref, r_ref, acc_ref, *, wide):
    t0 = pl.program_id(2) * (Q_CHAINS * Q_ROWS)
    chains = range(Q_CHAINS)
    qs = [q_ref[0, _chain_rows(c), :] for c in chains]
    r_ref[...] = jnp.zeros(r_ref.shape, F32)
    acc_ref[...] = jnp.zeros(acc_ref.shape, F32)
    u = u_ref[...]

    def tile(off, tail):
        ends = [(c + 1) * Q_ROWS if tail else wide for c in chains]
        k = k_ref[0, pl.ds(off, wide), :]
        z_all = [_qk_raw(q, k[:e]) * SCALE_LOG2E for q, e in zip(qs, ends)]
        later = [r_ref[c] for c in chains]
        weights = [[] for _ in chains]
        before = _own_causal(0, Q_ROWS, strict=True)
        for sb in reversed(range(wide // Q_ROWS)):
            live = [c for c in chains if ends[c] > sb * Q_ROWS]
            diag = [tail and c == sb for c in live]
            cols = slice(sb * Q_ROWS, (sb + 1) * Q_ROWS)
            zs = [z_all[c][:, cols] for c in live]
            log_beta = [jnp.minimum(z, 0.0) - jnp.log(1.0 + jnp.exp2(_neg_abs(z))) * LOG2E for z in zs]
            log_keep = [lb - z for lb, z in zip(log_beta, zs)]
            log_keep = [jnp.where(before, lk, 0.0) if d else lk for d, lk in zip(diag, log_keep)]
            inside = [jnp.dot(jnp.concatenate(_split3(lk)[:2], axis=1), u, preferred_element_type=F32)
                      for lk in log_keep]
            a = [jnp.exp2(lb + cs + later[c]) for c, lb, cs in zip(live, log_beta, inside)]
            a = [jnp.where(before, x, 0.0) if d else x for d, x in zip(diag, a)]
            for c, x, cs, lk in zip(live, a, inside, log_keep):
                weights[c].append(x.astype(BF16))
                later[c] = later[c] + cs[:, 0:1] + lk[:, 0:1]
        v = v_ref[0, pl.ds(off, wide), :]
        for c in chains:
            w = weights[c][0] if len(weights[c]) == 1 else jnp.concatenate(weights[c][::-1], axis=1)
            acc_ref[c] += jnp.dot(w, v[:ends[c]], preferred_element_type=F32)
            r_ref[c] = later[c]

    def keep_going(first_done):
        del first_done
        return jnp.max(r_ref[...]) > -STICK_DEAD_LOG2

    _sweep_tiles(t0, wide, functools.partial(tile, tail=True), functools.partial(tile, tail=False), keep_going)
    for c in chains:
        o_ref[0, _chain_rows(c), :] = acc_ref[c]


def _stick_attention(p3):
    b, t, _ = p3.shape
    tq = Q_CHAINS * Q_ROWS
    wide = tq
    ki = np.arange(Q_ROWS)
    u = (ki[:, None] > ki[None, :]).astype(np.float32)
    u = jnp.asarray(np.concatenate([u, u], axis=0), BF16)
    return pl.pallas_call(
        functools.partial(_stick_kernel, wide=wide),
        grid=(b, GROUP_HEADS, t // tq),
        in_specs=[pl.BlockSpec((1, tq, HEAD_DIM), lambda bi, h, i: (bi, i, COL["sq"] + h)),
                  pl.BlockSpec((1, t, HEAD_DIM), lambda bi, h, i: (bi, 0, COL["sk"] + h)),
                  pl.BlockSpec((1, t, HEAD_DIM), lambda bi, h, i: (bi, 0, COL["sv"] + h)),
                  pl.BlockSpec((2 * Q_ROWS, Q_ROWS), lambda bi, h, i: (0, 0))],
        out_specs=pl.BlockSpec((1, tq, HEAD_DIM), lambda bi, h, i: (bi, i, h)),
        out_shape=jax.ShapeDtypeStruct((b, t, GROUP_WIDTH), F32),
        scratch_shapes=[pltpu.VMEM((Q_CHAINS, Q_ROWS, 1), F32), pltpu.VMEM((Q_CHAINS, Q_ROWS, HEAD_DIM), F32)],
        compiler_params=_cparams(("parallel", "parallel", "arbitrary")),
        name="stick_attention",
    )(p3, p3, p3, u)


def _moba_kernel(slope_ref, q_ref, k_ref, v_ref, o_ref, kaug_ref, kmean_ref, knorm_ref, m_ref, l_ref, acc_ref, *,
                 n_blk, wide):
    h = pl.program_id(1)
    i = pl.program_id(2)
    tq = MOBA_BLOCK
    slope = slope_ref[h]

    @pl.when(i == 0)
    def _():
        kmean_ref[...] = jnp.zeros(kmean_ref.shape, F32)
        kb = k_ref[0].astype(F32).reshape(n_blk, tq, HEAD_DIM)
        kmean_ref[0:n_blk, :] = jnp.mean(kb, axis=1)

        def key_aux(pos, lane):
            onehot = jnp.where(lane == jnp.right_shift(pos, int(math.log2(tq))), 1.0, 0.0)
            hi, mid, lo = _split3(slope * pos.astype(F32) * (1.0 / ATTN_SCALE))
            return jnp.where(lane == n_blk, hi.astype(F32),
                             jnp.where(lane == n_blk + 1, mid.astype(F32),
                                       jnp.where(lane == n_blk + 2, lo.astype(F32), onehot)))

        _fill_key_aug(kaug_ref, k_ref, key_aux)
        _max_key_norm(k_ref, knorm_ref)

    chains = range(Q_CHAINS)
    t0 = i * (Q_CHAINS * Q_ROWS)
    qs = [q_ref[0, _chain_rows(c), :] for c in chains]
    q_norms = [_row_norms(q) for q in qs]
    own = [i * Q_CHAINS + c for c in chains]
    kmean = kmean_ref[...].astype(BF16)
    blk = lax.broadcasted_iota(jnp.int32, (Q_ROWS, LANES), 1)
    blkf = blk.astype(F32)
    scores = [jnp.where(blk < own[c], _qk_raw(qs[c], kmean), -jnp.inf) for c in chains]
    sels = _topk_rounds(scores, blkf, [jnp.where(blk == own[c], 1.0, 0.0) for c in chains], MOBA_TOPK)
    q_aux = [jnp.where(blk < n_blk, jnp.where(sel > 0.5, 0.0, NEG_BIG), jnp.where(blk < n_blk + 3, 1.0, 0.0))
             for sel in sels]
    q_aug = [jnp.concatenate([q, aux.astype(BF16)], axis=1) for q, aux in zip(qs, q_aux)]
    states = [(m_ref.at[c], l_ref.at[c], acc_ref.at[c]) for c in chains]
    _flash_init(m_ref, l_ref, acc_ref)

    def full_tile(off):
        k = kaug_ref[pl.ds(off, wide), :]
        _online_update_multi([_qk_raw(q, k) for q in q_aug], [v_ref[0, pl.ds(off, wide), :]] * Q_CHAINS, states)

    def tail_tile(off):
        k = kaug_ref[pl.ds(off, wide), :]
        v = v_ref[0, pl.ds(off, wide), :]
        ends = [(c + 1) * Q_ROWS for c in chains]
        raws = [jnp.where(_own_causal(c, e), _qk_raw(q, k[:e]), NEG_BIG) for c, (q, e) in enumerate(zip(q_aug, ends))]
        _online_update_multi(raws, [v[:e] for e in ends], states)

    def keep_going(first_done):
        cap = slope * first_done.astype(F32) * (1.0 / ATTN_SCALE)
        return _softmax_alive(q_norms, knorm_ref[...], [cap] * Q_CHAINS, states)

    _sweep_tiles(t0, wide, tail_tile, full_tile, keep_going)
    for c in chains:
        o_ref[0, _chain_rows(c), :] = acc_ref[c] / l_ref[c]


def _moba_attention(p3, slopes):
    b, t, _ = p3.shape
    tq = Q_CHAINS * Q_ROWS
    n_blk = t // MOBA_BLOCK
    wide = tq
    assert Q_ROWS == MOBA_BLOCK and t % tq == 0 and n_blk + 3 <= LANES
    return pl.pallas_call(
        functools.partial(_moba_kernel, n_blk=n_blk, wide=wide),
        grid=(b, GROUP_HEADS, t // tq),
        in_specs=[pl.BlockSpec(memory_space=pltpu.SMEM),
                  pl.BlockSpec((1, tq, HEAD_DIM), lambda bi, h, i: (bi, i, COL["mq"] + h)),
                  pl.BlockSpec((1, t, HEAD_DIM), lambda bi, h, i: (bi, 0, COL["mk"] + h)),
                  pl.BlockSpec((1, t, HEAD_DIM), lambda bi, h, i: (bi, 0, COL["mv"] + h))],
        out_specs=pl.BlockSpec((1, tq, HEAD_DIM), lambda bi, h, i: (bi, i, h)),
        out_shape=jax.ShapeDtypeStruct((b, t, GROUP_WIDTH), F32),
        scratch_shapes=[pltpu.VMEM((t, HEAD_DIM + LANES), BF16),
                        pltpu.VMEM((LANES, HEAD_DIM), F32), pltpu.VMEM((1, 1), F32)] + _attn_scratch(),
        compiler_params=_cparams(("parallel", "parallel", "arbitrary")),
        name="moba_attention",
    )(slopes, p3, p3, p3)


def _nsa_compress_kernel(x_ref, pos_ref, w1_ref, w2_ref, o_ref):
    x = (x_ref[0].astype(F32) + pos_ref[...]).astype(BF16)
    hmid = jnp.dot(x, w1_ref[...], preferred_element_type=F32)
    hmid = hmid / (1.0 + jnp.exp(-hmid))
    o_ref[0] = jnp.dot(hmid.astype(BF16), w2_ref[...], preferred_element_type=F32).astype(o_ref.dtype)


def _nsa_compress(blocks, pos, w1, w2):
    r, n, width = blocks.shape
    return pl.pallas_call(
        _nsa_compress_kernel,
        grid=(r,),
        in_specs=[pl.BlockSpec((1, n, width), lambda i: (i, 0, 0)),
                  pl.BlockSpec((1, width), lambda i: (0, 0)),
                  pl.BlockSpec((width, HEAD_DIM), lambda i: (0, 0)),
                  pl.BlockSpec((HEAD_DIM, HEAD_DIM), lambda i: (0, 0))],
        out_specs=pl.BlockSpec((1, n, HEAD_DIM), lambda i: (i, 0, 0)),
        out_shape=jax.ShapeDtypeStruct((r, n, HEAD_DIM), BF16),
        compiler_params=_cparams(("parallel",)),
        name="nsa_compress",
    )(blocks, pos.reshape(1, width).astype(F32), w1.astype(BF16), w2.astype(BF16))


def _nsa_select_kernel(slope_ref, q_ref, kc_ref, vc_ref, a_ref, oc_ref, sel_ref, *, tq, n_cmp, n_blk):
    g = pl.program_id(1)
    i = pl.program_id(2)
    t0 = i * tq
    kc = kc_ref[0, 0]
    vc = vc_ref[0, 0]
    ncp = kc.shape[0]
    tok = lax.broadcasted_iota(jnp.int32, (1, ncp), 1)
    cmp_end = tok * NSA_CMP_STRIDE + (NSA_CMP_LEN - 1)
    row = lax.broadcasted_iota(jnp.int32, (tq, 1), 0)
    admissible = (cmp_end <= row + t0) & (tok < n_cmp)
    rel_end = (cmp_end - t0).astype(F32)

    imp = jnp.zeros((tq, ncp), F32)
    for hh in range(NSA_Q_PER_KV):
        slope = slope_ref[g * NSA_Q_PER_KV + hh]
        q = q_ref[0, :, hh * HEAD_DIM:(hh + 1) * HEAD_DIM]
        s = jnp.where(admissible, _qk(q, kc) + slope * rel_end, NEG_BIG)
        m = jnp.max(s, axis=-1, keepdims=True)
        p = jnp.where(admissible, jnp.exp(s - m), 0.0)
        p = p / jnp.maximum(jnp.sum(p, axis=-1, keepdims=True), F32_TINY)
        oc_ref[0, :, hh * HEAD_DIM:(hh + 1) * HEAD_DIM] = jnp.dot(
            p.astype(BF16), vc, preferred_element_type=F32)
        imp = imp + p

    imp_blk = _dot_split(imp, a_ref[...], 3)
    blk = lax.broadcasted_iota(jnp.int32, (tq, LANES), 1)
    blkf = blk.astype(F32)
    jt = jnp.right_shift(row + t0, int(math.log2(NSA_SEL_BLOCK)))
    live = (blk <= jt) & (blk < n_blk)
    forced = ((blk == 0) | (blk == jt) | (blk == jt - 1)) & live
    sel = jnp.where(forced, 1.0, 0.0)
    score = jnp.where(live & jnp.logical_not(forced), imp_blk, -jnp.inf)
    sel = _topk_rounds([score], blkf, [sel], NSA_SEL_TOPK - 3)[0]
    sel_ref[0, 0] = sel.astype(sel_ref.dtype)


def _nsa_select(p3, kc, vc, slopes, tq):
    b, t, _ = p3.shape
    n_blk = t // NSA_SEL_BLOCK
    assert n_blk <= LANES and NSA_SEL_TOPK >= 3
    n_cmp = t // NSA_CMP_STRIDE - 1
    ncp = kc.shape[2]
    ratio = NSA_SEL_BLOCK // NSA_CMP_STRIDE
    ti = np.arange(ncp)[:, None]
    bj = np.arange(LANES)[None, :]
    amat = ((ti >= ratio * bj - 1) & (ti <= ratio * bj + ratio - 1) & (ti < n_cmp) & (bj < n_blk))
    amat = jnp.asarray(amat.astype(np.float32), BF16)
    width = NSA_Q_PER_KV * HEAD_DIM
    qblk = COL["nq"] * HEAD_DIM // width
    return pl.pallas_call(
        functools.partial(_nsa_select_kernel, tq=tq, n_cmp=n_cmp, n_blk=n_blk),
        grid=(b, NSA_KV_HEADS, t // tq),
        in_specs=[pl.BlockSpec(memory_space=pltpu.SMEM),
                  pl.BlockSpec((1, tq, width), lambda bi, g, i: (bi, i, qblk + g)),
                  pl.BlockSpec((1, 1, ncp, HEAD_DIM), lambda bi, g, i: (bi, g, 0, 0)),
                  pl.BlockSpec((1, 1, ncp, HEAD_DIM), lambda bi, g, i: (bi, g, 0, 0)),
                  pl.BlockSpec((ncp, LANES), lambda bi, g, i: (0, 0))],
        out_specs=[pl.BlockSpec((1, tq, width), lambda bi, g, i: (bi, i, g)),
                   pl.BlockSpec((1, 1, tq, LANES), lambda bi, g, i: (bi, g, i, 0))],
        out_shape=[jax.ShapeDtypeStruct((b, t, GROUP_WIDTH), F32),
                   jax.ShapeDtypeStruct((b, NSA_KV_HEADS, t, LANES), BF16)],
        compiler_params=_cparams(("parallel", "parallel", "parallel")),
        name="nsa_select",
    )(slopes, p3, kc, vc, amat)


def _nsa_attn_kernel(slope_ref, q_ref, ks_ref, vs_ref, kw_ref, vw_ref, sel_ref, oc_ref, gate_ref,
                     o_ref, kaug_ref, knorm_ref, m_ref, l_ref, acc_ref, *, tq, wide):
    g = pl.program_id(1)
    i = pl.program_id(2)
    t0 = pl.multiple_of(i * tq, tq)
    nh = NSA_Q_PER_KV

    @pl.when(i == 0)
    def _():
        sel_shift = int(math.log2(NSA_SEL_BLOCK))
        _fill_key_aug(kaug_ref, ks_ref,
                      lambda pos, lane: jnp.where(lane == jnp.right_shift(pos, sel_shift), 1.0, 0.0))
        _max_key_norm(ks_ref, knorm_ref)

    _flash_init(m_ref, l_ref, acc_ref)
    sel_bias = jnp.where(sel_ref[0, 0].astype(F32) > 0.5, 0.0, NEG_BIG).astype(BF16)
    qs = [q_ref[0, :, hh * HEAD_DIM:(hh + 1) * HEAD_DIM] for hh in range(nh)]
    q_aug = [jnp.concatenate([q, sel_bias], axis=1) for q in qs]
    slope_raw = [slope_ref[g * nh + hh] * (1.0 / ATTN_SCALE) for hh in range(nh)]
    row = lax.broadcasted_iota(jnp.int32, (tq, 1), 0)

    def rel_pos(off, width):
        return lax.broadcasted_iota(jnp.int32, (1, width), 1) + (off - t0)

    sel_states = [(m_ref.at[hh], l_ref.at[hh], acc_ref.at[hh]) for hh in range(nh)]
    win_states = [(m_ref.at[nh + hh], l_ref.at[nh + hh], acc_ref.at[nh + hh]) for hh in range(nh)]
    q_norms = [_row_norms(q) for q in qs]

    def sel_scores(off, tail):
        rel = rel_pos(off, wide)
        relf = rel.astype(F32)
        k = kaug_ref[pl.ds(off, wide), :]
        raws = [_qk_raw(q_aug[hh], k) + slope_raw[hh] * relf for hh in range(nh)]
        if tail:
            raws = [jnp.where(rel <= row, raw, NEG_BIG) for raw in raws]
        return raws, [vs_ref[0, pl.ds(off, wide), :]] * nh

    def window_scores(off, width, far):
        rel = rel_pos(off, width)
        relf = rel.astype(F32)
        k = kw_ref[0, pl.ds(off, width), :]
        ok = rel <= row
        if far:
            ok = ok & (row - rel < NSA_WINDOW)
        raws = [jnp.where(ok, _qk_raw(qs[hh], k) + slope_raw[hh] * relf, NEG_BIG) for hh in range(nh)]
        return raws, [vw_ref[0, pl.ds(off, width), :]] * nh

    def sel_tile(off, tail):
        raws, vs = sel_scores(off, tail)
        _online_update_multi(raws, vs, sel_states)

    def keep_going(first_done):
        gap = (first_done - t0).astype(F32)
        return _softmax_alive(q_norms, knorm_ref[...], [s * gap for s in slope_raw], sel_states)

    _sweep_tiles(t0, wide, functools.partial(sel_tile, tail=True), functools.partial(sel_tile, tail=False),
                 keep_going)

    n_back = NSA_WINDOW // tq
    for back in range(n_back + 1):
        @pl.when((i == back) if back < n_back else (i >= back))
        def _(back=back):
            raws, vs = window_scores(pl.multiple_of(t0 - back * tq, tq), (back + 1) * tq, back == n_back)
            _online_update_multi(raws, vs, win_states)

    gates = gate_ref[0, 0]
    gates = 1.0 / (1.0 + jnp.exp(-gates))
    for hh in range(nh):
        c0 = hh * NSA_N_BRANCHES
        o_c = oc_ref[0, :, hh * HEAD_DIM:(hh + 1) * HEAD_DIM]
        o_s = acc_ref[hh] / l_ref[hh]
        o_w = acc_ref[nh + hh] / l_ref[nh + hh]
        o_ref[0, :, hh * HEAD_DIM:(hh + 1) * HEAD_DIM] = (
            gates[:, c0:c0 + 1] * o_c + gates[:, c0 + 1:c0 + 2] * o_s + gates[:, c0 + 2:c0 + 3] * o_w)


def _nsa_attention(p3, sel, o_cmp, gate_logits, slopes, tq):
    b, t, _ = p3.shape
    assert tq == KV_UNIT and NSA_WINDOW == 2 * tq and t // NSA_SEL_BLOCK <= LANES
    width = NSA_Q_PER_KV * HEAD_DIM
    qblk = COL["nq"] * HEAD_DIM // width
    kv = lambda name: pl.BlockSpec((1, t, HEAD_DIM), lambda bi, g, i: (bi, 0, COL[name] + g))
    nst = 2 * NSA_Q_PER_KV
    return pl.pallas_call(
        functools.partial(_nsa_attn_kernel, tq=tq, wide=min(KV_WIDE, t)),
        grid=(b, NSA_KV_HEADS, t // tq),
        in_specs=[pl.BlockSpec(memory_space=pltpu.SMEM),
                  pl.BlockSpec((1, tq, width), lambda bi, g, i: (bi, i, qblk + g)),
                  kv("nks"), kv("nvs"), kv("nkw"), kv("nvw"),
                  pl.BlockSpec((1, 1, tq, LANES), lambda bi, g, i: (bi, g, i, 0)),
                  pl.BlockSpec((1, tq, width), lambda bi, g, i: (bi, i, g)),
                  pl.BlockSpec((1, 1, tq, LANES), lambda bi, g, i: (bi, g, i, 0))],
        out_specs=pl.BlockSpec((1, tq, width), lambda bi, g, i: (bi, i, g)),
        out_shape=jax.ShapeDtypeStruct((b, t, GROUP_WIDTH), F32),
        scratch_shapes=[pltpu.VMEM((t, HEAD_DIM + LANES), BF16), pltpu.VMEM((1, 1), F32),
                        pltpu.VMEM((nst, tq, 1), F32), pltpu.VMEM((nst, tq, 1), F32),
                        pltpu.VMEM((nst, tq, HEAD_DIM), F32)],
        compiler_params=_cparams(("parallel", "parallel", "arbitrary")),
        name="nsa_attention",
    )(slopes, p3, p3, p3, p3, p3, sel, o_cmp, gate_logits)


def _nsa_blocks(p3, name):
    b, t, _ = p3.shape
    c0 = COL[name] * HEAD_DIM
    x = p3[:, :, c0:c0 + NSA_KV_HEADS * HEAD_DIM].reshape(b, t // NSA_CMP_STRIDE, NSA_CMP_STRIDE,
                                                         NSA_KV_HEADS, HEAD_DIM)
    x = x.transpose(0, 3, 1, 2, 4).reshape(b * NSA_KV_HEADS, t // NSA_CMP_STRIDE, NSA_CMP_STRIDE * HEAD_DIM)
    blocks = jnp.concatenate([x[:, :-1], x[:, 1:]], axis=-1)
    return jnp.pad(blocks, ((0, 0), (0, 1), (0, 0)))


def _alibi_slopes():
    n = 2 * GROUP_HEADS
    slopes = 2.0 ** (-8.0 * np.arange(1, n + 1) / n)
    return jnp.asarray(slopes[0::2], F32), jnp.asarray(slopes[1::2], F32)


def _regroup_w_in(w_in):
    main = jnp.concatenate([w_in[:, REF_OFF[n][0]:REF_OFF[n][0] + REF_OFF[n][1]] for n, _ in _MAIN_PIECES],
                           axis=1).astype(BF16)
    ng0, ngw = REF_OFF["ng"]
    ff0, ffw = REF_OFF["ff"]
    small = jnp.concatenate([w_in[:, ng0:ng0 + ngw], w_in[:, ff0:ff0 + ffw]], axis=1)
    small = jnp.pad(small, ((0, 0), (0, LANES - ngw - ffw))).astype(BF16)
    return main, small


def _mixer(x2d, b, t, norm_gain, w_in, forget_bias, pos_k, w1_k, w2_k, pos_v, w1_v, w2_v, group_gain, w_out):
    m = x2d.shape[0]
    h = _rmsnorm(x2d, norm_gain, BF16)
    w_main, w_small = _regroup_w_in(w_in)
    tm = _pick_tile(m, (1024, 512, 256))
    p3 = _matmul(h, w_main, BF16, tm, 512).reshape(b, t, MAIN_BLOCKS * HEAD_DIM)
    small = _matmul(h, w_small, F32, tm, LANES).reshape(b, t, LANES)
    slopes_moba, slopes_nsa = _alibi_slopes()

    o_a = _moba_attention(p3, slopes_moba)

    n_gate = NSA_N_BRANCHES * GROUP_HEADS
    kc = _nsa_compress(_nsa_blocks(p3, "nkc"), pos_k, w1_k, w2_k)
    vc = _nsa_compress(_nsa_blocks(p3, "nvc"), pos_v, w1_v, w2_v)
    ncp = kc.shape[1]
    kc = kc.reshape(b, NSA_KV_HEADS, ncp, HEAD_DIM)
    vc = vc.reshape(b, NSA_KV_HEADS, ncp, HEAD_DIM)
    o_cmp, sel = _nsa_select(p3, kc, vc, slopes_nsa, _pick_tile(t, (512, 256)))
    per_group = n_gate // NSA_KV_HEADS
    gate_logits = small[:, :, :n_gate].reshape(b, t, NSA_KV_HEADS, per_group).transpose(0, 2, 1, 3)
    gate_logits = jnp.pad(gate_logits, ((0, 0), (0, 0), (0, 0), (0, LANES - per_group)))
    o_b = _nsa_attention(p3, sel, o_cmp, gate_logits, slopes_nsa, KV_UNIT)

    ff = small[:, :, n_gate:n_gate + GROUP_HEADS] + forget_bias
    neg_cum_f = _neg_cum_logf(ff.transpose(0, 2, 1).reshape(b * GROUP_HEADS, t))
    o_c = _fox_attention(p3, neg_cum_f)

    o_d = _stick_attention(p3)

    on = _groupnorm([o.reshape(m, GROUP_WIDTH) for o in (o_a, o_b, o_c, o_d)], group_gain)
    return _matmul_residual(on, w_out, x2d, tm, 512)


def _cast_pad_kernel(w_ref, o_ref, *, n_row_tiles, cols):
    @pl.when(pl.program_id(0) < n_row_tiles)
    def _():
        o_ref[:, 0:cols] = w_ref[0].astype(BF16)
        if o_ref.shape[1] > cols:
            o_ref[:, cols:] = jnp.zeros((o_ref.shape[0], o_ref.shape[1] - cols), BF16)

    @pl.when(pl.program_id(0) >= n_row_tiles)
    def _():
        o_ref[...] = jnp.zeros(o_ref.shape, BF16)


def _weight_bf16(stacked, layer, rows_out, cols_out, tr=256):
    _, rows, cols = stacked.shape
    assert rows % tr == 0 and rows_out % tr == 0 and cols % LANES == 0
    n_row_tiles = rows // tr
    return pl.pallas_call(
        functools.partial(_cast_pad_kernel, n_row_tiles=n_row_tiles, cols=cols),
        grid=(rows_out // tr,),
        in_specs=[pl.BlockSpec((1, tr, cols), lambda j: (layer, jnp.minimum(j, n_row_tiles - 1), 0))],
        out_specs=pl.BlockSpec((tr, cols_out), lambda j: (j, 0)),
        out_shape=jax.ShapeDtypeStruct((rows_out, cols_out), BF16),
        compiler_params=_cparams(("parallel",)),
        name="weight_bf16",
    )(stacked)


def kernel(x, ffn1_norm, ffn1_w_gate, ffn1_w_up, ffn1_w_down, mix_norm, w_in, fox_forget_bias, nsa_cmp_pos_k, nsa_cmp_w1_k, nsa_cmp_w2_k, nsa_cmp_pos_v, nsa_cmp_w1_v, nsa_cmp_w2_v, group_norm, w_out, ffn2_norm, ffn2_w_gate, ffn2_w_up, ffn2_w_down, final_norm):
    b, t, d = x.shape
    x2d = x.reshape(b * t, d)
    for i in range(ffn1_norm.shape[0]):
        x2d = _ffn(x2d, ffn1_norm[i], ffn1_w_gate, ffn1_w_up, ffn1_w_down, i)
        x2d = _mixer(x2d, b, t, mix_norm[i], w_in[i], fox_forget_bias[i], nsa_cmp_pos_k[i], nsa_cmp_w1_k[i],
                     nsa_cmp_w2_k[i], nsa_cmp_pos_v[i], nsa_cmp_w1_v[i], nsa_cmp_w2_v[i],
                     group_norm[i], _weight_bf16(w_out, i, d, d))
        x2d = _ffn(x2d, ffn2_norm[i], ffn2_w_gate, ffn2_w_up, ffn2_w_down, i)
    return _rmsnorm(x2d, final_norm, F32).reshape(b, t, d)
```

```python
import functools
import math

import numpy as np
import jax
import jax.numpy as jnp
from jax import lax
from jax.experimental import pallas as pl
from jax.experimental.pallas import tpu as pltpu

F32 = jnp.float32
BF16 = jnp.bfloat16

HEAD_DIM = 128
GROUP_HEADS = 8
GROUP_WIDTH = GROUP_HEADS * HEAD_DIM
N_MIXERS = 4
FFN_RESIDUAL = 0.5
RMS_EPS = 1e-6

MOBA_BLOCK = 256
MOBA_TOPK = 3

NSA_KV_HEADS = 2
NSA_Q_PER_KV = 4
NSA_CMP_STRIDE = 16
NSA_CMP_LEN = 32
NSA_SEL_BLOCK = 64
NSA_SEL_TOPK = 16
NSA_WINDOW = 512
NSA_N_BRANCHES = 3

ATTN_SCALE = HEAD_DIM ** -0.5
LOG2E = math.log2(math.e)
SCALE_LOG2E = ATTN_SCALE * LOG2E
NEG_BIG = -1e30
STICK_DEAD_LOG2 = 160.0
FLASH_DEAD_LOG2 = 170.0
BOUND_SLACK = 1.001
KV_UNIT = 256
KV_WIDE = 1024
Q_ROWS = 256
Q_CHAINS = 4
F32_TINY = float(np.finfo(np.float32).tiny)

LANES = 128
VMEM_LIMIT = 56 * 1024 * 1024
GATEUP_TILE = 256
FFN_PAD = 512

_MAIN_PIECES = (
    ("mq", 8), ("mk", 8), ("mv", 8), ("nq", 8),
    ("nkc", 2), ("nvc", 2), ("nks", 2), ("nvs", 2), ("nkw", 2), ("nvw", 2),
    ("fq", 8), ("fk", 8), ("fv", 8), ("sq", 8), ("sk", 8), ("sv", 8),
)
COL = {}
_o = 0
for _n, _w in _MAIN_PIECES:
    COL[_n] = _o
    _o += _w
MAIN_BLOCKS = _o

_REF_SIZES = (
    ("mq", 1024), ("mk", 1024), ("mv", 1024), ("nq", 1024),
    ("nkc", 256), ("nvc", 256), ("nks", 256), ("nvs", 256), ("nkw", 256), ("nvw", 256),
    ("ng", 24),
    ("fq", 1024), ("fk", 1024), ("fv", 1024), ("ff", 8),
    ("sq", 1024), ("sk", 1024), ("sv", 1024),
)
REF_OFF = {}
_o = 0
for _n, _w in _REF_SIZES:
    REF_OFF[_n] = (_o, _w)
    _o += _w


def _cparams(sem):
    return pltpu.CompilerParams(dimension_semantics=sem, vmem_limit_bytes=VMEM_LIMIT)


def _rms_kernel(x_ref, g_ref, o_ref):
    x = x_ref[...]
    ms = jnp.mean(x * x, axis=-1, keepdims=True)
    o_ref[...] = (x * lax.rsqrt(ms + RMS_EPS) * g_ref[...]).astype(o_ref.dtype)


def _rmsnorm(x2d, gain, out_dtype, tm=256):
    m, d = x2d.shape
    return pl.pallas_call(
        _rms_kernel,
        grid=(m // tm,),
        in_specs=[pl.BlockSpec((tm, d), lambda i: (i, 0)),
                  pl.BlockSpec((1, d), lambda i: (0, 0))],
        out_specs=pl.BlockSpec((tm, d), lambda i: (i, 0)),
        out_shape=jax.ShapeDtypeStruct((m, d), out_dtype),
        compiler_params=_cparams(("parallel",)),
        name="rmsnorm",
    )(x2d, gain.reshape(1, d).astype(F32))


def _groupnorm_kernel(a_ref, b_ref, c_ref, d_ref, g_ref, o_ref):
    for gi, ref in enumerate((a_ref, b_ref, c_ref, d_ref)):
        x = ref[...]
        ms = jnp.mean(x * x, axis=-1, keepdims=True)
        lo, hi = gi * GROUP_WIDTH, (gi + 1) * GROUP_WIDTH
        o_ref[:, lo:hi] = (x * lax.rsqrt(ms + RMS_EPS) * g_ref[:, lo:hi]).astype(o_ref.dtype)


def _groupnorm(parts, gain, tm=256):
    m = parts[0].shape[0]
    d = GROUP_WIDTH * N_MIXERS
    part_spec = pl.BlockSpec((tm, GROUP_WIDTH), lambda i: (i, 0))
    return pl.pallas_call(
        _groupnorm_kernel,
        grid=(m // tm,),
        in_specs=[part_spec] * 4 + [pl.BlockSpec((1, d), lambda i: (0, 0))],
        out_specs=pl.BlockSpec((tm, d), lambda i: (i, 0)),
        out_shape=jax.ShapeDtypeStruct((m, d), BF16),
        compiler_params=_cparams(("parallel",)),
        name="groupnorm",
    )(*parts, gain.reshape(1, d).astype(F32))


def _mm_kernel(a_ref, w_ref, o_ref):
    o_ref[...] = jnp.dot(a_ref[...], w_ref[...], preferred_element_type=F32).astype(o_ref.dtype)


def _matmul(a, w, out_dtype, tm, tn):
    m, k = a.shape
    n = w.shape[1]
    return pl.pallas_call(
        _mm_kernel,
        grid=(m // tm, n // tn),
        in_specs=[pl.BlockSpec((tm, k), lambda i, j: (i, 0)),
                  pl.BlockSpec((k, tn), lambda i, j: (0, j))],
        out_specs=pl.BlockSpec((tm, tn), lambda i, j: (i, j)),
        out_shape=jax.ShapeDtypeStruct((m, n), out_dtype),
        compiler_params=_cparams(("parallel", "arbitrary")),
        name="matmul",
    )(a, w)


def _mm_res_kernel(a_ref, w_ref, x_ref, o_ref):
    o_ref[...] = x_ref[...] + jnp.dot(a_ref[...], w_ref[...], preferred_element_type=F32)


def _matmul_residual(a, w, x, tm, tn):
    m, k = a.shape
    n = w.shape[1]
    return pl.pallas_call(
        _mm_res_kernel,
        grid=(m // tm, n // tn),
        in_specs=[pl.BlockSpec((tm, k), lambda i, j: (i, 0)),
                  pl.BlockSpec((k, tn), lambda i, j: (0, j)),
                  pl.BlockSpec((tm, tn), lambda i, j: (i, j))],
        out_specs=pl.BlockSpec((tm, tn), lambda i, j: (i, j)),
        out_shape=jax.ShapeDtypeStruct((m, n), F32),
        compiler_params=_cparams(("parallel", "arbitrary")),
        name="matmul_residual",
    )(a, w, x)


def _gateup_kernel(h_ref, wg_ref, wu_ref, o_ref, *, n_valid):
    @pl.when(pl.program_id(1) < n_valid)
    def _():
        h = h_ref[...]
        g = jnp.dot(h, wg_ref[0].astype(BF16), preferred_element_type=F32)
        u = jnp.dot(h, wu_ref[0].astype(BF16), preferred_element_type=F32)
        o_ref[...] = (g / (1.0 + jnp.exp(-g)) * u * FFN_RESIDUAL).astype(o_ref.dtype)

    @pl.when(pl.program_id(1) >= n_valid)
    def _():
        o_ref[...] = jnp.zeros(o_ref.shape, o_ref.dtype)


def _gateup(h, wg, wu, layer, f_out, tm, tf):
    m, d = h.shape
    f = wg.shape[2]
    assert f % tf == 0 and f_out % tf == 0
    n_valid = f // tf
    w_spec = pl.BlockSpec((1, d, tf), lambda i, j: (layer, 0, jnp.minimum(j, n_valid - 1)))
    return pl.pallas_call(
        functools.partial(_gateup_kernel, n_valid=n_valid),
        grid=(m // tm, f_out // tf),
        in_specs=[pl.BlockSpec((tm, d), lambda i, j: (i, 0), pipeline_mode=pl.Buffered(1)), w_spec, w_spec],
        out_specs=pl.BlockSpec((tm, tf), lambda i, j: (i, j)),
        out_shape=jax.ShapeDtypeStruct((m, f_out), BF16),
        compiler_params=_cparams(("parallel", "arbitrary")),
        name="ffn_gateup",
    )(h, wg, wu)


def _down_kernel(a_ref, w_ref, x_ref, o_ref):
    @pl.when(pl.program_id(2) == 0)
    def _():
        o_ref[...] = x_ref[...] + jnp.dot(a_ref[...], w_ref[...], preferred_element_type=F32)

    @pl.when(pl.program_id(2) != 0)
    def _():
        o_ref[...] = o_ref[...] + jnp.dot(a_ref[...], w_ref[...], preferred_element_type=F32)


def _down(act, wd, x, tm, tn, tk):
    m, f = act.shape
    n = wd.shape[1]
    return pl.pallas_call(
        _down_kernel,
        grid=(m // tm, n // tn, f // tk),
        in_specs=[pl.BlockSpec((tm, tk), lambda i, j, k: (i, k)),
                  pl.BlockSpec((tk, tn), lambda i, j, k: (k, j)),
                  pl.BlockSpec((tm, tn), lambda i, j, k: (i, j))],
        out_specs=pl.BlockSpec((tm, tn), lambda i, j, k: (i, j)),
        out_shape=jax.ShapeDtypeStruct((m, n), F32),
        compiler_params=_cparams(("parallel", "parallel", "arbitrary")),
        name="ffn_down",
    )(act, wd, x)


def _pick_tile(n, prefs):
    for t in prefs:
        if n % t == 0:
            return t
    return n


def _ffn(x2d, gain, wg, wu, wd, layer):
    m, d = x2d.shape
    f = wg.shape[2]
    fp = -(-f // FFN_PAD) * FFN_PAD
    h = _rmsnorm(x2d, gain, BF16)
    tm = _pick_tile(m, (1024, 512, 256))
    act = _gateup(h, wg, wu, layer, fp, _pick_tile(m, (2048, 1024, 512, 256)), _pick_tile(f, (GATEUP_TILE, 128)))
    return _down(act, _weight_bf16(wd, layer, fp, d), x2d, tm, _pick_tile(d, (1024, 512, 256, 128)),
                 _pick_tile(fp, (2816, 1024, 512, 256, 128)))


def _qk_raw(q, k):
    return lax.dot_general(q, k, (((1,), (1,)), ((), ())), preferred_element_type=F32)


def _qk(q, k):
    return _qk_raw(q, k) * ATTN_SCALE


def _online_update_multi(raws, vs, states):
    m_prev = [m_ref[...] for m_ref, _, _ in states]
    m_new = [jnp.maximum(mp, jnp.max(raw, axis=-1, keepdims=True)) for mp, raw in zip(m_prev, raws)]
    ps = [jnp.exp2((raw - mn) * SCALE_LOG2E) for raw, mn in zip(raws, m_new)]
    alphas = [jnp.exp2((mp - mn) * SCALE_LOG2E) for mp, mn in zip(m_prev, m_new)]
    pvs = [jnp.dot(p.astype(BF16), v, preferred_element_type=F32) for p, v in zip(ps, vs)]
    for (m_ref, l_ref, acc_ref), mn, a, p, pv in zip(states, m_new, alphas, ps, pvs):
        l_ref[...] = a * l_ref[...] + jnp.sum(p, axis=-1, keepdims=True)
        acc_ref[...] = a * acc_ref[...] + pv
        m_ref[...] = mn


def _flash_init(m_ref, l_ref, acc_ref):
    m_ref[...] = jnp.full(m_ref.shape, NEG_BIG, F32)
    l_ref[...] = jnp.zeros(l_ref.shape, F32)
    acc_ref[...] = jnp.zeros(acc_ref.shape, F32)


def _sweep_tiles(t0, wide, tail_tile, full_tile, keep_going):
    n_full = t0 // wide
    tail_off = pl.multiple_of(n_full * wide, wide)

    def full_off(jj):
        return pl.multiple_of((n_full - 1 - jj) * wide, wide)

    def step(state):
        jj, _ = state
        full_tile(full_off(jj))
        return jj + 1, keep_going(full_off(jj))

    tail_tile(tail_off)
    lax.while_loop(lambda s: (s[0] < n_full) & s[1], step, (jnp.int32(0), keep_going(tail_off)))


def _row_norms(q):
    q = q.astype(F32)
    return jnp.sqrt(jnp.sum(q * q, axis=-1, keepdims=True))


def _max_key_norm(k_ref, out_ref):
    def body(c, mx):
        k = k_ref[0, pl.ds(pl.multiple_of(c * KV_UNIT, KV_UNIT), KV_UNIT), :].astype(F32)
        return jnp.maximum(mx, jnp.max(jnp.sum(k * k, axis=-1, keepdims=True), axis=0, keepdims=True))

    out_ref[...] = jnp.sqrt(lax.fori_loop(0, k_ref.shape[1] // KV_UNIT, body, jnp.zeros((1, 1), F32)))


def _softmax_alive(q_norms, k_norm, bias_caps, states):
    gaps = [qn * (k_norm * BOUND_SLACK) + cap - m_ref[...]
            for qn, cap, (m_ref, _, _) in zip(q_norms, bias_caps, states)]
    worst = jnp.max(functools.reduce(jnp.maximum, gaps))
    return (worst + 1.0) * SCALE_LOG2E > -FLASH_DEAD_LOG2


def _fill_key_aug(kaug_ref, k_ref, aux_fn):
    t = kaug_ref.shape[0]

    def body(c, carry):
        off = pl.multiple_of(c * KV_UNIT, KV_UNIT)
        pos = lax.broadcasted_iota(jnp.int32, (KV_UNIT, LANES), 0) + off
        lane = lax.broadcasted_iota(jnp.int32, (KV_UNIT, LANES), 1)
        kaug_ref[pl.ds(off, KV_UNIT), 0:HEAD_DIM] = k_ref[0, pl.ds(off, KV_UNIT), :]
        kaug_ref[pl.ds(off, KV_UNIT), HEAD_DIM:] = aux_fn(pos, lane).astype(BF16)
        return carry

    lax.fori_loop(0, t // KV_UNIT, body, 0)


def _topk_rounds(scores, blkf, sels, rounds):
    for _ in range(rounds):
        mx = [jnp.max(s, axis=-1, keepdims=True) for s in scores]
        idx = [jnp.min(jnp.where(s == m, blkf, 1e9), axis=-1, keepdims=True) for s, m in zip(scores, mx)]
        picks = [(blkf == i) & (m > -jnp.inf) for i, m in zip(idx, mx)]
        sels = [jnp.where(p, 1.0, sel) for p, sel in zip(picks, sels)]
        scores = [jnp.where(p, -jnp.inf, s) for p, s in zip(picks, scores)]
    return sels


def _split3(x):
    hi = x.astype(BF16)
    r1 = x - hi.astype(F32)
    mid = r1.astype(BF16)
    lo = (r1 - mid.astype(F32)).astype(BF16)
    return hi, mid, lo


def _dot_split(x, w_bf16, terms):
    parts = _split3(x)[:terms]
    out = jnp.dot(parts[0], w_bf16, preferred_element_type=F32)
    for p in parts[1:]:
        out = out + jnp.dot(p, w_bf16, preferred_element_type=F32)
    return out


def _logf_cumsum_kernel(x_ref, tri_ref, low_ref, o_ref):
    x = x_ref[0]
    logf = jnp.minimum(x, 0.0) - jnp.log1p(jnp.exp(-jnp.abs(x)))
    within = _dot_split(logf, tri_ref[...], 3)
    hi, mid, lo = _split3(logf)
    low = low_ref[...]
    before = (jnp.dot(low, hi, preferred_element_type=F32) + jnp.dot(low, mid, preferred_element_type=F32)
              + jnp.dot(low, lo, preferred_element_type=F32))
    o_ref[0] = -(within + jnp.sum(before, axis=-1, keepdims=True))


def _neg_cum_logf(logits_rows):
    r, t = logits_rows.shape
    nc = t // LANES
    li = np.arange(LANES)
    tri = jnp.asarray((li[:, None] <= li[None, :]).astype(np.float32), BF16)
    ci = np.arange(nc)
    low = jnp.asarray((ci[None, :] < ci[:, None]).astype(np.float32), BF16)
    out = pl.pallas_call(
        _logf_cumsum_kernel,
        grid=(r,),
        in_specs=[pl.BlockSpec((1, nc, LANES), lambda i: (i, 0, 0)),
                  pl.BlockSpec((LANES, LANES), lambda i: (0, 0)),
                  pl.BlockSpec((nc, nc), lambda i: (0, 0))],
        out_specs=pl.BlockSpec((1, nc, LANES), lambda i: (i, 0, 0)),
        out_shape=jax.ShapeDtypeStruct((r, nc, LANES), F32),
        compiler_params=_cparams(("parallel",)),
        name="fox_logf_cumsum",
    )(logits_rows.reshape(r, nc, LANES), tri, low)
    return out.reshape(r, t)


def _chain_rows(c):
    return slice(c * Q_ROWS, (c + 1) * Q_ROWS)


def _own_causal(c, width, strict=False):
    col = lax.broadcasted_iota(jnp.int32, (1, width), 1)
    row = lax.broadcasted_iota(jnp.int32, (Q_ROWS, 1), 0) + c * Q_ROWS
    return (col < row) if strict else (col <= row)


def _fox_kernel(q_ref, k_ref, v_ref, nf_ref, o_ref, knorm_ref, m_ref, l_ref, acc_ref, *, wide):
    t0 = pl.program_id(2) * (Q_CHAINS * Q_ROWS)

    @pl.when(pl.program_id(2) == 0)
    def _():
        _max_key_norm(k_ref, knorm_ref)

    qs = [q_ref[0, _chain_rows(c), :] for c in range(Q_CHAINS)]
    q_norms = [_row_norms(q) for q in qs]
    states = [(m_ref.at[c], l_ref.at[c], acc_ref.at[c]) for c in range(Q_CHAINS)]
    _flash_init(m_ref, l_ref, acc_ref)

    def keep_going(first_done):
        prev = jnp.maximum(first_done // wide - 1, 0)
        cap = nf_ref[0, 0, pl.ds(prev, 1), :][:, wide - 1:wide] * (1.0 / ATTN_SCALE)
        return _softmax_alive(q_norms, knorm_ref[...], [cap] * Q_CHAINS, states)

    def full_tile(off):
        bias = nf_ref[0, 0, pl.ds(off // wide, 1), :] * (1.0 / ATTN_SCALE)
        k = k_ref[0, pl.ds(off, wide), :]
        raws = [_qk_raw(q, k) + bias for q in qs]
        _online_update_multi(raws, [v_ref[0, pl.ds(off, wide), :]] * Q_CHAINS, states)

    def tail_tile(off):
        bias = nf_ref[0, 0, pl.ds(off // wide, 1), :] * (1.0 / ATTN_SCALE)
        k = k_ref[0, pl.ds(off, wide), :]
        v = v_ref[0, pl.ds(off, wide), :]
        ends = [(c + 1) * Q_ROWS for c in range(Q_CHAINS)]
        raws = [jnp.where(_own_causal(c, e), _qk_raw(q, k[:e]) + bias[:, :e], NEG_BIG)
                for c, (q, e) in enumerate(zip(qs, ends))]
        _online_update_multi(raws, [v[:e] for e in ends], states)

    _sweep_tiles(t0, wide, tail_tile, full_tile, keep_going)
    for c in range(Q_CHAINS):
        o_ref[0, _chain_rows(c), :] = acc_ref[c] / l_ref[c]


def _attn_scratch():
    return [pltpu.VMEM((Q_CHAINS, Q_ROWS, 1), F32), pltpu.VMEM((Q_CHAINS, Q_ROWS, 1), F32),
            pltpu.VMEM((Q_CHAINS, Q_ROWS, HEAD_DIM), F32)]


def _fox_attention(p3, neg_cum_f):
    b, t, _ = p3.shape
    tq = Q_CHAINS * Q_ROWS
    wide = tq
    nf = neg_cum_f.reshape(b, GROUP_HEADS, t // wide, wide)
    return pl.pallas_call(
        functools.partial(_fox_kernel, wide=wide),
        grid=(b, GROUP_HEADS, t // tq),
        in_specs=[pl.BlockSpec((1, tq, HEAD_DIM), lambda bi, h, i: (bi, i, COL["fq"] + h)),
                  pl.BlockSpec((1, t, HEAD_DIM), lambda bi, h, i: (bi, 0, COL["fk"] + h)),
                  pl.BlockSpec((1, t, HEAD_DIM), lambda bi, h, i: (bi, 0, COL["fv"] + h)),
                  pl.BlockSpec((1, 1, t // wide, wide), lambda bi, h, i: (bi, h, 0, 0))],
        out_specs=pl.BlockSpec((1, tq, HEAD_DIM), lambda bi, h, i: (bi, i, h)),
        out_shape=jax.ShapeDtypeStruct((b, t, GROUP_WIDTH), F32),
        scratch_shapes=[pltpu.VMEM((1, 1), F32)] + _attn_scratch(),
        compiler_params=_cparams(("parallel", "parallel", "arbitrary")),
        name="fox_attention",
    )(p3, p3, p3, nf)


def _neg_abs(x):
    bits = lax.bitcast_convert_type(x, jnp.uint32) | jnp.uint32(0x80000000)
    return lax.bitcast_convert_type(bits, F32)


def _stick_kernel(q_ref, k_ref, v_ref, u_ref, o_ref, r_ref, acc_ref, *, wide):
    t0 = pl.program_id(2) * (Q_CHAINS * Q_ROWS)
    chains = range(Q_CHAINS)
    qs = [q_ref[0, _chain_rows(c), :] for c in chains]
    r_ref[...] = jnp.zeros(r_ref.shape, F32)
    acc_ref[...] = jnp.zeros(acc_ref.shape, F32)
    u = u_ref[...]

    def tile(off, tail):
        ends = [(c + 1) * Q_ROWS if tail else wide for c in chains]
        k = k_ref[0, pl.ds(off, wide), :]
        z_all = [_qk_raw(q, k[:e]) * SCALE_LOG2E for q, e in zip(qs, ends)]
        later = [r_ref[c] for c in chains]
        weights = [[] for _ in chains]
        before = _own_causal(0, Q_ROWS, strict=True)
        for sb in reversed(range(wide // Q_ROWS)):
            live = [c for c in chains if ends[c] > sb * Q_ROWS]
            diag = [tail and c == sb for c in live]
            cols = slice(sb * Q_ROWS, (sb + 1) * Q_ROWS)
            zs = [z_all[c][:, cols] for c in live]
            log_beta = [jnp.minimum(z, 0.0) - jnp.log(1.0 + jnp.exp2(_neg_abs(z))) * LOG2E for z in zs]
            log_keep = [lb - z for lb, z in zip(log_beta, zs)]
            log_keep = [jnp.where(before, lk, 0.0) if d else lk for d, lk in zip(diag, log_keep)]
            inside = [jnp.dot(jnp.concatenate(_split3(lk)[:2], axis=1), u, preferred_element_type=F32)
                      for lk in log_keep]
            a = [jnp.exp2(lb + cs + later[c]) for c, lb, cs in zip(live, log_beta, inside)]
            a = [jnp.where(before, x, 0.0) if d else x for d, x in zip(diag, a)]
            for c, x, cs, lk in zip(live, a, inside, log_keep):
                weights[c].append(x.astype(BF16))
                later[c] = later[c] + cs[:, 0:1] + lk[:, 0:1]
        v = v_ref[0, pl.ds(off, wide), :]
        for c in chains:
            w = weights[c][0] if len(weights[c]) == 1 else jnp.concatenate(weights[c][::-1], axis=1)
            acc_ref[c] += jnp.dot(w, v[:ends[c]], preferred_element_type=F32)
            r_ref[c] = later[c]

    def keep_going(first_done):
        del first_done
        return jnp.max(r_ref[...]) > -STICK_DEAD_LOG2

    _sweep_tiles(t0, wide, functools.partial(tile, tail=True), functools.partial(tile, tail=False), keep_going)
    for c in chains:
        o_ref[0, _chain_rows(c), :] = acc_ref[c]


def _stick_attention(p3):
    b, t, _ = p3.shape
    tq = Q_CHAINS * Q_ROWS
    wide = tq
    ki = np.arange(Q_ROWS)
    u = (ki[:, None] > ki[None, :]).astype(np.float32)
    u = jnp.asarray(np.concatenate([u, u], axis=0), BF16)
    return pl.pallas_call(
        functools.partial(_stick_kernel, wide=wide),
        grid=(b, GROUP_HEADS, t // tq),
        in_specs=[pl.BlockSpec((1, tq, HEAD_DIM), lambda bi, h, i: (bi, i, COL["sq"] + h)),
                  pl.BlockSpec((1, t, HEAD_DIM), lambda bi, h, i: (bi, 0, COL["sk"] + h)),
                  pl.BlockSpec((1, t, HEAD_DIM), lambda bi, h, i: (bi, 0, COL["sv"] + h)),
                  pl.BlockSpec((2 * Q_ROWS, Q_ROWS), lambda bi, h, i: (0, 0))],
        out_specs=pl.BlockSpec((1, tq, HEAD_DIM), lambda bi, h, i: (bi, i, h)),
        out_shape=jax.ShapeDtypeStruct((b, t, GROUP_WIDTH), F32),
        scratch_shapes=[pltpu.VMEM((Q_CHAINS, Q_ROWS, 1), F32), pltpu.VMEM((Q_CHAINS, Q_ROWS, HEAD_DIM), F32)],
        compiler_params=_cparams(("parallel", "parallel", "arbitrary")),
        name="stick_attention",
    )(p3, p3, p3, u)


def _moba_kernel(slope_ref, q_ref, k_ref, v_ref, o_ref, kaug_ref, kmean_ref, knorm_ref, m_ref, l_ref, acc_ref, *,
                 n_blk, wide):
    h = pl.program_id(1)
    i = pl.program_id(2)
    tq = MOBA_BLOCK
    slope = slope_ref[h]

    @pl.when(i == 0)
    def _():
        kmean_ref[...] = jnp.zeros(kmean_ref.shape, F32)
        kb = k_ref[0].astype(F32).reshape(n_blk, tq, HEAD_DIM)
        kmean_ref[0:n_blk, :] = jnp.mean(kb, axis=1)

        def key_aux(pos, lane):
            onehot = jnp.where(lane == jnp.right_shift(pos, int(math.log2(tq))), 1.0, 0.0)
            hi, mid, lo = _split3(slope * pos.astype(F32) * (1.0 / ATTN_SCALE))
            return jnp.where(lane == n_blk, hi.astype(F32),
                             jnp.where(lane == n_blk + 1, mid.astype(F32),
                                       jnp.where(lane == n_blk + 2, lo.astype(F32), onehot)))

        _fill_key_aug(kaug_ref, k_ref, key_aux)
        _max_key_norm(k_ref, knorm_ref)

    chains = range(Q_CHAINS)
    t0 = i * (Q_CHAINS * Q_ROWS)
    qs = [q_ref[0, _chain_rows(c), :] for c in chains]
    q_norms = [_row_norms(q) for q in qs]
    own = [i * Q_CHAINS + c for c in chains]
    kmean = kmean_ref[...].astype(BF16)
    blk = lax.broadcasted_iota(jnp.int32, (Q_ROWS, LANES), 1)
    blkf = blk.astype(F32)
    scores = [jnp.where(blk < own[c], _qk_raw(qs[c], kmean), -jnp.inf) for c in chains]
    sels = _topk_rounds(scores, blkf, [jnp.where(blk == own[c], 1.0, 0.0) for c in chains], MOBA_TOPK)
    q_aux = [jnp.where(blk < n_blk, jnp.where(sel > 0.5, 0.0, NEG_BIG), jnp.where(blk < n_blk + 3, 1.0, 0.0))
             for sel in sels]
    q_aug = [jnp.concatenate([q, aux.astype(BF16)], axis=1) for q, aux in zip(qs, q_aux)]
    states = [(m_ref.at[c], l_ref.at[c], acc_ref.at[c]) for c in chains]
    _flash_init(m_ref, l_ref, acc_ref)

    def full_tile(off):
        k = kaug_ref[pl.ds(off, wide), :]
        _online_update_multi([_qk_raw(q, k) for q in q_aug], [v_ref[0, pl.ds(off, wide), :]] * Q_CHAINS, states)

    def tail_tile(off):
        k = kaug_ref[pl.ds(off, wide), :]
        v = v_ref[0, pl.ds(off, wide), :]
        ends = [(c + 1) * Q_ROWS for c in chains]
        raws = [jnp.where(_own_causal(c, e), _qk_raw(q, k[:e]), NEG_BIG) for c, (q, e) in enumerate(zip(q_aug, ends))]
        _online_update_multi(raws, [v[:e] for e in ends], states)

    def keep_going(first_done):
        cap = slope * first_done.astype(F32) * (1.0 / ATTN_SCALE)
        return _softmax_alive(q_norms, knorm_ref[...], [cap] * Q_CHAINS, states)

    _sweep_tiles(t0, wide, tail_tile, full_tile, keep_going)
    for c in chains:
        o_ref[0, _chain_rows(c), :] = acc_ref[c] / l_ref[c]


def _moba_attention(p3, slopes):
    b, t, _ = p3.shape
    tq = Q_CHAINS * Q_ROWS
    n_blk = t // MOBA_BLOCK
    wide = tq
    assert Q_ROWS == MOBA_BLOCK and t % tq == 0 and n_blk + 3 <= LANES
    return pl.pallas_call(
        functools.partial(_moba_kernel, n_blk=n_blk, wide=wide),
        grid=(b, GROUP_HEADS, t // tq),
        in_specs=[pl.BlockSpec(memory_space=pltpu.SMEM),
                  pl.BlockSpec((1, tq, HEAD_DIM), lambda bi, h, i: (bi, i, COL["mq"] + h)),
                  pl.BlockSpec((1, t, HEAD_DIM), lambda bi, h, i: (bi, 0, COL["mk"] + h)),
                  pl.BlockSpec((1, t, HEAD_DIM), lambda bi, h, i: (bi, 0, COL["mv"] + h))],
        out_specs=pl.BlockSpec((1, tq, HEAD_DIM), lambda bi, h, i: (bi, i, h)),
        out_shape=jax.ShapeDtypeStruct((b, t, GROUP_WIDTH), F32),
        scratch_shapes=[pltpu.VMEM((t, HEAD_DIM + LANES), BF16),
                        pltpu.VMEM((LANES, HEAD_DIM), F32), pltpu.VMEM((1, 1), F32)] + _attn_scratch(),
        compiler_params=_cparams(("parallel", "parallel", "arbitrary")),
        name="moba_attention",
    )(slopes, p3, p3, p3)


def _nsa_compress_kernel(x_ref, pos_ref, w1_ref, w2_ref, o_ref):
    x = (x_ref[0].astype(F32) + pos_ref[...]).astype(BF16)
    hmid = jnp.dot(x, w1_ref[...], preferred_element_type=F32)
    hmid = hmid / (1.0 + jnp.exp(-hmid))
    o_ref[0] = jnp.dot(hmid.astype(BF16), w2_ref[...], preferred_element_type=F32).astype(o_ref.dtype)


def _nsa_compress(blocks, pos, w1, w2):
    r, n, width = blocks.shape
    return pl.pallas_call(
        _nsa_compress_kernel,
        grid=(r,),
        in_specs=[pl.BlockSpec((1, n, width), lambda i: (i, 0, 0)),
                  pl.BlockSpec((1, width), lambda i: (0, 0)),
                  pl.BlockSpec((width, HEAD_DIM), lambda i: (0, 0)),
                  pl.BlockSpec((HEAD_DIM, HEAD_DIM), lambda i: (0, 0))],
        out_specs=pl.BlockSpec((1, n, HEAD_DIM), lambda i: (i, 0, 0)),
        out_shape=jax.ShapeDtypeStruct((r, n, HEAD_DIM), BF16),
        compiler_params=_cparams(("parallel",)),
        name="nsa_compress",
    )(blocks, pos.reshape(1, width).astype(F32), w1.astype(BF16), w2.astype(BF16))


def _nsa_select_kernel(slope_ref, q_ref, kc_ref, vc_ref, a_ref, oc_ref, sel_ref, *, tq, n_cmp, n_blk):
    g = pl.program_id(1)
    i = pl.program_id(2)
    t0 = i * tq
    kc = kc_ref[0, 0]
    vc = vc_ref[0, 0]
    ncp = kc.shape[0]
    tok = lax.broadcasted_iota(jnp.int32, (1, ncp), 1)
    cmp_end = tok * NSA_CMP_STRIDE + (NSA_CMP_LEN - 1)
    row = lax.broadcasted_iota(jnp.int32, (tq, 1), 0)
    admissible = (cmp_end <= row + t0) & (tok < n_cmp)
    rel_end = (cmp_end - t0).astype(F32)

    imp = jnp.zeros((tq, ncp), F32)
    for hh in range(NSA_Q_PER_KV):
        slope = slope_ref[g * NSA_Q_PER_KV + hh]
        q = q_ref[0, :, hh * HEAD_DIM:(hh + 1) * HEAD_DIM]
        s = jnp.where(admissible, _qk(q, kc) + slope * rel_end, NEG_BIG)
        m = jnp.max(s, axis=-1, keepdims=True)
        p = jnp.where(admissible, jnp.exp(s - m), 0.0)
        p = p / jnp.maximum(jnp.sum(p, axis=-1, keepdims=True), F32_TINY)
        oc_ref[0, :, hh * HEAD_DIM:(hh + 1) * HEAD_DIM] = jnp.dot(
            p.astype(BF16), vc, preferred_element_type=F32)
        imp = imp + p

    imp_blk = _dot_split(imp, a_ref[...], 3)
    blk = lax.broadcasted_iota(jnp.int32, (tq, LANES), 1)
    blkf = blk.astype(F32)
    jt = jnp.right_shift(row + t0, int(math.log2(NSA_SEL_BLOCK)))
    live = (blk <= jt) & (blk < n_blk)
    forced = ((blk == 0) | (blk == jt) | (blk == jt - 1)) & live
    sel = jnp.where(forced, 1.0, 0.0)
    score = jnp.where(live & jnp.logical_not(forced), imp_blk, -jnp.inf)
    sel = _topk_rounds([score], blkf, [sel], NSA_SEL_TOPK - 3)[0]
    sel_ref[0, 0] = sel.astype(sel_ref.dtype)


def _nsa_select(p3, kc, vc, slopes, tq):
    b, t, _ = p3.shape
    n_blk = t // NSA_SEL_BLOCK
    assert n_blk <= LANES and NSA_SEL_TOPK >= 3
    n_cmp = t // NSA_CMP_STRIDE - 1
    ncp = kc.shape[2]
    ratio = NSA_SEL_BLOCK // NSA_CMP_STRIDE
    ti = np.arange(ncp)[:, None]
    bj = np.arange(LANES)[None, :]
    amat = ((ti >= ratio * bj - 1) & (ti <= ratio * bj + ratio - 1) & (ti < n_cmp) & (bj < n_blk))
    amat = jnp.asarray(amat.astype(np.float32), BF16)
    width = NSA_Q_PER_KV * HEAD_DIM
    qblk = COL["nq"] * HEAD_DIM // width
    return pl.pallas_call(
        functools.partial(_nsa_select_kernel, tq=tq, n_cmp=n_cmp, n_blk=n_blk),
        grid=(b, NSA_KV_HEADS, t // tq),
        in_specs=[pl.BlockSpec(memory_space=pltpu.SMEM),
                  pl.BlockSpec((1, tq, width), lambda bi, g, i: (bi, i, qblk + g)),
                  pl.BlockSpec((1, 1, ncp, HEAD_DIM), lambda bi, g, i: (bi, g, 0, 0)),
                  pl.BlockSpec((1, 1, ncp, HEAD_DIM), lambda bi, g, i: (bi, g, 0, 0)),
                  pl.BlockSpec((ncp, LANES), lambda bi, g, i: (0, 0))],
        out_specs=[pl.BlockSpec((1, tq, width), lambda bi, g, i: (bi, i, g)),
                   pl.BlockSpec((1, 1, tq, LANES), lambda bi, g, i: (bi, g, i, 0))],
        out_shape=[jax.ShapeDtypeStruct((b, t, GROUP_WIDTH), F32),
                   jax.ShapeDtypeStruct((b, NSA_KV_HEADS, t, LANES), BF16)],
        compiler_params=_cparams(("parallel", "parallel", "parallel")),
        name="nsa_select",
    )(slopes, p3, kc, vc, amat)


def _nsa_attn_kernel(slope_ref, q_ref, ks_ref, vs_ref, kw_ref, vw_ref, sel_ref, oc_ref, gate_ref,
                     o_ref, kaug_ref, knorm_ref, m_ref, l_ref, acc_ref, *, tq, wide):
    g = pl.program_id(1)
    i = pl.program_id(2)
    t0 = pl.multiple_of(i * tq, tq)
    nh = NSA_Q_PER_KV

    @pl.when(i == 0)
    def _():
        sel_shift = int(math.log2(NSA_SEL_BLOCK))
        _fill_key_aug(kaug_ref, ks_ref,
                      lambda pos, lane: jnp.where(lane == jnp.right_shift(pos, sel_shift), 1.0, 0.0))
        _max_key_norm(ks_ref, knorm_ref)

    _flash_init(m_ref, l_ref, acc_ref)
    sel_bias = jnp.where(sel_ref[0, 0].astype(F32) > 0.5, 0.0, NEG_BIG).astype(BF16)
    qs = [q_ref[0, :, hh * HEAD_DIM:(hh + 1) * HEAD_DIM] for hh in range(nh)]
    q_aug = [jnp.concatenate([q, sel_bias], axis=1) for q in qs]
    slope_raw = [slope_ref[g * nh + hh] * (1.0 / ATTN_SCALE) for hh in range(nh)]
    row = lax.broadcasted_iota(jnp.int32, (tq, 1), 0)

    def rel_pos(off, width):
        return lax.broadcasted_iota(jnp.int32, (1, width), 1) + (off - t0)

    sel_states = [(m_ref.at[hh], l_ref.at[hh], acc_ref.at[hh]) for hh in range(nh)]
    win_states = [(m_ref.at[nh + hh], l_ref.at[nh + hh], acc_ref.at[nh + hh]) for hh in range(nh)]
    q_norms = [_row_norms(q) for q in qs]

    def sel_scores(off, tail):
        rel = rel_pos(off, wide)
        relf = rel.astype(F32)
        k = kaug_ref[pl.ds(off, wide), :]
        raws = [_qk_raw(q_aug[hh], k) + slope_raw[hh] * relf for hh in range(nh)]
        if tail:
            raws = [jnp.where(rel <= row, raw, NEG_BIG) for raw in raws]
        return raws, [vs_ref[0, pl.ds(off, wide), :]] * nh

    def window_scores(off, width, far):
        rel = rel_pos(off, width)
        relf = rel.astype(F32)
        k = kw_ref[0, pl.ds(off, width), :]
        ok = rel <= row
        if far:
            ok = ok & (row - rel < NSA_WINDOW)
        raws = [jnp.where(ok, _qk_raw(qs[hh], k) + slope_raw[hh] * relf, NEG_BIG) for hh in range(nh)]
        return raws, [vw_ref[0, pl.ds(off, width), :]] * nh

    def sel_tile(off, tail):
        raws, vs = sel_scores(off, tail)
        _online_update_multi(raws, vs, sel_states)

    def keep_going(first_done):
        gap = (first_done - t0).astype(F32)
        return _softmax_alive(q_norms, knorm_ref[...], [s * gap for s in slope_raw], sel_states)

    _sweep_tiles(t0, wide, functools.partial(sel_tile, tail=True), functools.partial(sel_tile, tail=False),
                 keep_going)

    n_back = NSA_WINDOW // tq
    for back in range(n_back + 1):
        @pl.when((i == back) if back < n_back else (i >= back))
        def _(back=back):
            raws, vs = window_scores(pl.multiple_of(t0 - back * tq, tq), (back + 1) * tq, back == n_back)
            _online_update_multi(raws, vs, win_states)

    gates = gate_ref[0, 0]
    gates = 1.0 / (1.0 + jnp.exp(-gates))
    for hh in range(nh):
        c0 = hh * NSA_N_BRANCHES
        o_c = oc_ref[0, :, hh * HEAD_DIM:(hh + 1) * HEAD_DIM]
        o_s = acc_ref[hh] / l_ref[hh]
        o_w = acc_ref[nh + hh] / l_ref[nh + hh]
        o_ref[0, :, hh * HEAD_DIM:(hh + 1) * HEAD_DIM] = (
            gates[:, c0:c0 + 1] * o_c + gates[:, c0 + 1:c0 + 2] * o_s + gates[:, c0 + 2:c0 + 3] * o_w)


def _nsa_attention(p3, sel, o_cmp, gate_logits, slopes, tq):
    b, t, _ = p3.shape
    assert tq == KV_UNIT and NSA_WINDOW == 2 * tq and t // NSA_SEL_BLOCK <= LANES
    width = NSA_Q_PER_KV * HEAD_DIM
    qblk = COL["nq"] * HEAD_DIM // width
    kv = lambda name: pl.BlockSpec((1, t, HEAD_DIM), lambda bi, g, i: (bi, 0, COL[name] + g))
    nst = 2 * NSA_Q_PER_KV
    return pl.pallas_call(
        functools.partial(_nsa_attn_kernel, tq=tq, wide=min(KV_WIDE, t)),
        grid=(b, NSA_KV_HEADS, t // tq),
        in_specs=[pl.BlockSpec(memory_space=pltpu.SMEM),
                  pl.BlockSpec((1, tq, width), lambda bi, g, i: (bi, i, qblk + g)),
                  kv("nks"), kv("nvs"), kv("nkw"), kv("nvw"),
                  pl.BlockSpec((1, 1, tq, LANES), lambda bi, g, i: (bi, g, i, 0)),
                  pl.BlockSpec((1, tq, width), lambda bi, g, i: (bi, i, g)),
                  pl.BlockSpec((1, 1, tq, LANES), lambda bi, g, i: (bi, g, i, 0))],
        out_specs=pl.BlockSpec((1, tq, width), lambda bi, g, i: (bi, i, g)),
        out_shape=jax.ShapeDtypeStruct((b, t, GROUP_WIDTH), F32),
        scratch_shapes=[pltpu.VMEM((t, HEAD_DIM + LANES), BF16), pltpu.VMEM((1, 1), F32),
                        pltpu.VMEM((nst, tq, 1), F32), pltpu.VMEM((nst, tq, 1), F32),
                        pltpu.VMEM((nst, tq, HEAD_DIM), F32)],
        compiler_params=_cparams(("parallel", "parallel", "arbitrary")),
        name="nsa_attention",
    )(slopes, p3, p3, p3, p3, p3, sel, o_cmp, gate_logits)


def _nsa_blocks(p3, name):
    b, t, _ = p3.shape
    c0 = COL[name] * HEAD_DIM
    x = p3[:, :, c0:c0 + NSA_KV_HEADS * HEAD_DIM].reshape(b, t // NSA_CMP_STRIDE, NSA_CMP_STRIDE,
                                                         NSA_KV_HEADS, HEAD_DIM)
    x = x.transpose(0, 3, 1, 2, 4).reshape(b * NSA_KV_HEADS, t // NSA_CMP_STRIDE, NSA_CMP_STRIDE * HEAD_DIM)
    blocks = jnp.concatenate([x[:, :-1], x[:, 1:]], axis=-1)
    return jnp.pad(blocks, ((0, 0), (0, 1), (0, 0)))


def _alibi_slopes():
    n = 2 * GROUP_HEADS
    slopes = 2.0 ** (-8.0 * np.arange(1, n + 1) / n)
    return jnp.asarray(slopes[0::2], F32), jnp.asarray(slopes[1::2], F32)


def _regroup_w_in(w_in):
    main = jnp.concatenate([w_in[:, REF_OFF[n][0]:REF_OFF[n][0] + REF_OFF[n][1]] for n, _ in _MAIN_PIECES],
                           axis=1).astype(BF16)
    ng0, ngw = REF_OFF["ng"]
    ff0, ffw = REF_OFF["ff"]
    small = jnp.concatenate([w_in[:, ng0:ng0 + ngw], w_in[:, ff0:ff0 + ffw]], axis=1)
    small = jnp.pad(small, ((0, 0), (0, LANES - ngw - ffw))).astype(BF16)
    return main, small


def _mixer(x2d, b, t, norm_gain, w_in, forget_bias, pos_k, w1_k, w2_k, pos_v, w1_v, w2_v, group_gain, w_out):
    m = x2d.shape[0]
    h = _rmsnorm(x2d, norm_gain, BF16)
    w_main, w_small = _regroup_w_in(w_in)
    tm = _pick_tile(m, (1024, 512, 256))
    p3 = _matmul(h, w_main, BF16, tm, 512).reshape(b, t, MAIN_BLOCKS * HEAD_DIM)
    small = _matmul(h, w_small, F32, tm, LANES).reshape(b, t, LANES)
    slopes_moba, slopes_nsa = _alibi_slopes()

    o_a = _moba_attention(p3, slopes_moba)

    n_gate = NSA_N_BRANCHES * GROUP_HEADS
    kc = _nsa_compress(_nsa_blocks(p3, "nkc"), pos_k, w1_k, w2_k)
    vc = _nsa_compress(_nsa_blocks(p3, "nvc"), pos_v, w1_v, w2_v)
    ncp = kc.shape[1]
    kc = kc.reshape(b, NSA_KV_HEADS, ncp, HEAD_DIM)
    vc = vc.reshape(b, NSA_KV_HEADS, ncp, HEAD_DIM)
    o_cmp, sel = _nsa_select(p3, kc, vc, slopes_nsa, _pick_tile(t, (512, 256)))
    per_group = n_gate // NSA_KV_HEADS
    gate_logits = small[:, :, :n_gate].reshape(b, t, NSA_KV_HEADS, per_group).transpose(0, 2, 1, 3)
    gate_logits = jnp.pad(gate_logits, ((0, 0), (0, 0), (0, 0), (0, LANES - per_group)))
    o_b = _nsa_attention(p3, sel, o_cmp, gate_logits, slopes_nsa, KV_UNIT)

    ff = small[:, :, n_gate:n_gate + GROUP_HEADS] + forget_bias
    neg_cum_f = _neg_cum_logf(ff.transpose(0, 2, 1).reshape(b * GROUP_HEADS, t))
    o_c = _fox_attention(p3, neg_cum_f)

    o_d = _stick_attention(p3)

    on = _groupnorm([o.reshape(m, GROUP_WIDTH) for o in (o_a, o_b, o_c, o_d)], group_gain)
    return _matmul_residual(on, w_out, x2d, tm, 512)


def _cast_pad_kernel(w_ref, o_ref, *, n_row_tiles, cols):
    @pl.when(pl.program_id(0) < n_row_tiles)
    def _():
        o_ref[:, 0:cols] = w_ref[0].astype(BF16)
        if o_ref.shape[1] > cols:
            o_ref[:, cols:] = jnp.zeros((o_ref.shape[0], o_ref.shape[1] - cols), BF16)

    @pl.when(pl.program_id(0) >= n_row_tiles)
    def _():
        o_ref[...] = jnp.zeros(o_ref.shape, BF16)


def _weight_bf16(stacked, layer, rows_out, cols_out, tr=256):
    _, rows, cols = stacked.shape
    assert rows % tr == 0 and rows_out % tr == 0 and cols % LANES == 0
    n_row_tiles = rows // tr
    return pl.pallas_call(
        functools.partial(_cast_pad_kernel, n_row_tiles=n_row_tiles, cols=cols),
        grid=(rows_out // tr,),
        in_specs=[pl.BlockSpec((1, tr, cols), lambda j: (layer, jnp.minimum(j, n_row_tiles - 1), 0))],
        out_specs=pl.BlockSpec((tr, cols_out), lambda j: (j, 0)),
        out_shape=jax.ShapeDtypeStruct((rows_out, cols_out), BF16),
        compiler_params=_cparams(("parallel",)),
        name="weight_bf16",
    )(stacked)


def kernel(x, ffn1_norm, ffn1_w_gate, ffn1_w_up, ffn1_w_down, mix_norm, w_in, fox_forget_bias, nsa_cmp_pos_k, nsa_cmp_w1_k, nsa_cmp_w2_k, nsa_cmp_pos_v, nsa_cmp_w1_v, nsa_cmp_w2_v, group_norm, w_out, ffn2_norm, ffn2_w_gate, ffn2_w_up, ffn2_w_down, final_norm):
    b, t, d = x.shape
    x2d = x.reshape(b * t, d)
    for i in range(ffn1_norm.shape[0]):
        x2d = _ffn(x2d, ffn1_norm[i], ffn1_w_gate, ffn1_w_up, ffn1_w_down, i)
        x2d = _mixer(x2d, b, t, mix_norm[i], w_in[i], fox_forget_bias[i], nsa_cmp_pos_k[i], nsa_cmp_w1_k[i],
                     nsa_cmp_w2_k[i], nsa_cmp_pos_v[i], nsa_cmp_w1_v[i], nsa_cmp_w2_v[i],
                     group_norm[i], _weight_bf16(w_out, i, d, d))
        x2d = _ffn(x2d, ffn2_norm[i], ffn2_w_gate, ffn2_w_up, ffn2_w_down, i)
    return _rmsnorm(x2d, final_norm, F32).reshape(b, t, d)
```

```python
import functools
import math

import numpy as np
import jax
import jax.numpy as jnp
from jax import lax
from jax.experimental import pallas as pl
from jax.experimental.pallas import tpu as pltpu

F32 = jnp.float32
BF16 = jnp.bfloat16

HEAD_DIM = 128
GROUP_HEADS = 8
GROUP_WIDTH = GROUP_HEADS * HEAD_DIM
N_MIXERS = 4
FFN_RESIDUAL = 0.5
RMS_EPS = 1e-6

MOBA_BLOCK = 256
MOBA_TOPK = 3

NSA_KV_HEADS = 2
NSA_Q_PER_KV = 4
NSA_CMP_STRIDE = 16
NSA_CMP_LEN = 32
NSA_SEL_BLOCK = 64
NSA_SEL_TOPK = 16
NSA_WINDOW = 512
NSA_N_BRANCHES = 3

ATTN_SCALE = HEAD_DIM ** -0.5
LOG2E = math.log2(math.e)
SCALE_LOG2E = ATTN_SCALE * LOG2E
NEG_BIG = -1e30
STICK_DEAD_LOG2 = 160.0
FLASH_DEAD_LOG2 = 170.0
BOUND_SLACK = 1.001
KV_UNIT = 256
KV_WIDE = 1024
Q_ROWS = 256
Q_CHAINS = 4
F32_TINY = float(np.finfo(np.float32).tiny)

LANES = 128
VMEM_LIMIT = 56 * 1024 * 1024
GATEUP_TILE = 256
FFN_PAD = 512

_MAIN_PIECES = (
    ("mq", 8), ("mk", 8), ("mv", 8), ("nq", 8),
    ("nkc", 2), ("nvc", 2), ("nks", 2), ("nvs", 2), ("nkw", 2), ("nvw", 2),
    ("fq", 8), ("fk", 8), ("fv", 8), ("sq", 8), ("sk", 8), ("sv", 8),
)
COL = {}
_o = 0
for _n, _w in _MAIN_PIECES:
    COL[_n] = _o
    _o += _w
MAIN_BLOCKS = _o

_REF_SIZES = (
    ("mq", 1024), ("mk", 1024), ("mv", 1024), ("nq", 1024),
    ("nkc", 256), ("nvc", 256), ("nks", 256), ("nvs", 256), ("nkw", 256), ("nvw", 256),
    ("ng", 24),
    ("fq", 1024), ("fk", 1024), ("fv", 1024), ("ff", 8),
    ("sq", 1024), ("sk", 1024), ("sv", 1024),
)
REF_OFF = {}
_o = 0
for _n, _w in _REF_SIZES:
    REF_OFF[_n] = (_o, _w)
    _o += _w


def _cparams(sem):
    return pltpu.CompilerParams(dimension_semantics=sem, vmem_limit_bytes=VMEM_LIMIT)


def _rms_kernel(x_ref, g_ref, o_ref):
    x = x_ref[...]
    ms = jnp.mean(x * x, axis=-1, keepdims=True)
    o_ref[...] = (x * lax.rsqrt(ms + RMS_EPS) * g_ref[...]).astype(o_ref.dtype)


def _rmsnorm(x2d, gain, out_dtype, tm=256):
    m, d = x2d.shape
    return pl.pallas_call(
        _rms_kernel,
        grid=(m // tm,),
        in_specs=[pl.BlockSpec((tm, d), lambda i: (i, 0)),
                  pl.BlockSpec((1, d), lambda i: (0, 0))],
        out_specs=pl.BlockSpec((tm, d), lambda i: (i, 0)),
        out_shape=jax.ShapeDtypeStruct((m, d), out_dtype),
        compiler_params=_cparams(("parallel",)),
        name="rmsnorm",
    )(x2d, gain.reshape(1, d).astype(F32))


def _groupnorm_kernel(a_ref, b_ref, c_ref, d_ref, g_ref, o_ref):
    for gi, ref in enumerate((a_ref, b_ref, c_ref, d_ref)):
        x = ref[...]
        ms = jnp.mean(x * x, axis=-1, keepdims=True)
        lo, hi = gi * GROUP_WIDTH, (gi + 1) * GROUP_WIDTH
        o_ref[:, lo:hi] = (x * lax.rsqrt(ms + RMS_EPS) * g_ref[:, lo:hi]).astype(o_ref.dtype)


def _groupnorm(parts, gain, tm=256):
    m = parts[0].shape[0]
    d = GROUP_WIDTH * N_MIXERS
    part_spec = pl.BlockSpec((tm, GROUP_WIDTH), lambda i: (i, 0))
    return pl.pallas_call(
        _groupnorm_kernel,
        grid=(m // tm,),
        in_specs=[part_spec] * 4 + [pl.BlockSpec((1, d), lambda i: (0, 0))],
        out_specs=pl.BlockSpec((tm, d), lambda i: (i, 0)),
        out_shape=jax.ShapeDtypeStruct((m, d), BF16),
        compiler_params=_cparams(("parallel",)),
        name="groupnorm",
    )(*parts, gain.reshape(1, d).astype(F32))


def _mm_kernel(a_ref, w_ref, o_ref):
    o_ref[...] = jnp.dot(a_ref[...], w_ref[...], preferred_element_type=F32).astype(o_ref.dtype)


def _matmul(a, w, out_dtype, tm, tn):
    m, k = a.shape
    n = w.shape[1]
    return pl.pallas_call(
        _mm_kernel,
        grid=(m // tm, n // tn),
        in_specs=[pl.BlockSpec((tm, k), lambda i, j: (i, 0)),
                  pl.BlockSpec((k, tn), lambda i, j: (0, j))],
        out_specs=pl.BlockSpec((tm, tn), lambda i, j: (i, j)),
        out_shape=jax.ShapeDtypeStruct((m, n), out_dtype),
        compiler_params=_cparams(("parallel", "arbitrary")),
        name="matmul",
    )(a, w)


def _mm_res_kernel(a_ref, w_ref, x_ref, o_ref):
    o_ref[...] = x_ref[...] + jnp.dot(a_ref[...], w_ref[...], preferred_element_type=F32)


def _matmul_residual(a, w, x, tm, tn):
    m, k = a.shape
    n = w.shape[1]
    return pl.pallas_call(
        _mm_res_kernel,
        grid=(m // tm, n // tn),
        in_specs=[pl.BlockSpec((tm, k), lambda i, j: (i, 0)),
                  pl.BlockSpec((k, tn), lambda i, j: (0, j)),
                  pl.BlockSpec((tm, tn), lambda i, j: (i, j))],
        out_specs=pl.BlockSpec((tm, tn), lambda i, j: (i, j)),
        out_shape=jax.ShapeDtypeStruct((m, n), F32),
        compiler_params=_cparams(("parallel", "arbitrary")),
        name="matmul_residual",
    )(a, w, x)


def _gateup_kernel(h_ref, wg_ref, wu_ref, o_ref, *, n_valid):
    @pl.when(pl.program_id(1) < n_valid)
    def _():
        h = h_ref[...]
        g = jnp.dot(h, wg_ref[0].astype(BF16), preferred_element_type=F32)
        u = jnp.dot(h, wu_ref[0].astype(BF16), preferred_element_type=F32)
        o_ref[...] = (g / (1.0 + jnp.exp(-g)) * u * FFN_RESIDUAL).astype(o_ref.dtype)

    @pl.when(pl.program_id(1) >= n_valid)
    def _():
        o_ref[...] = jnp.zeros(o_ref.shape, o_ref.dtype)


def _gateup(h, wg, wu, layer, f_out, tm, tf):
    m, d = h.shape
    f = wg.shape[2]
    assert f % tf == 0 and f_out % tf == 0
    n_valid = f // tf
    w_spec = pl.BlockSpec((1, d, tf), lambda i, j: (layer, 0, jnp.minimum(j, n_valid - 1)))
    return pl.pallas_call(
        functools.partial(_gateup_kernel, n_valid=n_valid),
        grid=(m // tm, f_out // tf),
        in_specs=[pl.BlockSpec((tm, d), lambda i, j: (i, 0), pipeline_mode=pl.Buffered(1)), w_spec, w_spec],
        out_specs=pl.BlockSpec((tm, tf), lambda i, j: (i, j)),
        out_shape=jax.ShapeDtypeStruct((m, f_out), BF16),
        compiler_params=_cparams(("parallel", "arbitrary")),
        name="ffn_gateup",
    )(h, wg, wu)


def _down_kernel(a_ref, w_ref, x_ref, o_ref):
    @pl.when(pl.program_id(2) == 0)
    def _():
        o_ref[...] = x_ref[...] + jnp.dot(a_ref[...], w_ref[...], preferred_element_type=F32)

    @pl.when(pl.program_id(2) != 0)
    def _():
        o_ref[...] = o_ref[...] + jnp.dot(a_ref[...], w_ref[...], preferred_element_type=F32)


def _down(act, wd, x, tm, tn, tk):
    m, f = act.shape
    n = wd.shape[1]
    return pl.pallas_call(
        _down_kernel,
        grid=(m // tm, n // tn, f // tk),
        in_specs=[pl.BlockSpec((tm, tk), lambda i, j, k: (i, k)),
                  pl.BlockSpec((tk, tn), lambda i, j, k: (k, j)),
                  pl.BlockSpec((tm, tn), lambda i, j, k: (i, j))],
        out_specs=pl.BlockSpec((tm, tn), lambda i, j, k: (i, j)),
        out_shape=jax.ShapeDtypeStruct((m, n), F32),
        compiler_params=_cparams(("parallel", "parallel", "arbitrary")),
        name="ffn_down",
    )(act, wd, x)


def _pick_tile(n, prefs):
    for t in prefs:
        if n % t == 0:
            return t
    return n


def _ffn(x2d, gain, wg, wu, wd, layer):
    m, d = x2d.shape
    f = wg.shape[2]
    fp = -(-f // FFN_PAD) * FFN_PAD
    h = _rmsnorm(x2d, gain, BF16)
    tm = _pick_tile(m, (1024, 512, 256))
    act = _gateup(h, wg, wu, layer, fp, _pick_tile(m, (2048, 1024, 512, 256)), _pick_tile(f, (GATEUP_TILE, 128)))
    return _down(act, _weight_bf16(wd, layer, fp, d), x2d, tm, _pick_tile(d, (1024, 512, 256, 128)),
                 _pick_tile(fp, (2816, 1024, 512, 256, 128)))


def _qk_raw(q, k):
    return lax.dot_general(q, k, (((1,), (1,)), ((), ())), preferred_element_type=F32)


def _qk(q, k):
    return _qk_raw(q, k) * ATTN_SCALE


def _online_update_multi(raws, vs, states):
    m_prev = [m_ref[...] for m_ref, _ in states]
    m_new = [jnp.maximum(mp, jnp.max(raw, axis=-1, keepdims=True)) for mp, raw in zip(m_prev, raws)]
    ps = [jnp.exp2((raw - mn) * SCALE_LOG2E).astype(BF16) for raw, mn in zip(raws, m_new)]
    alphas = [jnp.exp2((mp - mn) * SCALE_LOG2E) for mp, mn in zip(m_prev, m_new)]
    pvs = [jnp.dot(p, v, preferred_element_type=F32) for p, v in zip(ps, vs)]
    for (m_ref, acc_ref), mn, a, pv in zip(states, m_new, alphas, pvs):
        acc_ref[...] = a * acc_ref[...] + pv
        m_ref[...] = mn


def _flash_init(m_ref, acc_ref):
    m_ref[...] = jnp.full(m_ref.shape, NEG_BIG, F32)
    acc_ref[...] = jnp.zeros(acc_ref.shape, F32)


def _flash_result(acc_ref):
    acc = acc_ref[...]
    return acc[:, :HEAD_DIM] / acc[:, HEAD_DIM:HEAD_DIM + 1]


def _fill_value_aug(vaug_ref, v_ref):
    def body(c, carry):
        off = pl.multiple_of(c * KV_UNIT, KV_UNIT)
        lane = lax.broadcasted_iota(jnp.int32, (KV_UNIT, LANES), 1)
        vaug_ref[pl.ds(off, KV_UNIT), 0:HEAD_DIM] = v_ref[0, pl.ds(off, KV_UNIT), :]
        vaug_ref[pl.ds(off, KV_UNIT), HEAD_DIM:] = jnp.where(lane == 0, 1.0, 0.0).astype(BF16)
        return carry

    lax.fori_loop(0, vaug_ref.shape[0] // KV_UNIT, body, 0)


def _sweep_tiles(t0, wide, tail_tile, full_tile, keep_going):
    n_full = t0 // wide
    tail_off = pl.multiple_of(n_full * wide, wide)

    def full_off(jj):
        return pl.multiple_of((n_full - 1 - jj) * wide, wide)

    def step(state):
        jj, _ = state
        full_tile(full_off(jj))
        return jj + 1, keep_going(full_off(jj))

    tail_tile(tail_off)
    lax.while_loop(lambda s: (s[0] < n_full) & s[1], step, (jnp.int32(0), keep_going(tail_off)))


def _row_norms(q):
    q = q.astype(F32)
    return jnp.sqrt(jnp.sum(q * q, axis=-1, keepdims=True))


def _max_key_norm(k_ref, out_ref):
    def body(c, mx):
        k = k_ref[0, pl.ds(pl.multiple_of(c * KV_UNIT, KV_UNIT), KV_UNIT), :].astype(F32)
        return jnp.maximum(mx, jnp.max(jnp.sum(k * k, axis=-1, keepdims=True), axis=0, keepdims=True))

    out_ref[...] = jnp.sqrt(lax.fori_loop(0, k_ref.shape[1] // KV_UNIT, body, jnp.zeros((1, 1), F32)))


def _softmax_alive(q_norms, k_norm, bias_caps, states):
    gaps = [qn * (k_norm * BOUND_SLACK) + cap - m_ref[...]
            for qn, cap, (m_ref, _) in zip(q_norms, bias_caps, states)]
    worst = jnp.max(functools.reduce(jnp.maximum, gaps))
    return (worst + 1.0) * SCALE_LOG2E > -FLASH_DEAD_LOG2


def _fill_key_aug(kaug_ref, k_ref, aux_fn):
    t = kaug_ref.shape[0]

    def body(c, carry):
        off = pl.multiple_of(c * KV_UNIT, KV_UNIT)
        pos = lax.broadcasted_iota(jnp.int32, (KV_UNIT, LANES), 0) + off
        lane = lax.broadcasted_iota(jnp.int32, (KV_UNIT, LANES), 1)
        kaug_ref[pl.ds(off, KV_UNIT), 0:HEAD_DIM] = k_ref[0, pl.ds(off, KV_UNIT), :]
        kaug_ref[pl.ds(off, KV_UNIT), HEAD_DIM:] = aux_fn(pos, lane).astype(BF16)
        return carry

    lax.fori_loop(0, t // KV_UNIT, body, 0)


def _topk_rounds(scores, blkf, sels, rounds):
    for _ in range(rounds):
        mx = [jnp.max(s, axis=-1, keepdims=True) for s in scores]
        idx = [jnp.min(jnp.where(s == m, blkf, 1e9), axis=-1, keepdims=True) for s, m in zip(scores, mx)]
        picks = [(blkf == i) & (m > -jnp.inf) for i, m in zip(idx, mx)]
        sels = [jnp.where(p, 1.0, sel) for p, sel in zip(picks, sels)]
        scores = [jnp.where(p, -jnp.inf, s) for p, s in zip(picks, scores)]
    return sels


def _split3(x):
    hi = x.astype(BF16)
    r1 = x - hi.astype(F32)
    mid = r1.astype(BF16)
    lo = (r1 - mid.astype(F32)).astype(BF16)
    return hi, mid, lo


def _dot_split(x, w_bf16, terms):
    parts = _split3(x)[:terms]
    out = jnp.dot(parts[0], w_bf16, preferred_element_type=F32)
    for p in parts[1:]:
        out = out + jnp.dot(p, w_bf16, preferred_element_type=F32)
    return out


def _logf_cumsum_kernel(x_ref, tri_ref, low_ref, o_ref):
    x = x_ref[0]
    logf = jnp.minimum(x, 0.0) - jnp.log1p(jnp.exp(-jnp.abs(x)))
    within = _dot_split(logf, tri_ref[...], 3)
    hi, mid, lo = _split3(logf)
    low = low_ref[...]
    before = (jnp.dot(low, hi, preferred_element_type=F32) + jnp.dot(low, mid, preferred_element_type=F32)
              + jnp.dot(low, lo, preferred_element_type=F32))
    o_ref[0] = -(within + jnp.sum(before, axis=-1, keepdims=True))


def _neg_cum_logf(logits_rows):
    r, t = logits_rows.shape
    nc = t // LANES
    li = np.arange(LANES)
    tri = jnp.asarray((li[:, None] <= li[None, :]).astype(np.float32), BF16)
    ci = np.arange(nc)
    low = jnp.asarray((ci[None, :] < ci[:, None]).astype(np.float32), BF16)
    out = pl.pallas_call(
        _logf_cumsum_kernel,
        grid=(r,),
        in_specs=[pl.BlockSpec((1, nc, LANES), lambda i: (i, 0, 0)),
                  pl.BlockSpec((LANES, LANES), lambda i: (0, 0)),
                  pl.BlockSpec((nc, nc), lambda i: (0, 0))],
        out_specs=pl.BlockSpec((1, nc, LANES), lambda i: (i, 0, 0)),
        out_shape=jax.ShapeDtypeStruct((r, nc, LANES), F32),
        compiler_params=_cparams(("parallel",)),
        name="fox_logf_cumsum",
    )(logits_rows.reshape(r, nc, LANES), tri, low)
    return out.reshape(r, t)


def _chain_rows(c):
    return slice(c * Q_ROWS, (c + 1) * Q_ROWS)


def _own_causal(c, width, strict=False):
    col = lax.broadcasted_iota(jnp.int32, (1, width), 1)
    row = lax.broadcasted_iota(jnp.int32, (Q_ROWS, 1), 0) + c * Q_ROWS
    return (col < row) if strict else (col <= row)


def _fox_kernel(q_ref, k_ref, v_ref, nf_ref, o_ref, knorm_ref, vaug_ref, m_ref, acc_ref, *, wide):
    t0 = pl.program_id(2) * (Q_CHAINS * Q_ROWS)

    @pl.when(pl.program_id(2) == 0)
    def _():
        _max_key_norm(k_ref, knorm_ref)
        _fill_value_aug(vaug_ref, v_ref)

    qs = [q_ref[0, _chain_rows(c), :] for c in range(Q_CHAINS)]
    q_norms = [_row_norms(q) for q in qs]
    states = [(m_ref.at[c], acc_ref.at[c]) for c in range(Q_CHAINS)]
    _flash_init(m_ref, acc_ref)

    def keep_going(first_done):
        prev = jnp.maximum(first_done // wide - 1, 0)
        cap = nf_ref[0, 0, pl.ds(prev, 1), :][:, wide - 1:wide] * (1.0 / ATTN_SCALE)
        return _softmax_alive(q_norms, knorm_ref[...], [cap] * Q_CHAINS, states)

    def full_tile(off):
        bias = nf_ref[0, 0, pl.ds(off // wide, 1), :] * (1.0 / ATTN_SCALE)
        k = k_ref[0, pl.ds(off, wide), :]
        raws = [_qk_raw(q, k) + bias for q in qs]
        _online_update_multi(raws, [vaug_ref[pl.ds(off, wide), :]] * Q_CHAINS, states)

    def tail_tile(off):
        bias = nf_ref[0, 0, pl.ds(off // wide, 1), :] * (1.0 / ATTN_SCALE)
        k = k_ref[0, pl.ds(off, wide), :]
        v = vaug_ref[pl.ds(off, wide), :]
        ends = [(c + 1) * Q_ROWS for c in range(Q_CHAINS)]
        raws = [jnp.where(_own_causal(c, e), _qk_raw(q, k[:e]) + bias[:, :e], NEG_BIG)
                for c, (q, e) in enumerate(zip(qs, ends))]
        _online_update_multi(raws, [v[:e] for e in ends], states)

    _sweep_tiles(t0, wide, tail_tile, full_tile, keep_going)
    for c in range(Q_CHAINS):
        o_ref[0, _chain_rows(c), :] = _flash_result(acc_ref.at[c])


def _attn_scratch(t, n_states):
    return [pltpu.VMEM((t, HEAD_DIM + LANES), BF16), pltpu.VMEM((n_states, Q_ROWS, 1), F32),
            pltpu.VMEM((n_states, Q_ROWS, HEAD_DIM + LANES), F32)]


def _fox_attention(p3, neg_cum_f):
    b, t, _ = p3.shape
    tq = Q_CHAINS * Q_ROWS
    wide = tq
    nf = neg_cum_f.reshape(b, GROUP_HEADS, t // wide, wide)
    return pl.pallas_call(
        functools.partial(_fox_kernel, wide=wide),
        grid=(b, GROUP_HEADS, t // tq),
        in_specs=[pl.BlockSpec((1, tq, HEAD_DIM), lambda bi, h, i: (bi, i, COL["fq"] + h)),
                  pl.BlockSpec((1, t, HEAD_DIM), lambda bi, h, i: (bi, 0, COL["fk"] + h)),
                  pl.BlockSpec((1, t, HEAD_DIM), lambda bi, h, i: (bi, 0, COL["fv"] + h)),
                  pl.BlockSpec((1, 1, t // wide, wide), lambda bi, h, i: (bi, h, 0, 0))],
        out_specs=pl.BlockSpec((1, tq, HEAD_DIM), lambda bi, h, i: (bi, i, h)),
        out_shape=jax.ShapeDtypeStruct((b, t, GROUP_WIDTH), F32),
        scratch_shapes=[pltpu.VMEM((1, 1), F32)] + _attn_scratch(t, Q_CHAINS),
        compiler_params=_cparams(("parallel", "parallel", "arbitrary")),
        name="fox_attention",
    )(p3, p3, p3, nf)


def _neg_abs(x):
    bits = lax.bitcast_convert_type(x, jnp.uint32) | jnp.uint32(0x80000000)
    return lax.bitcast_convert_type(bits, F32)


def _stick_kernel(q_ref, k_ref, v_ref, u_ref, o_ref, r_ref, acc_ref, *, wide):
    t0 = pl.program_id(2) * (Q_CHAINS * Q_ROWS)
    chains = range(Q_CHAINS)
    qs = [q_ref[0, _chain_rows(c), :] for c in chains]
    r_ref[...] = jnp.zeros(r_ref.shape, F32)
    acc_ref[...] = jnp.zeros(acc_ref.shape, F32)
    u = u_ref[...]

    def tile(off, tail):
        ends = [(c + 1) * Q_ROWS if tail else wide for c in chains]
        k = k_ref[0, pl.ds(off, wide), :]
        z_all = [_qk_raw(q, k[:e]) * SCALE_LOG2E for q, e in zip(qs, ends)]
        later = [r_ref[c] for c in chains]
        weights = [[] for _ in chains]
        before = _own_causal(0, Q_ROWS, strict=True)
        for sb in reversed(range(wide // Q_ROWS)):
            live = [c for c in chains if ends[c] > sb * Q_ROWS]
            diag = [tail and c == sb for c in live]
            cols = slice(sb * Q_ROWS, (sb + 1) * Q_ROWS)
            zs = [z_all[c][:, cols] for c in live]
            log_beta = [jnp.minimum(z, 0.0) - jnp.log(1.0 + jnp.exp2(_neg_abs(z))) * LOG2E for z in zs]
            log_keep = [lb - z for lb, z in zip(log_beta, zs)]
            log_keep = [jnp.where(before, lk, 0.0) if d else lk for d, lk in zip(diag, log_keep)]
            inside = [jnp.dot(jnp.concatenate(_split3(lk)[:2], axis=1), u, preferred_element_type=F32)
                      for lk in log_keep]
            a = [jnp.exp2(lb + cs + later[c]) for c, lb, cs in zip(live, log_beta, inside)]
            a = [jnp.where(before, x, 0.0) if d else x for d, x in zip(diag, a)]
            for c, x, cs, lk in zip(live, a, inside, log_keep):
                weights[c].append(x.astype(BF16))
                later[c] = later[c] + cs[:, 0:1] + lk[:, 0:1]
        v = v_ref[0, pl.ds(off, wide), :]
        for c in chains:
            w = weights[c][0] if len(weights[c]) == 1 else jnp.concatenate(weights[c][::-1], axis=1)
            acc_ref[c] += jnp.dot(w, v[:ends[c]], preferred_element_type=F32)
            r_ref[c] = later[c]

    def keep_going(first_done):
        del first_done
        return jnp.max(r_ref[...]) > -STICK_DEAD_LOG2

    _sweep_tiles(t0, wide, functools.partial(tile, tail=True), functools.partial(tile, tail=False), keep_going)
    for c in chains:
        o_ref[0, _chain_rows(c), :] = acc_ref[c]


def _stick_attention(p3):
    b, t, _ = p3.shape
    tq = Q_CHAINS * Q_ROWS
    wide = tq
    ki = np.arange(Q_ROWS)
    u = (ki[:, None] > ki[None, :]).astype(np.float32)
    u = jnp.asarray(np.concatenate([u, u], axis=0), BF16)
    return pl.pallas_call(
        functools.partial(_stick_kernel, wide=wide),
        grid=(b, GROUP_HEADS, t // tq),
        in_specs=[pl.BlockSpec((1, tq, HEAD_DIM), lambda bi, h, i: (bi, i, COL["sq"] + h)),
                  pl.BlockSpec((1, t, HEAD_DIM), lambda bi, h, i: (bi, 0, COL["sk"] + h)),
                  pl.BlockSpec((1, t, HEAD_DIM), lambda bi, h, i: (bi, 0, COL["sv"] + h)),
                  pl.BlockSpec((2 * Q_ROWS, Q_ROWS), lambda bi, h, i: (0, 0))],
        out_specs=pl.BlockSpec((1, tq, HEAD_DIM), lambda bi, h, i: (bi, i, h)),
        out_shape=jax.ShapeDtypeStruct((b, t, GROUP_WIDTH), F32),
        scratch_shapes=[pltpu.VMEM((Q_CHAINS, Q_ROWS, 1), F32), pltpu.VMEM((Q_CHAINS, Q_ROWS, HEAD_DIM), F32)],
        compiler_params=_cparams(("parallel", "parallel", "arbitrary")),
        name="stick_attention",
    )(p3, p3, p3, u)


def _moba_kernel(slope_ref, q_ref, k_ref, v_ref, o_ref, kaug_ref, kmean_ref, knorm_ref, vaug_ref, m_ref, acc_ref, *,
                 n_blk, wide):
    h = pl.program_id(1)
    i = pl.program_id(2)
    tq = MOBA_BLOCK
    slope = slope_ref[h]

    @pl.when(i == 0)
    def _():
        kmean_ref[...] = jnp.zeros(kmean_ref.shape, F32)
        kb = k_ref[0].astype(F32).reshape(n_blk, tq, HEAD_DIM)
        kmean_ref[0:n_blk, :] = jnp.mean(kb, axis=1)

        def key_aux(pos, lane):
            onehot = jnp.where(lane == jnp.right_shift(pos, int(math.log2(tq))), 1.0, 0.0)
            hi, mid, lo = _split3(slope * pos.astype(F32) * (1.0 / ATTN_SCALE))
            return jnp.where(lane == n_blk, hi.astype(F32),
                             jnp.where(lane == n_blk + 1, mid.astype(F32),
                                       jnp.where(lane == n_blk + 2, lo.astype(F32), onehot)))

        _fill_key_aug(kaug_ref, k_ref, key_aux)
        _max_key_norm(k_ref, knorm_ref)
        _fill_value_aug(vaug_ref, v_ref)

    chains = range(Q_CHAINS)
    t0 = i * (Q_CHAINS * Q_ROWS)
    qs = [q_ref[0, _chain_rows(c), :] for c in chains]
    q_norms = [_row_norms(q) for q in qs]
    own = [i * Q_CHAINS + c for c in chains]
    kmean = kmean_ref[...].astype(BF16)
    blk = lax.broadcasted_iota(jnp.int32, (Q_ROWS, LANES), 1)
    blkf = blk.astype(F32)
    scores = [jnp.where(blk < own[c], _qk_raw(qs[c], kmean), -jnp.inf) for c in chains]
    sels = _topk_rounds(scores, blkf, [jnp.where(blk == own[c], 1.0, 0.0) for c in chains], MOBA_TOPK)
    q_aux = [jnp.where(blk < n_blk, jnp.where(sel > 0.5, 0.0, NEG_BIG), jnp.where(blk < n_blk + 3, 1.0, 0.0))
             for sel in sels]
    q_aug = [jnp.concatenate([q, aux.astype(BF16)], axis=1) for q, aux in zip(qs, q_aux)]
    states = [(m_ref.at[c], acc_ref.at[c]) for c in chains]
    _flash_init(m_ref, acc_ref)

    def full_tile(off):
        k = kaug_ref[pl.ds(off, wide), :]
        _online_update_multi([_qk_raw(q, k) for q in q_aug], [vaug_ref[pl.ds(off, wide), :]] * Q_CHAINS, states)

    def tail_tile(off):
        k = kaug_ref[pl.ds(off, wide), :]
        v = vaug_ref[pl.ds(off, wide), :]
        ends = [(c + 1) * Q_ROWS for c in chains]
        raws = [jnp.where(_own_causal(c, e), _qk_raw(q, k[:e]), NEG_BIG) for c, (q, e) in enumerate(zip(q_aug, ends))]
        _online_update_multi(raws, [v[:e] for e in ends], states)

    def keep_going(first_done):
        cap = slope * first_done.astype(F32) * (1.0 / ATTN_SCALE)
        return _softmax_alive(q_norms, knorm_ref[...], [cap] * Q_CHAINS, states)

    _sweep_tiles(t0, wide, tail_tile, full_tile, keep_going)
    for c in chains:
        o_ref[0, _chain_rows(c), :] = _flash_result(acc_ref.at[c])


def _moba_attention(p3, slopes):
    b, t, _ = p3.shape
    tq = Q_CHAINS * Q_ROWS
    n_blk = t // MOBA_BLOCK
    wide = tq
    assert Q_ROWS == MOBA_BLOCK and t % tq == 0 and n_blk + 3 <= LANES
    return pl.pallas_call(
        functools.partial(_moba_kernel, n_blk=n_blk, wide=wide),
        grid=(b, GROUP_HEADS, t // tq),
        in_specs=[pl.BlockSpec(memory_space=pltpu.SMEM),
                  pl.BlockSpec((1, tq, HEAD_DIM), lambda bi, h, i: (bi, i, COL["mq"] + h)),
                  pl.BlockSpec((1, t, HEAD_DIM), lambda bi, h, i: (bi, 0, COL["mk"] + h)),
                  pl.BlockSpec((1, t, HEAD_DIM), lambda bi, h, i: (bi, 0, COL["mv"] + h))],
        out_specs=pl.BlockSpec((1, tq, HEAD_DIM), lambda bi, h, i: (bi, i, h)),
        out_shape=jax.ShapeDtypeStruct((b, t, GROUP_WIDTH), F32),
        scratch_shapes=[pltpu.VMEM((t, HEAD_DIM + LANES), BF16),
                        pltpu.VMEM((LANES, HEAD_DIM), F32), pltpu.VMEM((1, 1), F32)] + _attn_scratch(t, Q_CHAINS),
        compiler_params=_cparams(("parallel", "parallel", "arbitrary")),
        name="moba_attention",
    )(slopes, p3, p3, p3)


def _nsa_compress_kernel(x_ref, pos_ref, w1_ref, w2_ref, o_ref):
    x = (x_ref[0].astype(F32) + pos_ref[...]).astype(BF16)
    hmid = jnp.dot(x, w1_ref[...], preferred_element_type=F32)
    hmid = hmid / (1.0 + jnp.exp(-hmid))
    o_ref[0] = jnp.dot(hmid.astype(BF16), w2_ref[...], preferred_element_type=F32).astype(o_ref.dtype)


def _nsa_compress(blocks, pos, w1, w2):
    r, n, width = blocks.shape
    return pl.pallas_call(
        _nsa_compress_kernel,
        grid=(r,),
        in_specs=[pl.BlockSpec((1, n, width), lambda i: (i, 0, 0)),
                  pl.BlockSpec((1, width), lambda i: (0, 0)),
                  pl.BlockSpec((width, HEAD_DIM), lambda i: (0, 0)),
                  pl.BlockSpec((HEAD_DIM, HEAD_DIM), lambda i: (0, 0))],
        out_specs=pl.BlockSpec((1, n, HEAD_DIM), lambda i: (i, 0, 0)),
        out_shape=jax.ShapeDtypeStruct((r, n, HEAD_DIM), BF16),
        compiler_params=_cparams(("parallel",)),
        name="nsa_compress",
    )(blocks, pos.reshape(1, width).astype(F32), w1.astype(BF16), w2.astype(BF16))


def _nsa_select_kernel(slope_ref, q_ref, kc_ref, vc_ref, a_ref, oc_ref, sel_ref, *, tq, n_cmp, n_blk):
    g = pl.program_id(1)
    i = pl.program_id(2)
    t0 = i * tq
    kc = kc_ref[0, 0]
    vc = vc_ref[0, 0]
    ncp = kc.shape[0]
    tok = lax.broadcasted_iota(jnp.int32, (1, ncp), 1)
    cmp_end = tok * NSA_CMP_STRIDE + (NSA_CMP_LEN - 1)
    row = lax.broadcasted_iota(jnp.int32, (tq, 1), 0)
    admissible = (cmp_end <= row + t0) & (tok < n_cmp)
    rel_end = (cmp_end - t0).astype(F32)

    imp = jnp.zeros((tq, ncp), F32)
    for hh in range(NSA_Q_PER_KV):
        slope = slope_ref[g * NSA_Q_PER_KV + hh]
        q = q_ref[0, :, hh * HEAD_DIM:(hh + 1) * HEAD_DIM]
        s = jnp.where(admissible, _qk(q, kc) + slope * rel_end, NEG_BIG)
        m = jnp.max(s, axis=-1, keepdims=True)
        p = jnp.where(admissible, jnp.exp(s - m), 0.0)
        p = p / jnp.maximum(jnp.sum(p, axis=-1, keepdims=True), F32_TINY)
        oc_ref[0, :, hh * HEAD_DIM:(hh + 1) * HEAD_DIM] = jnp.dot(
            p.astype(BF16), vc, preferred_element_type=F32)
        imp = imp + p

    imp_blk = _dot_split(imp, a_ref[...], 3)
    blk = lax.broadcasted_iota(jnp.int32, (tq, LANES), 1)
    blkf = blk.astype(F32)
    jt = jnp.right_shift(row + t0, int(math.log2(NSA_SEL_BLOCK)))
    live = (blk <= jt) & (blk < n_blk)
    forced = ((blk == 0) | (blk == jt) | (blk == jt - 1)) & live
    sel = jnp.where(forced, 1.0, 0.0)
    score = jnp.where(live & jnp.logical_not(forced), imp_blk, -jnp.inf)
    sel = _topk_rounds([score], blkf, [sel], NSA_SEL_TOPK - 3)[0]
    sel_ref[0, 0] = sel.astype(sel_ref.dtype)


def _nsa_select(p3, kc, vc, slopes, tq):
    b, t, _ = p3.shape
    n_blk = t // NSA_SEL_BLOCK
    assert n_blk <= LANES and NSA_SEL_TOPK >= 3
    n_cmp = t // NSA_CMP_STRIDE - 1
    ncp = kc.shape[2]
    ratio = NSA_SEL_BLOCK // NSA_CMP_STRIDE
    ti = np.arange(ncp)[:, None]
    bj = np.arange(LANES)[None, :]
    amat = ((ti >= ratio * bj - 1) & (ti <= ratio * bj + ratio - 1) & (ti < n_cmp) & (bj < n_blk))
    amat = jnp.asarray(amat.astype(np.float32), BF16)
    width = NSA_Q_PER_KV * HEAD_DIM
    qblk = COL["nq"] * HEAD_DIM // width
    return pl.pallas_call(
        functools.partial(_nsa_select_kernel, tq=tq, n_cmp=n_cmp, n_blk=n_blk),
        grid=(b, NSA_KV_HEADS, t // tq),
        in_specs=[pl.BlockSpec(memory_space=pltpu.SMEM),
                  pl.BlockSpec((1, tq, width), lambda bi, g, i: (bi, i, qblk + g)),
                  pl.BlockSpec((1, 1, ncp, HEAD_DIM), lambda bi, g, i: (bi, g, 0, 0)),
                  pl.BlockSpec((1, 1, ncp, HEAD_DIM), lambda bi, g, i: (bi, g, 0, 0)),
                  pl.BlockSpec((ncp, LANES), lambda bi, g, i: (0, 0))],
        out_specs=[pl.BlockSpec((1, tq, width), lambda bi, g, i: (bi, i, g)),
                   pl.BlockSpec((1, 1, tq, LANES), lambda bi, g, i: (bi, g, i, 0))],
        out_shape=[jax.ShapeDtypeStruct((b, t, GROUP_WIDTH), F32),
                   jax.ShapeDtypeStruct((b, NSA_KV_HEADS, t, LANES), BF16)],
        compiler_params=_cparams(("parallel", "parallel", "parallel")),
        name="nsa_select",
    )(slopes, p3, kc, vc, amat)


def _nsa_attn_kernel(slope_ref, q_ref, ks_ref, vs_ref, kw_ref, vw_ref, sel_ref, oc_ref, gate_ref,
                     o_ref, kaug_ref, knorm_ref, vwaug_ref, vsaug_ref, m_ref, acc_ref, *, tq, wide):
    g = pl.program_id(1)
    i = pl.program_id(2)
    t0 = pl.multiple_of(i * tq, tq)
    nh = NSA_Q_PER_KV

    @pl.when(i == 0)
    def _():
        sel_shift = int(math.log2(NSA_SEL_BLOCK))
        _fill_key_aug(kaug_ref, ks_ref,
                      lambda pos, lane: jnp.where(lane == jnp.right_shift(pos, sel_shift), 1.0, 0.0))
        _max_key_norm(ks_ref, knorm_ref)
        _fill_value_aug(vsaug_ref, vs_ref)
        _fill_value_aug(vwaug_ref, vw_ref)

    _flash_init(m_ref, acc_ref)
    sel_bias = jnp.where(sel_ref[0, 0].astype(F32) > 0.5, 0.0, NEG_BIG).astype(BF16)
    qs = [q_ref[0, :, hh * HEAD_DIM:(hh + 1) * HEAD_DIM] for hh in range(nh)]
    q_aug = [jnp.concatenate([q, sel_bias], axis=1) for q in qs]
    slope_raw = [slope_ref[g * nh + hh] * (1.0 / ATTN_SCALE) for hh in range(nh)]
    row = lax.broadcasted_iota(jnp.int32, (tq, 1), 0)

    def rel_pos(off, width):
        return lax.broadcasted_iota(jnp.int32, (1, width), 1) + (off - t0)

    sel_states = [(m_ref.at[hh], acc_ref.at[hh]) for hh in range(nh)]
    win_states = [(m_ref.at[nh + hh], acc_ref.at[nh + hh]) for hh in range(nh)]
    q_norms = [_row_norms(q) for q in qs]

    def sel_scores(off, tail):
        rel = rel_pos(off, wide)
        relf = rel.astype(F32)
        k = kaug_ref[pl.ds(off, wide), :]
        raws = [_qk_raw(q_aug[hh], k) + slope_raw[hh] * relf for hh in range(nh)]
        if tail:
            raws = [jnp.where(rel <= row, raw, NEG_BIG) for raw in raws]
        return raws, [vsaug_ref[pl.ds(off, wide), :]] * nh

    def window_scores(off, width, far):
        rel = rel_pos(off, width)
        relf = rel.astype(F32)
        k = kw_ref[0, pl.ds(off, width), :]
        ok = rel <= row
        if far:
            ok = ok & (row - rel < NSA_WINDOW)
        raws = [jnp.where(ok, _qk_raw(qs[hh], k) + slope_raw[hh] * relf, NEG_BIG) for hh in range(nh)]
        return raws, [vwaug_ref[pl.ds(off, width), :]] * nh

    def sel_tile(off, tail):
        raws, vs = sel_scores(off, tail)
        _online_update_multi(raws, vs, sel_states)

    def keep_going(first_done):
        gap = (first_done - t0).astype(F32)
        return _softmax_alive(q_norms, knorm_ref[...], [s * gap for s in slope_raw], sel_states)

    _sweep_tiles(t0, wide, functools.partial(sel_tile, tail=True), functools.partial(sel_tile, tail=False),
                 keep_going)

    n_back = NSA_WINDOW // tq
    for back in range(n_back + 1):
        @pl.when((i == back) if back < n_back else (i >= back))
        def _(back=back):
            raws, vs = window_scores(pl.multiple_of(t0 - back * tq, tq), (back + 1) * tq, back == n_back)
            _online_update_multi(raws, vs, win_states)

    gates = gate_ref[0, 0]
    gates = 1.0 / (1.0 + jnp.exp(-gates))
    for hh in range(nh):
        c0 = hh * NSA_N_BRANCHES
        o_c = oc_ref[0, :, hh * HEAD_DIM:(hh + 1) * HEAD_DIM]
        o_s = _flash_result(acc_ref.at[hh])
        o_w = _flash_result(acc_ref.at[nh + hh])
        o_ref[0, :, hh * HEAD_DIM:(hh + 1) * HEAD_DIM] = (
            gates[:, c0:c0 + 1] * o_c + gates[:, c0 + 1:c0 + 2] * o_s + gates[:, c0 + 2:c0 + 3] * o_w)


def _nsa_attention(p3, sel, o_cmp, gate_logits, slopes, tq):
    b, t, _ = p3.shape
    assert tq == KV_UNIT and NSA_WINDOW == 2 * tq and t // NSA_SEL_BLOCK <= LANES
    width = NSA_Q_PER_KV * HEAD_DIM
    qblk = COL["nq"] * HEAD_DIM // width
    kv = lambda name: pl.BlockSpec((1, t, HEAD_DIM), lambda bi, g, i: (bi, 0, COL[name] + g))
    nst = 2 * NSA_Q_PER_KV
    return pl.pallas_call(
        functools.partial(_nsa_attn_kernel, tq=tq, wide=min(KV_WIDE, t)),
        grid=(b, NSA_KV_HEADS, t // tq),
        in_specs=[pl.BlockSpec(memory_space=pltpu.SMEM),
                  pl.BlockSpec((1, tq, width), lambda bi, g, i: (bi, i, qblk + g)),
                  kv("nks"), kv("nvs"), kv("nkw"), kv("nvw"),
                  pl.BlockSpec((1, 1, tq, LANES), lambda bi, g, i: (bi, g, i, 0)),
                  pl.BlockSpec((1, tq, width), lambda bi, g, i: (bi, i, g)),
                  pl.BlockSpec((1, 1, tq, LANES), lambda bi, g, i: (bi, g, i, 0))],
        out_specs=pl.BlockSpec((1, tq, width), lambda bi, g, i: (bi, i, g)),
        out_shape=jax.ShapeDtypeStruct((b, t, GROUP_WIDTH), F32),
        scratch_shapes=[pltpu.VMEM((t, HEAD_DIM + LANES), BF16), pltpu.VMEM((1, 1), F32),
                        pltpu.VMEM((t, HEAD_DIM + LANES), BF16)] + _attn_scratch(t, nst),
        compiler_params=_cparams(("parallel", "parallel", "arbitrary")),
        name="nsa_attention",
    )(slopes, p3, p3, p3, p3, p3, sel, o_cmp, gate_logits)


def _nsa_blocks(p3, name):
    b, t, _ = p3.shape
    c0 = COL[name] * HEAD_DIM
    x = p3[:, :, c0:c0 + NSA_KV_HEADS * HEAD_DIM].reshape(b, t // NSA_CMP_STRIDE, NSA_CMP_STRIDE,
                                                         NSA_KV_HEADS, HEAD_DIM)
    x = x.transpose(0, 3, 1, 2, 4).reshape(b * NSA_KV_HEADS, t // NSA_CMP_STRIDE, NSA_CMP_STRIDE * HEAD_DIM)
    blocks = jnp.concatenate([x[:, :-1], x[:, 1:]], axis=-1)
    return jnp.pad(blocks, ((0, 0), (0, 1), (0, 0)))


def _alibi_slopes():
    n = 2 * GROUP_HEADS
    slopes = 2.0 ** (-8.0 * np.arange(1, n + 1) / n)
    return jnp.asarray(slopes[0::2], F32), jnp.asarray(slopes[1::2], F32)


def _regroup_w_in(w_in):
    main = jnp.concatenate([w_in[:, REF_OFF[n][0]:REF_OFF[n][0] + REF_OFF[n][1]] for n, _ in _MAIN_PIECES],
                           axis=1).astype(BF16)
    ng0, ngw = REF_OFF["ng"]
    ff0, ffw = REF_OFF["ff"]
    small = jnp.concatenate([w_in[:, ng0:ng0 + ngw], w_in[:, ff0:ff0 + ffw]], axis=1)
    small = jnp.pad(small, ((0, 0), (0, LANES - ngw - ffw))).astype(BF16)
    return main, small


def _mixer(x2d, b, t, norm_gain, w_in, forget_bias, pos_k, w1_k, w2_k, pos_v, w1_v, w2_v, group_gain, w_out):
    m = x2d.shape[0]
    h = _rmsnorm(x2d, norm_gain, BF16)
    w_main, w_small = _regroup_w_in(w_in)
    tm = _pick_tile(m, (1024, 512, 256))
    p3 = _matmul(h, w_main, BF16, tm, 512).reshape(b, t, MAIN_BLOCKS * HEAD_DIM)
    small = _matmul(h, w_small, F32, tm, LANES).reshape(b, t, LANES)
    slopes_moba, slopes_nsa = _alibi_slopes()

    o_a = _moba_attention(p3, slopes_moba)

    n_gate = NSA_N_BRANCHES * GROUP_HEADS
    kc = _nsa_compress(_nsa_blocks(p3, "nkc"), pos_k, w1_k, w2_k)
    vc = _nsa_compress(_nsa_blocks(p3, "nvc"), pos_v, w1_v, w2_v)
    ncp = kc.shape[1]
    kc = kc.reshape(b, NSA_KV_HEADS, ncp, HEAD_DIM)
    vc = vc.reshape(b, NSA_KV_HEADS, ncp, HEAD_DIM)
    o_cmp, sel = _nsa_select(p3, kc, vc, slopes_nsa, _pick_tile(t, (512, 256)))
    per_group = n_gate // NSA_KV_HEADS
    gate_logits = small[:, :, :n_gate].reshape(b, t, NSA_KV_HEADS, per_group).transpose(0, 2, 1, 3)
    gate_logits = jnp.pad(gate_logits, ((0, 0), (0, 0), (0, 0), (0, LANES - per_group)))
    o_b = _nsa_attention(p3, sel, o_cmp, gate_logits, slopes_nsa, KV_UNIT)

    ff = small[:, :, n_gate:n_gate + GROUP_HEADS] + forget_bias
    neg_cum_f = _neg_cum_logf(ff.transpose(0, 2, 1).reshape(b * GROUP_HEADS, t))
    o_c = _fox_attention(p3, neg_cum_f)

    o_d = _stick_attention(p3)

    on = _groupnorm([o.reshape(m, GROUP_WIDTH) for o in (o_a, o_b, o_c, o_d)], group_gain)
    return _matmul_residual(on, w_out, x2d, tm, 512)


def _cast_pad_kernel(w_ref, o_ref, *, n_row_tiles, cols):
    @pl.when(pl.program_id(0) < n_row_tiles)
    def _():
        o_ref[:, 0:cols] = w_ref[0].astype(BF16)
        if o_ref.shape[1] > cols:
            o_ref[:, cols:] = jnp.zeros((o_ref.shape[0], o_ref.shape[1] - cols), BF16)

    @pl.when(pl.program_id(0) >= n_row_tiles)
    def _():
        o_ref[...] = jnp.zeros(o_ref.shape, BF16)


def _weight_bf16(stacked, layer, rows_out, cols_out, tr=256):
    _, rows, cols = stacked.shape
    assert rows % tr == 0 and rows_out % tr == 0 and cols % LANES == 0
    n_row_tiles = rows // tr
    return pl.pallas_call(
        functools.partial(_cast_pad_kernel, n_row_tiles=n_row_tiles, cols=cols),
        grid=(rows_out // tr,),
        in_specs=[pl.BlockSpec((1, tr, cols), lambda j: (layer, jnp.minimum(j, n_row_tiles - 1), 0))],
        out_specs=pl.BlockSpec((tr, cols_out), lambda j: (j, 0)),
        out_shape=jax.ShapeDtypeStruct((rows_out, cols_out), BF16),
        compiler_params=_cparams(("parallel",)),
        name="weight_bf16",
    )(stacked)


def kernel(x, ffn1_norm, ffn1_w_gate, ffn1_w_up, ffn1_w_down, mix_norm, w_in, fox_forget_bias, nsa_cmp_pos_k, nsa_cmp_w1_k, nsa_cmp_w2_k, nsa_cmp_pos_v, nsa_cmp_w1_v, nsa_cmp_w2_v, group_norm, w_out, ffn2_norm, ffn2_w_gate, ffn2_w_up, ffn2_w_down, final_norm):
    b, t, d = x.shape
    x2d = x.reshape(b * t, d)
    for i in range(ffn1_norm.shape[0]):
        x2d = _ffn(x2d, ffn1_norm[i], ffn1_w_gate, ffn1_w_up, ffn1_w_down, i)
        x2d = _mixer(x2d, b, t, mix_norm[i], w_in[i], fox_forget_bias[i], nsa_cmp_pos_k[i], nsa_cmp_w1_k[i],
                     nsa_cmp_w2_k[i], nsa_cmp_pos_v[i], nsa_cmp_w1_v[i], nsa_cmp_w2_v[i],
                     group_norm[i], _weight_bf16(w_out, i, d, d))
        x2d = _ffn(x2d, ffn2_norm[i], ffn2_w_gate, ffn2_w_up, ffn2_w_down, i)
    return _rmsnorm(x2d, final_norm, F32).reshape(b, t, d)
```

```python
import functools
import math

import numpy as np
import jax
import jax.numpy as jnp
from jax import lax
from jax.experimental import pallas as pl
from jax.experimental.pallas import tpu as pltpu

F32 = jnp.float32
BF16 = jnp.bfloat16

HEAD_DIM = 128
GROUP_HEADS = 8
GROUP_WIDTH = GROUP_HEADS * HEAD_DIM
N_MIXERS = 4
FFN_RESIDUAL = 0.5
RMS_EPS = 1e-6

MOBA_BLOCK = 256
MOBA_TOPK = 3

NSA_KV_HEADS = 2
NSA_Q_PER_KV = 4
NSA_CMP_STRIDE = 16
NSA_CMP_LEN = 32
NSA_SEL_BLOCK = 64
NSA_SEL_TOPK = 16
NSA_WINDOW = 512
NSA_N_BRANCHES = 3

ATTN_SCALE = HEAD_DIM ** -0.5
LOG2E = math.log2(math.e)
SCALE_LOG2E = ATTN_SCALE * LOG2E
NEG_BIG = -1e30
STICK_DEAD_LOG2 = 160.0
FLASH_DEAD_LOG2 = 170.0
BOUND_SLACK = 1.001
KV_UNIT = 256
KV_WIDE = 1024
Q_ROWS = 256
Q_CHAINS = 4
F32_TINY = float(np.finfo(np.float32).tiny)

LANES = 128
VMEM_LIMIT = 56 * 1024 * 1024
GATEUP_TILE = 256
FFN_PAD = 512

_MAIN_PIECES = (
    ("mq", 8), ("mk", 8), ("mv", 8), ("nq", 8),
    ("nkc", 2), ("nvc", 2), ("nks", 2), ("nvs", 2), ("nkw", 2), ("nvw", 2),
    ("fq", 8), ("fk", 8), ("fv", 8), ("sq", 8), ("sk", 8), ("sv", 8),
)
COL = {}
_o = 0
for _n, _w in _MAIN_PIECES:
    COL[_n] = _o
    _o += _w
MAIN_BLOCKS = _o

_REF_SIZES = (
    ("mq", 1024), ("mk", 1024), ("mv", 1024), ("nq", 1024),
    ("nkc", 256), ("nvc", 256), ("nks", 256), ("nvs", 256), ("nkw", 256), ("nvw", 256),
    ("ng", 24),
    ("fq", 1024), ("fk", 1024), ("fv", 1024), ("ff", 8),
    ("sq", 1024), ("sk", 1024), ("sv", 1024),
)
REF_OFF = {}
_o = 0
for _n, _w in _REF_SIZES:
    REF_OFF[_n] = (_o, _w)
    _o += _w


def _cparams(sem):
    return pltpu.CompilerParams(dimension_semantics=sem, vmem_limit_bytes=VMEM_LIMIT)


def _rms_kernel(x_ref, g_ref, o_ref):
    x = x_ref[...]
    ms = jnp.mean(x * x, axis=-1, keepdims=True)
    o_ref[...] = (x * lax.rsqrt(ms + RMS_EPS) * g_ref[...]).astype(o_ref.dtype)


def _rmsnorm(x2d, gain, out_dtype, tm=256):
    m, d = x2d.shape
    return pl.pallas_call(
        _rms_kernel,
        grid=(m // tm,),
        in_specs=[pl.BlockSpec((tm, d), lambda i: (i, 0)),
                  pl.BlockSpec((1, d), lambda i: (0, 0))],
        out_specs=pl.BlockSpec((tm, d), lambda i: (i, 0)),
        out_shape=jax.ShapeDtypeStruct((m, d), out_dtype),
        compiler_params=_cparams(("parallel",)),
        name="rmsnorm",
    )(x2d, gain.reshape(1, d).astype(F32))


def _emit_normed(x_new, gain_ref, xg_ref, ssq_ref):
    xg_ref[...] = (x_new * gain_ref[...]).astype(BF16)
    ssq_ref[...] = jnp.broadcast_to(jnp.sum(x_new * x_new, axis=-1, keepdims=True), ssq_ref.shape)


def _rstd(ssq_ref, d):
    ssq = ssq_ref[...]
    total = ssq[:, 0:1]
    for p in range(1, ssq.shape[1] // LANES):
        total = total + ssq[:, p * LANES:p * LANES + 1]
    return lax.rsqrt(total * (1.0 / d) + RMS_EPS)


def _prenorm_kernel(x_ref, g_ref, xg_ref, ssq_ref):
    _emit_normed(x_ref[...], g_ref, xg_ref, ssq_ref)


def _prenorm(x2d, gain, tm=256):
    m, d = x2d.shape
    return pl.pallas_call(
        _prenorm_kernel,
        grid=(m // tm,),
        in_specs=[pl.BlockSpec((tm, d), lambda i: (i, 0)),
                  pl.BlockSpec((1, d), lambda i: (0, 0))],
        out_specs=[pl.BlockSpec((tm, d), lambda i: (i, 0)), pl.BlockSpec((tm, LANES), lambda i: (i, 0))],
        out_shape=[jax.ShapeDtypeStruct((m, d), BF16), jax.ShapeDtypeStruct((m, LANES), F32)],
        compiler_params=_cparams(("parallel",)),
        name="prenorm",
    )(x2d, gain.reshape(1, d).astype(F32))


def _groupnorm_kernel(a_ref, b_ref, c_ref, d_ref, g_ref, o_ref):
    for gi, ref in enumerate((a_ref, b_ref, c_ref, d_ref)):
        x = ref[...]
        ms = jnp.mean(x * x, axis=-1, keepdims=True)
        lo, hi = gi * GROUP_WIDTH, (gi + 1) * GROUP_WIDTH
        o_ref[:, lo:hi] = (x * lax.rsqrt(ms + RMS_EPS) * g_ref[:, lo:hi]).astype(o_ref.dtype)


def _groupnorm(parts, gain, tm=256):
    m = parts[0].shape[0]
    d = GROUP_WIDTH * N_MIXERS
    part_spec = pl.BlockSpec((tm, GROUP_WIDTH), lambda i: (i, 0))
    return pl.pallas_call(
        _groupnorm_kernel,
        grid=(m // tm,),
        in_specs=[part_spec] * 4 + [pl.BlockSpec((1, d), lambda i: (0, 0))],
        out_specs=pl.BlockSpec((tm, d), lambda i: (i, 0)),
        out_shape=jax.ShapeDtypeStruct((m, d), BF16),
        compiler_params=_cparams(("parallel",)),
        name="groupnorm",
    )(*parts, gain.reshape(1, d).astype(F32))


def _mm_normed_kernel(a_ref, ssq_ref, w_ref, o_ref):
    acc = jnp.dot(a_ref[...], w_ref[...], preferred_element_type=F32)
    o_ref[...] = (acc * _rstd(ssq_ref, a_ref.shape[1])).astype(o_ref.dtype)


def _matmul_normed(normed, w, out_dtype, tm, tn):
    a, ssq = normed
    m, k = a.shape
    n = w.shape[1]
    return pl.pallas_call(
        _mm_normed_kernel,
        grid=(m // tm, n // tn),
        in_specs=[pl.BlockSpec((tm, k), lambda i, j: (i, 0)),
                  pl.BlockSpec((tm, ssq.shape[1]), lambda i, j: (i, 0)),
                  pl.BlockSpec((k, tn), lambda i, j: (0, j))],
        out_specs=pl.BlockSpec((tm, tn), lambda i, j: (i, j)),
        out_shape=jax.ShapeDtypeStruct((m, n), out_dtype),
        compiler_params=_cparams(("parallel", "arbitrary")),
        name="matmul",
    )(a, ssq, w)


def _normed_outputs(m, n, tm, tn, index):
    specs = [pl.BlockSpec((tm, tn), index), pl.BlockSpec((tm, LANES), index)]
    shapes = [jax.ShapeDtypeStruct((m, n), BF16), jax.ShapeDtypeStruct((m, LANES * (n // tn)), F32)]
    return specs, shapes


def _mm_res_kernel(a_ref, w_ref, x_ref, g_ref, o_ref, xg_ref, ssq_ref):
    x_new = x_ref[...] + jnp.dot(a_ref[...], w_ref[...], preferred_element_type=F32)
    o_ref[...] = x_new
    _emit_normed(x_new, g_ref, xg_ref, ssq_ref)


def _matmul_residual(a, w, x, next_gain, tm, tn):
    m, k = a.shape
    n = w.shape[1]
    index = lambda i, j: (i, j)
    n_specs, n_shapes = _normed_outputs(m, n, tm, tn, index)
    x_new, xg, ssq = pl.pallas_call(
        _mm_res_kernel,
        grid=(m // tm, n // tn),
        in_specs=[pl.BlockSpec((tm, k), lambda i, j: (i, 0)),
                  pl.BlockSpec((k, tn), lambda i, j: (0, j)),
                  pl.BlockSpec((tm, tn), index),
                  pl.BlockSpec((1, tn), lambda i, j: (0, j))],
        out_specs=[pl.BlockSpec((tm, tn), index)] + n_specs,
        out_shape=[jax.ShapeDtypeStruct((m, n), F32)] + n_shapes,
        compiler_params=_cparams(("parallel", "arbitrary")),
        name="matmul_residual",
    )(a, w, x, next_gain.reshape(1, n).astype(F32))
    return x_new, (xg, ssq)


def _gateup_kernel(h_ref, ssq_ref, wg_ref, wu_ref, o_ref, rstd_ref, rstd_res_ref, *, n_valid):
    @pl.when(pl.program_id(1) == 0)
    def _():
        rstd = _rstd(ssq_ref, h_ref.shape[1])
        rstd_ref[...] = rstd
        rstd_res_ref[...] = rstd * FFN_RESIDUAL

    @pl.when(pl.program_id(1) < n_valid)
    def _():
        h = h_ref[...]
        g = jnp.dot(h, wg_ref[0].astype(BF16), preferred_element_type=F32) * rstd_ref[...]
        u = jnp.dot(h, wu_ref[0].astype(BF16), preferred_element_type=F32)
        o_ref[...] = (g / (1.0 + jnp.exp(-g)) * u * rstd_res_ref[...]).astype(o_ref.dtype)

    @pl.when(pl.program_id(1) >= n_valid)
    def _():
        o_ref[...] = jnp.zeros(o_ref.shape, o_ref.dtype)


def _gateup(normed, wg, wu, layer, f_out, tm, tf):
    h, ssq = normed
    m, d = h.shape
    f = wg.shape[2]
    assert f % tf == 0 and f_out % tf == 0
    n_valid = f // tf
    w_spec = pl.BlockSpec((1, d, tf), lambda i, j: (layer, 0, jnp.minimum(j, n_valid - 1)))
    once = dict(pipeline_mode=pl.Buffered(1))
    return pl.pallas_call(
        functools.partial(_gateup_kernel, n_valid=n_valid),
        grid=(m // tm, f_out // tf),
        in_specs=[pl.BlockSpec((tm, d), lambda i, j: (i, 0), **once),
                  pl.BlockSpec((tm, ssq.shape[1]), lambda i, j: (i, 0), **once), w_spec, w_spec],
        out_specs=pl.BlockSpec((tm, tf), lambda i, j: (i, j)),
        out_shape=jax.ShapeDtypeStruct((m, f_out), BF16),
        scratch_shapes=[pltpu.VMEM((tm, 1), F32), pltpu.VMEM((tm, 1), F32)],
        compiler_params=_cparams(("parallel", "arbitrary")),
        name="ffn_gateup",
    )(h, ssq, wg, wu)


def _down_kernel(a_ref, w_ref, x_ref, *rest, emit):
    o_ref = rest[1] if emit else rest[0]
    @pl.when(pl.program_id(2) == 0)
    def _():
        o_ref[...] = x_ref[...] + jnp.dot(a_ref[...], w_ref[...], preferred_element_type=F32)

    @pl.when(pl.program_id(2) != 0)
    def _():
        o_ref[...] = o_ref[...] + jnp.dot(a_ref[...], w_ref[...], preferred_element_type=F32)

    if emit:
        g_ref, _, xg_ref, ssq_ref = rest

        @pl.when(pl.program_id(2) == pl.num_programs(2) - 1)
        def _():
            _emit_normed(o_ref[...], g_ref, xg_ref, ssq_ref)


def _down(act, wd, x, next_gain, tm, tn, tk):
    m, f = act.shape
    n = wd.shape[1]
    emit = next_gain is not None
    index = lambda i, j, k: (i, j)
    in_specs = [pl.BlockSpec((tm, tk), lambda i, j, k: (i, k)),
                pl.BlockSpec((tk, tn), lambda i, j, k: (k, j)),
                pl.BlockSpec((tm, tn), index)]
    out_specs, out_shape, args = [pl.BlockSpec((tm, tn), index)], [jax.ShapeDtypeStruct((m, n), F32)], [act, wd, x]
    if emit:
        n_specs, n_shapes = _normed_outputs(m, n, tm, tn, index)
        in_specs.append(pl.BlockSpec((1, tn), lambda i, j, k: (0, j)))
        out_specs, out_shape = out_specs + n_specs, out_shape + n_shapes
        args.append(next_gain.reshape(1, n).astype(F32))
    outs = pl.pallas_call(
        functools.partial(_down_kernel, emit=emit),
        grid=(m // tm, n // tn, f // tk),
        in_specs=in_specs,
        out_specs=out_specs,
        out_shape=out_shape,
        compiler_params=_cparams(("parallel", "parallel", "arbitrary")),
        name="ffn_down",
    )(*args)
    return (outs[0], (outs[1], outs[2])) if emit else (outs[0], None)


def _pick_tile(n, prefs):
    for t in prefs:
        if n % t == 0:
            return t
    return n


def _ffn(x2d, normed, wg, wu, wd, layer, next_gain):
    m, d = x2d.shape
    f = wg.shape[2]
    fp = -(-f // FFN_PAD) * FFN_PAD
    tm = _pick_tile(m, (1024, 512, 256))
    act = _gateup(normed, wg, wu, layer, fp, _pick_tile(m, (2048, 1024, 512, 256)),
                  _pick_tile(f, (GATEUP_TILE, 128)))
    return _down(act, _weight_bf16(wd, layer, fp, d), x2d, next_gain, tm, _pick_tile(d, (1024, 512, 256, 128)),
                 _pick_tile(fp, (2816, 1024, 512, 256, 128)))


def _qk_raw(q, k):
    return lax.dot_general(q, k, (((1,), (1,)), ((), ())), preferred_element_type=F32)


def _qk(q, k):
    return _qk_raw(q, k) * ATTN_SCALE


def _online_update_multi(raws, vs, states):
    m_prev = [m_ref[...] for m_ref, _ in states]
    m_new = [jnp.maximum(mp, jnp.max(raw, axis=-1, keepdims=True)) for mp, raw in zip(m_prev, raws)]
    ps = [jnp.exp2((raw - mn) * SCALE_LOG2E).astype(BF16) for raw, mn in zip(raws, m_new)]
    alphas = [jnp.exp2((mp - mn) * SCALE_LOG2E) for mp, mn in zip(m_prev, m_new)]
    pvs = [jnp.dot(p, v, preferred_element_type=F32) for p, v in zip(ps, vs)]
    for (m_ref, acc_ref), mn, a, pv in zip(states, m_new, alphas, pvs):
        acc_ref[...] = a * acc_ref[...] + pv
        m_ref[...] = mn


def _flash_init(m_ref, acc_ref):
    m_ref[...] = jnp.full(m_ref.shape, NEG_BIG, F32)
    acc_ref[...] = jnp.zeros(acc_ref.shape, F32)


def _flash_result(acc_ref):
    acc = acc_ref[...]
    return acc[:, :HEAD_DIM] / acc[:, HEAD_DIM:HEAD_DIM + 1]


def _fill_value_aug(vaug_ref, v_ref):
    def body(c, carry):
        off = pl.multiple_of(c * KV_UNIT, KV_UNIT)
        lane = lax.broadcasted_iota(jnp.int32, (KV_UNIT, LANES), 1)
        vaug_ref[pl.ds(off, KV_UNIT), 0:HEAD_DIM] = v_ref[0, pl.ds(off, KV_UNIT), :]
        vaug_ref[pl.ds(off, KV_UNIT), HEAD_DIM:] = jnp.where(lane == 0, 1.0, 0.0).astype(BF16)
        return carry

    lax.fori_loop(0, vaug_ref.shape[0] // KV_UNIT, body, 0)


def _sweep_tiles(t0, wide, tail_tile, full_tile, keep_going):
    n_full = t0 // wide
    tail_off = pl.multiple_of(n_full * wide, wide)

    def full_off(jj):
        return pl.multiple_of((n_full - 1 - jj) * wide, wide)

    def step(state):
        jj, _ = state
        full_tile(full_off(jj))
        return jj + 1, keep_going(full_off(jj))

    tail_tile(tail_off)
    lax.while_loop(lambda s: (s[0] < n_full) & s[1], step, (jnp.int32(0), keep_going(tail_off)))


def _row_norms(q):
    q = q.astype(F32)
    return jnp.sqrt(jnp.sum(q * q, axis=-1, keepdims=True))


def _max_key_norm(k_ref, out_ref):
    def body(c, mx):
        k = k_ref[0, pl.ds(pl.multiple_of(c * KV_UNIT, KV_UNIT), KV_UNIT), :].astype(F32)
        return jnp.maximum(mx, jnp.max(jnp.sum(k * k, axis=-1, keepdims=True), axis=0, keepdims=True))

    out_ref[...] = jnp.sqrt(lax.fori_loop(0, k_ref.shape[1] // KV_UNIT, body, jnp.zeros((1, 1), F32)))


def _softmax_alive(q_norms, k_norm, bias_caps, states):
    gaps = [qn * (k_norm * BOUND_SLACK) + cap - m_ref[...]
            for qn, cap, (m_ref, _) in zip(q_norms, bias_caps, states)]
    worst = jnp.max(functools.reduce(jnp.maximum, gaps))
    return (worst + 1.0) * SCALE_LOG2E > -FLASH_DEAD_LOG2


def _fill_key_aug(kaug_ref, k_ref, aux_fn):
    t = kaug_ref.shape[0]

    def body(c, carry):
        off = pl.multiple_of(c * KV_UNIT, KV_UNIT)
        pos = lax.broadcasted_iota(jnp.int32, (KV_UNIT, LANES), 0) + off
        lane = lax.broadcasted_iota(jnp.int32, (KV_UNIT, LANES), 1)
        kaug_ref[pl.ds(off, KV_UNIT), 0:HEAD_DIM] = k_ref[0, pl.ds(off, KV_UNIT), :]
        kaug_ref[pl.ds(off, KV_UNIT), HEAD_DIM:] = aux_fn(pos, lane).astype(BF16)
        return carry

    lax.fori_loop(0, t // KV_UNIT, body, 0)


def _topk_rounds(scores, blkf, sels, rounds):
    for _ in range(rounds):
        mx = [jnp.max(s, axis=-1, keepdims=True) for s in scores]
        idx = [jnp.min(jnp.where(s == m, blkf, 1e9), axis=-1, keepdims=True) for s, m in zip(scores, mx)]
        picks = [(blkf == i) & (m > -jnp.inf) for i, m in zip(idx, mx)]
        sels = [jnp.where(p, 1.0, sel) for p, sel in zip(picks, sels)]
        scores = [jnp.where(p, -jnp.inf, s) for p, s in zip(picks, scores)]
    return sels


def _split3(x):
    hi = x.astype(BF16)
    r1 = x - hi.astype(F32)
    mid = r1.astype(BF16)
    lo = (r1 - mid.astype(F32)).astype(BF16)
    return hi, mid, lo


def _dot_split(x, w_bf16, terms):
    parts = _split3(x)[:terms]
    out = jnp.dot(parts[0], w_bf16, preferred_element_type=F32)
    for p in parts[1:]:
        out = out + jnp.dot(p, w_bf16, preferred_element_type=F32)
    return out


def _logf_cumsum_kernel(x_ref, tri_ref, low_ref, o_ref):
    x = x_ref[0]
    logf = jnp.minimum(x, 0.0) - jnp.log1p(jnp.exp(-jnp.abs(x)))
    within = _dot_split(logf, tri_ref[...], 3)
    hi, mid, lo = _split3(logf)
    low = low_ref[...]
    before = (jnp.dot(low, hi, preferred_element_type=F32) + jnp.dot(low, mid, preferred_element_type=F32)
              + jnp.dot(low, lo, preferred_element_type=F32))
    o_ref[0] = -(within + jnp.sum(before, axis=-1, keepdims=True))


def _neg_cum_logf(logits_rows):
    r, t = logits_rows.shape
    nc = t // LANES
    li = np.arange(LANES)
    tri = jnp.asarray((li[:, None] <= li[None, :]).astype(np.float32), BF16)
    ci = np.arange(nc)
    low = jnp.asarray((ci[None, :] < ci[:, None]).astype(np.float32), BF16)
    out = pl.pallas_call(
        _logf_cumsum_kernel,
        grid=(r,),
        in_specs=[pl.BlockSpec((1, nc, LANES), lambda i: (i, 0, 0)),
                  pl.BlockSpec((LANES, LANES), lambda i: (0, 0)),
                  pl.BlockSpec((nc, nc), lambda i: (0, 0))],
        out_specs=pl.BlockSpec((1, nc, LANES), lambda i: (i, 0, 0)),
        out_shape=jax.ShapeDtypeStruct((r, nc, LANES), F32),
        compiler_params=_cparams(("parallel",)),
        name="fox_logf_cumsum",
    )(logits_rows.reshape(r, nc, LANES), tri, low)
    return out.reshape(r, t)


def _chain_rows(c):
    return slice(c * Q_ROWS, (c + 1) * Q_ROWS)


def _own_causal(c, width, strict=False):
    col = lax.broadcasted_iota(jnp.int32, (1, width), 1)
    row = lax.broadcasted_iota(jnp.int32, (Q_ROWS, 1), 0) + c * Q_ROWS
    return (col < row) if strict else (col <= row)


def _fox_kernel(q_ref, k_ref, v_ref, nf_ref, o_ref, knorm_ref, vaug_ref, m_ref, acc_ref, *, wide):
    t0 = pl.program_id(2) * (Q_CHAINS * Q_ROWS)

    @pl.when(pl.program_id(2) == 0)
    def _():
        _max_key_norm(k_ref, knorm_ref)
        _fill_value_aug(vaug_ref, v_ref)

    qs = [q_ref[0, _chain_rows(c), :] for c in range(Q_CHAINS)]
    q_norms = [_row_norms(q) for q in qs]
    states = [(m_ref.at[c], acc_ref.at[c]) for c in range(Q_CHAINS)]
    _flash_init(m_ref, acc_ref)

    def keep_going(first_done):
        prev = jnp.maximum(first_done // wide - 1, 0)
        cap = nf_ref[0, 0, pl.ds(prev, 1), :][:, wide - 1:wide] * (1.0 / ATTN_SCALE)
        return _softmax_alive(q_norms, knorm_ref[...], [cap] * Q_CHAINS, states)

    def full_tile(off):
        bias = nf_ref[0, 0, pl.ds(off // wide, 1), :] * (1.0 / ATTN_SCALE)
        k = k_ref[0, pl.ds(off, wide), :]
        raws = [_qk_raw(q, k) + bias for q in qs]
        _online_update_multi(raws, [vaug_ref[pl.ds(off, wide), :]] * Q_CHAINS, states)

    def tail_tile(off):
        bias = nf_ref[0, 0, pl.ds(off // wide, 1), :] * (1.0 / ATTN_SCALE)
        k = k_ref[0, pl.ds(off, wide), :]
        v = vaug_ref[pl.ds(off, wide), :]
        ends = [(c + 1) * Q_ROWS for c in range(Q_CHAINS)]
        raws = [jnp.where(_own_causal(c, e), _qk_raw(q, k[:e]) + bias[:, :e], NEG_BIG)
                for c, (q, e) in enumerate(zip(qs, ends))]
        _online_update_multi(raws, [v[:e] for e in ends], states)

    _sweep_tiles(t0, wide, tail_tile, full_tile, keep_going)
    for c in range(Q_CHAINS):
        o_ref[0, _chain_rows(c), :] = _flash_result(acc_ref.at[c])


def _attn_scratch(t, n_states):
    return [pltpu.VMEM((t, HEAD_DIM + LANES), BF16), pltpu.VMEM((n_states, Q_ROWS, 1), F32),
            pltpu.VMEM((n_states, Q_ROWS, HEAD_DIM + LANES), F32)]


def _fox_attention(p3, neg_cum_f):
    b, t, _ = p3.shape
    tq = Q_CHAINS * Q_ROWS
    wide = tq
    nf = neg_cum_f.reshape(b, GROUP_HEADS, t // wide, wide)
    return pl.pallas_call(
        functools.partial(_fox_kernel, wide=wide),
        grid=(b, GROUP_HEADS, t // tq),
        in_specs=[pl.BlockSpec((1, tq, HEAD_DIM), lambda bi, h, i: (bi, i, COL["fq"] + h)),
                  pl.BlockSpec((1, t, HEAD_DIM), lambda bi, h, i: (bi, 0, COL["fk"] + h)),
                  pl.BlockSpec((1, t, HEAD_DIM), lambda bi, h, i: (bi, 0, COL["fv"] + h)),
                  pl.BlockSpec((1, 1, t // wide, wide), lambda bi, h, i: (bi, h, 0, 0))],
        out_specs=pl.BlockSpec((1, tq, HEAD_DIM), lambda bi, h, i: (bi, i, h)),
        out_shape=jax.ShapeDtypeStruct((b, t, GROUP_WIDTH), F32),
        scratch_shapes=[pltpu.VMEM((1, 1), F32)] + _attn_scratch(t, Q_CHAINS),
        compiler_params=_cparams(("parallel", "parallel", "arbitrary")),
        name="fox_attention",
    )(p3, p3, p3, nf)


def _neg_abs(x):
    bits = lax.bitcast_convert_type(x, jnp.uint32) | jnp.uint32(0x80000000)
    return lax.bitcast_convert_type(bits, F32)


def _stick_kernel(q_ref, k_ref, v_ref, u_ref, o_ref, r_ref, acc_ref, *, wide):
    t0 = pl.program_id(2) * (Q_CHAINS * Q_ROWS)
    chains = range(Q_CHAINS)
    qs = [q_ref[0, _chain_rows(c), :] for c in chains]
    r_ref[...] = jnp.zeros(r_ref.shape, F32)
    acc_ref[...] = jnp.zeros(acc_ref.shape, F32)
    u = u_ref[...]

    def tile(off, tail):
        ends = [(c + 1) * Q_ROWS if tail else wide for c in chains]
        k = k_ref[0, pl.ds(off, wide), :]
        z_all = [_qk_raw(q, k[:e]) * SCALE_LOG2E for q, e in zip(qs, ends)]
        later = [r_ref[c] for c in chains]
        weights = [[] for _ in chains]
        before = _own_causal(0, Q_ROWS, strict=True)
        for sb in reversed(range(wide // Q_ROWS)):
            live = [c for c in chains if ends[c] > sb * Q_ROWS]
            diag = [tail and c == sb for c in live]
            cols = slice(sb * Q_ROWS, (sb + 1) * Q_ROWS)
            zs = [z_all[c][:, cols] for c in live]
            log_beta = [jnp.minimum(z, 0.0) - jnp.log(1.0 + jnp.exp2(_neg_abs(z))) * LOG2E for z in zs]
            log_keep = [lb - z for lb, z in zip(log_beta, zs)]
            log_keep = [jnp.where(before, lk, 0.0) if d else lk for d, lk in zip(diag, log_keep)]
            inside = [jnp.dot(jnp.concatenate(_split3(lk)[:2], axis=1), u, preferred_element_type=F32)
                      for lk in log_keep]
            a = [jnp.exp2(lb + cs + later[c]) for c, lb, cs in zip(live, log_beta, inside)]
            a = [jnp.where(before, x, 0.0) if d else x for d, x in zip(diag, a)]
            for c, x, cs, lk in zip(live, a, inside, log_keep):
                weights[c].append(x.astype(BF16))
                later[c] = later[c] + cs[:, 0:1] + lk[:, 0:1]
        v = v_ref[0, pl.ds(off, wide), :]
        for c in chains:
            w = weights[c][0] if len(weights[c]) == 1 else jnp.concatenate(weights[c][::-1], axis=1)
            acc_ref[c] += jnp.dot(w, v[:ends[c]], preferred_element_type=F32)
            r_ref[c] = later[c]

    def keep_going(first_done):
        del first_done
        return jnp.max(r_ref[...]) > -STICK_DEAD_LOG2

    _sweep_tiles(t0, wide, functools.partial(tile, tail=True), functools.partial(tile, tail=False), keep_going)
    for c in chains:
        o_ref[0, _chain_rows(c), :] = acc_ref[c]


def _stick_attention(p3):
    b, t, _ = p3.shape
    tq = Q_CHAINS * Q_ROWS
    wide = tq
    ki = np.arange(Q_ROWS)
    u = (ki[:, None] > ki[None, :]).astype(np.float32)
    u = jnp.asarray(np.concatenate([u, u], axis=0), BF16)
    return pl.pallas_call(
        functools.partial(_stick_kernel, wide=wide),
        grid=(b, GROUP_HEADS, t // tq),
        in_specs=[pl.BlockSpec((1, tq, HEAD_DIM), lambda bi, h, i: (bi, i, COL["sq"] + h)),
                  pl.BlockSpec((1, t, HEAD_DIM), lambda bi, h, i: (bi, 0, COL["sk"] + h)),
                  pl.BlockSpec((1, t, HEAD_DIM), lambda bi, h, i: (bi, 0, COL["sv"] + h)),
                  pl.BlockSpec((2 * Q_ROWS, Q_ROWS), lambda bi, h, i: (0, 0))],
        out_specs=pl.BlockSpec((1, tq, HEAD_DIM), lambda bi, h, i: (bi, i, h)),
        out_shape=jax.ShapeDtypeStruct((b, t, GROUP_WIDTH), F32),
        scratch_shapes=[pltpu.VMEM((Q_CHAINS, Q_ROWS, 1), F32), pltpu.VMEM((Q_CHAINS, Q_ROWS, HEAD_DIM), F32)],
        compiler_params=_cparams(("parallel", "parallel", "arbitrary")),
        name="stick_attention",
    )(p3, p3, p3, u)


def _moba_kernel(slope_ref, q_ref, k_ref, v_ref, o_ref, kaug_ref, kmean_ref, knorm_ref, vaug_ref, m_ref, acc_ref, *,
                 n_blk, wide):
    h = pl.program_id(1)
    i = pl.program_id(2)
    tq = MOBA_BLOCK
    slope = slope_ref[h]

    @pl.when(i == 0)
    def _():
        kmean_ref[...] = jnp.zeros(kmean_ref.shape, F32)
        kb = k_ref[0].astype(F32).reshape(n_blk, tq, HEAD_DIM)
        kmean_ref[0:n_blk, :] = jnp.mean(kb, axis=1)

        def key_aux(pos, lane):
            onehot = jnp.where(lane == jnp.right_shift(pos, int(math.log2(tq))), 1.0, 0.0)
            hi, mid, lo = _split3(slope * pos.astype(F32) * (1.0 / ATTN_SCALE))
            return jnp.where(lane == n_blk, hi.astype(F32),
                             jnp.where(lane == n_blk + 1, mid.astype(F32),
                                       jnp.where(lane == n_blk + 2, lo.astype(F32), onehot)))

        _fill_key_aug(kaug_ref, k_ref, key_aux)
        _max_key_norm(k_ref, knorm_ref)
        _fill_value_aug(vaug_ref, v_ref)

    chains = range(Q_CHAINS)
    t0 = i * (Q_CHAINS * Q_ROWS)
    qs = [q_ref[0, _chain_rows(c), :] for c in chains]
    q_norms = [_row_norms(q) for q in qs]
    own = [i * Q_CHAINS + c for c in chains]
    kmean = kmean_ref[...].astype(BF16)
    blk = lax.broadcasted_iota(jnp.int32, (Q_ROWS, LANES), 1)
    blkf = blk.astype(F32)
    scores = [jnp.where(blk < own[c], _qk_raw(qs[c], kmean), -jnp.inf) for c in chains]
    sels = _topk_rounds(scores, blkf, [jnp.where(blk == own[c], 1.0, 0.0) for c in chains], MOBA_TOPK)
    q_aux = [jnp.where(blk < n_blk, jnp.where(sel > 0.5, 0.0, NEG_BIG), jnp.where(blk < n_blk + 3, 1.0, 0.0))
             for sel in sels]
    q_aug = [jnp.concatenate([q, aux.astype(BF16)], axis=1) for q, aux in zip(qs, q_aux)]
    states = [(m_ref.at[c], acc_ref.at[c]) for c in chains]
    _flash_init(m_ref, acc_ref)

    def full_tile(off):
        k = kaug_ref[pl.ds(off, wide), :]
        _online_update_multi([_qk_raw(q, k) for q in q_aug], [vaug_ref[pl.ds(off, wide), :]] * Q_CHAINS, states)

    def tail_tile(off):
        k = kaug_ref[pl.ds(off, wide), :]
        v = vaug_ref[pl.ds(off, wide), :]
        ends = [(c + 1) * Q_ROWS for c in chains]
        raws = [jnp.where(_own_causal(c, e), _qk_raw(q, k[:e]), NEG_BIG) for c, (q, e) in enumerate(zip(q_aug, ends))]
        _online_update_multi(raws, [v[:e] for e in ends], states)

    def keep_going(first_done):
        cap = slope * first_done.astype(F32) * (1.0 / ATTN_SCALE)
        return _softmax_alive(q_norms, knorm_ref[...], [cap] * Q_CHAINS, states)

    _sweep_tiles(t0, wide, tail_tile, full_tile, keep_going)
    for c in chains:
        o_ref[0, _chain_rows(c), :] = _flash_result(acc_ref.at[c])


def _moba_attention(p3, slopes):
    b, t, _ = p3.shape
    tq = Q_CHAINS * Q_ROWS
    n_blk = t // MOBA_BLOCK
    wide = tq
    assert Q_ROWS == MOBA_BLOCK and t % tq == 0 and n_blk + 3 <= LANES
    return pl.pallas_call(
        functools.partial(_moba_kernel, n_blk=n_blk, wide=wide),
        grid=(b, GROUP_HEADS, t // tq),
        in_specs=[pl.BlockSpec(memory_space=pltpu.SMEM),
                  pl.BlockSpec((1, tq, HEAD_DIM), lambda bi, h, i: (bi, i, COL["mq"] + h)),
                  pl.BlockSpec((1, t, HEAD_DIM), lambda bi, h, i: (bi, 0, COL["mk"] + h)),
                  pl.BlockSpec((1, t, HEAD_DIM), lambda bi, h, i: (bi, 0, COL["mv"] + h))],
        out_specs=pl.BlockSpec((1, tq, HEAD_DIM), lambda bi, h, i: (bi, i, h)),
        out_shape=jax.ShapeDtypeStruct((b, t, GROUP_WIDTH), F32),
        scratch_shapes=[pltpu.VMEM((t, HEAD_DIM + LANES), BF16),
                        pltpu.VMEM((LANES, HEAD_DIM), F32), pltpu.VMEM((1, 1), F32)] + _attn_scratch(t, Q_CHAINS),
        compiler_params=_cparams(("parallel", "parallel", "arbitrary")),
        name="moba_attention",
    )(slopes, p3, p3, p3)


def _nsa_compress_kernel(x_ref, pos_ref, w1_ref, w2_ref, o_ref):
    x = (x_ref[0].astype(F32) + pos_ref[...]).astype(BF16)
    hmid = jnp.dot(x, w1_ref[...], preferred_element_type=F32)
    hmid = hmid / (1.0 + jnp.exp(-hmid))
    o_ref[0] = jnp.dot(hmid.astype(BF16), w2_ref[...], preferred_element_type=F32).astype(o_ref.dtype)


def _nsa_compress(blocks, pos, w1, w2):
    r, n, width = blocks.shape
    return pl.pallas_call(
        _nsa_compress_kernel,
        grid=(r,),
        in_specs=[pl.BlockSpec((1, n, width), lambda i: (i, 0, 0)),
                  pl.BlockSpec((1, width), lambda i: (0, 0)),
                  pl.BlockSpec((width, HEAD_DIM), lambda i: (0, 0)),
                  pl.BlockSpec((HEAD_DIM, HEAD_DIM), lambda i: (0, 0))],
        out_specs=pl.BlockSpec((1, n, HEAD_DIM), lambda i: (i, 0, 0)),
        out_shape=jax.ShapeDtypeStruct((r, n, HEAD_DIM), BF16),
        compiler_params=_cparams(("parallel",)),
        name="nsa_compress",
    )(blocks, pos.reshape(1, width).astype(F32), w1.astype(BF16), w2.astype(BF16))


def _nsa_select_kernel(slope_ref, q_ref, kc_ref, vc_ref, a_ref, oc_ref, sel_ref, *, tq, n_cmp, n_blk):
    g = pl.program_id(1)
    i = pl.program_id(2)
    t0 = i * tq
    kc = kc_ref[0, 0]
    vc = vc_ref[0, 0]
    ncp = kc.shape[0]
    tok = lax.broadcasted_iota(jnp.int32, (1, ncp), 1)
    cmp_end = tok * NSA_CMP_STRIDE + (NSA_CMP_LEN - 1)
    row = lax.broadcasted_iota(jnp.int32, (tq, 1), 0)
    admissible = (cmp_end <= row + t0) & (tok < n_cmp)
    rel_end = (cmp_end - t0).astype(F32)

    imp = jnp.zeros((tq, ncp), F32)
    for hh in range(NSA_Q_PER_KV):
        slope = slope_ref[g * NSA_Q_PER_KV + hh]
        q = q_ref[0, :, hh * HEAD_DIM:(hh + 1) * HEAD_DIM]
        s = jnp.where(admissible, _qk(q, kc) + slope * rel_end, NEG_BIG)
        m = jnp.max(s, axis=-1, keepdims=True)
        p = jnp.where(admissible, jnp.exp(s - m), 0.0)
        p = p / jnp.maximum(jnp.sum(p, axis=-1, keepdims=True), F32_TINY)
        oc_ref[0, :, hh * HEAD_DIM:(hh + 1) * HEAD_DIM] = jnp.dot(
            p.astype(BF16), vc, preferred_element_type=F32)
        imp = imp + p

    imp_blk = _dot_split(imp, a_ref[...], 3)
    blk = lax.broadcasted_iota(jnp.int32, (tq, LANES), 1)
    blkf = blk.astype(F32)
    jt = jnp.right_shift(row + t0, int(math.log2(NSA_SEL_BLOCK)))
    live = (blk <= jt) & (blk < n_blk)
    forced = ((blk == 0) | (blk == jt) | (blk == jt - 1)) & live
    sel = jnp.where(forced, 1.0, 0.0)
    score = jnp.where(live & jnp.logical_not(forced), imp_blk, -jnp.inf)
    sel = _topk_rounds([score], blkf, [sel], NSA_SEL_TOPK - 3)[0]
    sel_ref[0, 0] = sel.astype(sel_ref.dtype)


def _nsa_select(p3, kc, vc, slopes, tq):
    b, t, _ = p3.shape
    n_blk = t // NSA_SEL_BLOCK
    assert n_blk <= LANES and NSA_SEL_TOPK >= 3
    n_cmp = t // NSA_CMP_STRIDE - 1
    ncp = kc.shape[2]
    ratio = NSA_SEL_BLOCK // NSA_CMP_STRIDE
    ti = np.arange(ncp)[:, None]
    bj = np.arange(LANES)[None, :]
    amat = ((ti >= ratio * bj - 1) & (ti <= ratio * bj + ratio - 1) & (ti < n_cmp) & (bj < n_blk))
    amat = jnp.asarray(amat.astype(np.float32), BF16)
    width = NSA_Q_PER_KV * HEAD_DIM
    qblk = COL["nq"] * HEAD_DIM // width
    return pl.pallas_call(
        functools.partial(_nsa_select_kernel, tq=tq, n_cmp=n_cmp, n_blk=n_blk),
        grid=(b, NSA_KV_HEADS, t // tq),
        in_specs=[pl.BlockSpec(memory_space=pltpu.SMEM),
                  pl.BlockSpec((1, tq, width), lambda bi, g, i: (bi, i, qblk + g)),
                  pl.BlockSpec((1, 1, ncp, HEAD_DIM), lambda bi, g, i: (bi, g, 0, 0)),
                  pl.BlockSpec((1, 1, ncp, HEAD_DIM), lambda bi, g, i: (bi, g, 0, 0)),
                  pl.BlockSpec((ncp, LANES), lambda bi, g, i: (0, 0))],
        out_specs=[pl.BlockSpec((1, tq, width), lambda bi, g, i: (bi, i, g)),
                   pl.BlockSpec((1, 1, tq, LANES), lambda bi, g, i: (bi, g, i, 0))],
        out_shape=[jax.ShapeDtypeStruct((b, t, GROUP_WIDTH), F32),
                   jax.ShapeDtypeStruct((b, NSA_KV_HEADS, t, LANES), BF16)],
        compiler_params=_cparams(("parallel", "parallel", "parallel")),
        name="nsa_select",
    )(slopes, p3, kc, vc, amat)


def _nsa_attn_kernel(slope_ref, q_ref, ks_ref, vs_ref, kw_ref, vw_ref, sel_ref, oc_ref, gate_ref,
                     o_ref, kaug_ref, knorm_ref, vwaug_ref, vsaug_ref, m_ref, acc_ref, *, tq, wide):
    g = pl.program_id(1)
    i = pl.program_id(2)
    t0 = pl.multiple_of(i * tq, tq)
    nh = NSA_Q_PER_KV

    @pl.when(i == 0)
    def _():
        sel_shift = int(math.log2(NSA_SEL_BLOCK))
        _fill_key_aug(kaug_ref, ks_ref,
                      lambda pos, lane: jnp.where(lane == jnp.right_shift(pos, sel_shift), 1.0, 0.0))
        _max_key_norm(ks_ref, knorm_ref)
        _fill_value_aug(vsaug_ref, vs_ref)
        _fill_value_aug(vwaug_ref, vw_ref)

    _flash_init(m_ref, acc_ref)
    sel_bias = jnp.where(sel_ref[0, 0].astype(F32) > 0.5, 0.0, NEG_BIG).astype(BF16)
    qs = [q_ref[0, :, hh * HEAD_DIM:(hh + 1) * HEAD_DIM] for hh in range(nh)]
    q_aug = [jnp.concatenate([q, sel_bias], axis=1) for q in qs]
    slope_raw = [slope_ref[g * nh + hh] * (1.0 / ATTN_SCALE) for hh in range(nh)]
    row = lax.broadcasted_iota(jnp.int32, (tq, 1), 0)

    def rel_pos(off, width):
        return lax.broadcasted_iota(jnp.int32, (1, width), 1) + (off - t0)

    sel_states = [(m_ref.at[hh], acc_ref.at[hh]) for hh in range(nh)]
    win_states = [(m_ref.at[nh + hh], acc_ref.at[nh + hh]) for hh in range(nh)]
    q_norms = [_row_norms(q) for q in qs]

    def sel_scores(off, tail):
        rel = rel_pos(off, wide)
        relf = rel.astype(F32)
        k = kaug_ref[pl.ds(off, wide), :]
        raws = [_qk_raw(q_aug[hh], k) + slope_raw[hh] * relf for hh in range(nh)]
        if tail:
            raws = [jnp.where(rel <= row, raw, NEG_BIG) for raw in raws]
        return raws, [vsaug_ref[pl.ds(off, wide), :]] * nh

    def window_scores(off, width, far):
        rel = rel_pos(off, width)
        relf = rel.astype(F32)
        k = kw_ref[0, pl.ds(off, width), :]
        ok = rel <= row
        if far:
            ok = ok & (row - rel < NSA_WINDOW)
        raws = [jnp.where(ok, _qk_raw(qs[hh], k) + slope_raw[hh] * relf, NEG_BIG) for hh in range(nh)]
        return raws, [vwaug_ref[pl.ds(off, width), :]] * nh

    def sel_tile(off, tail):
        raws, vs = sel_scores(off, tail)
        _online_update_multi(raws, vs, sel_states)

    def keep_going(first_done):
        gap = (first_done - t0).astype(F32)
        return _softmax_alive(q_norms, knorm_ref[...], [s * gap for s in slope_raw], sel_states)

    _sweep_tiles(t0, wide, functools.partial(sel_tile, tail=True), functools.partial(sel_tile, tail=False),
                 keep_going)

    n_back = NSA_WINDOW // tq
    for back in range(n_back + 1):
        @pl.when((i == back) if back < n_back else (i >= back))
        def _(back=back):
            raws, vs = window_scores(pl.multiple_of(t0 - back * tq, tq), (back + 1) * tq, back == n_back)
            _online_update_multi(raws, vs, win_states)

    gates = gate_ref[0, 0]
    gates = 1.0 / (1.0 + jnp.exp(-gates))
    for hh in range(nh):
        c0 = hh * NSA_N_BRANCHES
        o_c = oc_ref[0, :, hh * HEAD_DIM:(hh + 1) * HEAD_DIM]
        o_s = _flash_result(acc_ref.at[hh])
        o_w = _flash_result(acc_ref.at[nh + hh])
        o_ref[0, :, hh * HEAD_DIM:(hh + 1) * HEAD_DIM] = (
            gates[:, c0:c0 + 1] * o_c + gates[:, c0 + 1:c0 + 2] * o_s + gates[:, c0 + 2:c0 + 3] * o_w)


def _nsa_attention(p3, sel, o_cmp, gate_logits, slopes, tq):
    b, t, _ = p3.shape
    assert tq == KV_UNIT and NSA_WINDOW == 2 * tq and t // NSA_SEL_BLOCK <= LANES
    width = NSA_Q_PER_KV * HEAD_DIM
    qblk = COL["nq"] * HEAD_DIM // width
    kv = lambda name: pl.BlockSpec((1, t, HEAD_DIM), lambda bi, g, i: (bi, 0, COL[name] + g))
    nst = 2 * NSA_Q_PER_KV
    return pl.pallas_call(
        functools.partial(_nsa_attn_kernel, tq=tq, wide=min(KV_WIDE, t)),
        grid=(b, NSA_KV_HEADS, t // tq),
        in_specs=[pl.BlockSpec(memory_space=pltpu.SMEM),
                  pl.BlockSpec((1, tq, width), lambda bi, g, i: (bi, i, qblk + g)),
                  kv("nks"), kv("nvs"), kv("nkw"), kv("nvw"),
                  pl.BlockSpec((1, 1, tq, LANES), lambda bi, g, i: (bi, g, i, 0)),
                  pl.BlockSpec((1, tq, width), lambda bi, g, i: (bi, i, g)),
                  pl.BlockSpec((1, 1, tq, LANES), lambda bi, g, i: (bi, g, i, 0))],
        out_specs=pl.BlockSpec((1, tq, width), lambda bi, g, i: (bi, i, g)),
        out_shape=jax.ShapeDtypeStruct((b, t, GROUP_WIDTH), F32),
        scratch_shapes=[pltpu.VMEM((t, HEAD_DIM + LANES), BF16), pltpu.VMEM((1, 1), F32),
                        pltpu.VMEM((t, HEAD_DIM + LANES), BF16)] + _attn_scratch(t, nst),
        compiler_params=_cparams(("parallel", "parallel", "arbitrary")),
        name="nsa_attention",
    )(slopes, p3, p3, p3, p3, p3, sel, o_cmp, gate_logits)


def _nsa_blocks(p3, name):
    b, t, _ = p3.shape
    c0 = COL[name] * HEAD_DIM
    x = p3[:, :, c0:c0 + NSA_KV_HEADS * HEAD_DIM].reshape(b, t // NSA_CMP_STRIDE, NSA_CMP_STRIDE,
                                                         NSA_KV_HEADS, HEAD_DIM)
    x = x.transpose(0, 3, 1, 2, 4).reshape(b * NSA_KV_HEADS, t // NSA_CMP_STRIDE, NSA_CMP_STRIDE * HEAD_DIM)
    blocks = jnp.concatenate([x[:, :-1], x[:, 1:]], axis=-1)
    return jnp.pad(blocks, ((0, 0), (0, 1), (0, 0)))


def _alibi_slopes():
    n = 2 * GROUP_HEADS
    slopes = 2.0 ** (-8.0 * np.arange(1, n + 1) / n)
    return jnp.asarray(slopes[0::2], F32), jnp.asarray(slopes[1::2], F32)


def _regroup_w_in(w_in):
    main = jnp.concatenate([w_in[:, REF_OFF[n][0]:REF_OFF[n][0] + REF_OFF[n][1]] for n, _ in _MAIN_PIECES],
                           axis=1).astype(BF16)
    ng0, ngw = REF_OFF["ng"]
    ff0, ffw = REF_OFF["ff"]
    small = jnp.concatenate([w_in[:, ng0:ng0 + ngw], w_in[:, ff0:ff0 + ffw]], axis=1)
    small = jnp.pad(small, ((0, 0), (0, LANES - ngw - ffw))).astype(BF16)
    return main, small


def _mixer(x2d, normed, b, t, w_in, forget_bias, pos_k, w1_k, w2_k, pos_v, w1_v, w2_v, group_gain, w_out, next_gain):
    m = x2d.shape[0]
    w_main, w_small = _regroup_w_in(w_in)
    tm = _pick_tile(m, (1024, 512, 256))
    p3 = _matmul_normed(normed, w_main, BF16, tm, 512).reshape(b, t, MAIN_BLOCKS * HEAD_DIM)
    small = _matmul_normed(normed, w_small, F32, tm, LANES).reshape(b, t, LANES)
    slopes_moba, slopes_nsa = _alibi_slopes()

    o_a = _moba_attention(p3, slopes_moba)

    n_gate = NSA_N_BRANCHES * GROUP_HEADS
    kc = _nsa_compress(_nsa_blocks(p3, "nkc"), pos_k, w1_k, w2_k)
    vc = _nsa_compress(_nsa_blocks(p3, "nvc"), pos_v, w1_v, w2_v)
    ncp = kc.shape[1]
    kc = kc.reshape(b, NSA_KV_HEADS, ncp, HEAD_DIM)
    vc = vc.reshape(b, NSA_KV_HEADS, ncp, HEAD_DIM)
    o_cmp, sel = _nsa_select(p3, kc, vc, slopes_nsa, _pick_tile(t, (512, 256)))
    per_group = n_gate // NSA_KV_HEADS
    gate_logits = small[:, :, :n_gate].reshape(b, t, NSA_KV_HEADS, per_group).transpose(0, 2, 1, 3)
    gate_logits = jnp.pad(gate_logits, ((0, 0), (0, 0), (0, 0), (0, LANES - per_group)))
    o_b = _nsa_attention(p3, sel, o_cmp, gate_logits, slopes_nsa, KV_UNIT)

    ff = small[:, :, n_gate:n_gate + GROUP_HEADS] + forget_bias
    neg_cum_f = _neg_cum_logf(ff.transpose(0, 2, 1).reshape(b * GROUP_HEADS, t))
    o_c = _fox_attention(p3, neg_cum_f)

    o_d = _stick_attention(p3)

    on = _groupnorm([o.reshape(m, GROUP_WIDTH) for o in (o_a, o_b, o_c, o_d)], group_gain)
    return _matmul_residual(on, w_out, x2d, next_gain, tm, 512)


def _cast_pad_kernel(w_ref, o_ref, *, n_row_tiles, cols):
    @pl.when(pl.program_id(0) < n_row_tiles)
    def _():
        o_ref[:, 0:cols] = w_ref[0].astype(BF16)
        if o_ref.shape[1] > cols:
            o_ref[:, cols:] = jnp.zeros((o_ref.shape[0], o_ref.shape[1] - cols), BF16)

    @pl.when(pl.program_id(0) >= n_row_tiles)
    def _():
        o_ref[...] = jnp.zeros(o_ref.shape, BF16)


def _weight_bf16(stacked, layer, rows_out, cols_out, tr=256):
    _, rows, cols = stacked.shape
    assert rows % tr == 0 and rows_out % tr == 0 and cols % LANES == 0
    n_row_tiles = rows // tr
    return pl.pallas_call(
        functools.partial(_cast_pad_kernel, n_row_tiles=n_row_tiles, cols=cols),
        grid=(rows_out // tr,),
        in_specs=[pl.BlockSpec((1, tr, cols), lambda j: (layer, jnp.minimum(j, n_row_tiles - 1), 0))],
        out_specs=pl.BlockSpec((tr, cols_out), lambda j: (j, 0)),
        out_shape=jax.ShapeDtypeStruct((rows_out, cols_out), BF16),
        compiler_params=_cparams(("parallel",)),
        name="weight_bf16",
    )(stacked)


def kernel(x, ffn1_norm, ffn1_w_gate, ffn1_w_up, ffn1_w_down, mix_norm, w_in, fox_forget_bias, nsa_cmp_pos_k, nsa_cmp_w1_k, nsa_cmp_w2_k, nsa_cmp_pos_v, nsa_cmp_w1_v, nsa_cmp_w2_v, group_norm, w_out, ffn2_norm, ffn2_w_gate, ffn2_w_up, ffn2_w_down, final_norm):
    b, t, d = x.shape
    x2d = x.reshape(b * t, d)
    depth = ffn1_norm.shape[0]
    normed = _prenorm(x2d, ffn1_norm[0])
    for i in range(depth):
        x2d, normed = _ffn(x2d, normed, ffn1_w_gate, ffn1_w_up, ffn1_w_down, i, mix_norm[i])
        x2d, normed = _mixer(x2d, normed, b, t, w_in[i], fox_forget_bias[i], nsa_cmp_pos_k[i], nsa_cmp_w1_k[i],
                             nsa_cmp_w2_k[i], nsa_cmp_pos_v[i], nsa_cmp_w1_v[i], nsa_cmp_w2_v[i],
                             group_norm[i], _weight_bf16(w_out, i, d, d), ffn2_norm[i])
        x2d, normed = _ffn(x2d, normed, ffn2_w_gate, ffn2_w_up, ffn2_w_down, i,
                           ffn1_norm[i + 1] if i + 1 < depth else None)
    return _rmsnorm(x2d, final_norm, F32).reshape(b, t, d)
```

```python
import functools
import math

import numpy as np
import jax
import jax.numpy as jnp
from jax import lax
from jax.experimental import pallas as pl
from jax.experimental.pallas import tpu as pltpu

F32 = jnp.float32
BF16 = jnp.bfloat16

HEAD_DIM = 128
GROUP_HEADS = 8
GROUP_WIDTH = GROUP_HEADS * HEAD_DIM
N_MIXERS = 4
FFN_RESIDUAL = 0.5
RMS_EPS = 1e-6

MOBA_BLOCK = 256
MOBA_TOPK = 3

NSA_KV_HEADS = 2
NSA_Q_PER_KV = 4
NSA_CMP_STRIDE = 16
NSA_CMP_LEN = 32
NSA_SEL_BLOCK = 64
NSA_SEL_TOPK = 16
NSA_WINDOW = 512
NSA_N_BRANCHES = 3

ATTN_SCALE = HEAD_DIM ** -0.5
LOG2E = math.log2(math.e)
SCALE_LOG2E = ATTN_SCALE * LOG2E
NEG_BIG = -1e30
STICK_DEAD_LOG2 = 160.0
FLASH_DEAD_LOG2 = 170.0
BOUND_SLACK = 1.001
KV_UNIT = 256
KV_WIDE = 1024
Q_ROWS = 256
Q_CHAINS = 4
F32_TINY = float(np.finfo(np.float32).tiny)

LANES = 128
VMEM_LIMIT = 56 * 1024 * 1024
GATEUP_TILE = 256
FFN_PAD = 512

_MAIN_PIECES = (
    ("mq", 8), ("mk", 8), ("mv", 8), ("nq", 8),
    ("nkc", 2), ("nvc", 2), ("nks", 2), ("nvs", 2), ("nkw", 2), ("nvw", 2),
    ("fq", 8), ("fk", 8), ("fv", 8), ("sq", 8), ("sk", 8), ("sv", 8),
)
COL = {}
_o = 0
for _n, _w in _MAIN_PIECES:
    COL[_n] = _o
    _o += _w
MAIN_BLOCKS = _o

_REF_SIZES = (
    ("mq", 1024), ("mk", 1024), ("mv", 1024), ("nq", 1024),
    ("nkc", 256), ("nvc", 256), ("nks", 256), ("nvs", 256), ("nkw", 256), ("nvw", 256),
    ("ng", 24),
    ("fq", 1024), ("fk", 1024), ("fv", 1024), ("ff", 8),
    ("sq", 1024), ("sk", 1024), ("sv", 1024),
)
REF_OFF = {}
_o = 0
for _n, _w in _REF_SIZES:
    REF_OFF[_n] = (_o, _w)
    _o += _w


def _cparams(sem):
    return pltpu.CompilerParams(dimension_semantics=sem, vmem_limit_bytes=VMEM_LIMIT)


def _rms_kernel(x_ref, g_ref, o_ref):
    x = x_ref[...]
    ms = jnp.mean(x * x, axis=-1, keepdims=True)
    o_ref[...] = (x * lax.rsqrt(ms + RMS_EPS) * g_ref[...]).astype(o_ref.dtype)


def _rmsnorm(x2d, gain, out_dtype, tm=256):
    m, d = x2d.shape
    return pl.pallas_call(
        _rms_kernel,
        grid=(m // tm,),
        in_specs=[pl.BlockSpec((tm, d), lambda i: (i, 0)),
                  pl.BlockSpec((1, d), lambda i: (0, 0))],
        out_specs=pl.BlockSpec((tm, d), lambda i: (i, 0)),
        out_shape=jax.ShapeDtypeStruct((m, d), out_dtype),
        compiler_params=_cparams(("parallel",)),
        name="rmsnorm",
    )(x2d, gain.reshape(1, d).astype(F32))


def _emit_normed(x_new, gain_ref, xg_ref, ssq_ref):
    xg_ref[...] = (x_new * gain_ref[...]).astype(BF16)
    ssq_ref[...] = jnp.broadcast_to(jnp.sum(x_new * x_new, axis=-1, keepdims=True), ssq_ref.shape)


def _rstd(ssq_ref, d):
    ssq = ssq_ref[...]
    total = ssq[:, 0:1]
    for p in range(1, ssq.shape[1] // LANES):
        total = total + ssq[:, p * LANES:p * LANES + 1]
    return lax.rsqrt(total * (1.0 / d) + RMS_EPS)


def _prenorm_kernel(x_ref, g_ref, xg_ref, ssq_ref):
    _emit_normed(x_ref[...], g_ref, xg_ref, ssq_ref)


def _prenorm(x2d, gain, tm=256):
    m, d = x2d.shape
    return pl.pallas_call(
        _prenorm_kernel,
        grid=(m // tm,),
        in_specs=[pl.BlockSpec((tm, d), lambda i: (i, 0)),
                  pl.BlockSpec((1, d), lambda i: (0, 0))],
        out_specs=[pl.BlockSpec((tm, d), lambda i: (i, 0)), pl.BlockSpec((tm, LANES), lambda i: (i, 0))],
        out_shape=[jax.ShapeDtypeStruct((m, d), BF16), jax.ShapeDtypeStruct((m, LANES), F32)],
        compiler_params=_cparams(("parallel",)),
        name="prenorm",
    )(x2d, gain.reshape(1, d).astype(F32))


def _groupnorm_kernel(a_ref, b_ref, c_ref, d_ref, g_ref, o_ref):
    for gi, ref in enumerate((a_ref, b_ref, c_ref, d_ref)):
        x = ref[...]
        ms = jnp.mean(x * x, axis=-1, keepdims=True)
        lo, hi = gi * GROUP_WIDTH, (gi + 1) * GROUP_WIDTH
        o_ref[:, lo:hi] = (x * lax.rsqrt(ms + RMS_EPS) * g_ref[:, lo:hi]).astype(o_ref.dtype)


def _groupnorm(parts, gain, tm=256):
    m = parts[0].shape[0]
    d = GROUP_WIDTH * N_MIXERS
    part_spec = pl.BlockSpec((tm, GROUP_WIDTH), lambda i: (i, 0))
    return pl.pallas_call(
        _groupnorm_kernel,
        grid=(m // tm,),
        in_specs=[part_spec] * 4 + [pl.BlockSpec((1, d), lambda i: (0, 0))],
        out_specs=pl.BlockSpec((tm, d), lambda i: (i, 0)),
        out_shape=jax.ShapeDtypeStruct((m, d), BF16),
        compiler_params=_cparams(("parallel",)),
        name="groupnorm",
    )(*parts, gain.reshape(1, d).astype(F32))


def _mm_normed_kernel(a_ref, ssq_ref, w_ref, o_ref):
    acc = jnp.dot(a_ref[...], w_ref[...], preferred_element_type=F32)
    o_ref[...] = (acc * _rstd(ssq_ref, a_ref.shape[1])).astype(o_ref.dtype)


def _matmul_normed(normed, w, out_dtype, tm, tn):
    a, ssq = normed
    m, k = a.shape
    n = w.shape[1]
    once = dict(pipeline_mode=pl.Buffered(1))
    return pl.pallas_call(
        _mm_normed_kernel,
        grid=(m // tm, n // tn),
        in_specs=[pl.BlockSpec((tm, k), lambda i, j: (i, 0), **once),
                  pl.BlockSpec((tm, ssq.shape[1]), lambda i, j: (i, 0), **once),
                  pl.BlockSpec((k, tn), lambda i, j: (0, j))],
        out_specs=pl.BlockSpec((tm, tn), lambda i, j: (i, j)),
        out_shape=jax.ShapeDtypeStruct((m, n), out_dtype),
        compiler_params=_cparams(("parallel", "arbitrary")),
        name="matmul",
    )(a, ssq, w)


def _normed_outputs(m, n, tm, tn, index):
    specs = [pl.BlockSpec((tm, tn), index), pl.BlockSpec((tm, LANES), index)]
    shapes = [jax.ShapeDtypeStruct((m, n), BF16), jax.ShapeDtypeStruct((m, LANES * (n // tn)), F32)]
    return specs, shapes


def _mm_res_kernel(a_ref, w_ref, x_ref, g_ref, o_ref, xg_ref, ssq_ref):
    x_new = x_ref[...] + jnp.dot(a_ref[...], w_ref[...], preferred_element_type=F32)
    o_ref[...] = x_new
    _emit_normed(x_new, g_ref, xg_ref, ssq_ref)


def _matmul_residual(a, w, x, next_gain, tm, tn):
    m, k = a.shape
    n = w.shape[1]
    index = lambda i, j: (i, j)
    n_specs, n_shapes = _normed_outputs(m, n, tm, tn, index)
    x_new, xg, ssq = pl.pallas_call(
        _mm_res_kernel,
        grid=(m // tm, n // tn),
        in_specs=[pl.BlockSpec((tm, k), lambda i, j: (i, 0)),
                  pl.BlockSpec((k, tn), lambda i, j: (0, j)),
                  pl.BlockSpec((tm, tn), index),
                  pl.BlockSpec((1, tn), lambda i, j: (0, j))],
        out_specs=[pl.BlockSpec((tm, tn), index)] + n_specs,
        out_shape=[jax.ShapeDtypeStruct((m, n), F32)] + n_shapes,
        compiler_params=_cparams(("parallel", "arbitrary")),
        name="matmul_residual",
    )(a, w, x, next_gain.reshape(1, n).astype(F32))
    return x_new, (xg, ssq)


def _gateup_kernel(h_ref, ssq_ref, wg_ref, wu_ref, o_ref, rstd_ref, rstd_res_ref, *, n_valid):
    @pl.when(pl.program_id(1) == 0)
    def _():
        rstd = _rstd(ssq_ref, h_ref.shape[1])
        rstd_ref[...] = rstd
        rstd_res_ref[...] = rstd * FFN_RESIDUAL

    @pl.when(pl.program_id(1) < n_valid)
    def _():
        h = h_ref[...]
        g = jnp.dot(h, wg_ref[0].astype(BF16), preferred_element_type=F32) * rstd_ref[...]
        u = jnp.dot(h, wu_ref[0].astype(BF16), preferred_element_type=F32)
        o_ref[...] = (g / (1.0 + jnp.exp(-g)) * u * rstd_res_ref[...]).astype(o_ref.dtype)

    @pl.when(pl.program_id(1) >= n_valid)
    def _():
        o_ref[...] = jnp.zeros(o_ref.shape, o_ref.dtype)


def _gateup(normed, wg, wu, layer, f_out, tm, tf):
    h, ssq = normed
    m, d = h.shape
    f = wg.shape[2]
    assert f % tf == 0 and f_out % tf == 0
    n_valid = f // tf
    w_spec = pl.BlockSpec((1, d, tf), lambda i, j: (layer, 0, jnp.minimum(j, n_valid - 1)))
    once = dict(pipeline_mode=pl.Buffered(1))
    return pl.pallas_call(
        functools.partial(_gateup_kernel, n_valid=n_valid),
        grid=(m // tm, f_out // tf),
        in_specs=[pl.BlockSpec((tm, d), lambda i, j: (i, 0), **once),
                  pl.BlockSpec((tm, ssq.shape[1]), lambda i, j: (i, 0), **once), w_spec, w_spec],
        out_specs=pl.BlockSpec((tm, tf), lambda i, j: (i, j)),
        out_shape=jax.ShapeDtypeStruct((m, f_out), BF16),
        scratch_shapes=[pltpu.VMEM((tm, 1), F32), pltpu.VMEM((tm, 1), F32)],
        compiler_params=_cparams(("parallel", "arbitrary")),
        name="ffn_gateup",
    )(h, ssq, wg, wu)


def _down_kernel(a_ref, w_ref, x_ref, *rest, emit, nk):
    o_ref = rest[1] if emit else rest[0]
    k = pl.program_id(2)

    def step(base_ref, final):
        x_new = base_ref[...] + jnp.dot(a_ref[...], w_ref[...], preferred_element_type=F32)
        o_ref[...] = x_new
        if final and emit:
            g_ref, _, xg_ref, ssq_ref = rest
            _emit_normed(x_new, g_ref, xg_ref, ssq_ref)

    pl.when(k == 0)(lambda: step(x_ref, nk == 1))
    if nk > 2:
        pl.when((k > 0) & (k < nk - 1))(lambda: step(o_ref, False))
    if nk > 1:
        pl.when(k == nk - 1)(lambda: step(o_ref, True))


def _down(act, wd, x, next_gain, tm, tn, tk):
    m, f = act.shape
    n = wd.shape[1]
    emit = next_gain is not None
    index = lambda i, j, k: (i, j)
    in_specs = [pl.BlockSpec((tm, tk), lambda i, j, k: (i, k)),
                pl.BlockSpec((tk, tn), lambda i, j, k: (k, j)),
                pl.BlockSpec((tm, tn), index)]
    out_specs, out_shape, args = [pl.BlockSpec((tm, tn), index)], [jax.ShapeDtypeStruct((m, n), F32)], [act, wd, x]
    if emit:
        n_specs, n_shapes = _normed_outputs(m, n, tm, tn, index)
        in_specs.append(pl.BlockSpec((1, tn), lambda i, j, k: (0, j)))
        out_specs, out_shape = out_specs + n_specs, out_shape + n_shapes
        args.append(next_gain.reshape(1, n).astype(F32))
    outs = pl.pallas_call(
        functools.partial(_down_kernel, emit=emit, nk=f // tk),
        grid=(m // tm, n // tn, f // tk),
        in_specs=in_specs,
        out_specs=out_specs,
        out_shape=out_shape,
        compiler_params=_cparams(("parallel", "parallel", "arbitrary")),
        name="ffn_down",
    )(*args)
    return (outs[0], (outs[1], outs[2])) if emit else (outs[0], None)


def _pick_tile(n, prefs):
    for t in prefs:
        if n % t == 0:
            return t
    return n


def _ffn(x2d, normed, wg, wu, wd, layer, next_gain):
    m, d = x2d.shape
    f = wg.shape[2]
    fp = -(-f // FFN_PAD) * FFN_PAD
    tm = _pick_tile(m, (1024, 512, 256))
    act = _gateup(normed, wg, wu, layer, fp, _pick_tile(m, (2048, 1024, 512, 256)),
                  _pick_tile(f, (GATEUP_TILE, 128)))
    return _down(act, _weight_bf16(wd, layer, fp, d), x2d, next_gain, tm, _pick_tile(d, (1024, 512, 256, 128)),
                 _pick_tile(fp, (2816, 1024, 512, 256, 128)))


def _qk_raw(q, k):
    return lax.dot_general(q, k, (((1,), (1,)), ((), ())), preferred_element_type=F32)


def _qk(q, k):
    return _qk_raw(q, k) * ATTN_SCALE


def _online_update_multi(raws, vs, states):
    m_prev = [m_ref[...] for m_ref, _ in states]
    m_new = [jnp.maximum(mp, jnp.max(raw, axis=-1, keepdims=True)) for mp, raw in zip(m_prev, raws)]
    ps = [jnp.exp2((raw - mn) * SCALE_LOG2E).astype(BF16) for raw, mn in zip(raws, m_new)]
    alphas = [jnp.exp2((mp - mn) * SCALE_LOG2E) for mp, mn in zip(m_prev, m_new)]
    pvs = [jnp.dot(p, v, preferred_element_type=F32) for p, v in zip(ps, vs)]
    for (m_ref, acc_ref), mn, a, pv in zip(states, m_new, alphas, pvs):
        acc_ref[...] = a * acc_ref[...] + pv
        m_ref[...] = mn


def _flash_init(m_ref, acc_ref):
    m_ref[...] = jnp.full(m_ref.shape, NEG_BIG, F32)
    acc_ref[...] = jnp.zeros(acc_ref.shape, F32)


def _flash_result(acc_ref):
    acc = acc_ref[...]
    return acc[:, :HEAD_DIM] / acc[:, HEAD_DIM:HEAD_DIM + 1]


def _fill_value_aug(vaug_ref, v_ref):
    def body(c, carry):
        off = pl.multiple_of(c * KV_UNIT, KV_UNIT)
        lane = lax.broadcasted_iota(jnp.int32, (KV_UNIT, LANES), 1)
        vaug_ref[pl.ds(off, KV_UNIT), 0:HEAD_DIM] = v_ref[0, pl.ds(off, KV_UNIT), :]
        vaug_ref[pl.ds(off, KV_UNIT), HEAD_DIM:] = jnp.where(lane == 0, 1.0, 0.0).astype(BF16)
        return carry

    lax.fori_loop(0, vaug_ref.shape[0] // KV_UNIT, body, 0)


def _sweep_tiles(t0, wide, tail_tile, full_tile, keep_going):
    n_full = t0 // wide
    tail_off = pl.multiple_of(n_full * wide, wide)

    def full_off(jj):
        return pl.multiple_of((n_full - 1 - jj) * wide, wide)

    def step(state):
        jj, _ = state
        full_tile(full_off(jj))
        return jj + 1, keep_going(full_off(jj))

    tail_tile(tail_off)
    lax.while_loop(lambda s: (s[0] < n_full) & s[1], step, (jnp.int32(0), keep_going(tail_off)))


def _row_norms(q):
    q = q.astype(F32)
    return jnp.sqrt(jnp.sum(q * q, axis=-1, keepdims=True))


def _max_key_norm(k_ref, out_ref):
    def body(c, mx):
        k = k_ref[0, pl.ds(pl.multiple_of(c * KV_UNIT, KV_UNIT), KV_UNIT), :].astype(F32)
        return jnp.maximum(mx, jnp.max(jnp.sum(k * k, axis=-1, keepdims=True), axis=0, keepdims=True))

    out_ref[...] = jnp.sqrt(lax.fori_loop(0, k_ref.shape[1] // KV_UNIT, body, jnp.zeros((1, 1), F32)))


def _softmax_alive(q_norms, k_norm, bias_caps, states):
    gaps = [qn * (k_norm * BOUND_SLACK) + cap - m_ref[...]
            for qn, cap, (m_ref, _) in zip(q_norms, bias_caps, states)]
    worst = jnp.max(functools.reduce(jnp.maximum, gaps))
    return (worst + 1.0) * SCALE_LOG2E > -FLASH_DEAD_LOG2


def _fill_key_aug(kaug_ref, k_ref, aux_fn):
    t = kaug_ref.shape[0]

    def body(c, carry):
        off = pl.multiple_of(c * KV_UNIT, KV_UNIT)
        pos = lax.broadcasted_iota(jnp.int32, (KV_UNIT, LANES), 0) + off
        lane = lax.broadcasted_iota(jnp.int32, (KV_UNIT, LANES), 1)
        kaug_ref[pl.ds(off, KV_UNIT), 0:HEAD_DIM] = k_ref[0, pl.ds(off, KV_UNIT), :]
        kaug_ref[pl.ds(off, KV_UNIT), HEAD_DIM:] = aux_fn(pos, lane).astype(BF16)
        return carry

    lax.fori_loop(0, t // KV_UNIT, body, 0)


def _topk_rounds(scores, blkf, sels, rounds):
    for _ in range(rounds):
        mx = [jnp.max(s, axis=-1, keepdims=True) for s in scores]
        idx = [jnp.min(jnp.where(s == m, blkf, 1e9), axis=-1, keepdims=True) for s, m in zip(scores, mx)]
        picks = [(blkf == i) & (m > -jnp.inf) for i, m in zip(idx, mx)]
        sels = [jnp.where(p, 1.0, sel) for p, sel in zip(picks, sels)]
        scores = [jnp.where(p, -jnp.inf, s) for p, s in zip(picks, scores)]
    return sels


def _split3(x):
    hi = x.astype(BF16)
    r1 = x - hi.astype(F32)
    mid = r1.astype(BF16)
    lo = (r1 - mid.astype(F32)).astype(BF16)
    return hi, mid, lo


def _dot_split(x, w_bf16, terms):
    parts = _split3(x)[:terms]
    out = jnp.dot(parts[0], w_bf16, preferred_element_type=F32)
    for p in parts[1:]:
        out = out + jnp.dot(p, w_bf16, preferred_element_type=F32)
    return out


def _logf_cumsum_kernel(x_ref, tri_ref, low_ref, o_ref):
    x = x_ref[0]
    logf = jnp.minimum(x, 0.0) - jnp.log1p(jnp.exp(-jnp.abs(x)))
    within = _dot_split(logf, tri_ref[...], 3)
    hi, mid, lo = _split3(logf)
    low = low_ref[...]
    before = (jnp.dot(low, hi, preferred_element_type=F32) + jnp.dot(low, mid, preferred_element_type=F32)
              + jnp.dot(low, lo, preferred_element_type=F32))
    o_ref[0] = -(within + jnp.sum(before, axis=-1, keepdims=True))


def _neg_cum_logf(logits_rows):
    r, t = logits_rows.shape
    nc = t // LANES
    li = np.arange(LANES)
    tri = jnp.asarray((li[:, None] <= li[None, :]).astype(np.float32), BF16)
    ci = np.arange(nc)
    low = jnp.asarray((ci[None, :] < ci[:, None]).astype(np.float32), BF16)
    out = pl.pallas_call(
        _logf_cumsum_kernel,
        grid=(r,),
        in_specs=[pl.BlockSpec((1, nc, LANES), lambda i: (i, 0, 0)),
                  pl.BlockSpec((LANES, LANES), lambda i: (0, 0)),
                  pl.BlockSpec((nc, nc), lambda i: (0, 0))],
        out_specs=pl.BlockSpec((1, nc, LANES), lambda i: (i, 0, 0)),
        out_shape=jax.ShapeDtypeStruct((r, nc, LANES), F32),
        compiler_params=_cparams(("parallel",)),
        name="fox_logf_cumsum",
    )(logits_rows.reshape(r, nc, LANES), tri, low)
    return out.reshape(r, t)


def _chain_rows(c):
    return slice(c * Q_ROWS, (c + 1) * Q_ROWS)


def _own_causal(c, width, strict=False):
    col = lax.broadcasted_iota(jnp.int32, (1, width), 1)
    row = lax.broadcasted_iota(jnp.int32, (Q_ROWS, 1), 0) + c * Q_ROWS
    return (col < row) if strict else (col <= row)


def _fox_kernel(q_ref, k_ref, v_ref, nf_ref, o_ref, knorm_ref, vaug_ref, m_ref, acc_ref, *, wide):
    t0 = pl.program_id(2) * (Q_CHAINS * Q_ROWS)

    @pl.when(pl.program_id(2) == 0)
    def _():
        _max_key_norm(k_ref, knorm_ref)
        _fill_value_aug(vaug_ref, v_ref)

    qs = [q_ref[0, _chain_rows(c), :] for c in range(Q_CHAINS)]
    q_norms = [_row_norms(q) for q in qs]
    states = [(m_ref.at[c], acc_ref.at[c]) for c in range(Q_CHAINS)]
    _flash_init(m_ref, acc_ref)

    def keep_going(first_done):
        prev = jnp.maximum(first_done // wide - 1, 0)
        cap = nf_ref[0, 0, pl.ds(prev, 1), :][:, wide - 1:wide] * (1.0 / ATTN_SCALE)
        return _softmax_alive(q_norms, knorm_ref[...], [cap] * Q_CHAINS, states)

    def full_tile(off):
        bias = nf_ref[0, 0, pl.ds(off // wide, 1), :] * (1.0 / ATTN_SCALE)
        k = k_ref[0, pl.ds(off, wide), :]
        raws = [_qk_raw(q, k) + bias for q in qs]
        _online_update_multi(raws, [vaug_ref[pl.ds(off, wide), :]] * Q_CHAINS, states)

    def tail_tile(off):
        bias = nf_ref[0, 0, pl.ds(off // wide, 1), :] * (1.0 / ATTN_SCALE)
        k = k_ref[0, pl.ds(off, wide), :]
        v = vaug_ref[pl.ds(off, wide), :]
        ends = [(c + 1) * Q_ROWS for c in range(Q_CHAINS)]
        raws = [jnp.where(_own_causal(c, e), _qk_raw(q, k[:e]) + bias[:, :e], NEG_BIG)
                for c, (q, e) in enumerate(zip(qs, ends))]
        _online_update_multi(raws, [v[:e] for e in ends], states)

    _sweep_tiles(t0, wide, tail_tile, full_tile, keep_going)
    for c in range(Q_CHAINS):
        o_ref[0, _chain_rows(c), :] = _flash_result(acc_ref.at[c])


def _attn_scratch(t, n_states):
    return [pltpu.VMEM((t, HEAD_DIM + LANES), BF16), pltpu.VMEM((n_states, Q_ROWS, 1), F32),
            pltpu.VMEM((n_states, Q_ROWS, HEAD_DIM + LANES), F32)]


def _fox_attention(p3, neg_cum_f):
    b, t, _ = p3.shape
    tq = Q_CHAINS * Q_ROWS
    wide = tq
    nf = neg_cum_f.reshape(b, GROUP_HEADS, t // wide, wide)
    return pl.pallas_call(
        functools.partial(_fox_kernel, wide=wide),
        grid=(b, GROUP_HEADS, t // tq),
        in_specs=[pl.BlockSpec((1, tq, HEAD_DIM), lambda bi, h, i: (bi, i, COL["fq"] + h)),
                  pl.BlockSpec((1, t, HEAD_DIM), lambda bi, h, i: (bi, 0, COL["fk"] + h)),
                  pl.BlockSpec((1, t, HEAD_DIM), lambda bi, h, i: (bi, 0, COL["fv"] + h)),
                  pl.BlockSpec((1, 1, t // wide, wide), lambda bi, h, i: (bi, h, 0, 0))],
        out_specs=pl.BlockSpec((1, tq, HEAD_DIM), lambda bi, h, i: (bi, i, h)),
        out_shape=jax.ShapeDtypeStruct((b, t, GROUP_WIDTH), F32),
        scratch_shapes=[pltpu.VMEM((1, 1), F32)] + _attn_scratch(t, Q_CHAINS),
        compiler_params=_cparams(("parallel", "parallel", "arbitrary")),
        name="fox_attention",
    )(p3, p3, p3, nf)


def _neg_abs(x):
    bits = lax.bitcast_convert_type(x, jnp.uint32) | jnp.uint32(0x80000000)
    return lax.bitcast_convert_type(bits, F32)


def _stick_kernel(q_ref, k_ref, v_ref, u_ref, o_ref, r_ref, acc_ref, *, wide):
    t0 = pl.program_id(2) * (Q_CHAINS * Q_ROWS)
    chains = range(Q_CHAINS)
    qs = [q_ref[0, _chain_rows(c), :] for c in chains]
    r_ref[...] = jnp.zeros(r_ref.shape, F32)
    acc_ref[...] = jnp.zeros(acc_ref.shape, F32)
    u = u_ref[...]

    def tile(off, tail):
        ends = [(c + 1) * Q_ROWS if tail else wide for c in chains]
        k = k_ref[0, pl.ds(off, wide), :]
        z_all = [_qk_raw(q, k[:e]) * SCALE_LOG2E for q, e in zip(qs, ends)]
        later = [r_ref[c] for c in chains]
        weights = [[] for _ in chains]
        before = _own_causal(0, Q_ROWS, strict=True)
        for sb in reversed(range(wide // Q_ROWS)):
            live = [c for c in chains if ends[c] > sb * Q_ROWS]
            diag = [tail and c == sb for c in live]
            cols = slice(sb * Q_ROWS, (sb + 1) * Q_ROWS)
            zs = [z_all[c][:, cols] for c in live]
            log_beta = [jnp.minimum(z, 0.0) - jnp.log(1.0 + jnp.exp2(_neg_abs(z))) * LOG2E for z in zs]
            log_keep = [lb - z for lb, z in zip(log_beta, zs)]
            log_keep = [jnp.where(before, lk, 0.0) if d else lk for d, lk in zip(diag, log_keep)]
            inside = [jnp.dot(jnp.concatenate(_split3(lk)[:2], axis=1), u, preferred_element_type=F32)
                      for lk in log_keep]
            a = [jnp.exp2(lb + cs + later[c]) for c, lb, cs in zip(live, log_beta, inside)]
            a = [jnp.where(before, x, 0.0) if d else x for d, x in zip(diag, a)]
            for c, x, cs, lk in zip(live, a, inside, log_keep):
                weights[c].append(x.astype(BF16))
                later[c] = later[c] + cs[:, 0:1] + lk[:, 0:1]
        v = v_ref[0, pl.ds(off, wide), :]
        for c in chains:
            w = weights[c][0] if len(weights[c]) == 1 else jnp.concatenate(weights[c][::-1], axis=1)
            acc_ref[c] += jnp.dot(w, v[:ends[c]], preferred_element_type=F32)
            r_ref[c] = later[c]

    def keep_going(first_done):
        del first_done
        return jnp.max(r_ref[...]) > -STICK_DEAD_LOG2

    _sweep_tiles(t0, wide, functools.partial(tile, tail=True), functools.partial(tile, tail=False), keep_going)
    for c in chains:
        o_ref[0, _chain_rows(c), :] = acc_ref[c]


def _stick_attention(p3):
    b, t, _ = p3.shape
    tq = Q_CHAINS * Q_ROWS
    wide = tq
    ki = np.arange(Q_ROWS)
    u = (ki[:, None] > ki[None, :]).astype(np.float32)
    u = jnp.asarray(np.concatenate([u, u], axis=0), BF16)
    return pl.pallas_call(
        functools.partial(_stick_kernel, wide=wide),
        grid=(b, GROUP_HEADS, t // tq),
        in_specs=[pl.BlockSpec((1, tq, HEAD_DIM), lambda bi, h, i: (bi, i, COL["sq"] + h)),
                  pl.BlockSpec((1, t, HEAD_DIM), lambda bi, h, i: (bi, 0, COL["sk"] + h)),
                  pl.BlockSpec((1, t, HEAD_DIM), lambda bi, h, i: (bi, 0, COL["sv"] + h)),
                  pl.BlockSpec((2 * Q_ROWS, Q_ROWS), lambda bi, h, i: (0, 0))],
        out_specs=pl.BlockSpec((1, tq, HEAD_DIM), lambda bi, h, i: (bi, i, h)),
        out_shape=jax.ShapeDtypeStruct((b, t, GROUP_WIDTH), F32),
        scratch_shapes=[pltpu.VMEM((Q_CHAINS, Q_ROWS, 1), F32), pltpu.VMEM((Q_CHAINS, Q_ROWS, HEAD_DIM), F32)],
        compiler_params=_cparams(("parallel", "parallel", "arbitrary")),
        name="stick_attention",
    )(p3, p3, p3, u)


def _moba_kernel(slope_ref, q_ref, k_ref, v_ref, o_ref, kaug_ref, kmean_ref, knorm_ref, vaug_ref, m_ref, acc_ref, *,
                 n_blk, wide):
    h = pl.program_id(1)
    i = pl.program_id(2)
    tq = MOBA_BLOCK
    slope = slope_ref[h]

    @pl.when(i == 0)
    def _():
        kmean_ref[...] = jnp.zeros(kmean_ref.shape, F32)
        kb = k_ref[0].astype(F32).reshape(n_blk, tq, HEAD_DIM)
        kmean_ref[0:n_blk, :] = jnp.mean(kb, axis=1)

        def key_aux(pos, lane):
            onehot = jnp.where(lane == jnp.right_shift(pos, int(math.log2(tq))), 1.0, 0.0)
            hi, mid, lo = _split3(slope * pos.astype(F32) * (1.0 / ATTN_SCALE))
            return jnp.where(lane == n_blk, hi.astype(F32),
                             jnp.where(lane == n_blk + 1, mid.astype(F32),
                                       jnp.where(lane == n_blk + 2, lo.astype(F32), onehot)))

        _fill_key_aug(kaug_ref, k_ref, key_aux)
        _max_key_norm(k_ref, knorm_ref)
        _fill_value_aug(vaug_ref, v_ref)

    chains = range(Q_CHAINS)
    t0 = i * (Q_CHAINS * Q_ROWS)
    qs = [q_ref[0, _chain_rows(c), :] for c in chains]
    q_norms = [_row_norms(q) for q in qs]
    own = [i * Q_CHAINS + c for c in chains]
    kmean = kmean_ref[...].astype(BF16)
    blk = lax.broadcasted_iota(jnp.int32, (Q_ROWS, LANES), 1)
    blkf = blk.astype(F32)
    scores = [jnp.where(blk < own[c], _qk_raw(qs[c], kmean), -jnp.inf) for c in chains]
    sels = _topk_rounds(scores, blkf, [jnp.where(blk == own[c], 1.0, 0.0) for c in chains], MOBA_TOPK)
    q_aux = [jnp.where(blk < n_blk, jnp.where(sel > 0.5, 0.0, NEG_BIG), jnp.where(blk < n_blk + 3, 1.0, 0.0))
             for sel in sels]
    q_aug = [jnp.concatenate([q, aux.astype(BF16)], axis=1) for q, aux in zip(qs, q_aux)]
    states = [(m_ref.at[c], acc_ref.at[c]) for c in chains]
    _flash_init(m_ref, acc_ref)

    def full_tile(off):
        k = kaug_ref[pl.ds(off, wide), :]
        _online_update_multi([_qk_raw(q, k) for q in q_aug], [vaug_ref[pl.ds(off, wide), :]] * Q_CHAINS, states)

    def tail_tile(off):
        k = kaug_ref[pl.ds(off, wide), :]
        v = vaug_ref[pl.ds(off, wide), :]
        ends = [(c + 1) * Q_ROWS for c in chains]
        raws = [jnp.where(_own_causal(c, e), _qk_raw(q, k[:e]), NEG_BIG) for c, (q, e) in enumerate(zip(q_aug, ends))]
        _online_update_multi(raws, [v[:e] for e in ends], states)

    def keep_going(first_done):
        cap = slope * first_done.astype(F32) * (1.0 / ATTN_SCALE)
        return _softmax_alive(q_norms, knorm_ref[...], [cap] * Q_CHAINS, states)

    _sweep_tiles(t0, wide, tail_tile, full_tile, keep_going)
    for c in chains:
        o_ref[0, _chain_rows(c), :] = _flash_result(acc_ref.at[c])


def _moba_attention(p3, slopes):
    b, t, _ = p3.shape
    tq = Q_CHAINS * Q_ROWS
    n_blk = t // MOBA_BLOCK
    wide = tq
    assert Q_ROWS == MOBA_BLOCK and t % tq == 0 and n_blk + 3 <= LANES
    return pl.pallas_call(
        functools.partial(_moba_kernel, n_blk=n_blk, wide=wide),
        grid=(b, GROUP_HEADS, t // tq),
        in_specs=[pl.BlockSpec(memory_space=pltpu.SMEM),
                  pl.BlockSpec((1, tq, HEAD_DIM), lambda bi, h, i: (bi, i, COL["mq"] + h)),
                  pl.BlockSpec((1, t, HEAD_DIM), lambda bi, h, i: (bi, 0, COL["mk"] + h)),
                  pl.BlockSpec((1, t, HEAD_DIM), lambda bi, h, i: (bi, 0, COL["mv"] + h))],
        out_specs=pl.BlockSpec((1, tq, HEAD_DIM), lambda bi, h, i: (bi, i, h)),
        out_shape=jax.ShapeDtypeStruct((b, t, GROUP_WIDTH), F32),
        scratch_shapes=[pltpu.VMEM((t, HEAD_DIM + LANES), BF16),
                        pltpu.VMEM((LANES, HEAD_DIM), F32), pltpu.VMEM((1, 1), F32)] + _attn_scratch(t, Q_CHAINS),
        compiler_params=_cparams(("parallel", "parallel", "arbitrary")),
        name="moba_attention",
    )(slopes, p3, p3, p3)


def _nsa_compress_kernel(x_ref, pos_ref, w1_ref, w2_ref, o_ref):
    x = (x_ref[0].astype(F32) + pos_ref[...]).astype(BF16)
    hmid = jnp.dot(x, w1_ref[...], preferred_element_type=F32)
    hmid = hmid / (1.0 + jnp.exp(-hmid))
    o_ref[0] = jnp.dot(hmid.astype(BF16), w2_ref[...], preferred_element_type=F32).astype(o_ref.dtype)


def _nsa_compress(blocks, pos, w1, w2):
    r, n, width = blocks.shape
    return pl.pallas_call(
        _nsa_compress_kernel,
        grid=(r,),
        in_specs=[pl.BlockSpec((1, n, width), lambda i: (i, 0, 0)),
                  pl.BlockSpec((1, width), lambda i: (0, 0)),
                  pl.BlockSpec((width, HEAD_DIM), lambda i: (0, 0)),
                  pl.BlockSpec((HEAD_DIM, HEAD_DIM), lambda i: (0, 0))],
        out_specs=pl.BlockSpec((1, n, HEAD_DIM), lambda i: (i, 0, 0)),
        out_shape=jax.ShapeDtypeStruct((r, n, HEAD_DIM), BF16),
        compiler_params=_cparams(("parallel",)),
        name="nsa_compress",
    )(blocks, pos.reshape(1, width).astype(F32), w1.astype(BF16), w2.astype(BF16))


def _nsa_select_kernel(slope_ref, q_ref, kc_ref, vc_ref, a_ref, oc_ref, sel_ref, *, tq, n_cmp, n_blk):
    g = pl.program_id(1)
    i = pl.program_id(2)
    t0 = i * tq
    kc = kc_ref[0, 0]
    vc = vc_ref[0, 0]
    ncp = kc.shape[0]
    tok = lax.broadcasted_iota(jnp.int32, (1, ncp), 1)
    cmp_end = tok * NSA_CMP_STRIDE + (NSA_CMP_LEN - 1)
    row = lax.broadcasted_iota(jnp.int32, (tq, 1), 0)
    admissible = (cmp_end <= row + t0) & (tok < n_cmp)
    rel_end = (cmp_end - t0).astype(F32)

    imp = jnp.zeros((tq, ncp), F32)
    for hh in range(NSA_Q_PER_KV):
        slope = slope_ref[g * NSA_Q_PER_KV + hh]
        q = q_ref[0, :, hh * HEAD_DIM:(hh + 1) * HEAD_DIM]
        s = jnp.where(admissible, _qk(q, kc) + slope * rel_end, NEG_BIG)
        m = jnp.max(s, axis=-1, keepdims=True)
        p = jnp.where(admissible, jnp.exp(s - m), 0.0)
        p = p / jnp.maximum(jnp.sum(p, axis=-1, keepdims=True), F32_TINY)
        oc_ref[0, :, hh * HEAD_DIM:(hh + 1) * HEAD_DIM] = jnp.dot(
            p.astype(BF16), vc, preferred_element_type=F32)
        imp = imp + p

    imp_blk = _dot_split(imp, a_ref[...], 3)
    blk = lax.broadcasted_iota(jnp.int32, (tq, LANES), 1)
    blkf = blk.astype(F32)
    jt = jnp.right_shift(row + t0, int(math.log2(NSA_SEL_BLOCK)))
    live = (blk <= jt) & (blk < n_blk)
    forced = ((blk == 0) | (blk == jt) | (blk == jt - 1)) & live
    sel = jnp.where(forced, 1.0, 0.0)
    score = jnp.where(live & jnp.logical_not(forced), imp_blk, -jnp.inf)
    sel = _topk_rounds([score], blkf, [sel], NSA_SEL_TOPK - 3)[0]
    sel_ref[0, 0] = sel.astype(sel_ref.dtype)


def _nsa_select(p3, kc, vc, slopes, tq):
    b, t, _ = p3.shape
    n_blk = t // NSA_SEL_BLOCK
    assert n_blk <= LANES and NSA_SEL_TOPK >= 3
    n_cmp = t // NSA_CMP_STRIDE - 1
    ncp = kc.shape[2]
    ratio = NSA_SEL_BLOCK // NSA_CMP_STRIDE
    ti = np.arange(ncp)[:, None]
    bj = np.arange(LANES)[None, :]
    amat = ((ti >= ratio * bj - 1) & (ti <= ratio * bj + ratio - 1) & (ti < n_cmp) & (bj < n_blk))
    amat = jnp.asarray(amat.astype(np.float32), BF16)
    width = NSA_Q_PER_KV * HEAD_DIM
    qblk = COL["nq"] * HEAD_DIM // width
    return pl.pallas_call(
        functools.partial(_nsa_select_kernel, tq=tq, n_cmp=n_cmp, n_blk=n_blk),
        grid=(b, NSA_KV_HEADS, t // tq),
        in_specs=[pl.BlockSpec(memory_space=pltpu.SMEM),
                  pl.BlockSpec((1, tq, width), lambda bi, g, i: (bi, i, qblk + g)),
                  pl.BlockSpec((1, 1, ncp, HEAD_DIM), lambda bi, g, i: (bi, g, 0, 0)),
                  pl.BlockSpec((1, 1, ncp, HEAD_DIM), lambda bi, g, i: (bi, g, 0, 0)),
                  pl.BlockSpec((ncp, LANES), lambda bi, g, i: (0, 0))],
        out_specs=[pl.BlockSpec((1, tq, width), lambda bi, g, i: (bi, i, g)),
                   pl.BlockSpec((1, 1, tq, LANES), lambda bi, g, i: (bi, g, i, 0))],
        out_shape=[jax.ShapeDtypeStruct((b, t, GROUP_WIDTH), F32),
                   jax.ShapeDtypeStruct((b, NSA_KV_HEADS, t, LANES), BF16)],
        compiler_params=_cparams(("parallel", "parallel", "parallel")),
        name="nsa_select",
    )(slopes, p3, kc, vc, amat)


def _nsa_attn_kernel(slope_ref, q_ref, ks_ref, vs_ref, kw_ref, vw_ref, sel_ref, oc_ref, gate_ref,
                     o_ref, kaug_ref, knorm_ref, vwaug_ref, vsaug_ref, m_ref, acc_ref, *, tq, wide):
    g = pl.program_id(1)
    i = pl.program_id(2)
    t0 = pl.multiple_of(i * tq, tq)
    nh = NSA_Q_PER_KV

    @pl.when(i == 0)
    def _():
        sel_shift = int(math.log2(NSA_SEL_BLOCK))
        _fill_key_aug(kaug_ref, ks_ref,
                      lambda pos, lane: jnp.where(lane == jnp.right_shift(pos, sel_shift), 1.0, 0.0))
        _max_key_norm(ks_ref, knorm_ref)
        _fill_value_aug(vsaug_ref, vs_ref)
        _fill_value_aug(vwaug_ref, vw_ref)

    _flash_init(m_ref, acc_ref)
    sel_bias = jnp.where(sel_ref[0, 0].astype(F32) > 0.5, 0.0, NEG_BIG).astype(BF16)
    qs = [q_ref[0, :, hh * HEAD_DIM:(hh + 1) * HEAD_DIM] for hh in range(nh)]
    q_aug = [jnp.concatenate([q, sel_bias], axis=1) for q in qs]
    slope_raw = [slope_ref[g * nh + hh] * (1.0 / ATTN_SCALE) for hh in range(nh)]
    row = lax.broadcasted_iota(jnp.int32, (tq, 1), 0)

    def rel_pos(off, width):
        return lax.broadcasted_iota(jnp.int32, (1, width), 1) + (off - t0)

    sel_states = [(m_ref.at[hh], acc_ref.at[hh]) for hh in range(nh)]
    win_states = [(m_ref.at[nh + hh], acc_ref.at[nh + hh]) for hh in range(nh)]
    q_norms = [_row_norms(q) for q in qs]

    def sel_scores(off, tail):
        rel = rel_pos(off, wide)
        relf = rel.astype(F32)
        k = kaug_ref[pl.ds(off, wide), :]
        raws = [_qk_raw(q_aug[hh], k) + slope_raw[hh] * relf for hh in range(nh)]
        if tail:
            raws = [jnp.where(rel <= row, raw, NEG_BIG) for raw in raws]
        return raws, [vsaug_ref[pl.ds(off, wide), :]] * nh

    def window_scores(off, width, far):
        rel = rel_pos(off, width)
        relf = rel.astype(F32)
        k = kw_ref[0, pl.ds(off, width), :]
        ok = rel <= row
        if far:
            ok = ok & (row - rel < NSA_WINDOW)
        raws = [jnp.where(ok, _qk_raw(qs[hh], k) + slope_raw[hh] * relf, NEG_BIG) for hh in range(nh)]
        return raws, [vwaug_ref[pl.ds(off, width), :]] * nh

    def sel_tile(off, tail):
        raws, vs = sel_scores(off, tail)
        _online_update_multi(raws, vs, sel_states)

    def keep_going(first_done):
        gap = (first_done - t0).astype(F32)
        return _softmax_alive(q_norms, knorm_ref[...], [s * gap for s in slope_raw], sel_states)

    _sweep_tiles(t0, wide, functools.partial(sel_tile, tail=True), functools.partial(sel_tile, tail=False),
                 keep_going)

    n_back = NSA_WINDOW // tq
    for back in range(n_back + 1):
        @pl.when((i == back) if back < n_back else (i >= back))
        def _(back=back):
            raws, vs = window_scores(pl.multiple_of(t0 - back * tq, tq), (back + 1) * tq, back == n_back)
            _online_update_multi(raws, vs, win_states)

    gates = gate_ref[0, 0]
    gates = 1.0 / (1.0 + jnp.exp(-gates))
    for hh in range(nh):
        c0 = hh * NSA_N_BRANCHES
        o_c = oc_ref[0, :, hh * HEAD_DIM:(hh + 1) * HEAD_DIM]
        o_s = _flash_result(acc_ref.at[hh])
        o_w = _flash_result(acc_ref.at[nh + hh])
        o_ref[0, :, hh * HEAD_DIM:(hh + 1) * HEAD_DIM] = (
            gates[:, c0:c0 + 1] * o_c + gates[:, c0 + 1:c0 + 2] * o_s + gates[:, c0 + 2:c0 + 3] * o_w)


def _nsa_attention(p3, sel, o_cmp, gate_logits, slopes, tq):
    b, t, _ = p3.shape
    assert tq == KV_UNIT and NSA_WINDOW == 2 * tq and t // NSA_SEL_BLOCK <= LANES
    width = NSA_Q_PER_KV * HEAD_DIM
    qblk = COL["nq"] * HEAD_DIM // width
    kv = lambda name: pl.BlockSpec((1, t, HEAD_DIM), lambda bi, g, i: (bi, 0, COL[name] + g))
    nst = 2 * NSA_Q_PER_KV
    return pl.pallas_call(
        functools.partial(_nsa_attn_kernel, tq=tq, wide=min(KV_WIDE, t)),
        grid=(b, NSA_KV_HEADS, t // tq),
        in_specs=[pl.BlockSpec(memory_space=pltpu.SMEM),
                  pl.BlockSpec((1, tq, width), lambda bi, g, i: (bi, i, qblk + g)),
                  kv("nks"), kv("nvs"), kv("nkw"), kv("nvw"),
                  pl.BlockSpec((1, 1, tq, LANES), lambda bi, g, i: (bi, g, i, 0)),
                  pl.BlockSpec((1, tq, width), lambda bi, g, i: (bi, i, g)),
                  pl.BlockSpec((1, 1, tq, LANES), lambda bi, g, i: (bi, g, i, 0))],
        out_specs=pl.BlockSpec((1, tq, width), lambda bi, g, i: (bi, i, g)),
        out_shape=jax.ShapeDtypeStruct((b, t, GROUP_WIDTH), F32),
        scratch_shapes=[pltpu.VMEM((t, HEAD_DIM + LANES), BF16), pltpu.VMEM((1, 1), F32),
                        pltpu.VMEM((t, HEAD_DIM + LANES), BF16)] + _attn_scratch(t, nst),
        compiler_params=_cparams(("parallel", "parallel", "arbitrary")),
        name="nsa_attention",
    )(slopes, p3, p3, p3, p3, p3, sel, o_cmp, gate_logits)


def _nsa_blocks(p3, name):
    b, t, _ = p3.shape
    c0 = COL[name] * HEAD_DIM
    x = p3[:, :, c0:c0 + NSA_KV_HEADS * HEAD_DIM].reshape(b, t // NSA_CMP_STRIDE, NSA_CMP_STRIDE,
                                                         NSA_KV_HEADS, HEAD_DIM)
    x = x.transpose(0, 3, 1, 2, 4).reshape(b * NSA_KV_HEADS, t // NSA_CMP_STRIDE, NSA_CMP_STRIDE * HEAD_DIM)
    blocks = jnp.concatenate([x[:, :-1], x[:, 1:]], axis=-1)
    return jnp.pad(blocks, ((0, 0), (0, 1), (0, 0)))


def _alibi_slopes():
    n = 2 * GROUP_HEADS
    slopes = 2.0 ** (-8.0 * np.arange(1, n + 1) / n)
    return jnp.asarray(slopes[0::2], F32), jnp.asarray(slopes[1::2], F32)


def _regroup_w_in(w_in):
    main = jnp.concatenate([w_in[:, REF_OFF[n][0]:REF_OFF[n][0] + REF_OFF[n][1]] for n, _ in _MAIN_PIECES],
                           axis=1).astype(BF16)
    ng0, ngw = REF_OFF["ng"]
    ff0, ffw = REF_OFF["ff"]
    small = jnp.concatenate([w_in[:, ng0:ng0 + ngw], w_in[:, ff0:ff0 + ffw]], axis=1)
    small = jnp.pad(small, ((0, 0), (0, LANES - ngw - ffw))).astype(BF16)
    return main, small


def _mixer(x2d, normed, b, t, w_in, forget_bias, pos_k, w1_k, w2_k, pos_v, w1_v, w2_v, group_gain, w_out, next_gain):
    m = x2d.shape[0]
    w_main, w_small = _regroup_w_in(w_in)
    tm = _pick_tile(m, (1024, 512, 256))
    tm_in = _pick_tile(m, (2048, 1024, 512, 256))
    p3 = _matmul_normed(normed, w_main, BF16, tm_in, 512).reshape(b, t, MAIN_BLOCKS * HEAD_DIM)
    small = _matmul_normed(normed, w_small, F32, tm_in, LANES).reshape(b, t, LANES)
    slopes_moba, slopes_nsa = _alibi_slopes()

    o_a = _moba_attention(p3, slopes_moba)

    n_gate = NSA_N_BRANCHES * GROUP_HEADS
    kc = _nsa_compress(_nsa_blocks(p3, "nkc"), pos_k, w1_k, w2_k)
    vc = _nsa_compress(_nsa_blocks(p3, "nvc"), pos_v, w1_v, w2_v)
    ncp = kc.shape[1]
    kc = kc.reshape(b, NSA_KV_HEADS, ncp, HEAD_DIM)
    vc = vc.reshape(b, NSA_KV_HEADS, ncp, HEAD_DIM)
    o_cmp, sel = _nsa_select(p3, kc, vc, slopes_nsa, _pick_tile(t, (512, 256)))
    per_group = n_gate // NSA_KV_HEADS
    gate_logits = small[:, :, :n_gate].reshape(b, t, NSA_KV_HEADS, per_group).transpose(0, 2, 1, 3)
    gate_logits = jnp.pad(gate_logits, ((0, 0), (0, 0), (0, 0), (0, LANES - per_group)))
    o_b = _nsa_attention(p3, sel, o_cmp, gate_logits, slopes_nsa, KV_UNIT)

    ff = small[:, :, n_gate:n_gate + GROUP_HEADS] + forget_bias
    neg_cum_f = _neg_cum_logf(ff.transpose(0, 2, 1).reshape(b * GROUP_HEADS, t))
    o_c = _fox_attention(p3, neg_cum_f)

    o_d = _stick_attention(p3)

    on = _groupnorm([o.reshape(m, GROUP_WIDTH) for o in (o_a, o_b, o_c, o_d)], group_gain)
    return _matmul_residual(on, w_out, x2d, next_gain, tm, 512)


def _cast_pad_kernel(w_ref, o_ref, *, n_row_tiles, cols):
    @pl.when(pl.program_id(0) < n_row_tiles)
    def _():
        o_ref[:, 0:cols] = w_ref[0].astype(BF16)
        if o_ref.shape[1] > cols:
            o_ref[:, cols:] = jnp.zeros((o_ref.shape[0], o_ref.shape[1] - cols), BF16)

    @pl.when(pl.program_id(0) >= n_row_tiles)
    def _():
        o_ref[...] = jnp.zeros(o_ref.shape, BF16)


def _weight_bf16(stacked, layer, rows_out, cols_out, tr=256):
    _, rows, cols = stacked.shape
    assert rows % tr == 0 and rows_out % tr == 0 and cols % LANES == 0
    n_row_tiles = rows // tr
    return pl.pallas_call(
        functools.partial(_cast_pad_kernel, n_row_tiles=n_row_tiles, cols=cols),
        grid=(rows_out // tr,),
        in_specs=[pl.BlockSpec((1, tr, cols), lambda j: (layer, jnp.minimum(j, n_row_tiles - 1), 0))],
        out_specs=pl.BlockSpec((tr, cols_out), lambda j: (j, 0)),
        out_shape=jax.ShapeDtypeStruct((rows_out, cols_out), BF16),
        compiler_params=_cparams(("parallel",)),
        name="weight_bf16",
    )(stacked)


def kernel(x, ffn1_norm, ffn1_w_gate, ffn1_w_up, ffn1_w_down, mix_norm, w_in, fox_forget_bias, nsa_cmp_pos_k, nsa_cmp_w1_k, nsa_cmp_w2_k, nsa_cmp_pos_v, nsa_cmp_w1_v, nsa_cmp_w2_v, group_norm, w_out, ffn2_norm, ffn2_w_gate, ffn2_w_up, ffn2_w_down, final_norm):
    b, t, d = x.shape
    x2d = x.reshape(b * t, d)
    depth = ffn1_norm.shape[0]
    normed = _prenorm(x2d, ffn1_norm[0])
    for i in range(depth):
        x2d, normed = _ffn(x2d, normed, ffn1_w_gate, ffn1_w_up, ffn1_w_down, i, mix_norm[i])
        x2d, normed = _mixer(x2d, normed, b, t, w_in[i], fox_forget_bias[i], nsa_cmp_pos_k[i], nsa_cmp_w1_k[i],
                             nsa_cmp_w2_k[i], nsa_cmp_pos_v[i], nsa_cmp_w1_v[i], nsa_cmp_w2_v[i],
                             group_norm[i], _weight_bf16(w_out, i, d, d), ffn2_norm[i])
        x2d, normed = _ffn(x2d, normed, ffn2_w_gate, ffn2_w_up, ffn2_w_down, i,
                           ffn1_norm[i + 1] if i + 1 < depth else None)
    return _rmsnorm(x2d, final_norm, F32).reshape(b, t, d)
```

```python
import functools
import math

import numpy as np
import jax
import jax.numpy as jnp
from jax import lax
from jax.experimental import pallas as pl
from jax.experimental.pallas import tpu as pltpu

F32 = jnp.float32
BF16 = jnp.bfloat16

HEAD_DIM = 128
GROUP_HEADS = 8
GROUP_WIDTH = GROUP_HEADS * HEAD_DIM
N_MIXERS = 4
FFN_RESIDUAL = 0.5
RMS_EPS = 1e-6

MOBA_BLOCK = 256
MOBA_TOPK = 3

NSA_KV_HEADS = 2
NSA_Q_PER_KV = 4
NSA_CMP_STRIDE = 16
NSA_CMP_LEN = 32
NSA_SEL_BLOCK = 64
NSA_SEL_TOPK = 16
NSA_WINDOW = 512
NSA_N_BRANCHES = 3

ATTN_SCALE = HEAD_DIM ** -0.5
LOG2E = math.log2(math.e)
SCALE_LOG2E = ATTN_SCALE * LOG2E
NEG_BIG = -1e30
STICK_DEAD_LOG2 = 160.0
FLASH_DEAD_LOG2 = 170.0
BOUND_SLACK = 1.001
KV_UNIT = 256
KV_WIDE = 1024
Q_ROWS = 256
Q_CHAINS = 4
F32_TINY = float(np.finfo(np.float32).tiny)

LANES = 128
VMEM_LIMIT = 56 * 1024 * 1024
GATEUP_TILE = 256
FFN_PAD = 512

_MAIN_PIECES = (
    ("mq", 8), ("mk", 8), ("mv", 8), ("nq", 8),
    ("nkc", 2), ("nvc", 2), ("nks", 2), ("nvs", 2), ("nkw", 2), ("nvw", 2),
    ("fq", 8), ("fk", 8), ("fv", 8), ("sq", 8), ("sk", 8), ("sv", 8),
)
COL = {}
_o = 0
for _n, _w in _MAIN_PIECES:
    COL[_n] = _o
    _o += _w
MAIN_BLOCKS = _o

_REF_SIZES = (
    ("mq", 1024), ("mk", 1024), ("mv", 1024), ("nq", 1024),
    ("nkc", 256), ("nvc", 256), ("nks", 256), ("nvs", 256), ("nkw", 256), ("nvw", 256),
    ("ng", 24),
    ("fq", 1024), ("fk", 1024), ("fv", 1024), ("ff", 8),
    ("sq", 1024), ("sk", 1024), ("sv", 1024),
)
REF_OFF = {}
_o = 0
for _n, _w in _REF_SIZES:
    REF_OFF[_n] = (_o, _w)
    _o += _w


def _cparams(sem):
    return pltpu.CompilerParams(dimension_semantics=sem, vmem_limit_bytes=VMEM_LIMIT)


def _rms_kernel(x_ref, g_ref, o_ref):
    x = x_ref[...]
    ms = jnp.mean(x * x, axis=-1, keepdims=True)
    o_ref[...] = (x * lax.rsqrt(ms + RMS_EPS) * g_ref[...]).astype(o_ref.dtype)


def _rmsnorm(x2d, gain, out_dtype, tm=256):
    m, d = x2d.shape
    return pl.pallas_call(
        _rms_kernel,
        grid=(m // tm,),
        in_specs=[pl.BlockSpec((tm, d), lambda i: (i, 0)),
                  pl.BlockSpec((1, d), lambda i: (0, 0))],
        out_specs=pl.BlockSpec((tm, d), lambda i: (i, 0)),
        out_shape=jax.ShapeDtypeStruct((m, d), out_dtype),
        compiler_params=_cparams(("parallel",)),
        name="rmsnorm",
    )(x2d, gain.reshape(1, d).astype(F32))


def _emit_normed(x_new, gain_ref, xg_ref, ssq_ref):
    xg_ref[...] = (x_new * gain_ref[...]).astype(BF16)
    ssq_ref[...] = jnp.broadcast_to(jnp.sum(x_new * x_new, axis=-1, keepdims=True), ssq_ref.shape)


def _rstd(ssq_ref, d):
    ssq = ssq_ref[...]
    total = ssq[:, 0:1]
    for p in range(1, ssq.shape[1] // LANES):
        total = total + ssq[:, p * LANES:p * LANES + 1]
    return lax.rsqrt(total * (1.0 / d) + RMS_EPS)


def _prenorm_kernel(x_ref, g_ref, xg_ref, ssq_ref):
    _emit_normed(x_ref[...], g_ref, xg_ref, ssq_ref)


def _prenorm(x2d, gain, tm=256):
    m, d = x2d.shape
    return pl.pallas_call(
        _prenorm_kernel,
        grid=(m // tm,),
        in_specs=[pl.BlockSpec((tm, d), lambda i: (i, 0)),
                  pl.BlockSpec((1, d), lambda i: (0, 0))],
        out_specs=[pl.BlockSpec((tm, d), lambda i: (i, 0)), pl.BlockSpec((tm, LANES), lambda i: (i, 0))],
        out_shape=[jax.ShapeDtypeStruct((m, d), BF16), jax.ShapeDtypeStruct((m, LANES), F32)],
        compiler_params=_cparams(("parallel",)),
        name="prenorm",
    )(x2d, gain.reshape(1, d).astype(F32))


def _groupnorm_kernel(a_ref, b_ref, c_ref, d_ref, g_ref, o_ref):
    for gi, ref in enumerate((a_ref, b_ref, c_ref, d_ref)):
        x = ref[...]
        ms = jnp.mean(x * x, axis=-1, keepdims=True)
        lo, hi = gi * GROUP_WIDTH, (gi + 1) * GROUP_WIDTH
        o_ref[:, lo:hi] = (x * lax.rsqrt(ms + RMS_EPS) * g_ref[:, lo:hi]).astype(o_ref.dtype)


def _groupnorm(parts, gain, tm=256):
    m = parts[0].shape[0]
    d = GROUP_WIDTH * N_MIXERS
    part_spec = pl.BlockSpec((tm, GROUP_WIDTH), lambda i: (i, 0))
    return pl.pallas_call(
        _groupnorm_kernel,
        grid=(m // tm,),
        in_specs=[part_spec] * 4 + [pl.BlockSpec((1, d), lambda i: (0, 0))],
        out_specs=pl.BlockSpec((tm, d), lambda i: (i, 0)),
        out_shape=jax.ShapeDtypeStruct((m, d), BF16),
        compiler_params=_cparams(("parallel",)),
        name="groupnorm",
    )(*parts, gain.reshape(1, d).astype(F32))


def _mm_normed_kernel(a_ref, ssq_ref, w_ref, o_ref):
    acc = jnp.dot(a_ref[...], w_ref[...], preferred_element_type=F32)
    o_ref[...] = (acc * _rstd(ssq_ref, a_ref.shape[1])).astype(o_ref.dtype)


def _matmul_normed(normed, w, out_dtype, tm, tn):
    a, ssq = normed
    m, k = a.shape
    n = w.shape[1]
    return pl.pallas_call(
        _mm_normed_kernel,
        grid=(m // tm, n // tn),
        in_specs=[pl.BlockSpec((tm, k), lambda i, j: (i, 0)),
                  pl.BlockSpec((tm, ssq.shape[1]), lambda i, j: (i, 0)),
                  pl.BlockSpec((k, tn), lambda i, j: (0, j))],
        out_specs=pl.BlockSpec((tm, tn), lambda i, j: (i, j)),
        out_shape=jax.ShapeDtypeStruct((m, n), out_dtype),
        compiler_params=_cparams(("parallel", "arbitrary")),
        name="matmul",
    )(a, ssq, w)


def _normed_outputs(m, n, tm, tn, index):
    specs = [pl.BlockSpec((tm, tn), index), pl.BlockSpec((tm, LANES), index)]
    shapes = [jax.ShapeDtypeStruct((m, n), BF16), jax.ShapeDtypeStruct((m, LANES * (n // tn)), F32)]
    return specs, shapes


def _mm_res_kernel(a_ref, w_ref, x_ref, g_ref, o_ref, xg_ref, ssq_ref):
    x_new = x_ref[...] + jnp.dot(a_ref[...], w_ref[...], preferred_element_type=F32)
    o_ref[...] = x_new
    _emit_normed(x_new, g_ref, xg_ref, ssq_ref)


def _matmul_residual(a, w, x, next_gain, tm, tn):
    m, k = a.shape
    n = w.shape[1]
    index = lambda i, j: (i, j)
    n_specs, n_shapes = _normed_outputs(m, n, tm, tn, index)
    x_new, xg, ssq = pl.pallas_call(
        _mm_res_kernel,
        grid=(m // tm, n // tn),
        in_specs=[pl.BlockSpec((tm, k), lambda i, j: (i, 0)),
                  pl.BlockSpec((k, tn), lambda i, j: (0, j)),
                  pl.BlockSpec((tm, tn), index),
                  pl.BlockSpec((1, tn), lambda i, j: (0, j))],
        out_specs=[pl.BlockSpec((tm, tn), index)] + n_specs,
        out_shape=[jax.ShapeDtypeStruct((m, n), F32)] + n_shapes,
        compiler_params=_cparams(("parallel", "arbitrary")),
        name="matmul_residual",
    )(a, w, x, next_gain.reshape(1, n).astype(F32))
    return x_new, (xg, ssq)


def _gateup_kernel(h_ref, ssq_ref, wg_ref, wu_ref, o_ref, rstd_ref, rstd_res_ref, *, n_valid):
    @pl.when(pl.program_id(1) == 0)
    def _():
        rstd = _rstd(ssq_ref, h_ref.shape[1])
        rstd_ref[...] = rstd
        rstd_res_ref[...] = rstd * FFN_RESIDUAL

    @pl.when(pl.program_id(1) < n_valid)
    def _():
        h = h_ref[...]
        g = jnp.dot(h, wg_ref[0].astype(BF16), preferred_element_type=F32) * rstd_ref[...]
        u = jnp.dot(h, wu_ref[0].astype(BF16), preferred_element_type=F32)
        o_ref[...] = (g / (1.0 + jnp.exp(-g)) * u * rstd_res_ref[...]).astype(o_ref.dtype)

    @pl.when(pl.program_id(1) >= n_valid)
    def _():
        o_ref[...] = jnp.zeros(o_ref.shape, o_ref.dtype)


def _gateup(normed, wg, wu, layer, f_out, tm, tf):
    h, ssq = normed
    m, d = h.shape
    f = wg.shape[2]
    assert f % tf == 0 and f_out % tf == 0
    n_valid = f // tf
    w_spec = pl.BlockSpec((1, d, tf), lambda i, j: (layer, 0, jnp.minimum(j, n_valid - 1)))
    once = dict(pipeline_mode=pl.Buffered(1))
    return pl.pallas_call(
        functools.partial(_gateup_kernel, n_valid=n_valid),
        grid=(m // tm, f_out // tf),
        in_specs=[pl.BlockSpec((tm, d), lambda i, j: (i, 0), **once),
                  pl.BlockSpec((tm, ssq.shape[1]), lambda i, j: (i, 0), **once), w_spec, w_spec],
        out_specs=pl.BlockSpec((tm, tf), lambda i, j: (i, j)),
        out_shape=jax.ShapeDtypeStruct((m, f_out), BF16),
        scratch_shapes=[pltpu.VMEM((tm, 1), F32), pltpu.VMEM((tm, 1), F32)],
        compiler_params=_cparams(("parallel", "arbitrary")),
        name="ffn_gateup",
    )(h, ssq, wg, wu)


def _down_kernel(a_ref, w_ref, x_ref, *rest, emit):
    o_ref = rest[1] if emit else rest[0]
    @pl.when(pl.program_id(2) == 0)
    def _():
        o_ref[...] = x_ref[...] + jnp.dot(a_ref[...], w_ref[...], preferred_element_type=F32)

    @pl.when(pl.program_id(2) != 0)
    def _():
        o_ref[...] = o_ref[...] + jnp.dot(a_ref[...], w_ref[...], preferred_element_type=F32)

    if emit:
        g_ref, _, xg_ref, ssq_ref = rest

        @pl.when(pl.program_id(2) == pl.num_programs(2) - 1)
        def _():
            _emit_normed(o_ref[...], g_ref, xg_ref, ssq_ref)


def _down(act, wd, x, next_gain, tm, tn, tk):
    m, f = act.shape
    n = wd.shape[1]
    emit = next_gain is not None
    index = lambda i, j, k: (i, j)
    in_specs = [pl.BlockSpec((tm, tk), lambda i, j, k: (i, k)),
                pl.BlockSpec((tk, tn), lambda i, j, k: (k, j)),
                pl.BlockSpec((tm, tn), index)]
    out_specs, out_shape, args = [pl.BlockSpec((tm, tn), index)], [jax.ShapeDtypeStruct((m, n), F32)], [act, wd, x]
    if emit:
        n_specs, n_shapes = _normed_outputs(m, n, tm, tn, index)
        in_specs.append(pl.BlockSpec((1, tn), lambda i, j, k: (0, j)))
        out_specs, out_shape = out_specs + n_specs, out_shape + n_shapes
        args.append(next_gain.reshape(1, n).astype(F32))
    outs = pl.pallas_call(
        functools.partial(_down_kernel, emit=emit),
        grid=(m // tm, n // tn, f // tk),
        in_specs=in_specs,
        out_specs=out_specs,
        out_shape=out_shape,
        compiler_params=_cparams(("parallel", "parallel", "arbitrary")),
        name="ffn_down",
    )(*args)
    return (outs[0], (outs[1], outs[2])) if emit else (outs[0], None)


def _pick_tile(n, prefs):
    for t in prefs:
        if n % t == 0:
            return t
    return n


def _ffn(x2d, normed, wg, wu, wd, layer, next_gain):
    m, d = x2d.shape
    f = wg.shape[2]
    fp = -(-f // FFN_PAD) * FFN_PAD
    tm = _pick_tile(m, (1024, 512, 256))
    act = _gateup(normed, wg, wu, layer, fp, _pick_tile(m, (2048, 1024, 512, 256)),
                  _pick_tile(f, (GATEUP_TILE, 128)))
    return _down(act, _weight_bf16(wd, layer, fp, d), x2d, next_gain, tm, _pick_tile(d, (1024, 512, 256, 128)),
                 _pick_tile(fp, (2816, 1024, 512, 256, 128)))


def _qk_raw(q, k):
    return lax.dot_general(q, k, (((1,), (1,)), ((), ())), preferred_element_type=F32)


def _qk(q, k):
    return _qk_raw(q, k) * ATTN_SCALE


def _online_update_multi(raws, vs, states):
    m_prev = [m_ref[...] for m_ref, _ in states]
    m_new = [jnp.maximum(mp, jnp.max(raw, axis=-1, keepdims=True)) for mp, raw in zip(m_prev, raws)]
    ps = [jnp.exp2((raw - mn) * SCALE_LOG2E).astype(BF16) for raw, mn in zip(raws, m_new)]
    alphas = [jnp.exp2((mp - mn) * SCALE_LOG2E) for mp, mn in zip(m_prev, m_new)]
    pvs = [jnp.dot(p, v, preferred_element_type=F32) for p, v in zip(ps, vs)]
    for (m_ref, acc_ref), mn, a, pv in zip(states, m_new, alphas, pvs):
        acc_ref[...] = a * acc_ref[...] + pv
        m_ref[...] = mn


def _flash_init(m_ref, acc_ref):
    m_ref[...] = jnp.full(m_ref.shape, NEG_BIG, F32)
    acc_ref[...] = jnp.zeros(acc_ref.shape, F32)


def _flash_result(acc_ref):
    acc = acc_ref[...]
    return acc[:, :HEAD_DIM] / acc[:, HEAD_DIM:HEAD_DIM + 1]


def _fill_value_aug(vaug_ref, v_ref):
    def body(c, carry):
        off = pl.multiple_of(c * KV_UNIT, KV_UNIT)
        lane = lax.broadcasted_iota(jnp.int32, (KV_UNIT, LANES), 1)
        vaug_ref[pl.ds(off, KV_UNIT), 0:HEAD_DIM] = v_ref[0, pl.ds(off, KV_UNIT), :]
        vaug_ref[pl.ds(off, KV_UNIT), HEAD_DIM:] = jnp.where(lane == 0, 1.0, 0.0).astype(BF16)
        return carry

    lax.fori_loop(0, vaug_ref.shape[0] // KV_UNIT, body, 0)


def _sweep_tiles(t0, wide, tail_tile, full_tile, keep_going):
    n_full = t0 // wide
    tail_off = pl.multiple_of(n_full * wide, wide)

    def full_off(jj):
        return pl.multiple_of((n_full - 1 - jj) * wide, wide)

    def step(state):
        jj, _ = state
        full_tile(full_off(jj))
        return jj + 1, keep_going(full_off(jj))

    tail_tile(tail_off)
    lax.while_loop(lambda s: (s[0] < n_full) & s[1], step, (jnp.int32(0), keep_going(tail_off)))


def _row_norms(q):
    q = q.astype(F32)
    return jnp.sqrt(jnp.sum(q * q, axis=-1, keepdims=True))


def _max_key_norm(k_ref, out_ref):
    def body(c, mx):
        k = k_ref[0, pl.ds(pl.multiple_of(c * KV_UNIT, KV_UNIT), KV_UNIT), :].astype(F32)
        return jnp.maximum(mx, jnp.max(jnp.sum(k * k, axis=-1, keepdims=True), axis=0, keepdims=True))

    out_ref[...] = jnp.sqrt(lax.fori_loop(0, k_ref.shape[1] // KV_UNIT, body, jnp.zeros((1, 1), F32)))


def _softmax_alive(q_norms, k_norm, bias_caps, states):
    gaps = [qn * (k_norm * BOUND_SLACK) + cap - m_ref[...]
            for qn, cap, (m_ref, _) in zip(q_norms, bias_caps, states)]
    worst = jnp.max(functools.reduce(jnp.maximum, gaps))
    return (worst + 1.0) * SCALE_LOG2E > -FLASH_DEAD_LOG2


def _fill_key_aug(kaug_ref, k_ref, aux_fn):
    t = kaug_ref.shape[0]

    def body(c, carry):
        off = pl.multiple_of(c * KV_UNIT, KV_UNIT)
        pos = lax.broadcasted_iota(jnp.int32, (KV_UNIT, LANES), 0) + off
        lane = lax.broadcasted_iota(jnp.int32, (KV_UNIT, LANES), 1)
        kaug_ref[pl.ds(off, KV_UNIT), 0:HEAD_DIM] = k_ref[0, pl.ds(off, KV_UNIT), :]
        kaug_ref[pl.ds(off, KV_UNIT), HEAD_DIM:] = aux_fn(pos, lane).astype(BF16)
        return carry

    lax.fori_loop(0, t // KV_UNIT, body, 0)


def _topk_rounds(scores, blkf, sels, rounds):
    for _ in range(rounds):
        mx = [jnp.max(s, axis=-1, keepdims=True) for s in scores]
        idx = [jnp.min(jnp.where(s == m, blkf, 1e9), axis=-1, keepdims=True) for s, m in zip(scores, mx)]
        picks = [(blkf == i) & (m > -jnp.inf) for i, m in zip(idx, mx)]
        sels = [jnp.where(p, 1.0, sel) for p, sel in zip(picks, sels)]
        scores = [jnp.where(p, -jnp.inf, s) for p, s in zip(picks, scores)]
    return sels


def _split3(x):
    hi = x.astype(BF16)
    r1 = x - hi.astype(F32)
    mid = r1.astype(BF16)
    lo = (r1 - mid.astype(F32)).astype(BF16)
    return hi, mid, lo


def _dot_split(x, w_bf16, terms):
    parts = _split3(x)[:terms]
    out = jnp.dot(parts[0], w_bf16, preferred_element_type=F32)
    for p in parts[1:]:
        out = out + jnp.dot(p, w_bf16, preferred_element_type=F32)
    return out


def _logf_cumsum_kernel(x_ref, tri_ref, low_ref, o_ref):
    x = x_ref[0]
    logf = jnp.minimum(x, 0.0) - jnp.log1p(jnp.exp(-jnp.abs(x)))
    within = _dot_split(logf, tri_ref[...], 3)
    hi, mid, lo = _split3(logf)
    low = low_ref[...]
    before = (jnp.dot(low, hi, preferred_element_type=F32) + jnp.dot(low, mid, preferred_element_type=F32)
              + jnp.dot(low, lo, preferred_element_type=F32))
    o_ref[0] = -(within + jnp.sum(before, axis=-1, keepdims=True))


def _neg_cum_logf(logits_rows):
    r, t = logits_rows.shape
    nc = t // LANES
    li = np.arange(LANES)
    tri = jnp.asarray((li[:, None] <= li[None, :]).astype(np.float32), BF16)
    ci = np.arange(nc)
    low = jnp.asarray((ci[None, :] < ci[:, None]).astype(np.float32), BF16)
    out = pl.pallas_call(
        _logf_cumsum_kernel,
        grid=(r,),
        in_specs=[pl.BlockSpec((1, nc, LANES), lambda i: (i, 0, 0)),
                  pl.BlockSpec((LANES, LANES), lambda i: (0, 0)),
                  pl.BlockSpec((nc, nc), lambda i: (0, 0))],
        out_specs=pl.BlockSpec((1, nc, LANES), lambda i: (i, 0, 0)),
        out_shape=jax.ShapeDtypeStruct((r, nc, LANES), F32),
        compiler_params=_cparams(("parallel",)),
        name="fox_logf_cumsum",
    )(logits_rows.reshape(r, nc, LANES), tri, low)
    return out.reshape(r, t)


def _chain_rows(c):
    return slice(c * Q_ROWS, (c + 1) * Q_ROWS)


def _own_causal(c, width, strict=False):
    col = lax.broadcasted_iota(jnp.int32, (1, width), 1)
    row = lax.broadcasted_iota(jnp.int32, (Q_ROWS, 1), 0) + c * Q_ROWS
    return (col < row) if strict else (col <= row)


def _fox_kernel(q_ref, k_ref, v_ref, nf_ref, o_ref, knorm_ref, vaug_ref, m_ref, acc_ref, *, wide):
    t0 = pl.program_id(2) * (Q_CHAINS * Q_ROWS)

    @pl.when(pl.program_id(2) == 0)
    def _():
        _max_key_norm(k_ref, knorm_ref)
        _fill_value_aug(vaug_ref, v_ref)

    qs = [q_ref[0, _chain_rows(c), :] for c in range(Q_CHAINS)]
    q_norms = [_row_norms(q) for q in qs]
    states = [(m_ref.at[c], acc_ref.at[c]) for c in range(Q_CHAINS)]
    _flash_init(m_ref, acc_ref)

    def keep_going(first_done):
        prev = jnp.maximum(first_done // wide - 1, 0)
        cap = nf_ref[0, 0, pl.ds(prev, 1), :][:, wide - 1:wide] * (1.0 / ATTN_SCALE)
        return _softmax_alive(q_norms, knorm_ref[...], [cap] * Q_CHAINS, states)

    def full_tile(off):
        bias = nf_ref[0, 0, pl.ds(off // wide, 1), :] * (1.0 / ATTN_SCALE)
        k = k_ref[0, pl.ds(off, wide), :]
        raws = [_qk_raw(q, k) + bias for q in qs]
        _online_update_multi(raws, [vaug_ref[pl.ds(off, wide), :]] * Q_CHAINS, states)

    def tail_tile(off):
        bias = nf_ref[0, 0, pl.ds(off // wide, 1), :] * (1.0 / ATTN_SCALE)
        k = k_ref[0, pl.ds(off, wide), :]
        v = vaug_ref[pl.ds(off, wide), :]
        ends = [(c + 1) * Q_ROWS for c in range(Q_CHAINS)]
        raws = [jnp.where(_own_causal(c, e), _qk_raw(q, k[:e]) + bias[:, :e], NEG_BIG)
                for c, (q, e) in enumerate(zip(qs, ends))]
        _online_update_multi(raws, [v[:e] for e in ends], states)

    _sweep_tiles(t0, wide, tail_tile, full_tile, keep_going)
    for c in range(Q_CHAINS):
        o_ref[0, _chain_rows(c), :] = _flash_result(acc_ref.at[c])


def _attn_scratch(t, n_states):
    return [pltpu.VMEM((t, HEAD_DIM + LANES), BF16), pltpu.VMEM((n_states, Q_ROWS, 1), F32),
            pltpu.VMEM((n_states, Q_ROWS, HEAD_DIM + LANES), F32)]


def _fox_attention(p3, neg_cum_f):
    b, t, _ = p3.shape
    tq = Q_CHAINS * Q_ROWS
    wide = tq
    nf = neg_cum_f.reshape(b, GROUP_HEADS, t // wide, wide)
    return pl.pallas_call(
        functools.partial(_fox_kernel, wide=wide),
        grid=(b, GROUP_HEADS, t // tq),
        in_specs=[pl.BlockSpec((1, tq, HEAD_DIM), lambda bi, h, i: (bi, i, COL["fq"] + h)),
                  pl.BlockSpec((1, t, HEAD_DIM), lambda bi, h, i: (bi, 0, COL["fk"] + h)),
                  pl.BlockSpec((1, t, HEAD_DIM), lambda bi, h, i: (bi, 0, COL["fv"] + h)),
                  pl.BlockSpec((1, 1, t // wide, wide), lambda bi, h, i: (bi, h, 0, 0))],
        out_specs=pl.BlockSpec((1, tq, HEAD_DIM), lambda bi, h, i: (bi, i, h)),
        out_shape=jax.ShapeDtypeStruct((b, t, GROUP_WIDTH), F32),
        scratch_shapes=[pltpu.VMEM((1, 1), F32)] + _attn_scratch(t, Q_CHAINS),
        compiler_params=_cparams(("parallel", "parallel", "arbitrary")),
        name="fox_attention",
    )(p3, p3, p3, nf)


def _neg_abs(x):
    bits = lax.bitcast_convert_type(x, jnp.uint32) | jnp.uint32(0x80000000)
    return lax.bitcast_convert_type(bits, F32)


def _stick_kernel(q_ref, k_ref, v_ref, u_ref, o_ref, r_ref, acc_ref, *, wide):
    t0 = pl.program_id(2) * (Q_CHAINS * Q_ROWS)
    chains = range(Q_CHAINS)
    qs = [q_ref[0, _chain_rows(c), :] for c in chains]
    r_ref[...] = jnp.zeros(r_ref.shape, F32)
    acc_ref[...] = jnp.zeros(acc_ref.shape, F32)
    u = u_ref[...]

    def tile(off, tail, which):
        ends = {c: (c + 1) * Q_ROWS if tail else wide for c in which}
        k = k_ref[0, pl.ds(off, wide), :]
        z_all = {c: _qk_raw(qs[c], k[:ends[c]]) * SCALE_LOG2E for c in which}
        later = {c: r_ref[c] for c in which}
        weights = {c: [] for c in which}
        before = _own_causal(0, Q_ROWS, strict=True)
        for sb in reversed(range(wide // Q_ROWS)):
            live = [c for c in which if ends[c] > sb * Q_ROWS]
            diag = [tail and c == sb for c in live]
            cols = slice(sb * Q_ROWS, (sb + 1) * Q_ROWS)
            zs = [z_all[c][:, cols] for c in live]
            log_beta = [jnp.minimum(z, 0.0) - jnp.log(1.0 + jnp.exp2(_neg_abs(z))) * LOG2E for z in zs]
            log_keep = [lb - z for lb, z in zip(log_beta, zs)]
            log_keep = [jnp.where(before, lk, 0.0) if d else lk for d, lk in zip(diag, log_keep)]
            inside = [jnp.dot(jnp.concatenate(_split3(lk)[:2], axis=1), u, preferred_element_type=F32)
                      for lk in log_keep]
            a = [jnp.exp2(lb + cs + later[c]) for c, lb, cs in zip(live, log_beta, inside)]
            a = [jnp.where(before, x, 0.0) if d else x for d, x in zip(diag, a)]
            for c, x, cs, lk in zip(live, a, inside, log_keep):
                weights[c].append(x.astype(BF16))
                later[c] = later[c] + cs[:, 0:1] + lk[:, 0:1]
        v = v_ref[0, pl.ds(off, wide), :]
        for c in which:
            w = weights[c][0] if len(weights[c]) == 1 else jnp.concatenate(weights[c][::-1], axis=1)
            acc_ref[c] += jnp.dot(w, v[:ends[c]], preferred_element_type=F32)
            r_ref[c] = later[c]

    def full_tile(off):
        first_alive = jnp.max(r_ref[0]) > -STICK_DEAD_LOG2
        rest_alive = jnp.max(r_ref[1:]) > -STICK_DEAD_LOG2
        pl.when(rest_alive)(functools.partial(tile, off, False, list(chains)))
        pl.when(first_alive & jnp.logical_not(rest_alive))(functools.partial(tile, off, False, [0]))

    def keep_going(first_done):
        del first_done
        return jnp.max(r_ref[...]) > -STICK_DEAD_LOG2

    _sweep_tiles(t0, wide, functools.partial(tile, tail=True, which=list(chains)), full_tile, keep_going)
    for c in chains:
        o_ref[0, _chain_rows(c), :] = acc_ref[c]


def _stick_attention(p3):
    b, t, _ = p3.shape
    tq = Q_CHAINS * Q_ROWS
    wide = tq
    ki = np.arange(Q_ROWS)
    u = (ki[:, None] > ki[None, :]).astype(np.float32)
    u = jnp.asarray(np.concatenate([u, u], axis=0), BF16)
    return pl.pallas_call(
        functools.partial(_stick_kernel, wide=wide),
        grid=(b, GROUP_HEADS, t // tq),
        in_specs=[pl.BlockSpec((1, tq, HEAD_DIM), lambda bi, h, i: (bi, i, COL["sq"] + h)),
                  pl.BlockSpec((1, t, HEAD_DIM), lambda bi, h, i: (bi, 0, COL["sk"] + h)),
                  pl.BlockSpec((1, t, HEAD_DIM), lambda bi, h, i: (bi, 0, COL["sv"] + h)),
                  pl.BlockSpec((2 * Q_ROWS, Q_ROWS), lambda bi, h, i: (0, 0))],
        out_specs=pl.BlockSpec((1, tq, HEAD_DIM), lambda bi, h, i: (bi, i, h)),
        out_shape=jax.ShapeDtypeStruct((b, t, GROUP_WIDTH), F32),
        scratch_shapes=[pltpu.VMEM((Q_CHAINS, Q_ROWS, 1), F32), pltpu.VMEM((Q_CHAINS, Q_ROWS, HEAD_DIM), F32)],
        compiler_params=_cparams(("parallel", "parallel", "arbitrary")),
        name="stick_attention",
    )(p3, p3, p3, u)


def _moba_kernel(slope_ref, q_ref, k_ref, v_ref, o_ref, kaug_ref, kmean_ref, knorm_ref, vaug_ref, m_ref, acc_ref, *,
                 n_blk, wide):
    h = pl.program_id(1)
    i = pl.program_id(2)
    tq = MOBA_BLOCK
    slope = slope_ref[h]

    @pl.when(i == 0)
    def _():
        kmean_ref[...] = jnp.zeros(kmean_ref.shape, F32)
        kb = k_ref[0].astype(F32).reshape(n_blk, tq, HEAD_DIM)
        kmean_ref[0:n_blk, :] = jnp.mean(kb, axis=1)

        def key_aux(pos, lane):
            onehot = jnp.where(lane == jnp.right_shift(pos, int(math.log2(tq))), 1.0, 0.0)
            hi, mid, lo = _split3(slope * pos.astype(F32) * (1.0 / ATTN_SCALE))
            return jnp.where(lane == n_blk, hi.astype(F32),
                             jnp.where(lane == n_blk + 1, mid.astype(F32),
                                       jnp.where(lane == n_blk + 2, lo.astype(F32), onehot)))

        _fill_key_aug(kaug_ref, k_ref, key_aux)
        _max_key_norm(k_ref, knorm_ref)
        _fill_value_aug(vaug_ref, v_ref)

    chains = range(Q_CHAINS)
    t0 = i * (Q_CHAINS * Q_ROWS)
    qs = [q_ref[0, _chain_rows(c), :] for c in chains]
    q_norms = [_row_norms(q) for q in qs]
    own = [i * Q_CHAINS + c for c in chains]
    kmean = kmean_ref[...].astype(BF16)
    blk = lax.broadcasted_iota(jnp.int32, (Q_ROWS, LANES), 1)
    blkf = blk.astype(F32)
    scores = [jnp.where(blk < own[c], _qk_raw(qs[c], kmean), -jnp.inf) for c in chains]
    sels = _topk_rounds(scores, blkf, [jnp.where(blk == own[c], 1.0, 0.0) for c in chains], MOBA_TOPK)
    q_aux = [jnp.where(blk < n_blk, jnp.where(sel > 0.5, 0.0, NEG_BIG), jnp.where(blk < n_blk + 3, 1.0, 0.0))
             for sel in sels]
    q_aug = [jnp.concatenate([q, aux.astype(BF16)], axis=1) for q, aux in zip(qs, q_aux)]
    states = [(m_ref.at[c], acc_ref.at[c]) for c in chains]
    _flash_init(m_ref, acc_ref)

    def full_tile(off):
        k = kaug_ref[pl.ds(off, wide), :]
        _online_update_multi([_qk_raw(q, k) for q in q_aug], [vaug_ref[pl.ds(off, wide), :]] * Q_CHAINS, states)

    def tail_tile(off):
        k = kaug_ref[pl.ds(off, wide), :]
        v = vaug_ref[pl.ds(off, wide), :]
        ends = [(c + 1) * Q_ROWS for c in chains]
        raws = [jnp.where(_own_causal(c, e), _qk_raw(q, k[:e]), NEG_BIG) for c, (q, e) in enumerate(zip(q_aug, ends))]
        _online_update_multi(raws, [v[:e] for e in ends], states)

    def keep_going(first_done):
        cap = slope * first_done.astype(F32) * (1.0 / ATTN_SCALE)
        return _softmax_alive(q_norms, knorm_ref[...], [cap] * Q_CHAINS, states)

    _sweep_tiles(t0, wide, tail_tile, full_tile, keep_going)
    for c in chains:
        o_ref[0, _chain_rows(c), :] = _flash_result(acc_ref.at[c])


def _moba_attention(p3, slopes):
    b, t, _ = p3.shape
    tq = Q_CHAINS * Q_ROWS
    n_blk = t // MOBA_BLOCK
    wide = tq
    assert Q_ROWS == MOBA_BLOCK and t % tq == 0 and n_blk + 3 <= LANES
    return pl.pallas_call(
        functools.partial(_moba_kernel, n_blk=n_blk, wide=wide),
        grid=(b, GROUP_HEADS, t // tq),
        in_specs=[pl.BlockSpec(memory_space=pltpu.SMEM),
                  pl.BlockSpec((1, tq, HEAD_DIM), lambda bi, h, i: (bi, i, COL["mq"] + h)),
                  pl.BlockSpec((1, t, HEAD_DIM), lambda bi, h, i: (bi, 0, COL["mk"] + h)),
                  pl.BlockSpec((1, t, HEAD_DIM), lambda bi, h, i: (bi, 0, COL["mv"] + h))],
        out_specs=pl.BlockSpec((1, tq, HEAD_DIM), lambda bi, h, i: (bi, i, h)),
        out_shape=jax.ShapeDtypeStruct((b, t, GROUP_WIDTH), F32),
        scratch_shapes=[pltpu.VMEM((t, HEAD_DIM + LANES), BF16),
                        pltpu.VMEM((LANES, HEAD_DIM), F32), pltpu.VMEM((1, 1), F32)] + _attn_scratch(t, Q_CHAINS),
        compiler_params=_cparams(("parallel", "parallel", "arbitrary")),
        name="moba_attention",
    )(slopes, p3, p3, p3)


def _nsa_compress_kernel(x_ref, pos_ref, w1_ref, w2_ref, o_ref):
    x = (x_ref[0].astype(F32) + pos_ref[...]).astype(BF16)
    hmid = jnp.dot(x, w1_ref[...], preferred_element_type=F32)
    hmid = hmid / (1.0 + jnp.exp(-hmid))
    o_ref[0] = jnp.dot(hmid.astype(BF16), w2_ref[...], preferred_element_type=F32).astype(o_ref.dtype)


def _nsa_compress(blocks, pos, w1, w2):
    r, n, width = blocks.shape
    return pl.pallas_call(
        _nsa_compress_kernel,
        grid=(r,),
        in_specs=[pl.BlockSpec((1, n, width), lambda i: (i, 0, 0)),
                  pl.BlockSpec((1, width), lambda i: (0, 0)),
                  pl.BlockSpec((width, HEAD_DIM), lambda i: (0, 0)),
                  pl.BlockSpec((HEAD_DIM, HEAD_DIM), lambda i: (0, 0))],
        out_specs=pl.BlockSpec((1, n, HEAD_DIM), lambda i: (i, 0, 0)),
        out_shape=jax.ShapeDtypeStruct((r, n, HEAD_DIM), BF16),
        compiler_params=_cparams(("parallel",)),
        name="nsa_compress",
    )(blocks, pos.reshape(1, width).astype(F32), w1.astype(BF16), w2.astype(BF16))


def _nsa_select_kernel(slope_ref, q_ref, kc_ref, vc_ref, a_ref, oc_ref, sel_ref, *, tq, n_cmp, n_blk):
    g = pl.program_id(1)
    i = pl.program_id(2)
    t0 = i * tq
    kc = kc_ref[0, 0]
    vc = vc_ref[0, 0]
    ncp = kc.shape[0]
    tok = lax.broadcasted_iota(jnp.int32, (1, ncp), 1)
    cmp_end = tok * NSA_CMP_STRIDE + (NSA_CMP_LEN - 1)
    row = lax.broadcasted_iota(jnp.int32, (tq, 1), 0)
    admissible = (cmp_end <= row + t0) & (tok < n_cmp)
    rel_end = (cmp_end - t0).astype(F32)

    imp = jnp.zeros((tq, ncp), F32)
    for hh in range(NSA_Q_PER_KV):
        slope = slope_ref[g * NSA_Q_PER_KV + hh]
        q = q_ref[0, :, hh * HEAD_DIM:(hh + 1) * HEAD_DIM]
        s = jnp.where(admissible, _qk(q, kc) + slope * rel_end, NEG_BIG)
        m = jnp.max(s, axis=-1, keepdims=True)
        p = jnp.where(admissible, jnp.exp(s - m), 0.0)
        p = p / jnp.maximum(jnp.sum(p, axis=-1, keepdims=True), F32_TINY)
        oc_ref[0, :, hh * HEAD_DIM:(hh + 1) * HEAD_DIM] = jnp.dot(
            p.astype(BF16), vc, preferred_element_type=F32)
        imp = imp + p

    imp_blk = _dot_split(imp, a_ref[...], 3)
    blk = lax.broadcasted_iota(jnp.int32, (tq, LANES), 1)
    blkf = blk.astype(F32)
    jt = jnp.right_shift(row + t0, int(math.log2(NSA_SEL_BLOCK)))
    live = (blk <= jt) & (blk < n_blk)
    forced = ((blk == 0) | (blk == jt) | (blk == jt - 1)) & live
    sel = jnp.where(forced, 1.0, 0.0)
    score = jnp.where(live & jnp.logical_not(forced), imp_blk, -jnp.inf)
    sel = _topk_rounds([score], blkf, [sel], NSA_SEL_TOPK - 3)[0]
    sel_ref[0, 0] = sel.astype(sel_ref.dtype)


def _nsa_select(p3, kc, vc, slopes, tq):
    b, t, _ = p3.shape
    n_blk = t // NSA_SEL_BLOCK
    assert n_blk <= LANES and NSA_SEL_TOPK >= 3
    n_cmp = t // NSA_CMP_STRIDE - 1
    ncp = kc.shape[2]
    ratio = NSA_SEL_BLOCK // NSA_CMP_STRIDE
    ti = np.arange(ncp)[:, None]
    bj = np.arange(LANES)[None, :]
    amat = ((ti >= ratio * bj - 1) & (ti <= ratio * bj + ratio - 1) & (ti < n_cmp) & (bj < n_blk))
    amat = jnp.asarray(amat.astype(np.float32), BF16)
    width = NSA_Q_PER_KV * HEAD_DIM
    qblk = COL["nq"] * HEAD_DIM // width
    return pl.pallas_call(
        functools.partial(_nsa_select_kernel, tq=tq, n_cmp=n_cmp, n_blk=n_blk),
        grid=(b, NSA_KV_HEADS, t // tq),
        in_specs=[pl.BlockSpec(memory_space=pltpu.SMEM),
                  pl.BlockSpec((1, tq, width), lambda bi, g, i: (bi, i, qblk + g)),
                  pl.BlockSpec((1, 1, ncp, HEAD_DIM), lambda bi, g, i: (bi, g, 0, 0)),
                  pl.BlockSpec((1, 1, ncp, HEAD_DIM), lambda bi, g, i: (bi, g, 0, 0)),
                  pl.BlockSpec((ncp, LANES), lambda bi, g, i: (0, 0))],
        out_specs=[pl.BlockSpec((1, tq, width), lambda bi, g, i: (bi, i, g)),
                   pl.BlockSpec((1, 1, tq, LANES), lambda bi, g, i: (bi, g, i, 0))],
        out_shape=[jax.ShapeDtypeStruct((b, t, GROUP_WIDTH), F32),
                   jax.ShapeDtypeStruct((b, NSA_KV_HEADS, t, LANES), BF16)],
        compiler_params=_cparams(("parallel", "parallel", "parallel")),
        name="nsa_select",
    )(slopes, p3, kc, vc, amat)


def _nsa_attn_kernel(slope_ref, q_ref, ks_ref, vs_ref, kw_ref, vw_ref, sel_ref, oc_ref, gate_ref,
                     o_ref, kaug_ref, knorm_ref, vwaug_ref, vsaug_ref, m_ref, acc_ref, *, tq, wide):
    g = pl.program_id(1)
    i = pl.program_id(2)
    t0 = pl.multiple_of(i * tq, tq)
    nh = NSA_Q_PER_KV

    @pl.when(i == 0)
    def _():
        sel_shift = int(math.log2(NSA_SEL_BLOCK))
        _fill_key_aug(kaug_ref, ks_ref,
                      lambda pos, lane: jnp.where(lane == jnp.right_shift(pos, sel_shift), 1.0, 0.0))
        _max_key_norm(ks_ref, knorm_ref)
        _fill_value_aug(vsaug_ref, vs_ref)
        _fill_value_aug(vwaug_ref, vw_ref)

    _flash_init(m_ref, acc_ref)
    sel_bias = jnp.where(sel_ref[0, 0].astype(F32) > 0.5, 0.0, NEG_BIG).astype(BF16)
    qs = [q_ref[0, :, hh * HEAD_DIM:(hh + 1) * HEAD_DIM] for hh in range(nh)]
    q_aug = [jnp.concatenate([q, sel_bias], axis=1) for q in qs]
    slope_raw = [slope_ref[g * nh + hh] * (1.0 / ATTN_SCALE) for hh in range(nh)]
    row = lax.broadcasted_iota(jnp.int32, (tq, 1), 0)

    def rel_pos(off, width):
        return lax.broadcasted_iota(jnp.int32, (1, width), 1) + (off - t0)

    sel_states = [(m_ref.at[hh], acc_ref.at[hh]) for hh in range(nh)]
    win_states = [(m_ref.at[nh + hh], acc_ref.at[nh + hh]) for hh in range(nh)]
    q_norms = [_row_norms(q) for q in qs]

    def sel_scores(off, tail):
        rel = rel_pos(off, wide)
        relf = rel.astype(F32)
        k = kaug_ref[pl.ds(off, wide), :]
        raws = [_qk_raw(q_aug[hh], k) + slope_raw[hh] * relf for hh in range(nh)]
        if tail:
            raws = [jnp.where(rel <= row, raw, NEG_BIG) for raw in raws]
        return raws, [vsaug_ref[pl.ds(off, wide), :]] * nh

    def window_scores(off, width, far):
        rel = rel_pos(off, width)
        relf = rel.astype(F32)
        k = kw_ref[0, pl.ds(off, width), :]
        ok = rel <= row
        if far:
            ok = ok & (row - rel < NSA_WINDOW)
        raws = [jnp.where(ok, _qk_raw(qs[hh], k) + slope_raw[hh] * relf, NEG_BIG) for hh in range(nh)]
        return raws, [vwaug_ref[pl.ds(off, width), :]] * nh

    def sel_tile(off, tail):
        raws, vs = sel_scores(off, tail)
        _online_update_multi(raws, vs, sel_states)

    def keep_going(first_done):
        gap = (first_done - t0).astype(F32)
        return _softmax_alive(q_norms, knorm_ref[...], [s * gap for s in slope_raw], sel_states)

    _sweep_tiles(t0, wide, functools.partial(sel_tile, tail=True), functools.partial(sel_tile, tail=False),
                 keep_going)

    n_back = NSA_WINDOW // tq
    for back in range(n_back + 1):
        @pl.when((i == back) if back < n_back else (i >= back))
        def _(back=back):
            raws, vs = window_scores(pl.multiple_of(t0 - back * tq, tq), (back + 1) * tq, back == n_back)
            _online_update_multi(raws, vs, win_states)

    gates = gate_ref[0, 0]
    gates = 1.0 / (1.0 + jnp.exp(-gates))
    for hh in range(nh):
        c0 = hh * NSA_N_BRANCHES
        o_c = oc_ref[0, :, hh * HEAD_DIM:(hh + 1) * HEAD_DIM]
        o_s = _flash_result(acc_ref.at[hh])
        o_w = _flash_result(acc_ref.at[nh + hh])
        o_ref[0, :, hh * HEAD_DIM:(hh + 1) * HEAD_DIM] = (
            gates[:, c0:c0 + 1] * o_c + gates[:, c0 + 1:c0 + 2] * o_s + gates[:, c0 + 2:c0 + 3] * o_w)


def _nsa_attention(p3, sel, o_cmp, gate_logits, slopes, tq):
    b, t, _ = p3.shape
    assert tq == KV_UNIT and NSA_WINDOW == 2 * tq and t // NSA_SEL_BLOCK <= LANES
    width = NSA_Q_PER_KV * HEAD_DIM
    qblk = COL["nq"] * HEAD_DIM // width
    kv = lambda name: pl.BlockSpec((1, t, HEAD_DIM), lambda bi, g, i: (bi, 0, COL[name] + g))
    nst = 2 * NSA_Q_PER_KV
    return pl.pallas_call(
        functools.partial(_nsa_attn_kernel, tq=tq, wide=min(KV_WIDE, t)),
        grid=(b, NSA_KV_HEADS, t // tq),
        in_specs=[pl.BlockSpec(memory_space=pltpu.SMEM),
                  pl.BlockSpec((1, tq, width), lambda bi, g, i: (bi, i, qblk + g)),
                  kv("nks"), kv("nvs"), kv("nkw"), kv("nvw"),
                  pl.BlockSpec((1, 1, tq, LANES), lambda bi, g, i: (bi, g, i, 0)),
                  pl.BlockSpec((1, tq, width), lambda bi, g, i: (bi, i, g)),
                  pl.BlockSpec((1, 1, tq, LANES), lambda bi, g, i: (bi, g, i, 0))],
        out_specs=pl.BlockSpec((1, tq, width), lambda bi, g, i: (bi, i, g)),
        out_shape=jax.ShapeDtypeStruct((b, t, GROUP_WIDTH), F32),
        scratch_shapes=[pltpu.VMEM((t, HEAD_DIM + LANES), BF16), pltpu.VMEM((1, 1), F32),
                        pltpu.VMEM((t, HEAD_DIM + LANES), BF16)] + _attn_scratch(t, nst),
        compiler_params=_cparams(("parallel", "parallel", "arbitrary")),
        name="nsa_attention",
    )(slopes, p3, p3, p3, p3, p3, sel, o_cmp, gate_logits)


def _nsa_blocks(p3, name):
    b, t, _ = p3.shape
    c0 = COL[name] * HEAD_DIM
    x = p3[:, :, c0:c0 + NSA_KV_HEADS * HEAD_DIM].reshape(b, t // NSA_CMP_STRIDE, NSA_CMP_STRIDE,
                                                         NSA_KV_HEADS, HEAD_DIM)
    x = x.transpose(0, 3, 1, 2, 4).reshape(b * NSA_KV_HEADS, t // NSA_CMP_STRIDE, NSA_CMP_STRIDE * HEAD_DIM)
    blocks = jnp.concatenate([x[:, :-1], x[:, 1:]], axis=-1)
    return jnp.pad(blocks, ((0, 0), (0, 1), (0, 0)))


def _alibi_slopes():
    n = 2 * GROUP_HEADS
    slopes = 2.0 ** (-8.0 * np.arange(1, n + 1) / n)
    return jnp.asarray(slopes[0::2], F32), jnp.asarray(slopes[1::2], F32)


def _regroup_w_in(w_in):
    main = jnp.concatenate([w_in[:, REF_OFF[n][0]:REF_OFF[n][0] + REF_OFF[n][1]] for n, _ in _MAIN_PIECES],
                           axis=1).astype(BF16)
    ng0, ngw = REF_OFF["ng"]
    ff0, ffw = REF_OFF["ff"]
    small = jnp.concatenate([w_in[:, ng0:ng0 + ngw], w_in[:, ff0:ff0 + ffw]], axis=1)
    small = jnp.pad(small, ((0, 0), (0, LANES - ngw - ffw))).astype(BF16)
    return main, small


def _mixer(x2d, normed, b, t, w_in, forget_bias, pos_k, w1_k, w2_k, pos_v, w1_v, w2_v, group_gain, w_out, next_gain):
    m = x2d.shape[0]
    w_main, w_small = _regroup_w_in(w_in)
    tm = _pick_tile(m, (1024, 512, 256))
    p3 = _matmul_normed(normed, w_main, BF16, tm, 512).reshape(b, t, MAIN_BLOCKS * HEAD_DIM)
    small = _matmul_normed(normed, w_small, F32, tm, LANES).reshape(b, t, LANES)
    slopes_moba, slopes_nsa = _alibi_slopes()

    o_a = _moba_attention(p3, slopes_moba)

    n_gate = NSA_N_BRANCHES * GROUP_HEADS
    kc = _nsa_compress(_nsa_blocks(p3, "nkc"), pos_k, w1_k, w2_k)
    vc = _nsa_compress(_nsa_blocks(p3, "nvc"), pos_v, w1_v, w2_v)
    ncp = kc.shape[1]
    kc = kc.reshape(b, NSA_KV_HEADS, ncp, HEAD_DIM)
    vc = vc.reshape(b, NSA_KV_HEADS, ncp, HEAD_DIM)
    o_cmp, sel = _nsa_select(p3, kc, vc, slopes_nsa, _pick_tile(t, (512, 256)))
    per_group = n_gate // NSA_KV_HEADS
    gate_logits = small[:, :, :n_gate].reshape(b, t, NSA_KV_HEADS, per_group).transpose(0, 2, 1, 3)
    gate_logits = jnp.pad(gate_logits, ((0, 0), (0, 0), (0, 0), (0, LANES - per_group)))
    o_b = _nsa_attention(p3, sel, o_cmp, gate_logits, slopes_nsa, KV_UNIT)

    ff = small[:, :, n_gate:n_gate + GROUP_HEADS] + forget_bias
    neg_cum_f = _neg_cum_logf(ff.transpose(0, 2, 1).reshape(b * GROUP_HEADS, t))
    o_c = _fox_attention(p3, neg_cum_f)

    o_d = _stick_attention(p3)

    on = _groupnorm([o.reshape(m, GROUP_WIDTH) for o in (o_a, o_b, o_c, o_d)], group_gain)
    return _matmul_residual(on, w_out, x2d, next_gain, tm, 512)


def _cast_pad_kernel(w_ref, o_ref, *, n_row_tiles, cols):
    @pl.when(pl.program_id(0) < n_row_tiles)
    def _():
        o_ref[:, 0:cols] = w_ref[0].astype(BF16)
        if o_ref.shape[1] > cols:
            o_ref[:, cols:] = jnp.zeros((o_ref.shape[0], o_ref.shape[1] - cols), BF16)

    @pl.when(pl.program_id(0) >= n_row_tiles)
    def _():
        o_ref[...] = jnp.zeros(o_ref.shape, BF16)


def _weight_bf16(stacked, layer, rows_out, cols_out, tr=256):
    _, rows, cols = stacked.shape
    assert rows % tr == 0 and rows_out % tr == 0 and cols % LANES == 0
    n_row_tiles = rows // tr
    return pl.pallas_call(
        functools.partial(_cast_pad_kernel, n_row_tiles=n_row_tiles, cols=cols),
        grid=(rows_out // tr,),
        in_specs=[pl.BlockSpec((1, tr, cols), lambda j: (layer, jnp.minimum(j, n_row_tiles - 1), 0))],
        out_specs=pl.BlockSpec((tr, cols_out), lambda j: (j, 0)),
        out_shape=jax.ShapeDtypeStruct((rows_out, cols_out), BF16),
        compiler_params=_cparams(("parallel",)),
        name="weight_bf16",
    )(stacked)


def kernel(x, ffn1_norm, ffn1_w_gate, ffn1_w_up, ffn1_w_down, mix_norm, w_in, fox_forget_bias, nsa_cmp_pos_k, nsa_cmp_w1_k, nsa_cmp_w2_k, nsa_cmp_pos_v, nsa_cmp_w1_v, nsa_cmp_w2_v, group_norm, w_out, ffn2_norm, ffn2_w_gate, ffn2_w_up, ffn2_w_down, final_norm):
    b, t, d = x.shape
    x2d = x.reshape(b * t, d)
    depth = ffn1_norm.shape[0]
    normed = _prenorm(x2d, ffn1_norm[0])
    for i in range(depth):
        x2d, normed = _ffn(x2d, normed, ffn1_w_gate, ffn1_w_up, ffn1_w_down, i, mix_norm[i])
        x2d, normed = _mixer(x2d, normed, b, t, w_in[i], fox_forget_bias[i], nsa_cmp_pos_k[i], nsa_cmp_w1_k[i],
                             nsa_cmp_w2_k[i], nsa_cmp_pos_v[i], nsa_cmp_w1_v[i], nsa_cmp_w2_v[i],
                             group_norm[i], _weight_bf16(w_out, i, d, d), ffn2_norm[i])
        x2d, normed = _ffn(x2d, normed, ffn2_w_gate, ffn2_w_up, ffn2_w_down, i,
                           ffn1_norm[i + 1] if i + 1 < depth else None)
    return _rmsnorm(x2d, final_norm, F32).reshape(b, t, d)
```

```python
import functools
import math

import numpy as np
import jax
import jax.numpy as jnp
from jax import lax
from jax.experimental import pallas as pl
from jax.experimental.pallas import tpu as pltpu

F32 = jnp.float32
BF16 = jnp.bfloat16

HEAD_DIM = 128
GROUP_HEADS = 8
GROUP_WIDTH = GROUP_HEADS * HEAD_DIM
N_MIXERS = 4
FFN_RESIDUAL = 0.5
RMS_EPS = 1e-6

MOBA_BLOCK = 256
MOBA_TOPK = 3

NSA_KV_HEADS = 2
NSA_Q_PER_KV = 4
NSA_CMP_STRIDE = 16
NSA_CMP_LEN = 32
NSA_SEL_BLOCK = 64
NSA_SEL_TOPK = 16
NSA_WINDOW = 512
NSA_N_BRANCHES = 3

ATTN_SCALE = HEAD_DIM ** -0.5
LOG2E = math.log2(math.e)
SCALE_LOG2E = ATTN_SCALE * LOG2E
NEG_BIG = -1e30
STICK_DEAD_LOG2 = 160.0
FLASH_DEAD_LOG2 = 170.0
BOUND_SLACK = 1.001
KV_UNIT = 256
KV_WIDE = 1024
Q_ROWS = 256
Q_CHAINS = 4
F32_TINY = float(np.finfo(np.float32).tiny)

LANES = 128
VMEM_LIMIT = 56 * 1024 * 1024
GATEUP_TILE = 256
FFN_PAD = 512

_MAIN_PIECES = (
    ("mq", 8), ("mk", 8), ("mv", 8), ("nq", 8),
    ("nkc", 2), ("nvc", 2), ("nks", 2), ("nvs", 2), ("nkw", 2), ("nvw", 2),
    ("fq", 8), ("fk", 8), ("fv", 8), ("sq", 8), ("sk", 8), ("sv", 8),
)
COL = {}
_o = 0
for _n, _w in _MAIN_PIECES:
    COL[_n] = _o
    _o += _w
MAIN_BLOCKS = _o

_REF_SIZES = (
    ("mq", 1024), ("mk", 1024), ("mv", 1024), ("nq", 1024),
    ("nkc", 256), ("nvc", 256), ("nks", 256), ("nvs", 256), ("nkw", 256), ("nvw", 256),
    ("ng", 24),
    ("fq", 1024), ("fk", 1024), ("fv", 1024), ("ff", 8),
    ("sq", 1024), ("sk", 1024), ("sv", 1024),
)
REF_OFF = {}
_o = 0
for _n, _w in _REF_SIZES:
    REF_OFF[_n] = (_o, _w)
    _o += _w


def _cparams(sem):
    return pltpu.CompilerParams(dimension_semantics=sem, vmem_limit_bytes=VMEM_LIMIT)


def _rms_kernel(x_ref, g_ref, o_ref):
    x = x_ref[...]
    ms = jnp.mean(x * x, axis=-1, keepdims=True)
    o_ref[...] = (x * lax.rsqrt(ms + RMS_EPS) * g_ref[...]).astype(o_ref.dtype)


def _rmsnorm(x2d, gain, out_dtype, tm=256):
    m, d = x2d.shape
    return pl.pallas_call(
        _rms_kernel,
        grid=(m // tm,),
        in_specs=[pl.BlockSpec((tm, d), lambda i: (i, 0)),
                  pl.BlockSpec((1, d), lambda i: (0, 0))],
        out_specs=pl.BlockSpec((tm, d), lambda i: (i, 0)),
        out_shape=jax.ShapeDtypeStruct((m, d), out_dtype),
        compiler_params=_cparams(("parallel",)),
        name="rmsnorm",
    )(x2d, gain.reshape(1, d).astype(F32))


def _emit_normed(x_new, gain_ref, xg_ref, ssq_ref):
    xg_ref[...] = (x_new * gain_ref[...]).astype(BF16)
    ssq_ref[...] = jnp.broadcast_to(jnp.sum(x_new * x_new, axis=-1, keepdims=True), ssq_ref.shape)


def _rstd(ssq_ref, d):
    ssq = ssq_ref[...]
    total = ssq[:, 0:1]
    for p in range(1, ssq.shape[1] // LANES):
        total = total + ssq[:, p * LANES:p * LANES + 1]
    return lax.rsqrt(total * (1.0 / d) + RMS_EPS)


def _prenorm_kernel(x_ref, g_ref, xg_ref, ssq_ref):
    _emit_normed(x_ref[...], g_ref, xg_ref, ssq_ref)


def _prenorm(x2d, gain, tm=256):
    m, d = x2d.shape
    return pl.pallas_call(
        _prenorm_kernel,
        grid=(m // tm,),
        in_specs=[pl.BlockSpec((tm, d), lambda i: (i, 0)),
                  pl.BlockSpec((1, d), lambda i: (0, 0))],
        out_specs=[pl.BlockSpec((tm, d), lambda i: (i, 0)), pl.BlockSpec((tm, LANES), lambda i: (i, 0))],
        out_shape=[jax.ShapeDtypeStruct((m, d), BF16), jax.ShapeDtypeStruct((m, LANES), F32)],
        compiler_params=_cparams(("parallel",)),
        name="prenorm",
    )(x2d, gain.reshape(1, d).astype(F32))


def _groupnorm_kernel(a_ref, b_ref, c_ref, d_ref, g_ref, o_ref):
    for gi, ref in enumerate((a_ref, b_ref, c_ref, d_ref)):
        x = ref[...]
        ms = jnp.mean(x * x, axis=-1, keepdims=True)
        lo, hi = gi * GROUP_WIDTH, (gi + 1) * GROUP_WIDTH
        o_ref[:, lo:hi] = (x * lax.rsqrt(ms + RMS_EPS) * g_ref[:, lo:hi]).astype(o_ref.dtype)


def _groupnorm(parts, gain, tm=256):
    m = parts[0].shape[0]
    d = GROUP_WIDTH * N_MIXERS
    part_spec = pl.BlockSpec((tm, GROUP_WIDTH), lambda i: (i, 0))
    return pl.pallas_call(
        _groupnorm_kernel,
        grid=(m // tm,),
        in_specs=[part_spec] * 4 + [pl.BlockSpec((1, d), lambda i: (0, 0))],
        out_specs=pl.BlockSpec((tm, d), lambda i: (i, 0)),
        out_shape=jax.ShapeDtypeStruct((m, d), BF16),
        compiler_params=_cparams(("parallel",)),
        name="groupnorm",
    )(*parts, gain.reshape(1, d).astype(F32))


def _mm_normed_kernel(a_ref, ssq_ref, w_ref, o_ref):
    acc = jnp.dot(a_ref[...], w_ref[...], preferred_element_type=F32)
    o_ref[...] = (acc * _rstd(ssq_ref, a_ref.shape[1])).astype(o_ref.dtype)


def _matmul_normed(normed, w, out_dtype, tm, tn):
    a, ssq = normed
    m, k = a.shape
    n = w.shape[1]
    return pl.pallas_call(
        _mm_normed_kernel,
        grid=(m // tm, n // tn),
        in_specs=[pl.BlockSpec((tm, k), lambda i, j: (i, 0)),
                  pl.BlockSpec((tm, ssq.shape[1]), lambda i, j: (i, 0)),
                  pl.BlockSpec((k, tn), lambda i, j: (0, j))],
        out_specs=pl.BlockSpec((tm, tn), lambda i, j: (i, j)),
        out_shape=jax.ShapeDtypeStruct((m, n), out_dtype),
        compiler_params=_cparams(("parallel", "arbitrary")),
        name="matmul",
    )(a, ssq, w)


def _normed_outputs(m, n, tm, tn, index):
    specs = [pl.BlockSpec((tm, tn), index), pl.BlockSpec((tm, LANES), index)]
    shapes = [jax.ShapeDtypeStruct((m, n), BF16), jax.ShapeDtypeStruct((m, LANES * (n // tn)), F32)]
    return specs, shapes


def _mm_res_kernel(a_ref, w_ref, x_ref, g_ref, o_ref, xg_ref, ssq_ref):
    x_new = x_ref[...] + jnp.dot(a_ref[...], w_ref[...], preferred_element_type=F32)
    o_ref[...] = x_new
    _emit_normed(x_new, g_ref, xg_ref, ssq_ref)


def _matmul_residual(a, w, x, next_gain, tm, tn):
    m, k = a.shape
    n = w.shape[1]
    index = lambda i, j: (i, j)
    n_specs, n_shapes = _normed_outputs(m, n, tm, tn, index)
    x_new, xg, ssq = pl.pallas_call(
        _mm_res_kernel,
        grid=(m // tm, n // tn),
        in_specs=[pl.BlockSpec((tm, k), lambda i, j: (i, 0)),
                  pl.BlockSpec((k, tn), lambda i, j: (0, j)),
                  pl.BlockSpec((tm, tn), index),
                  pl.BlockSpec((1, tn), lambda i, j: (0, j))],
        out_specs=[pl.BlockSpec((tm, tn), index)] + n_specs,
        out_shape=[jax.ShapeDtypeStruct((m, n), F32)] + n_shapes,
        compiler_params=_cparams(("parallel", "arbitrary")),
        name="matmul_residual",
    )(a, w, x, next_gain.reshape(1, n).astype(F32))
    return x_new, (xg, ssq)


def _gateup_kernel(h_ref, ssq_ref, wg_ref, wu_ref, o_ref, rstd_ref, rstd_res_ref, *, n_valid):
    @pl.when(pl.program_id(1) == 0)
    def _():
        rstd = _rstd(ssq_ref, h_ref.shape[1])
        rstd_ref[...] = rstd
        rstd_res_ref[...] = rstd * FFN_RESIDUAL

    @pl.when(pl.program_id(1) < n_valid)
    def _():
        h = h_ref[...]
        g = jnp.dot(h, wg_ref[0].astype(BF16), preferred_element_type=F32) * rstd_ref[...]
        u = jnp.dot(h, wu_ref[0].astype(BF16), preferred_element_type=F32)
        o_ref[...] = (g / (1.0 + jnp.exp(-g)) * u * rstd_res_ref[...]).astype(o_ref.dtype)

    @pl.when(pl.program_id(1) >= n_valid)
    def _():
        o_ref[...] = jnp.zeros(o_ref.shape, o_ref.dtype)


def _gateup(normed, wg, wu, layer, f_out, tm, tf):
    h, ssq = normed
    m, d = h.shape
    f = wg.shape[2]
    assert f % tf == 0 and f_out % tf == 0
    n_valid = f // tf
    w_spec = pl.BlockSpec((1, d, tf), lambda i, j: (layer, 0, jnp.minimum(j, n_valid - 1)))
    once = dict(pipeline_mode=pl.Buffered(1))
    return pl.pallas_call(
        functools.partial(_gateup_kernel, n_valid=n_valid),
        grid=(m // tm, f_out // tf),
        in_specs=[pl.BlockSpec((tm, d), lambda i, j: (i, 0), **once),
                  pl.BlockSpec((tm, ssq.shape[1]), lambda i, j: (i, 0), **once), w_spec, w_spec],
        out_specs=pl.BlockSpec((tm, tf), lambda i, j: (i, j)),
        out_shape=jax.ShapeDtypeStruct((m, f_out), BF16),
        scratch_shapes=[pltpu.VMEM((tm, 1), F32), pltpu.VMEM((tm, 1), F32)],
        compiler_params=_cparams(("parallel", "arbitrary")),
        name="ffn_gateup",
    )(h, ssq, wg, wu)


def _down_kernel(a_ref, w_ref, x_ref, *rest, emit):
    o_ref = rest[1] if emit else rest[0]
    @pl.when(pl.program_id(2) == 0)
    def _():
        o_ref[...] = x_ref[...] + jnp.dot(a_ref[...], w_ref[...], preferred_element_type=F32)

    @pl.when(pl.program_id(2) != 0)
    def _():
        o_ref[...] = o_ref[...] + jnp.dot(a_ref[...], w_ref[...], preferred_element_type=F32)

    if emit:
        g_ref, _, xg_ref, ssq_ref = rest

        @pl.when(pl.program_id(2) == pl.num_programs(2) - 1)
        def _():
            _emit_normed(o_ref[...], g_ref, xg_ref, ssq_ref)


def _down(act, wd, x, next_gain, tm, tn, tk):
    m, f = act.shape
    n = wd.shape[1]
    emit = next_gain is not None
    index = lambda i, j, k: (i, j)
    in_specs = [pl.BlockSpec((tm, tk), lambda i, j, k: (i, k)),
                pl.BlockSpec((tk, tn), lambda i, j, k: (k, j)),
                pl.BlockSpec((tm, tn), index)]
    out_specs, out_shape, args = [pl.BlockSpec((tm, tn), index)], [jax.ShapeDtypeStruct((m, n), F32)], [act, wd, x]
    if emit:
        n_specs, n_shapes = _normed_outputs(m, n, tm, tn, index)
        in_specs.append(pl.BlockSpec((1, tn), lambda i, j, k: (0, j)))
        out_specs, out_shape = out_specs + n_specs, out_shape + n_shapes
        args.append(next_gain.reshape(1, n).astype(F32))
    outs = pl.pallas_call(
        functools.partial(_down_kernel, emit=emit),
        grid=(m // tm, n // tn, f // tk),
        in_specs=in_specs,
        out_specs=out_specs,
        out_shape=out_shape,
        compiler_params=_cparams(("parallel", "parallel", "arbitrary")),
        name="ffn_down",
    )(*args)
    return (outs[0], (outs[1], outs[2])) if emit else (outs[0], None)


def _pick_tile(n, prefs):
    for t in prefs:
        if n % t == 0:
            return t
    return n


def _ffn(x2d, normed, wg, wu, wd, layer, next_gain):
    m, d = x2d.shape
    f = wg.shape[2]
    fp = -(-f // FFN_PAD) * FFN_PAD
    tm = _pick_tile(m, (1024, 512, 256))
    act = _gateup(normed, wg, wu, layer, fp, _pick_tile(m, (2048, 1024, 512, 256)),
                  _pick_tile(f, (GATEUP_TILE, 128)))
    return _down(act, _weight_bf16(wd, layer, fp, d), x2d, next_gain, tm, _pick_tile(d, (1024, 512, 256, 128)),
                 _pick_tile(fp, (2816, 1024, 512, 256, 128)))


def _qk_raw(q, k):
    return lax.dot_general(q, k, (((1,), (1,)), ((), ())), preferred_element_type=F32)


def _qk(q, k):
    return _qk_raw(q, k) * ATTN_SCALE


def _online_update_multi(raws, vs, states):
    m_prev = [m_ref[...] for m_ref, _ in states]
    m_new = [jnp.maximum(mp, jnp.max(raw, axis=-1, keepdims=True)) for mp, raw in zip(m_prev, raws)]
    ps = [jnp.exp2((raw - mn) * SCALE_LOG2E).astype(BF16) for raw, mn in zip(raws, m_new)]
    alphas = [jnp.exp2((mp - mn) * SCALE_LOG2E) for mp, mn in zip(m_prev, m_new)]
    pvs = [jnp.dot(p, v, preferred_element_type=F32) for p, v in zip(ps, vs)]
    for (m_ref, acc_ref), mn, a, pv in zip(states, m_new, alphas, pvs):
        acc_ref[...] = a * acc_ref[...] + pv
        m_ref[...] = mn


def _flash_init(m_ref, acc_ref):
    m_ref[...] = jnp.full(m_ref.shape, NEG_BIG, F32)
    acc_ref[...] = jnp.zeros(acc_ref.shape, F32)


def _flash_result(acc_ref):
    acc = acc_ref[...]
    return acc[:, :HEAD_DIM] / acc[:, HEAD_DIM:HEAD_DIM + 1]


def _fill_value_aug(vaug_ref, v_ref):
    def body(c, carry):
        off = pl.multiple_of(c * KV_UNIT, KV_UNIT)
        lane = lax.broadcasted_iota(jnp.int32, (KV_UNIT, LANES), 1)
        vaug_ref[pl.ds(off, KV_UNIT), 0:HEAD_DIM] = v_ref[0, pl.ds(off, KV_UNIT), :]
        vaug_ref[pl.ds(off, KV_UNIT), HEAD_DIM:] = jnp.where(lane == 0, 1.0, 0.0).astype(BF16)
        return carry

    lax.fori_loop(0, vaug_ref.shape[0] // KV_UNIT, body, 0)


def _sweep_tiles(t0, wide, tail_tile, full_tile, keep_going):
    n_full = t0 // wide
    tail_off = pl.multiple_of(n_full * wide, wide)

    def full_off(jj):
        return pl.multiple_of((n_full - 1 - jj) * wide, wide)

    def step(state):
        jj, _ = state
        full_tile(full_off(jj))
        return jj + 1, keep_going(full_off(jj))

    tail_tile(tail_off)
    lax.while_loop(lambda s: (s[0] < n_full) & s[1], step, (jnp.int32(0), keep_going(tail_off)))


def _row_norms(q):
    q = q.astype(F32)
    return jnp.sqrt(jnp.sum(q * q, axis=-1, keepdims=True))


def _max_key_norm(k_ref, out_ref):
    def body(c, mx):
        k = k_ref[0, pl.ds(pl.multiple_of(c * KV_UNIT, KV_UNIT), KV_UNIT), :].astype(F32)
        return jnp.maximum(mx, jnp.max(jnp.sum(k * k, axis=-1, keepdims=True), axis=0, keepdims=True))

    out_ref[...] = jnp.sqrt(lax.fori_loop(0, k_ref.shape[1] // KV_UNIT, body, jnp.zeros((1, 1), F32)))


def _softmax_alive(q_norms, k_norm, bias_caps, states):
    gaps = [qn * (k_norm * BOUND_SLACK) + cap - m_ref[...]
            for qn, cap, (m_ref, _) in zip(q_norms, bias_caps, states)]
    worst = jnp.max(functools.reduce(jnp.maximum, gaps))
    return (worst + 1.0) * SCALE_LOG2E > -FLASH_DEAD_LOG2


def _fill_key_aug(kaug_ref, k_ref, aux_fn):
    t = kaug_ref.shape[0]

    def body(c, carry):
        off = pl.multiple_of(c * KV_UNIT, KV_UNIT)
        pos = lax.broadcasted_iota(jnp.int32, (KV_UNIT, LANES), 0) + off
        lane = lax.broadcasted_iota(jnp.int32, (KV_UNIT, LANES), 1)
        kaug_ref[pl.ds(off, KV_UNIT), 0:HEAD_DIM] = k_ref[0, pl.ds(off, KV_UNIT), :]
        kaug_ref[pl.ds(off, KV_UNIT), HEAD_DIM:] = aux_fn(pos, lane).astype(BF16)
        return carry

    lax.fori_loop(0, t // KV_UNIT, body, 0)


def _topk_rounds(scores, blkf, sels, rounds, axis=-1):
    for _ in range(rounds):
        mx = [jnp.max(s, axis=axis, keepdims=True) for s in scores]
        idx = [jnp.min(jnp.where(s == m, blkf, 1e9), axis=axis, keepdims=True) for s, m in zip(scores, mx)]
        picks = [(blkf == i) & (m > -jnp.inf) for i, m in zip(idx, mx)]
        sels = [jnp.where(p, 1.0, sel) for p, sel in zip(picks, sels)]
        scores = [jnp.where(p, -jnp.inf, s) for p, s in zip(picks, scores)]
    return sels


def _split3(x):
    hi = x.astype(BF16)
    r1 = x - hi.astype(F32)
    mid = r1.astype(BF16)
    lo = (r1 - mid.astype(F32)).astype(BF16)
    return hi, mid, lo


def _dot_split(x, w_bf16, terms):
    parts = _split3(x)[:terms]
    out = jnp.dot(parts[0], w_bf16, preferred_element_type=F32)
    for p in parts[1:]:
        out = out + jnp.dot(p, w_bf16, preferred_element_type=F32)
    return out


def _logf_cumsum_kernel(x_ref, tri_ref, low_ref, o_ref):
    x = x_ref[0]
    logf = jnp.minimum(x, 0.0) - jnp.log1p(jnp.exp(-jnp.abs(x)))
    within = _dot_split(logf, tri_ref[...], 3)
    hi, mid, lo = _split3(logf)
    low = low_ref[...]
    before = (jnp.dot(low, hi, preferred_element_type=F32) + jnp.dot(low, mid, preferred_element_type=F32)
              + jnp.dot(low, lo, preferred_element_type=F32))
    o_ref[0] = -(within + jnp.sum(before, axis=-1, keepdims=True))


def _neg_cum_logf(logits_rows):
    r, t = logits_rows.shape
    nc = t // LANES
    li = np.arange(LANES)
    tri = jnp.asarray((li[:, None] <= li[None, :]).astype(np.float32), BF16)
    ci = np.arange(nc)
    low = jnp.asarray((ci[None, :] < ci[:, None]).astype(np.float32), BF16)
    out = pl.pallas_call(
        _logf_cumsum_kernel,
        grid=(r,),
        in_specs=[pl.BlockSpec((1, nc, LANES), lambda i: (i, 0, 0)),
                  pl.BlockSpec((LANES, LANES), lambda i: (0, 0)),
                  pl.BlockSpec((nc, nc), lambda i: (0, 0))],
        out_specs=pl.BlockSpec((1, nc, LANES), lambda i: (i, 0, 0)),
        out_shape=jax.ShapeDtypeStruct((r, nc, LANES), F32),
        compiler_params=_cparams(("parallel",)),
        name="fox_logf_cumsum",
    )(logits_rows.reshape(r, nc, LANES), tri, low)
    return out.reshape(r, t)


def _chain_rows(c):
    return slice(c * Q_ROWS, (c + 1) * Q_ROWS)


def _own_causal(c, width, strict=False):
    col = lax.broadcasted_iota(jnp.int32, (1, width), 1)
    row = lax.broadcasted_iota(jnp.int32, (Q_ROWS, 1), 0) + c * Q_ROWS
    return (col < row) if strict else (col <= row)


def _fox_kernel(q_ref, k_ref, v_ref, nf_ref, o_ref, knorm_ref, vaug_ref, m_ref, acc_ref, *, wide):
    t0 = pl.program_id(2) * (Q_CHAINS * Q_ROWS)

    @pl.when(pl.program_id(2) == 0)
    def _():
        _max_key_norm(k_ref, knorm_ref)
        _fill_value_aug(vaug_ref, v_ref)

    qs = [q_ref[0, _chain_rows(c), :] for c in range(Q_CHAINS)]
    q_norms = [_row_norms(q) for q in qs]
    states = [(m_ref.at[c], acc_ref.at[c]) for c in range(Q_CHAINS)]
    _flash_init(m_ref, acc_ref)

    def keep_going(first_done):
        prev = jnp.maximum(first_done // wide - 1, 0)
        cap = nf_ref[0, 0, pl.ds(prev, 1), :][:, wide - 1:wide] * (1.0 / ATTN_SCALE)
        return _softmax_alive(q_norms, knorm_ref[...], [cap] * Q_CHAINS, states)

    def full_tile(off):
        bias = nf_ref[0, 0, pl.ds(off // wide, 1), :] * (1.0 / ATTN_SCALE)
        k = k_ref[0, pl.ds(off, wide), :]
        raws = [_qk_raw(q, k) + bias for q in qs]
        _online_update_multi(raws, [vaug_ref[pl.ds(off, wide), :]] * Q_CHAINS, states)

    def tail_tile(off):
        bias = nf_ref[0, 0, pl.ds(off // wide, 1), :] * (1.0 / ATTN_SCALE)
        k = k_ref[0, pl.ds(off, wide), :]
        v = vaug_ref[pl.ds(off, wide), :]
        ends = [(c + 1) * Q_ROWS for c in range(Q_CHAINS)]
        raws = [jnp.where(_own_causal(c, e), _qk_raw(q, k[:e]) + bias[:, :e], NEG_BIG)
                for c, (q, e) in enumerate(zip(qs, ends))]
        _online_update_multi(raws, [v[:e] for e in ends], states)

    _sweep_tiles(t0, wide, tail_tile, full_tile, keep_going)
    for c in range(Q_CHAINS):
        o_ref[0, _chain_rows(c), :] = _flash_result(acc_ref.at[c])


def _attn_scratch(t, n_states):
    return [pltpu.VMEM((t, HEAD_DIM + LANES), BF16), pltpu.VMEM((n_states, Q_ROWS, 1), F32),
            pltpu.VMEM((n_states, Q_ROWS, HEAD_DIM + LANES), F32)]


def _fox_attention(p3, neg_cum_f):
    b, t, _ = p3.shape
    tq = Q_CHAINS * Q_ROWS
    wide = tq
    nf = neg_cum_f.reshape(b, GROUP_HEADS, t // wide, wide)
    return pl.pallas_call(
        functools.partial(_fox_kernel, wide=wide),
        grid=(b, GROUP_HEADS, t // tq),
        in_specs=[pl.BlockSpec((1, tq, HEAD_DIM), lambda bi, h, i: (bi, i, COL["fq"] + h)),
                  pl.BlockSpec((1, t, HEAD_DIM), lambda bi, h, i: (bi, 0, COL["fk"] + h)),
                  pl.BlockSpec((1, t, HEAD_DIM), lambda bi, h, i: (bi, 0, COL["fv"] + h)),
                  pl.BlockSpec((1, 1, t // wide, wide), lambda bi, h, i: (bi, h, 0, 0))],
        out_specs=pl.BlockSpec((1, tq, HEAD_DIM), lambda bi, h, i: (bi, i, h)),
        out_shape=jax.ShapeDtypeStruct((b, t, GROUP_WIDTH), F32),
        scratch_shapes=[pltpu.VMEM((1, 1), F32)] + _attn_scratch(t, Q_CHAINS),
        compiler_params=_cparams(("parallel", "parallel", "arbitrary")),
        name="fox_attention",
    )(p3, p3, p3, nf)


def _neg_abs(x):
    bits = lax.bitcast_convert_type(x, jnp.uint32) | jnp.uint32(0x80000000)
    return lax.bitcast_convert_type(bits, F32)


def _stick_kernel(q_ref, k_ref, v_ref, u_ref, o_ref, r_ref, acc_ref, *, wide):
    t0 = pl.program_id(2) * (Q_CHAINS * Q_ROWS)
    chains = range(Q_CHAINS)
    qs = [q_ref[0, _chain_rows(c), :] for c in chains]
    r_ref[...] = jnp.zeros(r_ref.shape, F32)
    acc_ref[...] = jnp.zeros(acc_ref.shape, F32)
    u = u_ref[...]

    def tile(off, tail, which):
        ends = {c: (c + 1) * Q_ROWS if tail else wide for c in which}
        k = k_ref[0, pl.ds(off, wide), :]
        z_all = {c: _qk_raw(qs[c], k[:ends[c]]) * SCALE_LOG2E for c in which}
        later = {c: r_ref[c] for c in which}
        weights = {c: [] for c in which}
        before = _own_causal(0, Q_ROWS, strict=True)
        for sb in reversed(range(wide // Q_ROWS)):
            live = [c for c in which if ends[c] > sb * Q_ROWS]
            diag = [tail and c == sb for c in live]
            cols = slice(sb * Q_ROWS, (sb + 1) * Q_ROWS)
            zs = [z_all[c][:, cols] for c in live]
            log_beta = [jnp.minimum(z, 0.0) - jnp.log(1.0 + jnp.exp2(_neg_abs(z))) * LOG2E for z in zs]
            log_keep = [lb - z for lb, z in zip(log_beta, zs)]
            log_keep = [jnp.where(before, lk, 0.0) if d else lk for d, lk in zip(diag, log_keep)]
            inside = [jnp.dot(jnp.concatenate(_split3(lk)[:2], axis=1), u, preferred_element_type=F32)
                      for lk in log_keep]
            a = [jnp.exp2(lb + cs + later[c]) for c, lb, cs in zip(live, log_beta, inside)]
            a = [jnp.where(before, x, 0.0) if d else x for d, x in zip(diag, a)]
            for c, x, cs, lk in zip(live, a, inside, log_keep):
                weights[c].append(x.astype(BF16))
                later[c] = later[c] + cs[:, 0:1] + lk[:, 0:1]
        v = v_ref[0, pl.ds(off, wide), :]
        for c in which:
            w = weights[c][0] if len(weights[c]) == 1 else jnp.concatenate(weights[c][::-1], axis=1)
            acc_ref[c] += jnp.dot(w, v[:ends[c]], preferred_element_type=F32)
            r_ref[c] = later[c]

    def full_tile(off):
        first_alive = jnp.max(r_ref[0]) > -STICK_DEAD_LOG2
        rest_alive = jnp.max(r_ref[1:]) > -STICK_DEAD_LOG2
        pl.when(rest_alive)(functools.partial(tile, off, False, list(chains)))
        pl.when(first_alive & jnp.logical_not(rest_alive))(functools.partial(tile, off, False, [0]))

    def keep_going(first_done):
        del first_done
        return jnp.max(r_ref[...]) > -STICK_DEAD_LOG2

    _sweep_tiles(t0, wide, functools.partial(tile, tail=True, which=list(chains)), full_tile, keep_going)
    for c in chains:
        o_ref[0, _chain_rows(c), :] = acc_ref[c]


def _stick_attention(p3):
    b, t, _ = p3.shape
    tq = Q_CHAINS * Q_ROWS
    wide = tq
    ki = np.arange(Q_ROWS)
    u = (ki[:, None] > ki[None, :]).astype(np.float32)
    u = jnp.asarray(np.concatenate([u, u], axis=0), BF16)
    return pl.pallas_call(
        functools.partial(_stick_kernel, wide=wide),
        grid=(b, GROUP_HEADS, t // tq),
        in_specs=[pl.BlockSpec((1, tq, HEAD_DIM), lambda bi, h, i: (bi, i, COL["sq"] + h)),
                  pl.BlockSpec((1, t, HEAD_DIM), lambda bi, h, i: (bi, 0, COL["sk"] + h)),
                  pl.BlockSpec((1, t, HEAD_DIM), lambda bi, h, i: (bi, 0, COL["sv"] + h)),
                  pl.BlockSpec((2 * Q_ROWS, Q_ROWS), lambda bi, h, i: (0, 0))],
        out_specs=pl.BlockSpec((1, tq, HEAD_DIM), lambda bi, h, i: (bi, i, h)),
        out_shape=jax.ShapeDtypeStruct((b, t, GROUP_WIDTH), F32),
        scratch_shapes=[pltpu.VMEM((Q_CHAINS, Q_ROWS, 1), F32), pltpu.VMEM((Q_CHAINS, Q_ROWS, HEAD_DIM), F32)],
        compiler_params=_cparams(("parallel", "parallel", "arbitrary")),
        name="stick_attention",
    )(p3, p3, p3, u)


def _moba_kernel(slope_ref, q_ref, k_ref, v_ref, o_ref, kaug_ref, kmean_ref, knorm_ref, vaug_ref, m_ref, acc_ref, *,
                 n_blk, wide):
    h = pl.program_id(1)
    i = pl.program_id(2)
    tq = MOBA_BLOCK
    slope = slope_ref[h]

    @pl.when(i == 0)
    def _():
        kmean_ref[...] = jnp.zeros(kmean_ref.shape, F32)
        kb = k_ref[0].astype(F32).reshape(n_blk, tq, HEAD_DIM)
        kmean_ref[0:n_blk, :] = jnp.mean(kb, axis=1)

        def key_aux(pos, lane):
            onehot = jnp.where(lane == jnp.right_shift(pos, int(math.log2(tq))), 1.0, 0.0)
            hi, mid, lo = _split3(slope * pos.astype(F32) * (1.0 / ATTN_SCALE))
            return jnp.where(lane == n_blk, hi.astype(F32),
                             jnp.where(lane == n_blk + 1, mid.astype(F32),
                                       jnp.where(lane == n_blk + 2, lo.astype(F32), onehot)))

        _fill_key_aug(kaug_ref, k_ref, key_aux)
        _max_key_norm(k_ref, knorm_ref)
        _fill_value_aug(vaug_ref, v_ref)

    chains = range(Q_CHAINS)
    t0 = i * (Q_CHAINS * Q_ROWS)
    qs = [q_ref[0, _chain_rows(c), :] for c in chains]
    q_norms = [_row_norms(q) for q in qs]
    own = [i * Q_CHAINS + c for c in chains]
    nb8 = -(-n_blk // 8) * 8
    kmean = kmean_ref[0:nb8, :].astype(BF16)
    blk_t = lax.broadcasted_iota(jnp.int32, (nb8, Q_ROWS), 0)
    scores = [jnp.where(blk_t < own[c], _qk_raw(kmean, qs[c]), -jnp.inf) for c in chains]
    sels_t = _topk_rounds(scores, blk_t.astype(F32), [jnp.where(blk_t == own[c], 1.0, 0.0) for c in chains],
                          MOBA_TOPK, axis=0)
    unused = jnp.zeros((LANES - nb8, Q_ROWS), F32)
    sels = [jnp.concatenate([s, unused], axis=0).T for s in sels_t]
    blk = lax.broadcasted_iota(jnp.int32, (Q_ROWS, LANES), 1)
    q_aux = [jnp.where(blk < n_blk, jnp.where(sel > 0.5, 0.0, NEG_BIG), jnp.where(blk < n_blk + 3, 1.0, 0.0))
             for sel in sels]
    q_aug = [jnp.concatenate([q, aux.astype(BF16)], axis=1) for q, aux in zip(qs, q_aux)]
    states = [(m_ref.at[c], acc_ref.at[c]) for c in chains]
    _flash_init(m_ref, acc_ref)

    def full_tile(off):
        k = kaug_ref[pl.ds(off, wide), :]
        _online_update_multi([_qk_raw(q, k) for q in q_aug], [vaug_ref[pl.ds(off, wide), :]] * Q_CHAINS, states)

    def tail_tile(off):
        k = kaug_ref[pl.ds(off, wide), :]
        v = vaug_ref[pl.ds(off, wide), :]
        ends = [(c + 1) * Q_ROWS for c in chains]
        raws = [jnp.where(_own_causal(c, e), _qk_raw(q, k[:e]), NEG_BIG) for c, (q, e) in enumerate(zip(q_aug, ends))]
        _online_update_multi(raws, [v[:e] for e in ends], states)

    def keep_going(first_done):
        cap = slope * first_done.astype(F32) * (1.0 / ATTN_SCALE)
        return _softmax_alive(q_norms, knorm_ref[...], [cap] * Q_CHAINS, states)

    _sweep_tiles(t0, wide, tail_tile, full_tile, keep_going)
    for c in chains:
        o_ref[0, _chain_rows(c), :] = _flash_result(acc_ref.at[c])


def _moba_attention(p3, slopes):
    b, t, _ = p3.shape
    tq = Q_CHAINS * Q_ROWS
    n_blk = t // MOBA_BLOCK
    wide = tq
    assert Q_ROWS == MOBA_BLOCK and t % tq == 0 and n_blk + 3 <= LANES
    return pl.pallas_call(
        functools.partial(_moba_kernel, n_blk=n_blk, wide=wide),
        grid=(b, GROUP_HEADS, t // tq),
        in_specs=[pl.BlockSpec(memory_space=pltpu.SMEM),
                  pl.BlockSpec((1, tq, HEAD_DIM), lambda bi, h, i: (bi, i, COL["mq"] + h)),
                  pl.BlockSpec((1, t, HEAD_DIM), lambda bi, h, i: (bi, 0, COL["mk"] + h)),
                  pl.BlockSpec((1, t, HEAD_DIM), lambda bi, h, i: (bi, 0, COL["mv"] + h))],
        out_specs=pl.BlockSpec((1, tq, HEAD_DIM), lambda bi, h, i: (bi, i, h)),
        out_shape=jax.ShapeDtypeStruct((b, t, GROUP_WIDTH), F32),
        scratch_shapes=[pltpu.VMEM((t, HEAD_DIM + LANES), BF16),
                        pltpu.VMEM((LANES, HEAD_DIM), F32), pltpu.VMEM((1, 1), F32)] + _attn_scratch(t, Q_CHAINS),
        compiler_params=_cparams(("parallel", "parallel", "arbitrary")),
        name="moba_attention",
    )(slopes, p3, p3, p3)


def _nsa_compress_kernel(x_ref, pos_ref, w1_ref, w2_ref, o_ref):
    x = (x_ref[0].astype(F32) + pos_ref[...]).astype(BF16)
    hmid = jnp.dot(x, w1_ref[...], preferred_element_type=F32)
    hmid = hmid / (1.0 + jnp.exp(-hmid))
    o_ref[0] = jnp.dot(hmid.astype(BF16), w2_ref[...], preferred_element_type=F32).astype(o_ref.dtype)


def _nsa_compress(blocks, pos, w1, w2):
    r, n, width = blocks.shape
    return pl.pallas_call(
        _nsa_compress_kernel,
        grid=(r,),
        in_specs=[pl.BlockSpec((1, n, width), lambda i: (i, 0, 0)),
                  pl.BlockSpec((1, width), lambda i: (0, 0)),
                  pl.BlockSpec((width, HEAD_DIM), lambda i: (0, 0)),
                  pl.BlockSpec((HEAD_DIM, HEAD_DIM), lambda i: (0, 0))],
        out_specs=pl.BlockSpec((1, n, HEAD_DIM), lambda i: (i, 0, 0)),
        out_shape=jax.ShapeDtypeStruct((r, n, HEAD_DIM), BF16),
        compiler_params=_cparams(("parallel",)),
        name="nsa_compress",
    )(blocks, pos.reshape(1, width).astype(F32), w1.astype(BF16), w2.astype(BF16))


def _nsa_select_kernel(slope_ref, q_ref, kc_ref, vc_ref, a_ref, oc_ref, sel_ref, *, tq, n_cmp, n_blk):
    g = pl.program_id(1)
    i = pl.program_id(2)
    t0 = i * tq
    kc = kc_ref[0, 0]
    vc = vc_ref[0, 0]
    ncp = kc.shape[0]
    tok = lax.broadcasted_iota(jnp.int32, (1, ncp), 1)
    cmp_end = tok * NSA_CMP_STRIDE + (NSA_CMP_LEN - 1)
    row = lax.broadcasted_iota(jnp.int32, (tq, 1), 0)
    admissible = (cmp_end <= row + t0) & (tok < n_cmp)
    rel_end = (cmp_end - t0).astype(F32)

    imp = jnp.zeros((tq, ncp), F32)
    for hh in range(NSA_Q_PER_KV):
        slope = slope_ref[g * NSA_Q_PER_KV + hh]
        q = q_ref[0, :, hh * HEAD_DIM:(hh + 1) * HEAD_DIM]
        s = jnp.where(admissible, _qk(q, kc) + slope * rel_end, NEG_BIG)
        m = jnp.max(s, axis=-1, keepdims=True)
        p = jnp.where(admissible, jnp.exp(s - m), 0.0)
        p = p / jnp.maximum(jnp.sum(p, axis=-1, keepdims=True), F32_TINY)
        oc_ref[0, :, hh * HEAD_DIM:(hh + 1) * HEAD_DIM] = jnp.dot(
            p.astype(BF16), vc, preferred_element_type=F32)
        imp = imp + p

    imp_blk = _dot_split(imp, a_ref[...], 3).T
    blk = lax.broadcasted_iota(jnp.int32, (LANES, tq), 0)
    jt = jnp.right_shift(lax.broadcasted_iota(jnp.int32, (1, tq), 1) + t0, int(math.log2(NSA_SEL_BLOCK)))
    live = (blk <= jt) & (blk < n_blk)
    forced = ((blk == 0) | (blk == jt) | (blk == jt - 1)) & live
    sel = jnp.where(forced, 1.0, 0.0)
    score = jnp.where(live & jnp.logical_not(forced), imp_blk, -jnp.inf)
    sel = _topk_rounds([score], blk.astype(F32), [sel], NSA_SEL_TOPK - 3, axis=0)[0]
    sel_ref[0, 0] = sel.T.astype(sel_ref.dtype)


def _nsa_select(p3, kc, vc, slopes, tq):
    b, t, _ = p3.shape
    n_blk = t // NSA_SEL_BLOCK
    assert n_blk <= LANES and NSA_SEL_TOPK >= 3
    n_cmp = t // NSA_CMP_STRIDE - 1
    ncp = kc.shape[2]
    ratio = NSA_SEL_BLOCK // NSA_CMP_STRIDE
    ti = np.arange(ncp)[:, None]
    bj = np.arange(LANES)[None, :]
    amat = ((ti >= ratio * bj - 1) & (ti <= ratio * bj + ratio - 1) & (ti < n_cmp) & (bj < n_blk))
    amat = jnp.asarray(amat.astype(np.float32), BF16)
    width = NSA_Q_PER_KV * HEAD_DIM
    qblk = COL["nq"] * HEAD_DIM // width
    return pl.pallas_call(
        functools.partial(_nsa_select_kernel, tq=tq, n_cmp=n_cmp, n_blk=n_blk),
        grid=(b, NSA_KV_HEADS, t // tq),
        in_specs=[pl.BlockSpec(memory_space=pltpu.SMEM),
                  pl.BlockSpec((1, tq, width), lambda bi, g, i: (bi, i, qblk + g)),
                  pl.BlockSpec((1, 1, ncp, HEAD_DIM), lambda bi, g, i: (bi, g, 0, 0)),
                  pl.BlockSpec((1, 1, ncp, HEAD_DIM), lambda bi, g, i: (bi, g, 0, 0)),
                  pl.BlockSpec((ncp, LANES), lambda bi, g, i: (0, 0))],
        out_specs=[pl.BlockSpec((1, tq, width), lambda bi, g, i: (bi, i, g)),
                   pl.BlockSpec((1, 1, tq, LANES), lambda bi, g, i: (bi, g, i, 0))],
        out_shape=[jax.ShapeDtypeStruct((b, t, GROUP_WIDTH), F32),
                   jax.ShapeDtypeStruct((b, NSA_KV_HEADS, t, LANES), BF16)],
        compiler_params=_cparams(("parallel", "parallel", "parallel")),
        name="nsa_select",
    )(slopes, p3, kc, vc, amat)


def _nsa_attn_kernel(slope_ref, q_ref, ks_ref, vs_ref, kw_ref, vw_ref, sel_ref, oc_ref, gate_ref,
                     o_ref, kaug_ref, knorm_ref, vwaug_ref, vsaug_ref, m_ref, acc_ref, *, tq, wide):
    g = pl.program_id(1)
    i = pl.program_id(2)
    t0 = pl.multiple_of(i * tq, tq)
    nh = NSA_Q_PER_KV

    @pl.when(i == 0)
    def _():
        sel_shift = int(math.log2(NSA_SEL_BLOCK))
        _fill_key_aug(kaug_ref, ks_ref,
                      lambda pos, lane: jnp.where(lane == jnp.right_shift(pos, sel_shift), 1.0, 0.0))
        _max_key_norm(ks_ref, knorm_ref)
        _fill_value_aug(vsaug_ref, vs_ref)
        _fill_value_aug(vwaug_ref, vw_ref)

    _flash_init(m_ref, acc_ref)
    sel_bias = jnp.where(sel_ref[0, 0].astype(F32) > 0.5, 0.0, NEG_BIG).astype(BF16)
    qs = [q_ref[0, :, hh * HEAD_DIM:(hh + 1) * HEAD_DIM] for hh in range(nh)]
    q_aug = [jnp.concatenate([q, sel_bias], axis=1) for q in qs]
    slope_raw = [slope_ref[g * nh + hh] * (1.0 / ATTN_SCALE) for hh in range(nh)]
    row = lax.broadcasted_iota(jnp.int32, (tq, 1), 0)

    def rel_pos(off, width):
        return lax.broadcasted_iota(jnp.int32, (1, width), 1) + (off - t0)

    sel_states = [(m_ref.at[hh], acc_ref.at[hh]) for hh in range(nh)]
    win_states = [(m_ref.at[nh + hh], acc_ref.at[nh + hh]) for hh in range(nh)]
    q_norms = [_row_norms(q) for q in qs]

    def sel_scores(off, tail):
        rel = rel_pos(off, wide)
        relf = rel.astype(F32)
        k = kaug_ref[pl.ds(off, wide), :]
        raws = [_qk_raw(q_aug[hh], k) + slope_raw[hh] * relf for hh in range(nh)]
        if tail:
            raws = [jnp.where(rel <= row, raw, NEG_BIG) for raw in raws]
        return raws, [vsaug_ref[pl.ds(off, wide), :]] * nh

    def window_scores(off, width, far):
        rel = rel_pos(off, width)
        relf = rel.astype(F32)
        k = kw_ref[0, pl.ds(off, width), :]
        ok = rel <= row
        if far:
            ok = ok & (row - rel < NSA_WINDOW)
        raws = [jnp.where(ok, _qk_raw(qs[hh], k) + slope_raw[hh] * relf, NEG_BIG) for hh in range(nh)]
        return raws, [vwaug_ref[pl.ds(off, width), :]] * nh

    def sel_tile(off, tail):
        raws, vs = sel_scores(off, tail)
        _online_update_multi(raws, vs, sel_states)

    def keep_going(first_done):
        gap = (first_done - t0).astype(F32)
        return _softmax_alive(q_norms, knorm_ref[...], [s * gap for s in slope_raw], sel_states)

    _sweep_tiles(t0, wide, functools.partial(sel_tile, tail=True), functools.partial(sel_tile, tail=False),
                 keep_going)

    n_back = NSA_WINDOW // tq
    for back in range(n_back + 1):
        @pl.when((i == back) if back < n_back else (i >= back))
        def _(back=back):
            raws, vs = window_scores(pl.multiple_of(t0 - back * tq, tq), (back + 1) * tq, back == n_back)
            _online_update_multi(raws, vs, win_states)

    gates = gate_ref[0, 0]
    gates = 1.0 / (1.0 + jnp.exp(-gates))
    for hh in range(nh):
        c0 = hh * NSA_N_BRANCHES
        o_c = oc_ref[0, :, hh * HEAD_DIM:(hh + 1) * HEAD_DIM]
        o_s = _flash_result(acc_ref.at[hh])
        o_w = _flash_result(acc_ref.at[nh + hh])
        o_ref[0, :, hh * HEAD_DIM:(hh + 1) * HEAD_DIM] = (
            gates[:, c0:c0 + 1] * o_c + gates[:, c0 + 1:c0 + 2] * o_s + gates[:, c0 + 2:c0 + 3] * o_w)


def _nsa_attention(p3, sel, o_cmp, gate_logits, slopes, tq):
    b, t, _ = p3.shape
    assert tq == KV_UNIT and NSA_WINDOW == 2 * tq and t // NSA_SEL_BLOCK <= LANES
    width = NSA_Q_PER_KV * HEAD_DIM
    qblk = COL["nq"] * HEAD_DIM // width
    kv = lambda name: pl.BlockSpec((1, t, HEAD_DIM), lambda bi, g, i: (bi, 0, COL[name] + g))
    nst = 2 * NSA_Q_PER_KV
    return pl.pallas_call(
        functools.partial(_nsa_attn_kernel, tq=tq, wide=min(KV_WIDE, t)),
        grid=(b, NSA_KV_HEADS, t // tq),
        in_specs=[pl.BlockSpec(memory_space=pltpu.SMEM),
                  pl.BlockSpec((1, tq, width), lambda bi, g, i: (bi, i, qblk + g)),
                  kv("nks"), kv("nvs"), kv("nkw"), kv("nvw"),
                  pl.BlockSpec((1, 1, tq, LANES), lambda bi, g, i: (bi, g, i, 0)),
                  pl.BlockSpec((1, tq, width), lambda bi, g, i: (bi, i, g)),
                  pl.BlockSpec((1, 1, tq, LANES), lambda bi, g, i: (bi, g, i, 0))],
        out_specs=pl.BlockSpec((1, tq, width), lambda bi, g, i: (bi, i, g)),
        out_shape=jax.ShapeDtypeStruct((b, t, GROUP_WIDTH), F32),
        scratch_shapes=[pltpu.VMEM((t, HEAD_DIM + LANES), BF16), pltpu.VMEM((1, 1), F32),
                        pltpu.VMEM((t, HEAD_DIM + LANES), BF16)] + _attn_scratch(t, nst),
        compiler_params=_cparams(("parallel", "parallel", "arbitrary")),
        name="nsa_attention",
    )(slopes, p3, p3, p3, p3, p3, sel, o_cmp, gate_logits)


def _nsa_blocks(p3, name):
    b, t, _ = p3.shape
    c0 = COL[name] * HEAD_DIM
    x = p3[:, :, c0:c0 + NSA_KV_HEADS * HEAD_DIM].reshape(b, t // NSA_CMP_STRIDE, NSA_CMP_STRIDE,
                                                         NSA_KV_HEADS, HEAD_DIM)
    x = x.transpose(0, 3, 1, 2, 4).reshape(b * NSA_KV_HEADS, t // NSA_CMP_STRIDE, NSA_CMP_STRIDE * HEAD_DIM)
    blocks = jnp.concatenate([x[:, :-1], x[:, 1:]], axis=-1)
    return jnp.pad(blocks, ((0, 0), (0, 1), (0, 0)))


def _alibi_slopes():
    n = 2 * GROUP_HEADS
    slopes = 2.0 ** (-8.0 * np.arange(1, n + 1) / n)
    return jnp.asarray(slopes[0::2], F32), jnp.asarray(slopes[1::2], F32)


def _regroup_w_in(w_in):
    main = jnp.concatenate([w_in[:, REF_OFF[n][0]:REF_OFF[n][0] + REF_OFF[n][1]] for n, _ in _MAIN_PIECES],
                           axis=1).astype(BF16)
    ng0, ngw = REF_OFF["ng"]
    ff0, ffw = REF_OFF["ff"]
    small = jnp.concatenate([w_in[:, ng0:ng0 + ngw], w_in[:, ff0:ff0 + ffw]], axis=1)
    small = jnp.pad(small, ((0, 0), (0, LANES - ngw - ffw))).astype(BF16)
    return main, small


def _mixer(x2d, normed, b, t, w_in, forget_bias, pos_k, w1_k, w2_k, pos_v, w1_v, w2_v, group_gain, w_out, next_gain):
    m = x2d.shape[0]
    w_main, w_small = _regroup_w_in(w_in)
    tm = _pick_tile(m, (1024, 512, 256))
    p3 = _matmul_normed(normed, w_main, BF16, tm, 512).reshape(b, t, MAIN_BLOCKS * HEAD_DIM)
    small = _matmul_normed(normed, w_small, F32, tm, LANES).reshape(b, t, LANES)
    slopes_moba, slopes_nsa = _alibi_slopes()

    o_a = _moba_attention(p3, slopes_moba)

    n_gate = NSA_N_BRANCHES * GROUP_HEADS
    kc = _nsa_compress(_nsa_blocks(p3, "nkc"), pos_k, w1_k, w2_k)
    vc = _nsa_compress(_nsa_blocks(p3, "nvc"), pos_v, w1_v, w2_v)
    ncp = kc.shape[1]
    kc = kc.reshape(b, NSA_KV_HEADS, ncp, HEAD_DIM)
    vc = vc.reshape(b, NSA_KV_HEADS, ncp, HEAD_DIM)
    o_cmp, sel = _nsa_select(p3, kc, vc, slopes_nsa, _pick_tile(t, (512, 256)))
    per_group = n_gate // NSA_KV_HEADS
    gate_logits = small[:, :, :n_gate].reshape(b, t, NSA_KV_HEADS, per_group).transpose(0, 2, 1, 3)
    gate_logits = jnp.pad(gate_logits, ((0, 0), (0, 0), (0, 0), (0, LANES - per_group)))
    o_b = _nsa_attention(p3, sel, o_cmp, gate_logits, slopes_nsa, KV_UNIT)

    ff = small[:, :, n_gate:n_gate + GROUP_HEADS] + forget_bias
    neg_cum_f = _neg_cum_logf(ff.transpose(0, 2, 1).reshape(b * GROUP_HEADS, t))
    o_c = _fox_attention(p3, neg_cum_f)

    o_d = _stick_attention(p3)

    on = _groupnorm([o.reshape(m, GROUP_WIDTH) for o in (o_a, o_b, o_c, o_d)], group_gain)
    return _matmul_residual(on, w_out, x2d, next_gain, tm, 512)


def _cast_pad_kernel(w_ref, o_ref, *, n_row_tiles, cols):
    @pl.when(pl.program_id(0) < n_row_tiles)
    def _():
        o_ref[:, 0:cols] = w_ref[0].astype(BF16)
        if o_ref.shape[1] > cols:
            o_ref[:, cols:] = jnp.zeros((o_ref.shape[0], o_ref.shape[1] - cols), BF16)

    @pl.when(pl.program_id(0) >= n_row_tiles)
    def _():
        o_ref[...] = jnp.zeros(o_ref.shape, BF16)


def _weight_bf16(stacked, layer, rows_out, cols_out, tr=256):
    _, rows, cols = stacked.shape
    assert rows % tr == 0 and rows_out % tr == 0 and cols % LANES == 0
    n_row_tiles = rows // tr
    return pl.pallas_call(
        functools.partial(_cast_pad_kernel, n_row_tiles=n_row_tiles, cols=cols),
        grid=(rows_out // tr,),
        in_specs=[pl.BlockSpec((1, tr, cols), lambda j: (layer, jnp.minimum(j, n_row_tiles - 1), 0))],
        out_specs=pl.BlockSpec((tr, cols_out), lambda j: (j, 0)),
        out_shape=jax.ShapeDtypeStruct((rows_out, cols_out), BF16),
        compiler_params=_cparams(("parallel",)),
        name="weight_bf16",
    )(stacked)


def kernel(x, ffn1_norm, ffn1_w_gate, ffn1_w_up, ffn1_w_down, mix_norm, w_in, fox_forget_bias, nsa_cmp_pos_k, nsa_cmp_w1_k, nsa_cmp_w2_k, nsa_cmp_pos_v, nsa_cmp_w1_v, nsa_cmp_w2_v, group_norm, w_out, ffn2_norm, ffn2_w_gate, ffn2_w_up, ffn2_w_down, final_norm):
    b, t, d = x.shape
    x2d = x.reshape(b * t, d)
    depth = ffn1_norm.shape[0]
    normed = _prenorm(x2d, ffn1_norm[0])
    for i in range(depth):
        x2d, normed = _ffn(x2d, normed, ffn1_w_gate, ffn1_w_up, ffn1_w_down, i, mix_norm[i])
        x2d, normed = _mixer(x2d, normed, b, t, w_in[i], fox_forget_bias[i], nsa_cmp_pos_k[i], nsa_cmp_w1_k[i],
                             nsa_cmp_w2_k[i], nsa_cmp_pos_v[i], nsa_cmp_w1_v[i], nsa_cmp_w2_v[i],
                             group_norm[i], _weight_bf16(w_out, i, d, d), ffn2_norm[i])
        x2d, normed = _ffn(x2d, normed, ffn2_w_gate, ffn2_w_up, ffn2_w_down, i,
                           ffn1_norm[i + 1] if i + 1 < depth else None)
    return _rmsnorm(x2d, final_norm, F32).reshape(b, t, d)
```

```python
import functools
import math

import numpy as np
import jax
import jax.numpy as jnp
from jax import lax
from jax.experimental import pallas as pl
from jax.experimental.pallas import tpu as pltpu

F32 = jnp.float32
BF16 = jnp.bfloat16

HEAD_DIM = 128
GROUP_HEADS = 8
GROUP_WIDTH = GROUP_HEADS * HEAD_DIM
N_MIXERS = 4
FFN_RESIDUAL = 0.5
RMS_EPS = 1e-6

MOBA_BLOCK = 256
MOBA_TOPK = 3

NSA_KV_HEADS = 2
NSA_Q_PER_KV = 4
NSA_CMP_STRIDE = 16
NSA_CMP_LEN = 32
NSA_SEL_BLOCK = 64
NSA_SEL_TOPK = 16
NSA_WINDOW = 512
NSA_N_BRANCHES = 3

ATTN_SCALE = HEAD_DIM ** -0.5
LOG2E = math.log2(math.e)
SCALE_LOG2E = ATTN_SCALE * LOG2E
NEG_BIG = -1e30
STICK_DEAD_LOG2 = 160.0
FLASH_DEAD_LOG2 = 170.0
BOUND_SLACK = 1.001
KV_UNIT = 256
KV_WIDE = 1024
Q_ROWS = 256
Q_CHAINS = 4
F32_TINY = float(np.finfo(np.float32).tiny)

LANES = 128
VMEM_LIMIT = 56 * 1024 * 1024
GATEUP_TILE = 256
WEIGHT_BUFS = 3
FFN_PAD = 512

_MAIN_PIECES = (
    ("mq", 8), ("mk", 8), ("mv", 8), ("nq", 8),
    ("nkc", 2), ("nvc", 2), ("nks", 2), ("nvs", 2), ("nkw", 2), ("nvw", 2),
    ("fq", 8), ("fk", 8), ("fv", 8), ("sq", 8), ("sk", 8), ("sv", 8),
)
COL = {}
_o = 0
for _n, _w in _MAIN_PIECES:
    COL[_n] = _o
    _o += _w
MAIN_BLOCKS = _o

_REF_SIZES = (
    ("mq", 1024), ("mk", 1024), ("mv", 1024), ("nq", 1024),
    ("nkc", 256), ("nvc", 256), ("nks", 256), ("nvs", 256), ("nkw", 256), ("nvw", 256),
    ("ng", 24),
    ("fq", 1024), ("fk", 1024), ("fv", 1024), ("ff", 8),
    ("sq", 1024), ("sk", 1024), ("sv", 1024),
)
REF_OFF = {}
_o = 0
for _n, _w in _REF_SIZES:
    REF_OFF[_n] = (_o, _w)
    _o += _w


def _cparams(sem):
    return pltpu.CompilerParams(dimension_semantics=sem, vmem_limit_bytes=VMEM_LIMIT)


def _rms_kernel(x_ref, g_ref, o_ref):
    x = x_ref[...]
    ms = jnp.mean(x * x, axis=-1, keepdims=True)
    o_ref[...] = (x * lax.rsqrt(ms + RMS_EPS) * g_ref[...]).astype(o_ref.dtype)


def _rmsnorm(x2d, gain, out_dtype, tm=256):
    m, d = x2d.shape
    return pl.pallas_call(
        _rms_kernel,
        grid=(m // tm,),
        in_specs=[pl.BlockSpec((tm, d), lambda i: (i, 0)),
                  pl.BlockSpec((1, d), lambda i: (0, 0))],
        out_specs=pl.BlockSpec((tm, d), lambda i: (i, 0)),
        out_shape=jax.ShapeDtypeStruct((m, d), out_dtype),
        compiler_params=_cparams(("parallel",)),
        name="rmsnorm",
    )(x2d, gain.reshape(1, d).astype(F32))


def _emit_normed(x_new, gain_ref, xg_ref, ssq_ref):
    xg_ref[...] = (x_new * gain_ref[...]).astype(BF16)
    ssq_ref[...] = jnp.broadcast_to(jnp.sum(x_new * x_new, axis=-1, keepdims=True), ssq_ref.shape)


def _rstd(ssq_ref, d):
    ssq = ssq_ref[...]
    total = ssq[:, 0:1]
    for p in range(1, ssq.shape[1] // LANES):
        total = total + ssq[:, p * LANES:p * LANES + 1]
    return lax.rsqrt(total * (1.0 / d) + RMS_EPS)


def _prenorm_kernel(x_ref, g_ref, xg_ref, ssq_ref):
    _emit_normed(x_ref[...], g_ref, xg_ref, ssq_ref)


def _prenorm(x2d, gain, tm=256):
    m, d = x2d.shape
    return pl.pallas_call(
        _prenorm_kernel,
        grid=(m // tm,),
        in_specs=[pl.BlockSpec((tm, d), lambda i: (i, 0)),
                  pl.BlockSpec((1, d), lambda i: (0, 0))],
        out_specs=[pl.BlockSpec((tm, d), lambda i: (i, 0)), pl.BlockSpec((tm, LANES), lambda i: (i, 0))],
        out_shape=[jax.ShapeDtypeStruct((m, d), BF16), jax.ShapeDtypeStruct((m, LANES), F32)],
        compiler_params=_cparams(("parallel",)),
        name="prenorm",
    )(x2d, gain.reshape(1, d).astype(F32))


def _groupnorm_kernel(a_ref, b_ref, c_ref, d_ref, g_ref, o_ref):
    for gi, ref in enumerate((a_ref, b_ref, c_ref, d_ref)):
        x = ref[...]
        ms = jnp.mean(x * x, axis=-1, keepdims=True)
        lo, hi = gi * GROUP_WIDTH, (gi + 1) * GROUP_WIDTH
        o_ref[:, lo:hi] = (x * lax.rsqrt(ms + RMS_EPS) * g_ref[:, lo:hi]).astype(o_ref.dtype)


def _groupnorm(parts, gain, tm=256):
    m = parts[0].shape[0]
    d = GROUP_WIDTH * N_MIXERS
    part_spec = pl.BlockSpec((tm, GROUP_WIDTH), lambda i: (i, 0))
    return pl.pallas_call(
        _groupnorm_kernel,
        grid=(m // tm,),
        in_specs=[part_spec] * 4 + [pl.BlockSpec((1, d), lambda i: (0, 0))],
        out_specs=pl.BlockSpec((tm, d), lambda i: (i, 0)),
        out_shape=jax.ShapeDtypeStruct((m, d), BF16),
        compiler_params=_cparams(("parallel",)),
        name="groupnorm",
    )(*parts, gain.reshape(1, d).astype(F32))


def _mm_normed_kernel(a_ref, ssq_ref, w_ref, o_ref):
    acc = jnp.dot(a_ref[...], w_ref[...], preferred_element_type=F32)
    o_ref[...] = (acc * _rstd(ssq_ref, a_ref.shape[1])).astype(o_ref.dtype)


def _matmul_normed(normed, w, out_dtype, tm, tn):
    a, ssq = normed
    m, k = a.shape
    n = w.shape[1]
    return pl.pallas_call(
        _mm_normed_kernel,
        grid=(m // tm, n // tn),
        in_specs=[pl.BlockSpec((tm, k), lambda i, j: (i, 0)),
                  pl.BlockSpec((tm, ssq.shape[1]), lambda i, j: (i, 0)),
                  pl.BlockSpec((k, tn), lambda i, j: (0, j))],
        out_specs=pl.BlockSpec((tm, tn), lambda i, j: (i, j)),
        out_shape=jax.ShapeDtypeStruct((m, n), out_dtype),
        compiler_params=_cparams(("parallel", "arbitrary")),
        name="matmul",
    )(a, ssq, w)


def _normed_outputs(m, n, tm, tn, index):
    specs = [pl.BlockSpec((tm, tn), index), pl.BlockSpec((tm, LANES), index)]
    shapes = [jax.ShapeDtypeStruct((m, n), BF16), jax.ShapeDtypeStruct((m, LANES * (n // tn)), F32)]
    return specs, shapes


def _mm_res_kernel(a_ref, w_ref, x_ref, g_ref, o_ref, xg_ref, ssq_ref):
    x_new = x_ref[...] + jnp.dot(a_ref[...], w_ref[...], preferred_element_type=F32)
    o_ref[...] = x_new
    _emit_normed(x_new, g_ref, xg_ref, ssq_ref)


def _matmul_residual(a, w, x, next_gain, tm, tn):
    m, k = a.shape
    n = w.shape[1]
    index = lambda i, j: (i, j)
    n_specs, n_shapes = _normed_outputs(m, n, tm, tn, index)
    x_new, xg, ssq = pl.pallas_call(
        _mm_res_kernel,
        grid=(m // tm, n // tn),
        in_specs=[pl.BlockSpec((tm, k), lambda i, j: (i, 0)),
                  pl.BlockSpec((k, tn), lambda i, j: (0, j)),
                  pl.BlockSpec((tm, tn), index),
                  pl.BlockSpec((1, tn), lambda i, j: (0, j))],
        out_specs=[pl.BlockSpec((tm, tn), index)] + n_specs,
        out_shape=[jax.ShapeDtypeStruct((m, n), F32)] + n_shapes,
        compiler_params=_cparams(("parallel", "arbitrary")),
        name="matmul_residual",
    )(a, w, x, next_gain.reshape(1, n).astype(F32))
    return x_new, (xg, ssq)


def _gateup_kernel(h_ref, ssq_ref, wg_hbm, wu_hbm, o_ref, rstd_ref, rstd_res_ref, wg_buf, wu_buf, sem, *,
                   n_valid, n_row_tiles, layer):
    i = pl.program_id(0)
    j = pl.program_id(1)
    tf = o_ref.shape[1]
    q = i * n_valid + j
    n_steps = n_row_tiles * n_valid

    def copies(step):
        col = pl.multiple_of(lax.rem(step, n_valid) * tf, tf)
        slot = lax.rem(step, WEIGHT_BUFS)
        return [pltpu.make_async_copy(w.at[layer, :, pl.ds(col, tf)], buf.at[slot], sem.at[k, slot])
                for k, (w, buf) in enumerate(((wg_hbm, wg_buf), (wu_hbm, wu_buf)))]

    @pl.when((i == 0) & (j == 0))
    def _():
        for ahead in range(min(WEIGHT_BUFS - 1, n_steps)):
            for c in copies(ahead):
                c.start()

    @pl.when(j == 0)
    def _():
        rstd = _rstd(ssq_ref, h_ref.shape[1])
        rstd_ref[...] = rstd
        rstd_res_ref[...] = rstd * FFN_RESIDUAL

    @pl.when(j < n_valid)
    def _():
        @pl.when(q + (WEIGHT_BUFS - 1) < n_steps)
        def _():
            for c in copies(q + (WEIGHT_BUFS - 1)):
                c.start()

        for c in copies(q):
            c.wait()
        slot = lax.rem(q, WEIGHT_BUFS)
        h = h_ref[...]
        g = jnp.dot(h, wg_buf[slot].astype(BF16), preferred_element_type=F32) * rstd_ref[...]
        u = jnp.dot(h, wu_buf[slot].astype(BF16), preferred_element_type=F32)
        o_ref[...] = (g / (1.0 + jnp.exp(-g)) * u * rstd_res_ref[...]).astype(o_ref.dtype)

    @pl.when(j >= n_valid)
    def _():
        o_ref[...] = jnp.zeros(o_ref.shape, o_ref.dtype)


def _gateup(normed, wg, wu, layer, f_out, tm, tf):
    h, ssq = normed
    m, d = h.shape
    f = wg.shape[2]
    assert f % tf == 0 and f_out % tf == 0
    n_valid = f // tf
    w_spec = pl.BlockSpec(memory_space=pl.ANY)
    once = dict(pipeline_mode=pl.Buffered(1))
    return pl.pallas_call(
        functools.partial(_gateup_kernel, n_valid=n_valid, n_row_tiles=m // tm, layer=layer),
        grid=(m // tm, f_out // tf),
        in_specs=[pl.BlockSpec((tm, d), lambda i, j: (i, 0), **once),
                  pl.BlockSpec((tm, ssq.shape[1]), lambda i, j: (i, 0), **once), w_spec, w_spec],
        out_specs=pl.BlockSpec((tm, tf), lambda i, j: (i, j)),
        out_shape=jax.ShapeDtypeStruct((m, f_out), BF16),
        scratch_shapes=[pltpu.VMEM((tm, 1), F32), pltpu.VMEM((tm, 1), F32),
                        pltpu.VMEM((WEIGHT_BUFS, d, tf), F32), pltpu.VMEM((WEIGHT_BUFS, d, tf), F32),
                        pltpu.SemaphoreType.DMA((2, WEIGHT_BUFS))],
        compiler_params=_cparams(("arbitrary", "arbitrary")),
        name="ffn_gateup",
    )(h, ssq, wg, wu)


def _down_kernel(a_ref, w_ref, x_ref, *rest, emit):
    o_ref = rest[1] if emit else rest[0]
    @pl.when(pl.program_id(2) == 0)
    def _():
        o_ref[...] = x_ref[...] + jnp.dot(a_ref[...], w_ref[...], preferred_element_type=F32)

    @pl.when(pl.program_id(2) != 0)
    def _():
        o_ref[...] = o_ref[...] + jnp.dot(a_ref[...], w_ref[...], preferred_element_type=F32)

    if emit:
        g_ref, _, xg_ref, ssq_ref = rest

        @pl.when(pl.program_id(2) == pl.num_programs(2) - 1)
        def _():
            _emit_normed(o_ref[...], g_ref, xg_ref, ssq_ref)


def _down(act, wd, x, next_gain, tm, tn, tk):
    m, f = act.shape
    n = wd.shape[1]
    emit = next_gain is not None
    index = lambda i, j, k: (i, j)
    in_specs = [pl.BlockSpec((tm, tk), lambda i, j, k: (i, k)),
                pl.BlockSpec((tk, tn), lambda i, j, k: (k, j)),
                pl.BlockSpec((tm, tn), index)]
    out_specs, out_shape, args = [pl.BlockSpec((tm, tn), index)], [jax.ShapeDtypeStruct((m, n), F32)], [act, wd, x]
    if emit:
        n_specs, n_shapes = _normed_outputs(m, n, tm, tn, index)
        in_specs.append(pl.BlockSpec((1, tn), lambda i, j, k: (0, j)))
        out_specs, out_shape = out_specs + n_specs, out_shape + n_shapes
        args.append(next_gain.reshape(1, n).astype(F32))
    outs = pl.pallas_call(
        functools.partial(_down_kernel, emit=emit),
        grid=(m // tm, n // tn, f // tk),
        in_specs=in_specs,
        out_specs=out_specs,
        out_shape=out_shape,
        compiler_params=_cparams(("parallel", "parallel", "arbitrary")),
        name="ffn_down",
    )(*args)
    return (outs[0], (outs[1], outs[2])) if emit else (outs[0], None)


def _pick_tile(n, prefs):
    for t in prefs:
        if n % t == 0:
            return t
    return n


def _ffn(x2d, normed, wg, wu, wd, layer, next_gain):
    m, d = x2d.shape
    f = wg.shape[2]
    fp = -(-f // FFN_PAD) * FFN_PAD
    tm = _pick_tile(m, (1024, 512, 256))
    act = _gateup(normed, wg, wu, layer, fp, tm,
                  _pick_tile(f, (GATEUP_TILE, 128)))
    return _down(act, _weight_bf16(wd, layer, fp, d), x2d, next_gain, tm, _pick_tile(d, (1024, 512, 256, 128)),
                 _pick_tile(fp, (2816, 1024, 512, 256, 128)))


def _qk_raw(q, k):
    return lax.dot_general(q, k, (((1,), (1,)), ((), ())), preferred_element_type=F32)


def _qk(q, k):
    return _qk_raw(q, k) * ATTN_SCALE


def _online_update_multi(raws, vs, states):
    m_prev = [m_ref[...] for m_ref, _ in states]
    m_new = [jnp.maximum(mp, jnp.max(raw, axis=-1, keepdims=True)) for mp, raw in zip(m_prev, raws)]
    ps = [jnp.exp2((raw - mn) * SCALE_LOG2E).astype(BF16) for raw, mn in zip(raws, m_new)]
    alphas = [jnp.exp2((mp - mn) * SCALE_LOG2E) for mp, mn in zip(m_prev, m_new)]
    pvs = [jnp.dot(p, v, preferred_element_type=F32) for p, v in zip(ps, vs)]
    for (m_ref, acc_ref), mn, a, pv in zip(states, m_new, alphas, pvs):
        acc_ref[...] = a * acc_ref[...] + pv
        m_ref[...] = mn


def _flash_init(m_ref, acc_ref):
    m_ref[...] = jnp.full(m_ref.shape, NEG_BIG, F32)
    acc_ref[...] = jnp.zeros(acc_ref.shape, F32)


def _flash_result(acc_ref):
    acc = acc_ref[...]
    return acc[:, :HEAD_DIM] / acc[:, HEAD_DIM:HEAD_DIM + 1]


def _fill_value_aug(vaug_ref, v_ref):
    def body(c, carry):
        off = pl.multiple_of(c * KV_UNIT, KV_UNIT)
        lane = lax.broadcasted_iota(jnp.int32, (KV_UNIT, LANES), 1)
        vaug_ref[pl.ds(off, KV_UNIT), 0:HEAD_DIM] = v_ref[0, pl.ds(off, KV_UNIT), :]
        vaug_ref[pl.ds(off, KV_UNIT), HEAD_DIM:] = jnp.where(lane == 0, 1.0, 0.0).astype(BF16)
        return carry

    lax.fori_loop(0, vaug_ref.shape[0] // KV_UNIT, body, 0)


def _sweep_tiles(t0, wide, tail_tile, full_tile, keep_going):
    n_full = t0 // wide
    tail_off = pl.multiple_of(n_full * wide, wide)

    def full_off(jj):
        return pl.multiple_of((n_full - 1 - jj) * wide, wide)

    def step(state):
        jj, _ = state
        full_tile(full_off(jj))
        return jj + 1, keep_going(full_off(jj))

    tail_tile(tail_off)
    lax.while_loop(lambda s: (s[0] < n_full) & s[1], step, (jnp.int32(0), keep_going(tail_off)))


def _row_norms(q):
    q = q.astype(F32)
    return jnp.sqrt(jnp.sum(q * q, axis=-1, keepdims=True))


def _max_key_norm(k_ref, out_ref):
    def body(c, mx):
        k = k_ref[0, pl.ds(pl.multiple_of(c * KV_UNIT, KV_UNIT), KV_UNIT), :].astype(F32)
        return jnp.maximum(mx, jnp.max(jnp.sum(k * k, axis=-1, keepdims=True), axis=0, keepdims=True))

    out_ref[...] = jnp.sqrt(lax.fori_loop(0, k_ref.shape[1] // KV_UNIT, body, jnp.zeros((1, 1), F32)))


def _softmax_alive(q_norms, k_norm, bias_caps, states):
    gaps = [qn * (k_norm * BOUND_SLACK) + cap - m_ref[...]
            for qn, cap, (m_ref, _) in zip(q_norms, bias_caps, states)]
    worst = jnp.max(functools.reduce(jnp.maximum, gaps))
    return (worst + 1.0) * SCALE_LOG2E > -FLASH_DEAD_LOG2


def _fill_key_aug(kaug_ref, k_ref, aux_fn):
    t = kaug_ref.shape[0]

    def body(c, carry):
        off = pl.multiple_of(c * KV_UNIT, KV_UNIT)
        pos = lax.broadcasted_iota(jnp.int32, (KV_UNIT, LANES), 0) + off
        lane = lax.broadcasted_iota(jnp.int32, (KV_UNIT, LANES), 1)
        kaug_ref[pl.ds(off, KV_UNIT), 0:HEAD_DIM] = k_ref[0, pl.ds(off, KV_UNIT), :]
        kaug_ref[pl.ds(off, KV_UNIT), HEAD_DIM:] = aux_fn(pos, lane).astype(BF16)
        return carry

    lax.fori_loop(0, t // KV_UNIT, body, 0)


def _topk_rounds(scores, blkf, sels, rounds, axis=-1):
    for _ in range(rounds):
        mx = [jnp.max(s, axis=axis, keepdims=True) for s in scores]
        idx = [jnp.min(jnp.where(s == m, blkf, 1e9), axis=axis, keepdims=True) for s, m in zip(scores, mx)]
        picks = [(blkf == i) & (m > -jnp.inf) for i, m in zip(idx, mx)]
        sels = [jnp.where(p, 1.0, sel) for p, sel in zip(picks, sels)]
        scores = [jnp.where(p, -jnp.inf, s) for p, s in zip(picks, scores)]
    return sels


def _split3(x):
    hi = x.astype(BF16)
    r1 = x - hi.astype(F32)
    mid = r1.astype(BF16)
    lo = (r1 - mid.astype(F32)).astype(BF16)
    return hi, mid, lo


def _dot_split(x, w_bf16, terms):
    parts = _split3(x)[:terms]
    out = jnp.dot(parts[0], w_bf16, preferred_element_type=F32)
    for p in parts[1:]:
        out = out + jnp.dot(p, w_bf16, preferred_element_type=F32)
    return out


def _logf_cumsum_kernel(x_ref, tri_ref, low_ref, o_ref):
    x = x_ref[0]
    logf = jnp.minimum(x, 0.0) - jnp.log1p(jnp.exp(-jnp.abs(x)))
    within = _dot_split(logf, tri_ref[...], 3)
    hi, mid, lo = _split3(logf)
    low = low_ref[...]
    before = (jnp.dot(low, hi, preferred_element_type=F32) + jnp.dot(low, mid, preferred_element_type=F32)
              + jnp.dot(low, lo, preferred_element_type=F32))
    o_ref[0] = -(within + jnp.sum(before, axis=-1, keepdims=True))


def _neg_cum_logf(logits_rows):
    r, t = logits_rows.shape
    nc = t // LANES
    li = np.arange(LANES)
    tri = jnp.asarray((li[:, None] <= li[None, :]).astype(np.float32), BF16)
    ci = np.arange(nc)
    low = jnp.asarray((ci[None, :] < ci[:, None]).astype(np.float32), BF16)
    out = pl.pallas_call(
        _logf_cumsum_kernel,
        grid=(r,),
        in_specs=[pl.BlockSpec((1, nc, LANES), lambda i: (i, 0, 0)),
                  pl.BlockSpec((LANES, LANES), lambda i: (0, 0)),
                  pl.BlockSpec((nc, nc), lambda i: (0, 0))],
        out_specs=pl.BlockSpec((1, nc, LANES), lambda i: (i, 0, 0)),
        out_shape=jax.ShapeDtypeStruct((r, nc, LANES), F32),
        compiler_params=_cparams(("parallel",)),
        name="fox_logf_cumsum",
    )(logits_rows.reshape(r, nc, LANES), tri, low)
    return out.reshape(r, t)


def _chain_rows(c):
    return slice(c * Q_ROWS, (c + 1) * Q_ROWS)


def _own_causal(c, width, strict=False):
    col = lax.broadcasted_iota(jnp.int32, (1, width), 1)
    row = lax.broadcasted_iota(jnp.int32, (Q_ROWS, 1), 0) + c * Q_ROWS
    return (col < row) if strict else (col <= row)


def _fox_kernel(q_ref, k_ref, v_ref, nf_ref, o_ref, knorm_ref, vaug_ref, m_ref, acc_ref, *, wide):
    t0 = pl.program_id(2) * (Q_CHAINS * Q_ROWS)

    @pl.when(pl.program_id(2) == 0)
    def _():
        _max_key_norm(k_ref, knorm_ref)
        _fill_value_aug(vaug_ref, v_ref)

    qs = [q_ref[0, _chain_rows(c), :] for c in range(Q_CHAINS)]
    q_norms = [_row_norms(q) for q in qs]
    states = [(m_ref.at[c], acc_ref.at[c]) for c in range(Q_CHAINS)]
    _flash_init(m_ref, acc_ref)

    def keep_going(first_done):
        prev = jnp.maximum(first_done // wide - 1, 0)
        cap = nf_ref[0, 0, pl.ds(prev, 1), :][:, wide - 1:wide] * (1.0 / ATTN_SCALE)
        return _softmax_alive(q_norms, knorm_ref[...], [cap] * Q_CHAINS, states)

    def full_tile(off):
        bias = nf_ref[0, 0, pl.ds(off // wide, 1), :] * (1.0 / ATTN_SCALE)
        k = k_ref[0, pl.ds(off, wide), :]
        raws = [_qk_raw(q, k) + bias for q in qs]
        _online_update_multi(raws, [vaug_ref[pl.ds(off, wide), :]] * Q_CHAINS, states)

    def tail_tile(off):
        bias = nf_ref[0, 0, pl.ds(off // wide, 1), :] * (1.0 / ATTN_SCALE)
        k = k_ref[0, pl.ds(off, wide), :]
        v = vaug_ref[pl.ds(off, wide), :]
        ends = [(c + 1) * Q_ROWS for c in range(Q_CHAINS)]
        raws = [jnp.where(_own_causal(c, e), _qk_raw(q, k[:e]) + bias[:, :e], NEG_BIG)
                for c, (q, e) in enumerate(zip(qs, ends))]
        _online_update_multi(raws, [v[:e] for e in ends], states)

    _sweep_tiles(t0, wide, tail_tile, full_tile, keep_going)
    for c in range(Q_CHAINS):
        o_ref[0, _chain_rows(c), :] = _flash_result(acc_ref.at[c])


def _attn_scratch(t, n_states):
    return [pltpu.VMEM((t, HEAD_DIM + LANES), BF16), pltpu.VMEM((n_states, Q_ROWS, 1), F32),
            pltpu.VMEM((n_states, Q_ROWS, HEAD_DIM + LANES), F32)]


def _fox_attention(p3, neg_cum_f):
    b, t, _ = p3.shape
    tq = Q_CHAINS * Q_ROWS
    wide = tq
    nf = neg_cum_f.reshape(b, GROUP_HEADS, t // wide, wide)
    return pl.pallas_call(
        functools.partial(_fox_kernel, wide=wide),
        grid=(b, GROUP_HEADS, t // tq),
        in_specs=[pl.BlockSpec((1, tq, HEAD_DIM), lambda bi, h, i: (bi, i, COL["fq"] + h)),
                  pl.BlockSpec((1, t, HEAD_DIM), lambda bi, h, i: (bi, 0, COL["fk"] + h)),
                  pl.BlockSpec((1, t, HEAD_DIM), lambda bi, h, i: (bi, 0, COL["fv"] + h)),
                  pl.BlockSpec((1, 1, t // wide, wide), lambda bi, h, i: (bi, h, 0, 0))],
        out_specs=pl.BlockSpec((1, tq, HEAD_DIM), lambda bi, h, i: (bi, i, h)),
        out_shape=jax.ShapeDtypeStruct((b, t, GROUP_WIDTH), F32),
        scratch_shapes=[pltpu.VMEM((1, 1), F32)] + _attn_scratch(t, Q_CHAINS),
        compiler_params=_cparams(("parallel", "parallel", "arbitrary")),
        name="fox_attention",
    )(p3, p3, p3, nf)


def _neg_abs(x):
    bits = lax.bitcast_convert_type(x, jnp.uint32) | jnp.uint32(0x80000000)
    return lax.bitcast_convert_type(bits, F32)


def _stick_kernel(q_ref, k_ref, v_ref, u_ref, o_ref, r_ref, acc_ref, *, wide):
    t0 = pl.program_id(2) * (Q_CHAINS * Q_ROWS)
    chains = range(Q_CHAINS)
    qs = [q_ref[0, _chain_rows(c), :] for c in chains]
    r_ref[...] = jnp.zeros(r_ref.shape, F32)
    acc_ref[...] = jnp.zeros(acc_ref.shape, F32)
    u = u_ref[...]

    def tile(off, tail, which):
        ends = {c: (c + 1) * Q_ROWS if tail else wide for c in which}
        k = k_ref[0, pl.ds(off, wide), :]
        z_all = {c: _qk_raw(qs[c], k[:ends[c]]) * SCALE_LOG2E for c in which}
        later = {c: r_ref[c] for c in which}
        weights = {c: [] for c in which}
        before = _own_causal(0, Q_ROWS, strict=True)
        for sb in reversed(range(wide // Q_ROWS)):
            live = [c for c in which if ends[c] > sb * Q_ROWS]
            diag = [tail and c == sb for c in live]
            cols = slice(sb * Q_ROWS, (sb + 1) * Q_ROWS)
            zs = [z_all[c][:, cols] for c in live]
            log_beta = [jnp.minimum(z, 0.0) - jnp.log(1.0 + jnp.exp2(_neg_abs(z))) * LOG2E for z in zs]
            log_keep = [lb - z for lb, z in zip(log_beta, zs)]
            log_keep = [jnp.where(before, lk, 0.0) if d else lk for d, lk in zip(diag, log_keep)]
            inside = [jnp.dot(jnp.concatenate(_split3(lk)[:2], axis=1), u, preferred_element_type=F32)
                      for lk in log_keep]
            a = [jnp.exp2(lb + cs + later[c]) for c, lb, cs in zip(live, log_beta, inside)]
            a = [jnp.where(before, x, 0.0) if d else x for d, x in zip(diag, a)]
            for c, x, cs, lk in zip(live, a, inside, log_keep):
                weights[c].append(x.astype(BF16))
                later[c] = later[c] + cs[:, 0:1] + lk[:, 0:1]
        v = v_ref[0, pl.ds(off, wide), :]
        for c in which:
            w = weights[c][0] if len(weights[c]) == 1 else jnp.concatenate(weights[c][::-1], axis=1)
            acc_ref[c] += jnp.dot(w, v[:ends[c]], preferred_element_type=F32)
            r_ref[c] = later[c]

    def full_tile(off):
        first_alive = jnp.max(r_ref[0]) > -STICK_DEAD_LOG2
        rest_alive = jnp.max(r_ref[1:]) > -STICK_DEAD_LOG2
        pl.when(rest_alive)(functools.partial(tile, off, False, list(chains)))
        pl.when(first_alive & jnp.logical_not(rest_alive))(functools.partial(tile, off, False, [0]))

    def keep_going(first_done):
        del first_done
        return jnp.max(r_ref[...]) > -STICK_DEAD_LOG2

    _sweep_tiles(t0, wide, functools.partial(tile, tail=True, which=list(chains)), full_tile, keep_going)
    for c in chains:
        o_ref[0, _chain_rows(c), :] = acc_ref[c]


def _stick_attention(p3):
    b, t, _ = p3.shape
    tq = Q_CHAINS * Q_ROWS
    wide = tq
    ki = np.arange(Q_ROWS)
    u = (ki[:, None] > ki[None, :]).astype(np.float32)
    u = jnp.asarray(np.concatenate([u, u], axis=0), BF16)
    return pl.pallas_call(
        functools.partial(_stick_kernel, wide=wide),
        grid=(b, GROUP_HEADS, t // tq),
        in_specs=[pl.BlockSpec((1, tq, HEAD_DIM), lambda bi, h, i: (bi, i, COL["sq"] + h)),
                  pl.BlockSpec((1, t, HEAD_DIM), lambda bi, h, i: (bi, 0, COL["sk"] + h)),
                  pl.BlockSpec((1, t, HEAD_DIM), lambda bi, h, i: (bi, 0, COL["sv"] + h)),
                  pl.BlockSpec((2 * Q_ROWS, Q_ROWS), lambda bi, h, i: (0, 0))],
        out_specs=pl.BlockSpec((1, tq, HEAD_DIM), lambda bi, h, i: (bi, i, h)),
        out_shape=jax.ShapeDtypeStruct((b, t, GROUP_WIDTH), F32),
        scratch_shapes=[pltpu.VMEM((Q_CHAINS, Q_ROWS, 1), F32), pltpu.VMEM((Q_CHAINS, Q_ROWS, HEAD_DIM), F32)],
        compiler_params=_cparams(("parallel", "parallel", "arbitrary")),
        name="stick_attention",
    )(p3, p3, p3, u)


def _moba_kernel(slope_ref, q_ref, k_ref, v_ref, o_ref, kaug_ref, kmean_ref, knorm_ref, vaug_ref, m_ref, acc_ref, *,
                 n_blk, wide):
    h = pl.program_id(1)
    i = pl.program_id(2)
    tq = MOBA_BLOCK
    slope = slope_ref[h]

    @pl.when(i == 0)
    def _():
        kmean_ref[...] = jnp.zeros(kmean_ref.shape, F32)
        kb = k_ref[0].astype(F32).reshape(n_blk, tq, HEAD_DIM)
        kmean_ref[0:n_blk, :] = jnp.mean(kb, axis=1)

        def key_aux(pos, lane):
            onehot = jnp.where(lane == jnp.right_shift(pos, int(math.log2(tq))), 1.0, 0.0)
            hi, mid, lo = _split3(slope * pos.astype(F32) * (1.0 / ATTN_SCALE))
            return jnp.where(lane == n_blk, hi.astype(F32),
                             jnp.where(lane == n_blk + 1, mid.astype(F32),
                                       jnp.where(lane == n_blk + 2, lo.astype(F32), onehot)))

        _fill_key_aug(kaug_ref, k_ref, key_aux)
        _max_key_norm(k_ref, knorm_ref)
        _fill_value_aug(vaug_ref, v_ref)

    chains = range(Q_CHAINS)
    t0 = i * (Q_CHAINS * Q_ROWS)
    qs = [q_ref[0, _chain_rows(c), :] for c in chains]
    q_norms = [_row_norms(q) for q in qs]
    own = [i * Q_CHAINS + c for c in chains]
    nb8 = -(-n_blk // 8) * 8
    kmean = kmean_ref[0:nb8, :].astype(BF16)
    blk_t = lax.broadcasted_iota(jnp.int32, (nb8, Q_ROWS), 0)
    scores = [jnp.where(blk_t < own[c], _qk_raw(kmean, qs[c]), -jnp.inf) for c in chains]
    sels_t = _topk_rounds(scores, blk_t.astype(F32), [jnp.where(blk_t == own[c], 1.0, 0.0) for c in chains],
                          MOBA_TOPK, axis=0)
    unused = jnp.zeros((LANES - nb8, Q_ROWS), F32)
    sels = [jnp.concatenate([s, unused], axis=0).T for s in sels_t]
    blk = lax.broadcasted_iota(jnp.int32, (Q_ROWS, LANES), 1)
    q_aux = [jnp.where(blk < n_blk, jnp.where(sel > 0.5, 0.0, NEG_BIG), jnp.where(blk < n_blk + 3, 1.0, 0.0))
             for sel in sels]
    q_aug = [jnp.concatenate([q, aux.astype(BF16)], axis=1) for q, aux in zip(qs, q_aux)]
    states = [(m_ref.at[c], acc_ref.at[c]) for c in chains]
    _flash_init(m_ref, acc_ref)

    def full_tile(off):
        k = kaug_ref[pl.ds(off, wide), :]
        _online_update_multi([_qk_raw(q, k) for q in q_aug], [vaug_ref[pl.ds(off, wide), :]] * Q_CHAINS, states)

    def tail_tile(off):
        k = kaug_ref[pl.ds(off, wide), :]
        v = vaug_ref[pl.ds(off, wide), :]
        ends = [(c + 1) * Q_ROWS for c in chains]
        raws = [jnp.where(_own_causal(c, e), _qk_raw(q, k[:e]), NEG_BIG) for c, (q, e) in enumerate(zip(q_aug, ends))]
        _online_update_multi(raws, [v[:e] for e in ends], states)

    def keep_going(first_done):
        cap = slope * first_done.astype(F32) * (1.0 / ATTN_SCALE)
        return _softmax_alive(q_norms, knorm_ref[...], [cap] * Q_CHAINS, states)

    _sweep_tiles(t0, wide, tail_tile, full_tile, keep_going)
    for c in chains:
        o_ref[0, _chain_rows(c), :] = _flash_result(acc_ref.at[c])


def _moba_attention(p3, slopes):
    b, t, _ = p3.shape
    tq = Q_CHAINS * Q_ROWS
    n_blk = t // MOBA_BLOCK
    wide = tq
    assert Q_ROWS == MOBA_BLOCK and t % tq == 0 and n_blk + 3 <= LANES
    return pl.pallas_call(
        functools.partial(_moba_kernel, n_blk=n_blk, wide=wide),
        grid=(b, GROUP_HEADS, t // tq),
        in_specs=[pl.BlockSpec(memory_space=pltpu.SMEM),
                  pl.BlockSpec((1, tq, HEAD_DIM), lambda bi, h, i: (bi, i, COL["mq"] + h)),
                  pl.BlockSpec((1, t, HEAD_DIM), lambda bi, h, i: (bi, 0, COL["mk"] + h)),
                  pl.BlockSpec((1, t, HEAD_DIM), lambda bi, h, i: (bi, 0, COL["mv"] + h))],
        out_specs=pl.BlockSpec((1, tq, HEAD_DIM), lambda bi, h, i: (bi, i, h)),
        out_shape=jax.ShapeDtypeStruct((b, t, GROUP_WIDTH), F32),
        scratch_shapes=[pltpu.VMEM((t, HEAD_DIM + LANES), BF16),
                        pltpu.VMEM((LANES, HEAD_DIM), F32), pltpu.VMEM((1, 1), F32)] + _attn_scratch(t, Q_CHAINS),
        compiler_params=_cparams(("parallel", "parallel", "arbitrary")),
        name="moba_attention",
    )(slopes, p3, p3, p3)


def _nsa_compress_kernel(x_ref, pos_ref, w1_ref, w2_ref, o_ref):
    x = (x_ref[0].astype(F32) + pos_ref[...]).astype(BF16)
    hmid = jnp.dot(x, w1_ref[...], preferred_element_type=F32)
    hmid = hmid / (1.0 + jnp.exp(-hmid))
    o_ref[0] = jnp.dot(hmid.astype(BF16), w2_ref[...], preferred_element_type=F32).astype(o_ref.dtype)


def _nsa_compress(blocks, pos, w1, w2):
    r, n, width = blocks.shape
    return pl.pallas_call(
        _nsa_compress_kernel,
        grid=(r,),
        in_specs=[pl.BlockSpec((1, n, width), lambda i: (i, 0, 0)),
                  pl.BlockSpec((1, width), lambda i: (0, 0)),
                  pl.BlockSpec((width, HEAD_DIM), lambda i: (0, 0)),
                  pl.BlockSpec((HEAD_DIM, HEAD_DIM), lambda i: (0, 0))],
        out_specs=pl.BlockSpec((1, n, HEAD_DIM), lambda i: (i, 0, 0)),
        out_shape=jax.ShapeDtypeStruct((r, n, HEAD_DIM), BF16),
        compiler_params=_cparams(("parallel",)),
        name="nsa_compress",
    )(blocks, pos.reshape(1, width).astype(F32), w1.astype(BF16), w2.astype(BF16))


def _nsa_select_kernel(slope_ref, q_ref, kc_ref, vc_ref, a_ref, oc_ref, sel_ref, *, tq, n_cmp, n_blk):
    g = pl.program_id(1)
    i = pl.program_id(2)
    t0 = i * tq
    kc = kc_ref[0, 0]
    vc = vc_ref[0, 0]
    ncp = kc.shape[0]
    tok = lax.broadcasted_iota(jnp.int32, (1, ncp), 1)
    cmp_end = tok * NSA_CMP_STRIDE + (NSA_CMP_LEN - 1)
    row = lax.broadcasted_iota(jnp.int32, (tq, 1), 0)
    admissible = (cmp_end <= row + t0) & (tok < n_cmp)
    rel_end = (cmp_end - t0).astype(F32)

    imp = jnp.zeros((tq, ncp), F32)
    for hh in range(NSA_Q_PER_KV):
        slope = slope_ref[g * NSA_Q_PER_KV + hh]
        q = q_ref[0, :, hh * HEAD_DIM:(hh + 1) * HEAD_DIM]
        s = jnp.where(admissible, _qk(q, kc) + slope * rel_end, NEG_BIG)
        m = jnp.max(s, axis=-1, keepdims=True)
        p = jnp.where(admissible, jnp.exp(s - m), 0.0)
        p = p / jnp.maximum(jnp.sum(p, axis=-1, keepdims=True), F32_TINY)
        oc_ref[0, :, hh * HEAD_DIM:(hh + 1) * HEAD_DIM] = jnp.dot(
            p.astype(BF16), vc, preferred_element_type=F32)
        imp = imp + p

    imp_blk = _dot_split(imp, a_ref[...], 3).T
    blk = lax.broadcasted_iota(jnp.int32, (LANES, tq), 0)
    jt = jnp.right_shift(lax.broadcasted_iota(jnp.int32, (1, tq), 1) + t0, int(math.log2(NSA_SEL_BLOCK)))
    live = (blk <= jt) & (blk < n_blk)
    forced = ((blk == 0) | (blk == jt) | (blk == jt - 1)) & live
    sel = jnp.where(forced, 1.0, 0.0)
    score = jnp.where(live & jnp.logical_not(forced), imp_blk, -jnp.inf)
    sel = _topk_rounds([score], blk.astype(F32), [sel], NSA_SEL_TOPK - 3, axis=0)[0]
    sel_ref[0, 0] = sel.T.astype(sel_ref.dtype)


def _nsa_select(p3, kc, vc, slopes, tq):
    b, t, _ = p3.shape
    n_blk = t // NSA_SEL_BLOCK
    assert n_blk <= LANES and NSA_SEL_TOPK >= 3
    n_cmp = t // NSA_CMP_STRIDE - 1
    ncp = kc.shape[2]
    ratio = NSA_SEL_BLOCK // NSA_CMP_STRIDE
    ti = np.arange(ncp)[:, None]
    bj = np.arange(LANES)[None, :]
    amat = ((ti >= ratio * bj - 1) & (ti <= ratio * bj + ratio - 1) & (ti < n_cmp) & (bj < n_blk))
    amat = jnp.asarray(amat.astype(np.float32), BF16)
    width = NSA_Q_PER_KV * HEAD_DIM
    qblk = COL["nq"] * HEAD_DIM // width
    return pl.pallas_call(
        functools.partial(_nsa_select_kernel, tq=tq, n_cmp=n_cmp, n_blk=n_blk),
        grid=(b, NSA_KV_HEADS, t // tq),
        in_specs=[pl.BlockSpec(memory_space=pltpu.SMEM),
                  pl.BlockSpec((1, tq, width), lambda bi, g, i: (bi, i, qblk + g)),
                  pl.BlockSpec((1, 1, ncp, HEAD_DIM), lambda bi, g, i: (bi, g, 0, 0)),
                  pl.BlockSpec((1, 1, ncp, HEAD_DIM), lambda bi, g, i: (bi, g, 0, 0)),
                  pl.BlockSpec((ncp, LANES), lambda bi, g, i: (0, 0))],
        out_specs=[pl.BlockSpec((1, tq, width), lambda bi, g, i: (bi, i, g)),
                   pl.BlockSpec((1, 1, tq, LANES), lambda bi, g, i: (bi, g, i, 0))],
        out_shape=[jax.ShapeDtypeStruct((b, t, GROUP_WIDTH), F32),
                   jax.ShapeDtypeStruct((b, NSA_KV_HEADS, t, LANES), BF16)],
        compiler_params=_cparams(("parallel", "parallel", "parallel")),
        name="nsa_select",
    )(slopes, p3, kc, vc, amat)


def _nsa_attn_kernel(slope_ref, q_ref, ks_ref, vs_ref, kw_ref, vw_ref, sel_ref, oc_ref, gate_ref,
                     o_ref, kaug_ref, knorm_ref, vwaug_ref, vsaug_ref, m_ref, acc_ref, *, tq, wide):
    g = pl.program_id(1)
    i = pl.program_id(2)
    t0 = pl.multiple_of(i * tq, tq)
    nh = NSA_Q_PER_KV

    @pl.when(i == 0)
    def _():
        sel_shift = int(math.log2(NSA_SEL_BLOCK))
        _fill_key_aug(kaug_ref, ks_ref,
                      lambda pos, lane: jnp.where(lane == jnp.right_shift(pos, sel_shift), 1.0, 0.0))
        _max_key_norm(ks_ref, knorm_ref)
        _fill_value_aug(vsaug_ref, vs_ref)
        _fill_value_aug(vwaug_ref, vw_ref)

    _flash_init(m_ref, acc_ref)
    sel_bias = jnp.where(sel_ref[0, 0].astype(F32) > 0.5, 0.0, NEG_BIG).astype(BF16)
    qs = [q_ref[0, :, hh * HEAD_DIM:(hh + 1) * HEAD_DIM] for hh in range(nh)]
    q_aug = [jnp.concatenate([q, sel_bias], axis=1) for q in qs]
    slope_raw = [slope_ref[g * nh + hh] * (1.0 / ATTN_SCALE) for hh in range(nh)]
    row = lax.broadcasted_iota(jnp.int32, (tq, 1), 0)

    def rel_pos(off, width):
        return lax.broadcasted_iota(jnp.int32, (1, width), 1) + (off - t0)

    sel_states = [(m_ref.at[hh], acc_ref.at[hh]) for hh in range(nh)]
    win_states = [(m_ref.at[nh + hh], acc_ref.at[nh + hh]) for hh in range(nh)]
    q_norms = [_row_norms(q) for q in qs]

    def sel_scores(off, tail):
        rel = rel_pos(off, wide)
        relf = rel.astype(F32)
        k = kaug_ref[pl.ds(off, wide), :]
        raws = [_qk_raw(q_aug[hh], k) + slope_raw[hh] * relf for hh in range(nh)]
        if tail:
            raws = [jnp.where(rel <= row, raw, NEG_BIG) for raw in raws]
        return raws, [vsaug_ref[pl.ds(off, wide), :]] * nh

    def window_scores(off, width, far):
        rel = rel_pos(off, width)
        relf = rel.astype(F32)
        k = kw_ref[0, pl.ds(off, width), :]
        ok = rel <= row
        if far:
            ok = ok & (row - rel < NSA_WINDOW)
        raws = [jnp.where(ok, _qk_raw(qs[hh], k) + slope_raw[hh] * relf, NEG_BIG) for hh in range(nh)]
        return raws, [vwaug_ref[pl.ds(off, width), :]] * nh

    def sel_tile(off, tail):
        raws, vs = sel_scores(off, tail)
        _online_update_multi(raws, vs, sel_states)

    def keep_going(first_done):
        gap = (first_done - t0).astype(F32)
        return _softmax_alive(q_norms, knorm_ref[...], [s * gap for s in slope_raw], sel_states)

    _sweep_tiles(t0, wide, functools.partial(sel_tile, tail=True), functools.partial(sel_tile, tail=False),
                 keep_going)

    n_back = NSA_WINDOW // tq
    for back in range(n_back + 1):
        @pl.when((i == back) if back < n_back else (i >= back))
        def _(back=back):
            raws, vs = window_scores(pl.multiple_of(t0 - back * tq, tq), (back + 1) * tq, back == n_back)
            _online_update_multi(raws, vs, win_states)

    gates = gate_ref[0, 0]
    gates = 1.0 / (1.0 + jnp.exp(-gates))
    for hh in range(nh):
        c0 = hh * NSA_N_BRANCHES
        o_c = oc_ref[0, :, hh * HEAD_DIM:(hh + 1) * HEAD_DIM]
        o_s = _flash_result(acc_ref.at[hh])
        o_w = _flash_result(acc_ref.at[nh + hh])
        o_ref[0, :, hh * HEAD_DIM:(hh + 1) * HEAD_DIM] = (
            gates[:, c0:c0 + 1] * o_c + gates[:, c0 + 1:c0 + 2] * o_s + gates[:, c0 + 2:c0 + 3] * o_w)


def _nsa_attention(p3, sel, o_cmp, gate_logits, slopes, tq):
    b, t, _ = p3.shape
    assert tq == KV_UNIT and NSA_WINDOW == 2 * tq and t // NSA_SEL_BLOCK <= LANES
    width = NSA_Q_PER_KV * HEAD_DIM
    qblk = COL["nq"] * HEAD_DIM // width
    kv = lambda name: pl.BlockSpec((1, t, HEAD_DIM), lambda bi, g, i: (bi, 0, COL[name] + g))
    nst = 2 * NSA_Q_PER_KV
    return pl.pallas_call(
        functools.partial(_nsa_attn_kernel, tq=tq, wide=min(KV_WIDE, t)),
        grid=(b, NSA_KV_HEADS, t // tq),
        in_specs=[pl.BlockSpec(memory_space=pltpu.SMEM),
                  pl.BlockSpec((1, tq, width), lambda bi, g, i: (bi, i, qblk + g)),
                  kv("nks"), kv("nvs"), kv("nkw"), kv("nvw"),
                  pl.BlockSpec((1, 1, tq, LANES), lambda bi, g, i: (bi, g, i, 0)),
                  pl.BlockSpec((1, tq, width), lambda bi, g, i: (bi, i, g)),
                  pl.BlockSpec((1, 1, tq, LANES), lambda bi, g, i: (bi, g, i, 0))],
        out_specs=pl.BlockSpec((1, tq, width), lambda bi, g, i: (bi, i, g)),
        out_shape=jax.ShapeDtypeStruct((b, t, GROUP_WIDTH), F32),
        scratch_shapes=[pltpu.VMEM((t, HEAD_DIM + LANES), BF16), pltpu.VMEM((1, 1), F32),
                        pltpu.VMEM((t, HEAD_DIM + LANES), BF16)] + _attn_scratch(t, nst),
        compiler_params=_cparams(("parallel", "parallel", "arbitrary")),
        name="nsa_attention",
    )(slopes, p3, p3, p3, p3, p3, sel, o_cmp, gate_logits)


def _nsa_blocks(p3, name):
    b, t, _ = p3.shape
    c0 = COL[name] * HEAD_DIM
    x = p3[:, :, c0:c0 + NSA_KV_HEADS * HEAD_DIM].reshape(b, t // NSA_CMP_STRIDE, NSA_CMP_STRIDE,
                                                         NSA_KV_HEADS, HEAD_DIM)
    x = x.transpose(0, 3, 1, 2, 4).reshape(b * NSA_KV_HEADS, t // NSA_CMP_STRIDE, NSA_CMP_STRIDE * HEAD_DIM)
    blocks = jnp.concatenate([x[:, :-1], x[:, 1:]], axis=-1)
    return jnp.pad(blocks, ((0, 0), (0, 1), (0, 0)))


def _alibi_slopes():
    n = 2 * GROUP_HEADS
    slopes = 2.0 ** (-8.0 * np.arange(1, n + 1) / n)
    return jnp.asarray(slopes[0::2], F32), jnp.asarray(slopes[1::2], F32)


def _regroup_w_in(w_in):
    main = jnp.concatenate([w_in[:, REF_OFF[n][0]:REF_OFF[n][0] + REF_OFF[n][1]] for n, _ in _MAIN_PIECES],
                           axis=1).astype(BF16)
    ng0, ngw = REF_OFF["ng"]
    ff0, ffw = REF_OFF["ff"]
    small = jnp.concatenate([w_in[:, ng0:ng0 + ngw], w_in[:, ff0:ff0 + ffw]], axis=1)
    small = jnp.pad(small, ((0, 0), (0, LANES - ngw - ffw))).astype(BF16)
    return main, small


def _mixer(x2d, normed, b, t, w_in, forget_bias, pos_k, w1_k, w2_k, pos_v, w1_v, w2_v, group_gain, w_out, next_gain):
    m = x2d.shape[0]
    w_main, w_small = _regroup_w_in(w_in)
    tm = _pick_tile(m, (1024, 512, 256))
    p3 = _matmul_normed(normed, w_main, BF16, tm, 512).reshape(b, t, MAIN_BLOCKS * HEAD_DIM)
    small = _matmul_normed(normed, w_small, F32, tm, LANES).reshape(b, t, LANES)
    slopes_moba, slopes_nsa = _alibi_slopes()

    o_a = _moba_attention(p3, slopes_moba)

    n_gate = NSA_N_BRANCHES * GROUP_HEADS
    kc = _nsa_compress(_nsa_blocks(p3, "nkc"), pos_k, w1_k, w2_k)
    vc = _nsa_compress(_nsa_blocks(p3, "nvc"), pos_v, w1_v, w2_v)
    ncp = kc.shape[1]
    kc = kc.reshape(b, NSA_KV_HEADS, ncp, HEAD_DIM)
    vc = vc.reshape(b, NSA_KV_HEADS, ncp, HEAD_DIM)
    o_cmp, sel = _nsa_select(p3, kc, vc, slopes_nsa, _pick_tile(t, (512, 256)))
    per_group = n_gate // NSA_KV_HEADS
    gate_logits = small[:, :, :n_gate].reshape(b, t, NSA_KV_HEADS, per_group).transpose(0, 2, 1, 3)
    gate_logits = jnp.pad(gate_logits, ((0, 0), (0, 0), (0, 0), (0, LANES - per_group)))
    o_b = _nsa_attention(p3, sel, o_cmp, gate_logits, slopes_nsa, KV_UNIT)

    ff = small[:, :, n_gate:n_gate + GROUP_HEADS] + forget_bias
    neg_cum_f = _neg_cum_logf(ff.transpose(0, 2, 1).reshape(b * GROUP_HEADS, t))
    o_c = _fox_attention(p3, neg_cum_f)

    o_d = _stick_attention(p3)

    on = _groupnorm([o.reshape(m, GROUP_WIDTH) for o in (o_a, o_b, o_c, o_d)], group_gain)
    return _matmul_residual(on, w_out, x2d, next_gain, tm, 512)


def _cast_pad_kernel(w_ref, o_ref, *, n_row_tiles, cols):
    @pl.when(pl.program_id(0) < n_row_tiles)
    def _():
        o_ref[:, 0:cols] = w_ref[0].astype(BF16)
        if o_ref.shape[1] > cols:
            o_ref[:, cols:] = jnp.zeros((o_ref.shape[0], o_ref.shape[1] - cols), BF16)

    @pl.when(pl.program_id(0) >= n_row_tiles)
    def _():
        o_ref[...] = jnp.zeros(o_ref.shape, BF16)


def _weight_bf16(stacked, layer, rows_out, cols_out, tr=256):
    _, rows, cols = stacked.shape
    assert rows % tr == 0 and rows_out % tr == 0 and cols % LANES == 0
    n_row_tiles = rows // tr
    return pl.pallas_call(
        functools.partial(_cast_pad_kernel, n_row_tiles=n_row_tiles, cols=cols),
        grid=(rows_out // tr,),
        in_specs=[pl.BlockSpec((1, tr, cols), lambda j: (layer, jnp.minimum(j, n_row_tiles - 1), 0))],
        out_specs=pl.BlockSpec((tr, cols_out), lambda j: (j, 0)),
        out_shape=jax.ShapeDtypeStruct((rows_out, cols_out), BF16),
        compiler_params=_cparams(("parallel",)),
        name="weight_bf16",
    )(stacked)


def kernel(x, ffn1_norm, ffn1_w_gate, ffn1_w_up, ffn1_w_down, mix_norm, w_in, fox_forget_bias, nsa_cmp_pos_k, nsa_cmp_w1_k, nsa_cmp_w2_k, nsa_cmp_pos_v, nsa_cmp_w1_v, nsa_cmp_w2_v, group_norm, w_out, ffn2_norm, ffn2_w_gate, ffn2_w_up, ffn2_w_down, final_norm):
    b, t, d = x.shape
    x2d = x.reshape(b * t, d)
    depth = ffn1_norm.shape[0]
    normed = _prenorm(x2d, ffn1_norm[0])
    for i in range(depth):
        x2d, normed = _ffn(x2d, normed, ffn1_w_gate, ffn1_w_up, ffn1_w_down, i, mix_norm[i])
        x2d, normed = _mixer(x2d, normed, b, t, w_in[i], fox_forget_bias[i], nsa_cmp_pos_k[i], nsa_cmp_w1_k[i],
                             nsa_cmp_w2_k[i], nsa_cmp_pos_v[i], nsa_cmp_w1_v[i], nsa_cmp_w2_v[i],
                             group_norm[i], _weight_bf16(w_out, i, d, d), ffn2_norm[i])
        x2d, normed = _ffn(x2d, normed, ffn2_w_gate, ffn2_w_up, ffn2_w_down, i,
                           ffn1_norm[i + 1] if i + 1 < depth else None)
    return _rmsnorm(x2d, final_norm, F32).reshape(b, t, d)
```
